```python
import math
import jax, jax.numpy as jnp
from jax import lax
import numpy as np

D_MODEL = 2048
BATCH = 8
SEQ = 4096
DEPTH = 2

CHUNK = 64
Q_BLOCK = 128
MLA_HEADS = 8
MLA_Q_LORA = 512
MLA_KV_LORA = 256
MLA_NOPE_DIM = 128
MLA_ROPE_DIM = 64
MLA_V_DIM = 128
RET_HEADS = 4
RET_QK_DIM = 256
RET_V_DIM = 256
MIX_WIDTH = MLA_HEADS * MLA_V_DIM + RET_HEADS * RET_V_DIM
D_FF = -(-(8 * D_MODEL) // (3 * 256)) * 256
ROPE_THETA = 10000.0
LN_EPS = 1e-5
RMS_EPS = 1e-6
GN_EPS = 1e-5
ALPHA = (2 * DEPTH) ** 0.25
BETA = (8 * DEPTH) ** -0.25
IN_SIZES = (MLA_Q_LORA, MLA_KV_LORA, MLA_ROPE_DIM,
            RET_HEADS * RET_QK_DIM, RET_HEADS * RET_QK_DIM,
            RET_HEADS * RET_V_DIM, RET_HEADS * RET_V_DIM)
D_IN = sum(IN_SIZES)

kernel_name = "hybrid_mla_retention_deepnorm"


def layer_norm(x, g, b):
    xf = x.astype(jnp.float32)
    mu = xf.mean(-1, keepdims=True)
    var = jnp.square(xf - mu).mean(-1, keepdims=True)
    return ((xf - mu) * lax.rsqrt(var + LN_EPS) * g + b).astype(x.dtype)


def rms_norm(x, g):
    xf = x.astype(jnp.float32)
    return (xf * lax.rsqrt(jnp.square(xf).mean(-1, keepdims=True) + RMS_EPS) * g).astype(x.dtype)


def rope_tables(positions, dim):
    inv_freq = ROPE_THETA ** (-jnp.arange(0, dim, 2, dtype=jnp.float32) / dim)
    ang = positions.astype(jnp.float32)[..., None] * inv_freq
    return jnp.cos(ang), jnp.sin(ang)


def apply_rope(t, cos, sin):
    tf = t.astype(jnp.float32)
    half = t.shape[-1] // 2
    t1, t2 = tf[..., :half], tf[..., half:]
    c, s = cos[:, :, None, :], sin[:, :, None, :]
    return jnp.concatenate([t1 * c - t2 * s, t2 * c + t1 * s], axis=-1).astype(t.dtype)


def split_columns(h):
    parts, start = [], 0
    for size in IN_SIZES:
        parts.append(h[..., start:start + size])
        start += size
    return parts


def mla_group(c_q, c_kv, k_rope, cos, sin, q_norm_g, kv_norm_g, w_uq, w_ukv):
    B, S, _ = c_q.shape
    H = MLA_HEADS
    q = (rms_norm(c_q, q_norm_g) @ w_uq).reshape(B, S, H, MLA_NOPE_DIM + MLA_ROPE_DIM)
    q_nope = q[..., :MLA_NOPE_DIM]
    q_rope = apply_rope(q[..., MLA_NOPE_DIM:], cos, sin)
    kv = (rms_norm(c_kv, kv_norm_g) @ w_ukv).reshape(B, S, H, MLA_NOPE_DIM + MLA_V_DIM)
    k_nope, v = kv[..., :MLA_NOPE_DIM], kv[..., MLA_NOPE_DIM:]
    k_r = apply_rope(k_rope[:, :, None, :], cos, sin)[:, :, 0, :]
    scale = (MLA_NOPE_DIM + MLA_ROPE_DIM) ** -0.5
    chunk_id = jnp.arange(S) // CHUNK
    neg = jnp.finfo(jnp.float32).min
    outs = []
    for blk in range(S // Q_BLOCK):
        q0 = blk * Q_BLOCK
        kend = q0 + Q_BLOCK
        s = (jnp.einsum('bqhd,bkhd->bhqk', q_nope[:, q0:kend], k_nope[:, :kend])
             + jnp.einsum('bqhr,bkr->bhqk', q_rope[:, q0:kend], k_r[:, :kend]))
        s = s.astype(jnp.float32) * scale
        mask = chunk_id[q0:kend, None] >= chunk_id[None, :kend]
        s = jnp.where(mask[None, None], s, neg)
        p = jax.nn.softmax(s, axis=-1).astype(v.dtype)
        outs.append(jnp.einsum('bhqk,bkhd->bqhd', p, v[:, :kend]))
    o = jnp.concatenate(outs, axis=1)
    return o.reshape(B, S, H * MLA_V_DIM)


def retention_group(rq, rk, rv, rg, cos, sin, gn_g, gn_b):
    B, S, _ = rq.shape
    H, DK, DV, L = RET_HEADS, RET_QK_DIM, RET_V_DIM, CHUNK
    NC = S // L
    f32 = jnp.float32
    q = apply_rope(rq.reshape(B, S, H, DK), cos, sin).astype(f32) * (DK ** -0.5)
    k = apply_rope(rk.reshape(B, S, H, DK), cos, sin).astype(f32)
    v = rv.reshape(B, S, H, DV).astype(f32)
    q = q.reshape(B, NC, L, H, DK)
    k = k.reshape(B, NC, L, H, DK)
    v = v.reshape(B, NC, L, H, DV)
    log_gamma = jnp.log1p(-jnp.exp2(-5.0 - jnp.arange(H, dtype=f32)))
    idx = jnp.arange(L, dtype=f32)
    intra_decay = jnp.exp(log_gamma[:, None, None] * jnp.abs(idx[:, None] - idx[None, :]))
    scores = jnp.einsum('bcnhd,bcmhd->bchnm', q, k) * intra_decay[None, None]
    o_intra = jnp.einsum('bchnm,bcmhe->bcnhe', scores, v)
    q_decay = jnp.exp(log_gamma[:, None] * (idx + 1.0))[None]
    k_decay = jnp.exp(log_gamma[:, None] * (L - 1.0 - idx))
    chunk_decay = jnp.exp(log_gamma * L)
    q_decay = q_decay[0]

    def step(state, inp):
        qc, kc, vc = inp
        o_inter = jnp.einsum('bnhd,hn,bhde->bnhe', qc, q_decay, state)
        state = (state * chunk_decay[None, :, None, None]
                 + jnp.einsum('bmhd,hm,bmhe->bhde', kc, k_decay, vc))
        return state, o_inter

    state0 = jnp.zeros((B, H, DK, DV), f32)
    xs = (q.transpose(1, 0, 2, 3, 4), k.transpose(1, 0, 2, 3, 4), v.transpose(1, 0, 2, 3, 4))
    _, o_inter = lax.scan(step, state0, xs)
    o = (o_intra + o_inter.transpose(1, 0, 2, 3, 4)).reshape(B, S, H, DV)
    mu = o.mean(-1, keepdims=True)
    var = jnp.square(o - mu).mean(-1, keepdims=True)
    o = ((o - mu) * lax.rsqrt(var + GN_EPS)).reshape(B, S, H * DV) * gn_g + gn_b
    o = jax.nn.silu(rg.astype(f32)) * o
    return o.astype(rq.dtype)


def _fwd_setup_inputs(seed: int = 0) -> dict:
    key = jax.random.key(seed)
    ks = list(jax.random.split(key, 24))
    f32 = jnp.float32

    def nrm(k, shape, scale):
        return jax.random.normal(k, shape, f32) * scale

    x = jax.random.normal(ks[0], (BATCH, SEQ, D_MODEL), f32)
    start = jax.random.randint(ks[1], (BATCH, 1), 0, 4096, dtype=jnp.int32)
    positions = (start + jnp.arange(SEQ, dtype=jnp.int32)[None, :]).astype(jnp.int32)
    return {
        "x": x,
        "positions": positions,
        "ln_in_g": 1.0 + nrm(ks[2], (D_MODEL,), 0.02),
        "ln_in_b": nrm(ks[3], (D_MODEL,), 0.02),
        "w_in": nrm(ks[4], (DEPTH, D_MODEL, D_IN), D_MODEL ** -0.5),
        "q_norm_g": 1.0 + nrm(ks[5], (DEPTH, MLA_Q_LORA), 0.02),
        "kv_norm_g": 1.0 + nrm(ks[6], (DEPTH, MLA_KV_LORA), 0.02),
        "w_uq": nrm(ks[7], (DEPTH, MLA_Q_LORA, MLA_HEADS * (MLA_NOPE_DIM + MLA_ROPE_DIM)), MLA_Q_LORA ** -0.5),
        "w_ukv": nrm(ks[8], (DEPTH, MLA_KV_LORA, MLA_HEADS * (MLA_NOPE_DIM + MLA_V_DIM)), MLA_KV_LORA ** -0.5),
        "ret_gn_g": 1.0 + nrm(ks[9], (DEPTH, RET_HEADS * RET_V_DIM), 0.02),
        "ret_gn_b": nrm(ks[10], (DEPTH, RET_HEADS * RET_V_DIM), 0.02),
        "w_out": nrm(ks[11], (DEPTH, MIX_WIDTH, D_MODEL), (MIX_WIDTH ** -0.5) * BETA),
        "ln1_g": 1.0 + nrm(ks[12], (DEPTH, D_MODEL), 0.02),
        "ln1_b": nrm(ks[13], (DEPTH, D_MODEL), 0.02),
        "w_gate": nrm(ks[14], (DEPTH, D_MODEL, D_FF), D_MODEL ** -0.5),
        "w_up": nrm(ks[15], (DEPTH, D_MODEL, D_FF), D_MODEL ** -0.5),
        "w_down": nrm(ks[16], (DEPTH, D_FF, D_MODEL), (D_FF ** -0.5) * BETA),
        "ln2_g": 1.0 + nrm(ks[17], (DEPTH, D_MODEL), 0.02),
        "ln2_b": nrm(ks[18], (DEPTH, D_MODEL), 0.02),
    }


def _fwd_reference(x, positions, ln_in_g, ln_in_b, w_in, q_norm_g, kv_norm_g, w_uq, w_ukv,
              ret_gn_g, ret_gn_b, w_out, ln1_g, ln1_b, w_gate, w_up, w_down, ln2_g, ln2_b):
    cos_m, sin_m = rope_tables(positions, MLA_ROPE_DIM)
    cos_r, sin_r = rope_tables(positions, RET_QK_DIM)
    x = layer_norm(x, ln_in_g, ln_in_b)
    for l in range(DEPTH):
        h = x @ w_in[l]
        c_q, c_kv, k_rope, rq, rk, rv, rg = split_columns(h)
        a = mla_group(c_q, c_kv, k_rope, cos_m, sin_m, q_norm_g[l], kv_norm_g[l], w_uq[l], w_ukv[l])
        r = retention_group(rq, rk, rv, rg, cos_r, sin_r, ret_gn_g[l], ret_gn_b[l])
        mix = jnp.concatenate([a, r], axis=-1) @ w_out[l]
        x = layer_norm(ALPHA * x + mix, ln1_g[l], ln1_b[l])
        f = (jax.nn.silu(x @ w_gate[l]) * (x @ w_up[l])) @ w_down[l]
        x = layer_norm(ALPHA * x + f, ln2_g[l], ln2_b[l])
    return x


import jax as _jax
import jax.numpy as _jnp

TWIN_FORMAT = 'train_step'
FWD_PARAMS = ['x', 'positions', 'ln_in_g', 'ln_in_b', 'w_in', 'q_norm_g', 'kv_norm_g', 'w_uq', 'w_ukv', 'ret_gn_g', 'ret_gn_b', 'w_out', 'ln1_g', 'ln1_b', 'w_gate', 'w_up', 'w_down', 'ln2_g', 'ln2_b']
TWIN_WEIGHTS = ['ln_in_g', 'ln_in_b', 'w_in', 'q_norm_g', 'kv_norm_g', 'w_uq', 'w_ukv', 'ret_gn_g', 'ret_gn_b', 'w_out', 'ln1_g', 'ln1_b', 'w_gate', 'w_up', 'w_down', 'ln2_g', 'ln2_b']
TWIN_DIFF_INPUT = 'x'
TWIN_INPUTS = ['x', 'positions', 'ln_in_g', 'ln_in_b', 'w_in', 'q_norm_g', 'kv_norm_g', 'w_uq', 'w_ukv', 'ret_gn_g', 'ret_gn_b', 'w_out', 'ln1_g', 'ln1_b', 'w_gate', 'w_up', 'w_down', 'ln2_g', 'ln2_b', 'loss_target', 'm_ln_in_g', 'm_ln_in_b', 'm_w_in', 'm_q_norm_g', 'm_kv_norm_g', 'm_w_uq', 'm_w_ukv', 'm_ret_gn_g', 'm_ret_gn_b', 'm_w_out', 'm_ln1_g', 'm_ln1_b', 'm_w_gate', 'm_w_up', 'm_w_down', 'm_ln2_g', 'm_ln2_b', 'v_ln_in_g', 'v_ln_in_b', 'v_w_in', 'v_q_norm_g', 'v_kv_norm_g', 'v_w_uq', 'v_w_ukv', 'v_ret_gn_g', 'v_ret_gn_b', 'v_w_out', 'v_ln1_g', 'v_ln1_b', 'v_w_gate', 'v_w_up', 'v_w_down', 'v_ln2_g', 'v_ln2_b']
TWIN_OUTPUTS = ['loss', 'grad_x', 'grad_ln_in_g', 'grad_ln_in_b', 'grad_w_in', 'grad_q_norm_g', 'grad_kv_norm_g', 'grad_w_uq', 'grad_w_ukv', 'grad_ret_gn_g', 'grad_ret_gn_b', 'grad_w_out', 'grad_ln1_g', 'grad_ln1_b', 'grad_w_gate', 'grad_w_up', 'grad_w_down', 'grad_ln2_g', 'grad_ln2_b', 'delta_ln_in_g', 'delta_ln_in_b', 'delta_w_in', 'delta_q_norm_g', 'delta_kv_norm_g', 'delta_w_uq', 'delta_w_ukv', 'delta_ret_gn_g', 'delta_ret_gn_b', 'delta_w_out', 'delta_ln1_g', 'delta_ln1_b', 'delta_w_gate', 'delta_w_up', 'delta_w_down', 'delta_ln2_g', 'delta_ln2_b', 'new_m_ln_in_g', 'new_m_ln_in_b', 'new_m_w_in', 'new_m_q_norm_g', 'new_m_kv_norm_g', 'new_m_w_uq', 'new_m_w_ukv', 'new_m_ret_gn_g', 'new_m_ret_gn_b', 'new_m_w_out', 'new_m_ln1_g', 'new_m_ln1_b', 'new_m_w_gate', 'new_m_w_up', 'new_m_w_down', 'new_m_ln2_g', 'new_m_ln2_b', 'new_v_ln_in_g', 'new_v_ln_in_b', 'new_v_w_in', 'new_v_q_norm_g', 'new_v_kv_norm_g', 'new_v_w_uq', 'new_v_w_ukv', 'new_v_ret_gn_g', 'new_v_ret_gn_b', 'new_v_w_out', 'new_v_ln1_g', 'new_v_ln1_b', 'new_v_w_gate', 'new_v_w_up', 'new_v_w_down', 'new_v_ln2_g', 'new_v_ln2_b']
TWIN_LEAF_KINDS = {'loss': 'loss', 'grad_x': 'grad_x', 'grad_ln_in_g': 'grad_w', 'grad_ln_in_b': 'grad_w', 'grad_w_in': 'grad_w', 'grad_q_norm_g': 'grad_w', 'grad_kv_norm_g': 'grad_w', 'grad_w_uq': 'grad_w', 'grad_w_ukv': 'grad_w', 'grad_ret_gn_g': 'grad_w', 'grad_ret_gn_b': 'grad_w', 'grad_w_out': 'grad_w', 'grad_ln1_g': 'grad_w', 'grad_ln1_b': 'grad_w', 'grad_w_gate': 'grad_w', 'grad_w_up': 'grad_w', 'grad_w_down': 'grad_w', 'grad_ln2_g': 'grad_w', 'grad_ln2_b': 'grad_w', 'delta_ln_in_g': 'delta_w', 'delta_ln_in_b': 'delta_w', 'delta_w_in': 'delta_w', 'delta_q_norm_g': 'delta_w', 'delta_kv_norm_g': 'delta_w', 'delta_w_uq': 'delta_w', 'delta_w_ukv': 'delta_w', 'delta_ret_gn_g': 'delta_w', 'delta_ret_gn_b': 'delta_w', 'delta_w_out': 'delta_w', 'delta_ln1_g': 'delta_w', 'delta_ln1_b': 'delta_w', 'delta_w_gate': 'delta_w', 'delta_w_up': 'delta_w', 'delta_w_down': 'delta_w', 'delta_ln2_g': 'delta_w', 'delta_ln2_b': 'delta_w', 'new_m_ln_in_g': 'new_m', 'new_m_ln_in_b': 'new_m', 'new_m_w_in': 'new_m', 'new_m_q_norm_g': 'new_m', 'new_m_kv_norm_g': 'new_m', 'new_m_w_uq': 'new_m', 'new_m_w_ukv': 'new_m', 'new_m_ret_gn_g': 'new_m', 'new_m_ret_gn_b': 'new_m', 'new_m_w_out': 'new_m', 'new_m_ln1_g': 'new_m', 'new_m_ln1_b': 'new_m', 'new_m_w_gate': 'new_m', 'new_m_w_up': 'new_m', 'new_m_w_down': 'new_m', 'new_m_ln2_g': 'new_m', 'new_m_ln2_b': 'new_m', 'new_v_ln_in_g': 'new_v', 'new_v_ln_in_b': 'new_v', 'new_v_w_in': 'new_v', 'new_v_q_norm_g': 'new_v', 'new_v_kv_norm_g': 'new_v', 'new_v_w_uq': 'new_v', 'new_v_w_ukv': 'new_v', 'new_v_ret_gn_g': 'new_v', 'new_v_ret_gn_b': 'new_v', 'new_v_w_out': 'new_v', 'new_v_ln1_g': 'new_v', 'new_v_ln1_b': 'new_v', 'new_v_w_gate': 'new_v', 'new_v_w_up': 'new_v', 'new_v_w_down': 'new_v', 'new_v_ln2_g': 'new_v', 'new_v_ln2_b': 'new_v'}


def _forward(args):
    return _fwd_reference(*[args[k] for k in FWD_PARAMS])


def _output_shape():
    def fwd():
        inp = _fwd_setup_inputs(0)
        return _fwd_reference(*[inp[k] for k in FWD_PARAMS])
    out = _jax.eval_shape(fwd)
    return out.shape, out.dtype

N_MICROBATCH = 1
ADAM_LR = 0.001
ADAM_B1 = 0.9
ADAM_B2 = 0.999
ADAM_EPS = 1e-08
ADAM_WD = 0.01
ADAM_STEP = 10
PER_EXAMPLE_BATCH_AXIS = {'x': 0, 'positions': 0, 'loss_target': 0}
SHARED_INPUTS = []
_WEIGHT_DTYPES = {'ln_in_g': _jnp.float32, 'ln_in_b': _jnp.float32, 'w_in': _jnp.float32, 'q_norm_g': _jnp.float32, 'kv_norm_g': _jnp.float32, 'w_uq': _jnp.float32, 'w_ukv': _jnp.float32, 'ret_gn_g': _jnp.float32, 'ret_gn_b': _jnp.float32, 'w_out': _jnp.float32, 'ln1_g': _jnp.float32, 'ln1_b': _jnp.float32, 'w_gate': _jnp.float32, 'w_up': _jnp.float32, 'w_down': _jnp.float32, 'ln2_g': _jnp.float32, 'ln2_b': _jnp.float32}
MOMENT_SCALE = {'ln_in_g': 5.199655e-01, 'ln_in_b': 2.889393e-01, 'w_in': 1.828408e-02, 'q_norm_g': 6.455948e-03, 'kv_norm_g': 1.445740e-02, 'w_uq': 3.885138e-03, 'w_ukv': 4.643539e-03, 'ret_gn_g': 1.963290e-02, 'ret_gn_b': 2.639912e-02, 'w_out': 2.831799e-02, 'ln1_g': 5.511517e-01, 'ln1_b': 2.840230e-01, 'w_gate': 1.169803e-02, 'w_up': 1.134056e-02, 'w_down': 3.759385e-02, 'ln2_g': 1.133381e+01, 'ln2_b': 4.948030e-01}


def _to_microbatches(a, axis):
    t = _jnp.moveaxis(a, axis, 0)
    t = t.reshape((N_MICROBATCH, t.shape[0] // N_MICROBATCH) + t.shape[1:])
    return _jnp.moveaxis(t, 1, axis + 1)


def setup_inputs(seed: int = 0) -> dict:
    inp = _fwd_setup_inputs(seed)
    key = _jax.random.fold_in(_jax.random.key(seed), 7919)
    shape, _ = _output_shape()
    out = dict(inp)
    out["loss_target"] = _jax.random.normal(_jax.random.fold_in(key, 0), shape, _jnp.float32)
    for i, name in enumerate(TWIN_WEIGHTS):
        w = inp[name].astype(_jnp.float32)
        if MOMENT_SCALE is None:
            s = _jnp.sqrt(_jnp.mean(_jnp.square(w)) + 1e-30)
        else:
            s = MOMENT_SCALE[name]
        km, kv = _jax.random.split(_jax.random.fold_in(key, i + 1))
        out[name] = w
        out["m_" + name] = s * _jax.random.normal(km, w.shape, _jnp.float32)
        out["v_" + name] = (s * s) * _jax.random.uniform(kv, w.shape, _jnp.float32, 0.5, 1.5)
    if N_MICROBATCH > 1:
        for name, axis in PER_EXAMPLE_BATCH_AXIS.items():
            out[name] = _to_microbatches(out[name], axis)
    return {'x': out['x'], 'positions': out['positions'], 'ln_in_g': out['ln_in_g'], 'ln_in_b': out['ln_in_b'], 'w_in': out['w_in'], 'q_norm_g': out['q_norm_g'], 'kv_norm_g': out['kv_norm_g'], 'w_uq': out['w_uq'], 'w_ukv': out['w_ukv'], 'ret_gn_g': out['ret_gn_g'], 'ret_gn_b': out['ret_gn_b'], 'w_out': out['w_out'], 'ln1_g': out['ln1_g'], 'ln1_b': out['ln1_b'], 'w_gate': out['w_gate'], 'w_up': out['w_up'], 'w_down': out['w_down'], 'ln2_g': out['ln2_g'], 'ln2_b': out['ln2_b'], 'loss_target': out['loss_target'], 'm_ln_in_g': out['m_ln_in_g'], 'm_ln_in_b': out['m_ln_in_b'], 'm_w_in': out['m_w_in'], 'm_q_norm_g': out['m_q_norm_g'], 'm_kv_norm_g': out['m_kv_norm_g'], 'm_w_uq': out['m_w_uq'], 'm_w_ukv': out['m_w_ukv'], 'm_ret_gn_g': out['m_ret_gn_g'], 'm_ret_gn_b': out['m_ret_gn_b'], 'm_w_out': out['m_w_out'], 'm_ln1_g': out['m_ln1_g'], 'm_ln1_b': out['m_ln1_b'], 'm_w_gate': out['m_w_gate'], 'm_w_up': out['m_w_up'], 'm_w_down': out['m_w_down'], 'm_ln2_g': out['m_ln2_g'], 'm_ln2_b': out['m_ln2_b'], 'v_ln_in_g': out['v_ln_in_g'], 'v_ln_in_b': out['v_ln_in_b'], 'v_w_in': out['v_w_in'], 'v_q_norm_g': out['v_q_norm_g'], 'v_kv_norm_g': out['v_kv_norm_g'], 'v_w_uq': out['v_w_uq'], 'v_w_ukv': out['v_w_ukv'], 'v_ret_gn_g': out['v_ret_gn_g'], 'v_ret_gn_b': out['v_ret_gn_b'], 'v_w_out': out['v_w_out'], 'v_ln1_g': out['v_ln1_g'], 'v_ln1_b': out['v_ln1_b'], 'v_w_gate': out['v_w_gate'], 'v_w_up': out['v_w_up'], 'v_w_down': out['v_w_down'], 'v_ln2_g': out['v_ln2_g'], 'v_ln2_b': out['v_ln2_b']}


def _loss(weights, diff, rest, loss_target):
    with _jax.named_scope("forward"):
        args = {**rest, TWIN_DIFF_INPUT: diff, **{k: w.astype(_WEIGHT_DTYPES[k]) for k, w in weights.items()}}
        y = _forward(args)
    with _jax.named_scope("loss_head"):
        err = _jnp.square(y.astype(_jnp.float32) - loss_target)
        return 0.5 * _jnp.sum(_jnp.mean(err, axis=-1)) if err.ndim else 0.5 * err


def _adamw(w, g, m, v):
    m = ADAM_B1 * m + (1.0 - ADAM_B1) * g
    v = ADAM_B2 * v + (1.0 - ADAM_B2) * _jnp.square(g)
    m_hat = m / (1.0 - ADAM_B1 ** ADAM_STEP)
    v_hat = v / (1.0 - ADAM_B2 ** ADAM_STEP)
    delta = -ADAM_LR * (m_hat / (_jnp.sqrt(v_hat) + ADAM_EPS) + ADAM_WD * w)
    return delta, m, v


def reference(x, positions, ln_in_g, ln_in_b, w_in, q_norm_g, kv_norm_g, w_uq, w_ukv, ret_gn_g, ret_gn_b, w_out, ln1_g, ln1_b, w_gate, w_up, w_down, ln2_g, ln2_b, loss_target, m_ln_in_g, m_ln_in_b, m_w_in, m_q_norm_g, m_kv_norm_g, m_w_uq, m_w_ukv, m_ret_gn_g, m_ret_gn_b, m_w_out, m_ln1_g, m_ln1_b, m_w_gate, m_w_up, m_w_down, m_ln2_g, m_ln2_b, v_ln_in_g, v_ln_in_b, v_w_in, v_q_norm_g, v_kv_norm_g, v_w_uq, v_w_ukv, v_ret_gn_g, v_ret_gn_b, v_w_out, v_ln1_g, v_ln1_b, v_w_gate, v_w_up, v_w_down, v_ln2_g, v_ln2_b):
    given = dict(x=x, positions=positions, ln_in_g=ln_in_g, ln_in_b=ln_in_b, w_in=w_in, q_norm_g=q_norm_g, kv_norm_g=kv_norm_g, w_uq=w_uq, w_ukv=w_ukv, ret_gn_g=ret_gn_g, ret_gn_b=ret_gn_b, w_out=w_out, ln1_g=ln1_g, ln1_b=ln1_b, w_gate=w_gate, w_up=w_up, w_down=w_down, ln2_g=ln2_g, ln2_b=ln2_b, loss_target=loss_target, m_ln_in_g=m_ln_in_g, m_ln_in_b=m_ln_in_b, m_w_in=m_w_in, m_q_norm_g=m_q_norm_g, m_kv_norm_g=m_kv_norm_g, m_w_uq=m_w_uq, m_w_ukv=m_w_ukv, m_ret_gn_g=m_ret_gn_g, m_ret_gn_b=m_ret_gn_b, m_w_out=m_w_out, m_ln1_g=m_ln1_g, m_ln1_b=m_ln1_b, m_w_gate=m_w_gate, m_w_up=m_w_up, m_w_down=m_w_down, m_ln2_g=m_ln2_g, m_ln2_b=m_ln2_b, v_ln_in_g=v_ln_in_g, v_ln_in_b=v_ln_in_b, v_w_in=v_w_in, v_q_norm_g=v_q_norm_g, v_kv_norm_g=v_kv_norm_g, v_w_uq=v_w_uq, v_w_ukv=v_w_ukv, v_ret_gn_g=v_ret_gn_g, v_ret_gn_b=v_ret_gn_b, v_w_out=v_w_out, v_ln1_g=v_ln1_g, v_ln1_b=v_ln1_b, v_w_gate=v_w_gate, v_w_up=v_w_up, v_w_down=v_w_down, v_ln2_g=v_ln2_g, v_ln2_b=v_ln2_b)
    weights = {n: given[n] for n in TWIN_WEIGHTS}
    shared = {n: given[n] for n in SHARED_INPUTS}
    per_example = {n: given[n] for n in ['x', 'positions']}
    grad_fn = _jax.value_and_grad(_loss, argnums=(0, 1))

    def one_microbatch(ex, loss_target):
        ex = dict(ex)
        diff = ex.pop(TWIN_DIFF_INPUT)
        return grad_fn(weights, diff, {**shared, **ex}, loss_target)

    if N_MICROBATCH == 1:
        loss, (grad_w, grad_x) = one_microbatch(per_example, given["loss_target"])
    else:
        def body(carry, xs):
            loss_sum, grad_sum = carry
            l_k, (gw_k, gx_k) = one_microbatch(xs[0], xs[1])
            with _jax.named_scope("update"):
                return (loss_sum + l_k, _jax.tree.map(_jnp.add, grad_sum, gw_k)), gx_k

        init = (_jnp.zeros((), _jnp.float32), _jax.tree.map(_jnp.zeros_like, weights))
        (loss, grad_w), grad_x = _jax.lax.scan(body, init, (per_example, given["loss_target"]))
    with _jax.named_scope("update"):
        delta_w, new_m, new_v = {}, {}, {}
        for n in TWIN_WEIGHTS:
            delta_w[n], new_m[n], new_v[n] = _adamw(weights[n], grad_w[n], given["m_" + n], given["v_" + n])
    return (loss, grad_x, *[grad_w[n] for n in TWIN_WEIGHTS], *[delta_w[n] for n in TWIN_WEIGHTS],
            *[new_m[n] for n in TWIN_WEIGHTS], *[new_v[n] for n in TWIN_WEIGHTS])
```

```python
import jax
import jax.numpy as jnp
from jax import lax
from jax.experimental import pallas as pl
from jax.experimental.pallas import tpu as pltpu

F32 = jnp.float32
BF16 = jnp.bfloat16

CHUNK = 64
MLA_HEADS = 8
MLA_Q_LORA = 512
MLA_KV_LORA = 256
MLA_NOPE = 128
MLA_ROPE = 64
MLA_V = 128
MLA_QK = MLA_NOPE + MLA_ROPE
RET_HEADS = 4
RET_QK = 256
RET_V = 256
ROPE_THETA = 10000.0
LN_EPS = 1e-5
RMS_EPS = 1e-6
GN_EPS = 1e-5
ADAM_LR = 0.001
ADAM_B1 = 0.9
ADAM_B2 = 0.999
ADAM_EPS = 1e-08
ADAM_WD = 0.01
ADAM_STEP = 10

N_DEV = 8
MESH_AXES = ("x", "y", "c")
VMEM_LIMIT_BYTES = 56 * 1024 * 1024
PACK_ROW_ALIGN = 16


def _pick(n, pref, mult):
    best = None
    d = mult
    while d <= min(n, pref):
        if n % d == 0:
            best = d
        d += mult
    return n if best is None else best


def _pcall(body, **kw):
    return pl.pallas_call(body, **kw)


def _params(sem):
    return pltpu.CompilerParams(dimension_semantics=sem, vmem_limit_bytes=VMEM_LIMIT_BYTES)


def _mm(a, b, *, ta=False, tb=False, add=None, add_scale=1.0, name):
    if ta:
        K, M = a.shape
    else:
        M, K = a.shape
    if tb:
        N, K2 = b.shape
    else:
        K2, N = b.shape
    assert K == K2, (a.shape, b.shape, ta, tb)
    bm = _pick(M, 512 if ta else 1024, 128 if ta else 8)
    bn = _pick(N, 512, 128)
    bk = _pick(K, 1024 if ta else 2048, 128)
    nk = K // bk
    a_spec = (pl.BlockSpec((bk, bm), lambda i, j, k: (k, i)) if ta
              else pl.BlockSpec((bm, bk), lambda i, j, k: (i, k)))
    b_spec = (pl.BlockSpec((bn, bk), lambda i, j, k: (j, k)) if tb
              else pl.BlockSpec((bk, bn), lambda i, j, k: (k, j)))
    o_spec = pl.BlockSpec((bm, bn), lambda i, j, k: (i, j))
    dims = (((0 if ta else 1,), (1 if tb else 0,)), ((), ()))
    has_add = add is not None

    def body(*refs):
        if has_add:
            a_ref, b_ref, c_ref, o_ref, acc_ref = refs
        else:
            a_ref, b_ref, o_ref, acc_ref = refs
        k = pl.program_id(2)

        @pl.when(k == 0)
        def _():
            acc_ref[...] = jnp.zeros_like(acc_ref)

        acc_ref[...] += lax.dot_general(a_ref[...].astype(BF16), b_ref[...].astype(BF16), dims,
                                        preferred_element_type=F32)

        @pl.when(k == nk - 1)
        def _():
            r = acc_ref[...]
            if has_add:
                r = r + add_scale * c_ref[...]
            o_ref[...] = r

    in_specs = [a_spec, b_spec] + ([o_spec] if has_add else [])
    args = (a, b) + ((add,) if has_add else ())
    return _pcall(
        body, name=name, grid=(M // bm, N // bn, nk), in_specs=in_specs, out_specs=o_spec,
        out_shape=jax.ShapeDtypeStruct((M, N), F32),
        scratch_shapes=[pltpu.VMEM((bm, bn), F32)],
        compiler_params=_params(("parallel", "parallel", "arbitrary")),
    )(*args)


def _norm_fwd(x, g, b, *, res=None, alpha=1.0, center=True, eps, col=0, name):
    S = x.shape[0]
    W = g.shape[-1]
    bs = _pick(S, 256, 8)
    has_res, has_b = res is not None, b is not None

    def body(*refs):
        it = iter(refs)
        x_ref = next(it)
        res_ref = next(it) if has_res else None
        g_ref = next(it)
        b_ref = next(it) if has_b else None
        y_ref, xh_ref, r_ref = next(it), next(it), next(it)
        z = x_ref[...]
        if has_res:
            z = alpha * z + res_ref[...]
        if center:
            z = z - jnp.mean(z, axis=-1, keepdims=True)
        rstd = lax.rsqrt(jnp.mean(z * z, axis=-1, keepdims=True) + eps)
        xh = z * rstd
        y = xh * g_ref[...]
        if has_b:
            y = y + b_ref[...]
        y_ref[...] = y
        xh_ref[...] = xh
        r_ref[...] = rstd

    row = pl.BlockSpec((bs, W), lambda i: (i, 0))
    vec = pl.BlockSpec((1, W), lambda i: (0, 0))
    in_specs = [pl.BlockSpec((bs, W), lambda i: (i, col))] + ([row] if has_res else []) + [vec] + ([vec] if has_b else [])
    args = (x,) + ((res,) if has_res else ()) + (g.reshape(1, W),) + ((b.reshape(1, W),) if has_b else ())
    return _pcall(
        body, name=name, grid=(S // bs,), in_specs=in_specs,
        out_specs=[row, row, pl.BlockSpec((bs, 1), lambda i: (i, 0))],
        out_shape=[jax.ShapeDtypeStruct((S, W), F32), jax.ShapeDtypeStruct((S, W), F32),
                   jax.ShapeDtypeStruct((S, 1), F32)],
        compiler_params=_params(("parallel",)),
    )(*args)


def _norm_bwd(dy, xh, rstd, g, *, center=True, name):
    S, W = dy.shape
    bs = _pick(S, 256, 8)

    def body(dy_ref, xh_ref, r_ref, g_ref, dz_ref, dg_ref, db_ref):
        @pl.when(pl.program_id(0) == 0)
        def _():
            dg_ref[...] = jnp.zeros_like(dg_ref)
            db_ref[...] = jnp.zeros_like(db_ref)

        dyv = dy_ref[...]
        xhv = xh_ref[...]
        dyg = dyv * g_ref[...]
        m2 = jnp.mean(dyg * xhv, axis=-1, keepdims=True)
        t = dyg - xhv * m2
        if center:
            t = t - jnp.mean(dyg, axis=-1, keepdims=True)
        dz_ref[...] = r_ref[...] * t
        dg_ref[...] += jnp.sum(dyv * xhv, axis=0, keepdims=True)
        db_ref[...] += jnp.sum(dyv, axis=0, keepdims=True)

    row = pl.BlockSpec((bs, W), lambda i: (i, 0))
    vec = pl.BlockSpec((1, W), lambda i: (0, 0))
    dz, dg, db = _pcall(
        body, name=name, grid=(S // bs,),
        in_specs=[row, row, pl.BlockSpec((bs, 1), lambda i: (i, 0)), vec],
        out_specs=[row, vec, vec],
        out_shape=[jax.ShapeDtypeStruct((S, W), F32), jax.ShapeDtypeStruct((1, W), F32),
                   jax.ShapeDtypeStruct((1, W), F32)],
        compiler_params=_params(("arbitrary",)),
    )(dy, xh, rstd, g.reshape(1, W))
    return dz, dg[0], db[0]


def _rope(t1, t2, cos, sin, *, name):
    S, C = t1.shape
    bs = _pick(S, 512, 8)

    def body(t1_ref, t2_ref, c_ref, s_ref, o1_ref, o2_ref):
        a, b, c, s = t1_ref[...], t2_ref[...], c_ref[...], s_ref[...]
        o1_ref[...] = a * c - b * s
        o2_ref[...] = b * c + a * s

    row = pl.BlockSpec((bs, C), lambda i: (i, 0))
    return _pcall(
        body, name=name, grid=(S // bs,), in_specs=[row] * 4, out_specs=[row, row],
        out_shape=[jax.ShapeDtypeStruct((S, C), F32)] * 2,
        compiler_params=_params(("parallel",)),
    )(t1, t2, cos, sin)


def _attn_mask(qi, bq, S):
    rows = (qi * bq + lax.broadcasted_iota(jnp.int32, (bq, S), 0)) // CHUNK
    cols = lax.broadcasted_iota(jnp.int32, (bq, S), 1) // CHUNK
    return rows >= cols


def _attn_fwd(qh, kh, vh, *, name):
    H, S, DQ = qh.shape
    DV = vh.shape[-1]
    bq = _pick(S, 256, 8)
    scale = float(DQ) ** -0.5
    neg = float(jnp.finfo(jnp.float32).min)

    def body(q_ref, k_ref, v_ref, o_ref, lse_ref):
        qi = pl.program_id(1)
        s = lax.dot_general(q_ref[...], k_ref[...], (((1,), (1,)), ((), ())), preferred_element_type=F32) * scale
        s = jnp.where(_attn_mask(qi, bq, S), s, neg)
        m = jnp.max(s, axis=-1, keepdims=True)
        e = jnp.exp(s - m)
        l = jnp.sum(e, axis=-1, keepdims=True)
        p = (e / l).astype(BF16)
        o_ref[...] = lax.dot_general(p, v_ref[...], (((1,), (0,)), ((), ())), preferred_element_type=F32)
        lse_ref[...] = m + jnp.log(l)

    return _pcall(
        body, name=name, grid=(H, S // bq),
        in_specs=[pl.BlockSpec((None, bq, DQ), lambda h, i: (h, i, 0)),
                  pl.BlockSpec((None, S, DQ), lambda h, i: (h, 0, 0)),
                  pl.BlockSpec((None, S, DV), lambda h, i: (h, 0, 0))],
        out_specs=[pl.BlockSpec((bq, DV), lambda h, i: (i, h)),
                   pl.BlockSpec((None, bq, 1), lambda h, i: (h, i, 0))],
        out_shape=[jax.ShapeDtypeStruct((S, H * DV), F32), jax.ShapeDtypeStruct((H, S, 1), F32)],
        compiler_params=_params(("parallel", "parallel")),
    )(qh, kh, vh)


def _attn_bwd(qh, kh, vh, o, do, lse, *, do_col0, name):
    H, S, DQ = qh.shape
    DV = vh.shape[-1]
    bq = _pick(S, 256, 8)
    nq = S // bq
    scale = float(DQ) ** -0.5

    def body(q_ref, k_ref, v_ref, o_ref, do_ref, lse_ref, dq_ref, dk_ref, dv_ref):
        qi = pl.program_id(1)

        @pl.when(qi == 0)
        def _():
            dk_ref[...] = jnp.zeros_like(dk_ref)
            dv_ref[...] = jnp.zeros_like(dv_ref)

        q, k, v = q_ref[...], k_ref[...], v_ref[...]
        dov = do_ref[...]
        s = lax.dot_general(q, k, (((1,), (1,)), ((), ())), preferred_element_type=F32) * scale
        p = jnp.where(_attn_mask(qi, bq, S), jnp.exp(s - lse_ref[...]), 0.0)
        dob = dov.astype(BF16)
        dp = lax.dot_general(dob, v, (((1,), (1,)), ((), ())), preferred_element_type=F32)
        dsum = jnp.sum(dov * o_ref[...], axis=-1, keepdims=True)
        ds = (p * (dp - dsum) * scale).astype(BF16)
        dv_ref[...] += lax.dot_general(p.astype(BF16), dob, (((0,), (0,)), ((), ())), preferred_element_type=F32)
        dk_ref[...] += lax.dot_general(ds, q, (((0,), (0,)), ((), ())), preferred_element_type=F32)
        dq_ref[...] = lax.dot_general(ds, k, (((1,), (0,)), ((), ())), preferred_element_type=F32)

    return _pcall(
        body, name=name, grid=(H, nq),
        in_specs=[pl.BlockSpec((None, bq, DQ), lambda h, i: (h, i, 0)),
                  pl.BlockSpec((None, S, DQ), lambda h, i: (h, 0, 0)),
                  pl.BlockSpec((None, S, DV), lambda h, i: (h, 0, 0)),
                  pl.BlockSpec((bq, DV), lambda h, i: (i, h)),
                  pl.BlockSpec((bq, DV), lambda h, i: (i, do_col0 + h)),
                  pl.BlockSpec((None, bq, 1), lambda h, i: (h, i, 0))],
        out_specs=[pl.BlockSpec((None, bq, DQ), lambda h, i: (h, i, 0)),
                   pl.BlockSpec((None, S, DQ), lambda h, i: (h, 0, 0)),
                   pl.BlockSpec((None, S, DV), lambda h, i: (h, 0, 0))],
        out_shape=[jax.ShapeDtypeStruct((H, S, DQ), F32), jax.ShapeDtypeStruct((H, S, DQ), F32),
                   jax.ShapeDtypeStruct((H, S, DV), F32)],
        compiler_params=_params(("parallel", "arbitrary")),
    )(qh, kh, vh, o, do, lse)


def _ret_tables():
    H, L = RET_HEADS, CHUNK
    log_gamma = jnp.log1p(-jnp.exp2(-5.0 - jnp.arange(H, dtype=F32)))
    idx = jnp.arange(L, dtype=F32)
    intra = jnp.exp(log_gamma[:, None, None] * jnp.abs(idx[:, None] - idx[None, :]))
    qd = jnp.exp(log_gamma[:, None] * (idx + 1.0))[:, :, None]
    kd = jnp.exp(log_gamma[:, None] * (L - 1.0 - idx))[:, :, None]
    cd = jnp.exp(log_gamma * L)[:, None, None]
    return intra, qd, kd, cd


def _rot(t, c, s):
    half = t.shape[-1] // 2
    t1, t2 = t[:, :half], t[:, half:]
    return jnp.concatenate([t1 * c - t2 * s, t2 * c + t1 * s], axis=-1)


def _rot_t(t, c, s):
    half = t.shape[-1] // 2
    t1, t2 = t[:, :half], t[:, half:]
    return jnp.concatenate([t1 * c + t2 * s, t2 * c - t1 * s], axis=-1)


def _dot(a, b, ca, cb):
    return lax.dot_general(a.astype(BF16), b.astype(BF16), (((ca,), (cb,)), ((), ())), preferred_element_type=F32)


def _ret_fwd(hR, cos, sin, tables, *, name):
    S = hR.shape[0]
    H, L, DK, DV = RET_HEADS, CHUNK, RET_QK, RET_V
    NC = S // L
    qscale = float(DK) ** -0.5
    intra, qd, kd, cd = tables

    def body(q_ref, k_ref, v_ref, c_ref, s_ref, in_ref, qd_ref, kd_ref, cd_ref, o_ref, st_ref, state):
        @pl.when(pl.program_id(1) == 0)
        def _():
            state[...] = jnp.zeros_like(state)

        c, s = c_ref[...], s_ref[...]
        q = _rot(q_ref[...], c, s) * qscale
        k = _rot(k_ref[...], c, s)
        v = v_ref[...]
        st = state[...]
        st_ref[...] = st.astype(BF16)
        scores = _dot(q, k, 1, 1) * in_ref[...]
        o_ref[...] = _dot(scores, v, 1, 0) + _dot(q * qd_ref[...], st, 1, 0)
        state[...] = st * cd_ref[...] + _dot(k * kd_ref[...], v, 0, 0)

    blk = lambda off: pl.BlockSpec((L, DK), lambda h, c: (c, off + h))
    rope = pl.BlockSpec((L, DK // 2), lambda h, c: (c, 0))
    return _pcall(
        body, name=name, grid=(H, NC),
        in_specs=[blk(0), blk(H), blk(2 * H), rope, rope,
                  pl.BlockSpec((None, L, L), lambda h, c: (h, 0, 0)),
                  pl.BlockSpec((None, L, 1), lambda h, c: (h, 0, 0)),
                  pl.BlockSpec((None, L, 1), lambda h, c: (h, 0, 0)),
                  pl.BlockSpec((None, 1, 1), lambda h, c: (h, 0, 0))],
        out_specs=[pl.BlockSpec((L, DV), lambda h, c: (c, h)),
                   pl.BlockSpec((None, None, DK, DV), lambda h, c: (h, c, 0, 0))],
        out_shape=[jax.ShapeDtypeStruct((S, H * DV), F32), jax.ShapeDtypeStruct((H, NC, DK, DV), BF16)],
        scratch_shapes=[pltpu.VMEM((DK, DV), F32)],
        compiler_params=_params(("arbitrary", "arbitrary")),
    )(hR, hR, hR, cos, sin, intra, qd, kd, cd)


def _ret_bwd(do, hR, states, cos, sin, tables, *, name):
    S = hR.shape[0]
    H, L, DK, DV = RET_HEADS, CHUNK, RET_QK, RET_V
    NC = S // L
    qscale = float(DK) ** -0.5
    intra, qd, kd, cd = tables

    def body(do_ref, q_ref, k_ref, v_ref, st_ref, c_ref, s_ref, in_ref, qd_ref, kd_ref, cd_ref,
             dq_ref, dk_ref, dv_ref, dstate):
        @pl.when(pl.program_id(1) == 0)
        def _():
            dstate[...] = jnp.zeros_like(dstate)

        c, s = c_ref[...], s_ref[...]
        q = _rot(q_ref[...], c, s) * qscale
        k = _rot(k_ref[...], c, s)
        v = v_ref[...]
        dov = do_ref[...]
        dst = dstate[...]
        dec = in_ref[...]
        qdv, kdv = qd_ref[...], kd_ref[...]
        scores = _dot(q, k, 1, 1) * dec
        da = _dot(dov, v, 1, 1) * dec
        dv_ref[...] = _dot(scores, dov, 0, 0) + _dot(k * kdv, dst, 1, 0)
        dq = _dot(da, k, 1, 0) + _dot(dov, st_ref[...], 1, 1) * qdv
        dk = _dot(da, q, 0, 0) + _dot(v, dst, 1, 1) * kdv
        dq_ref[...] = _rot_t(dq * qscale, c, s)
        dk_ref[...] = _rot_t(dk, c, s)
        dstate[...] = dst * cd_ref[...] + _dot(q * qdv, dov, 0, 0)

    rev = lambda c: NC - 1 - c
    blk = lambda off: pl.BlockSpec((L, DK), lambda h, c: (rev(c), off + h))
    rope = pl.BlockSpec((L, DK // 2), lambda h, c: (rev(c), 0))
    out = pl.BlockSpec((L, DK), lambda h, c: (rev(c), h))
    return _pcall(
        body, name=name, grid=(H, NC),
        in_specs=[out, blk(0), blk(H), blk(2 * H),
                  pl.BlockSpec((None, None, DK, DV), lambda h, c: (h, rev(c), 0, 0)), rope, rope,
                  pl.BlockSpec((None, L, L), lambda h, c: (h, 0, 0)),
                  pl.BlockSpec((None, L, 1), lambda h, c: (h, 0, 0)),
                  pl.BlockSpec((None, L, 1), lambda h, c: (h, 0, 0)),
                  pl.BlockSpec((None, 1, 1), lambda h, c: (h, 0, 0))],
        out_specs=[out, out, out],
        out_shape=[jax.ShapeDtypeStruct((S, H * DK), F32)] * 3,
        scratch_shapes=[pltpu.VMEM((DK, DV), F32)],
        compiler_params=_params(("arbitrary", "arbitrary")),
    )(do, hR, hR, hR, states, cos, sin, intra, qd, kd, cd)


def _gn_gate_fwd(o, hR, g, b, *, name):
    S = o.shape[0]
    H, DV = RET_HEADS, RET_V
    bs = _pick(S, 512, 8)

    def body(o_ref, rg_ref, g_ref, b_ref, r_ref):
        z = o_ref[...]
        z = z - jnp.mean(z, axis=-1, keepdims=True)
        xh = z * lax.rsqrt(jnp.mean(z * z, axis=-1, keepdims=True) + GN_EPS)
        rg = rg_ref[...]
        r_ref[...] = (rg * jax.nn.sigmoid(rg)) * (xh * g_ref[...] + b_ref[...])

    row = pl.BlockSpec((bs, DV), lambda i, h: (i, h))
    vec = pl.BlockSpec((1, DV), lambda i, h: (0, h))
    return _pcall(
        body, name=name, grid=(S // bs, H),
        in_specs=[row, pl.BlockSpec((bs, DV), lambda i, h: (i, 3 * H + h)), vec, vec],
        out_specs=row, out_shape=jax.ShapeDtypeStruct((S, H * DV), F32),
        compiler_params=_params(("parallel", "parallel")),
    )(o, hR, g.reshape(1, H * DV), b.reshape(1, H * DV))


def _gn_gate_bwd(dr, o, hR, g, b, *, dr_col0, name):
    S = o.shape[0]
    H, DV = RET_HEADS, RET_V
    bs = _pick(S, 512, 8)

    def body(dr_ref, o_ref, rg_ref, g_ref, b_ref, do_ref, drg_ref, dg_ref, db_ref):
        @pl.when(pl.program_id(1) == 0)
        def _():
            dg_ref[...] = jnp.zeros_like(dg_ref)
            db_ref[...] = jnp.zeros_like(db_ref)

        z = o_ref[...]
        z = z - jnp.mean(z, axis=-1, keepdims=True)
        rstd = lax.rsqrt(jnp.mean(z * z, axis=-1, keepdims=True) + GN_EPS)
        xh = z * rstd
        gv = g_ref[...]
        y = xh * gv + b_ref[...]
        rg = rg_ref[...]
        sg = jax.nn.sigmoid(rg)
        drv = dr_ref[...]
        dy = drv * (rg * sg)
        drg_ref[...] = drv * y * (sg * (1.0 + rg * (1.0 - sg)))
        dg_ref[...] += jnp.sum(dy * xh, axis=0, keepdims=True)
        db_ref[...] += jnp.sum(dy, axis=0, keepdims=True)
        dxh = dy * gv
        do_ref[...] = rstd * (dxh - jnp.mean(dxh, axis=-1, keepdims=True)
                              - xh * jnp.mean(dxh * xh, axis=-1, keepdims=True))

    row = pl.BlockSpec((bs, DV), lambda h, i: (i, h))
    vec = pl.BlockSpec((1, DV), lambda h, i: (0, h))
    do, drg, dg, db = _pcall(
        body, name=name, grid=(H, S // bs),
        in_specs=[pl.BlockSpec((bs, DV), lambda h, i: (i, dr_col0 + h)), row,
                  pl.BlockSpec((bs, DV), lambda h, i: (i, 3 * H + h)), vec, vec],
        out_specs=[row, row, vec, vec],
        out_shape=[jax.ShapeDtypeStruct((S, H * DV), F32), jax.ShapeDtypeStruct((S, H * DV), F32),
                   jax.ShapeDtypeStruct((1, H * DV), F32), jax.ShapeDtypeStruct((1, H * DV), F32)],
        compiler_params=_params(("arbitrary", "arbitrary")),
    )(dr, o, hR, g.reshape(1, H * DV), b.reshape(1, H * DV))
    return do, drg, dg[0], db[0]


def _swiglu_fwd(g, u, *, name):
    S, F = g.shape
    bs, bf = _pick(S, 512, 8), _pick(F, 1408, 128)

    def body(g_ref, u_ref, a_ref):
        gv = g_ref[...]
        a_ref[...] = (gv * jax.nn.sigmoid(gv)) * u_ref[...]

    blk = pl.BlockSpec((bs, bf), lambda i, j: (i, j))
    return _pcall(body, name=name, grid=(S // bs, F // bf), in_specs=[blk, blk], out_specs=blk,
                  out_shape=jax.ShapeDtypeStruct((S, F), F32), compiler_params=_params(("parallel", "parallel")))(g, u)


def _swiglu_bwd(da, g, u, *, name):
    S, F = g.shape
    bs, bf = _pick(S, 512, 8), _pick(F, 1408, 128)

    def body(da_ref, g_ref, u_ref, dg_ref, du_ref):
        gv, dav = g_ref[...], da_ref[...]
        sg = jax.nn.sigmoid(gv)
        du_ref[...] = dav * (gv * sg)
        dg_ref[...] = dav * u_ref[...] * (sg * (1.0 + gv * (1.0 - sg)))

    blk = pl.BlockSpec((bs, bf), lambda i, j: (i, j))
    return _pcall(body, name=name, grid=(S // bs, F // bf), in_specs=[blk] * 3, out_specs=[blk, blk],
                  out_shape=[jax.ShapeDtypeStruct((S, F), F32)] * 2,
                  compiler_params=_params(("parallel", "parallel")))(da, g, u)


def _loss_head(y, t, *, name):
    S, D = y.shape
    bs = _pick(S, 256, 8)
    inv_d = 1.0 / D

    def body(y_ref, t_ref, dy_ref, l_ref):
        @pl.when(pl.program_id(0) == 0)
        def _():
            l_ref[...] = jnp.zeros_like(l_ref)

        e = y_ref[...] - t_ref[...]
        dy_ref[...] = e * inv_d
        l_ref[...] += 0.5 * jnp.sum(jnp.mean(e * e, axis=-1, keepdims=True), axis=0, keepdims=True)

    row = pl.BlockSpec((bs, D), lambda i: (i, 0))
    dy, l = _pcall(
        body, name=name, grid=(S // bs,), in_specs=[row, row],
        out_specs=[row, pl.BlockSpec((1, 1), lambda i: (0, 0))],
        out_shape=[jax.ShapeDtypeStruct((S, D), F32), jax.ShapeDtypeStruct((1, 1), F32)],
        compiler_params=_params(("arbitrary",)),
    )(y, t)
    return dy, l[0, 0]


def _adamw(w, g, m, v, *, name):
    shape = w.shape
    C = shape[-1]
    R = w.size // C
    br = _pick(R, 512, 8)

    def body(w_ref, g_ref, m_ref, v_ref, d_ref, nm_ref, nv_ref):
        gv = g_ref[...]
        mn = ADAM_B1 * m_ref[...] + (1.0 - ADAM_B1) * gv
        vn = ADAM_B2 * v_ref[...] + (1.0 - ADAM_B2) * (gv * gv)
        m_hat = mn / (1.0 - ADAM_B1 ** ADAM_STEP)
        v_hat = vn / (1.0 - ADAM_B2 ** ADAM_STEP)
        d_ref[...] = -ADAM_LR * (m_hat / (jnp.sqrt(v_hat) + ADAM_EPS) + ADAM_WD * w_ref[...])
        nm_ref[...] = mn
        nv_ref[...] = vn

    blk = pl.BlockSpec((br, C), lambda i: (i, 0))
    outs = _pcall(body, name=name, grid=(R // br,), in_specs=[blk] * 4, out_specs=[blk] * 3,
                  out_shape=[jax.ShapeDtypeStruct((R, C), F32)] * 3,
                  compiler_params=_params(("parallel",)))(*[a.reshape(R, C) for a in (w, g, m, v)])
    return tuple(o.reshape(shape) for o in outs)


def _sum_slots(x, *, name):
    _, R, C = x.shape
    br = _pick(R, 256, 16)

    def body(x_ref, o_ref):
        acc = x_ref[0].astype(F32)
        for s in range(1, N_DEV):
            acc = acc + x_ref[s].astype(F32)
        o_ref[...] = acc

    return _pcall(body, name=name, grid=(R // br,),
                  in_specs=[pl.BlockSpec((N_DEV, br, C), lambda i: (0, i, 0))],
                  out_specs=pl.BlockSpec((br, C), lambda i: (i, 0)),
                  out_shape=jax.ShapeDtypeStruct((R, C), F32), compiler_params=_params(("parallel",)))(x)


def _coords():
    return lax.axis_index("x"), lax.axis_index("y"), lax.axis_index("c")


def _all_gather(xl, *, name):
    R, C = xl.shape

    def body(x_ref, out_ref, send_sems, recv_sems, local_sem):
        x, y, c = _coords()
        me, sibling = (x, y, c), (x, y, 1 - c)
        chips = [(1 - x, y), (x, 1 - y), (1 - x, 1 - y)]

        def slot(px, py, pc):
            return out_ref.at[4 * px + 2 * py + pc]

        def copy(k, block, to, src=None):
            return pltpu.make_async_remote_copy(
                src_ref=slot(*block) if src is None else src, dst_ref=slot(*block),
                send_sem=send_sems.at[k], recv_sem=recv_sems.at[k],
                device_id=to, device_id_type=pl.DeviceIdType.MESH)

        mine = pltpu.make_async_copy(x_ref, slot(*me), local_sem)
        mine.start()
        first = [copy(0, me, sibling, src=x_ref)]
        first += [copy(1 + j, me, (*chip, c), src=x_ref) for j, chip in enumerate(chips)]
        for cp in first:
            cp.start()
        passed = [copy(4 + j, (*chip, c), sibling) for j, chip in enumerate(chips)]
        for j, chip in enumerate(chips):
            copy(1 + j, (*chip, c), me).wait_recv()
            passed[j].start()
        copy(0, sibling, me).wait_recv()
        for j, chip in enumerate(chips):
            copy(4 + j, (*chip, 1 - c), me).wait_recv()
        for cp in first + passed:
            cp.wait_send()
        mine.wait()

    return _pcall(
        body, name=name, out_shape=jax.ShapeDtypeStruct((N_DEV, R, C), xl.dtype),
        in_specs=[pl.BlockSpec(memory_space=pl.ANY)], out_specs=pl.BlockSpec(memory_space=pl.ANY),
        scratch_shapes=[pltpu.SemaphoreType.DMA((7,)), pltpu.SemaphoreType.DMA((7,)), pltpu.SemaphoreType.DMA(())],
    )(xl)


def _all_to_all(g, *, name):
    _, R, C = g.shape

    def body(g_ref, out_ref, send_sems, recv_sems, local_sem):
        x, y, c = _coords()
        me = 4 * x + 2 * y + c
        mine = pltpu.make_async_copy(g_ref.at[me], out_ref.at[me], local_sem)
        mine.start()
        copies = []
        for k in range(1, N_DEV):
            px = 1 - x if k & 4 else x
            py = 1 - y if k & 2 else y
            pc = 1 - c if k & 1 else c
            peer = 4 * px + 2 * py + pc
            send = pltpu.make_async_remote_copy(
                src_ref=g_ref.at[peer], dst_ref=out_ref.at[me], send_sem=send_sems.at[k - 1],
                recv_sem=recv_sems.at[k - 1], device_id=(px, py, pc), device_id_type=pl.DeviceIdType.MESH)
            recv = pltpu.make_async_remote_copy(
                src_ref=g_ref.at[peer], dst_ref=out_ref.at[peer], send_sem=send_sems.at[k - 1],
                recv_sem=recv_sems.at[k - 1], device_id=(px, py, pc), device_id_type=pl.DeviceIdType.MESH)
            send.start()
            copies.append((send, recv))
        for send, recv in copies:
            recv.wait_recv()
        for send, recv in copies:
            send.wait_send()
        mine.wait()

    return _pcall(
        body, name=name, out_shape=jax.ShapeDtypeStruct(g.shape, g.dtype),
        in_specs=[pl.BlockSpec(memory_space=pl.ANY)], out_specs=pl.BlockSpec(memory_space=pl.ANY),
        scratch_shapes=[pltpu.SemaphoreType.DMA((7,)), pltpu.SemaphoreType.DMA((7,)), pltpu.SemaphoreType.DMA(())],
    )(g)


_PARTS = (("w_in", True), ("w_out", False), ("w_gate", True), ("w_up", True), ("w_down", False),
          ("w_uq", True), ("w_ukv", True))


def _pack_layout(shards, D):
    layout = []
    for name, transposed in _PARTS:
        _, a, b = shards[name].shape
        n, k = (b, a) if transposed else (a, b)
        rows = n * k // D
        assert rows * D == n * k
        layout.append((name, transposed, n, k, rows, -(-rows // PACK_ROW_ALIGN) * PACK_ROW_ALIGN))
    return layout


def _pack(shards, layout, D, depth, dtype):
    parts = []
    for l in range(depth):
        for name, transposed, n, k, rows, prows in layout:
            w = shards[name][l]
            w = (w.T if transposed else w).reshape(rows, D)
            parts.append(jnp.pad(w, ((0, prows - rows), (0, 0))).astype(dtype))
    return jnp.concatenate(parts, axis=0)


def _unpack_full(gathered, layout, depth):
    out, off = [], 0
    for l in range(depth):
        ws = {}
        for name, transposed, n, k, rows, prows in layout:
            ws[name] = gathered[:, off:off + rows, :].reshape(N_DEV * n, k)
            off += prows
        out.append(ws)
    return out


def _pack_full(grads, layout, D, depth, dtype):
    parts = []
    for l in range(depth):
        for name, transposed, n, k, rows, prows in layout:
            gfull = grads[l][name].reshape(N_DEV, rows, D)
            parts.append(jnp.pad(gfull, ((0, 0), (0, prows - rows), (0, 0))).astype(dtype))
    return jnp.concatenate(parts, axis=1)


def _unpack_shard(packed, layout, depth):
    out = {name: [] for name, *_ in layout}
    off = 0
    for l in range(depth):
        for name, transposed, n, k, rows, prows in layout:
            w = packed[off:off + rows, :].reshape(n, k)
            out[name].append(w.T if transposed else w)
            off += prows
    return {name: jnp.stack(v) for name, v in out.items()}


def _rope_tables(pos, dim):
    inv_freq = ROPE_THETA ** (-jnp.arange(0, dim, 2, dtype=F32) / dim)
    ang = pos.astype(F32)[:, None] * inv_freq
    return jnp.cos(ang), jnp.sin(ang)


def _split_in(wt_in):
    a, b, c = MLA_Q_LORA, MLA_Q_LORA + MLA_KV_LORA, MLA_Q_LORA + MLA_KV_LORA + MLA_ROPE
    return wt_in[:a], wt_in[a:b], wt_in[b:c], wt_in[c:]


def _mla_pack_rope(q, kr):
    S = q.shape[0]
    H, hr = MLA_HEADS, MLA_ROPE // 2
    q3 = q.reshape(S, H, MLA_QK)
    t1 = jnp.concatenate([q3[:, :, MLA_NOPE:MLA_NOPE + hr].reshape(S, H * hr), kr[:, :hr]], axis=1)
    t2 = jnp.concatenate([q3[:, :, MLA_NOPE + hr:].reshape(S, H * hr), kr[:, hr:]], axis=1)
    return q3[:, :, :MLA_NOPE], t1, t2


def _mla_unpack_rope(q_nope, o1, o2):
    S = o1.shape[0]
    H, hr = MLA_HEADS, MLA_ROPE // 2
    q3 = jnp.concatenate([q_nope, o1[:, :H * hr].reshape(S, H, hr), o2[:, :H * hr].reshape(S, H, hr)], axis=-1)
    kr = jnp.concatenate([o1[:, H * hr:], o2[:, H * hr:]], axis=1)
    return q3, kr


def _layer_fwd(x, w, p, tabs, l):
    S, D = x.shape
    H = MLA_HEADS
    nm = lambda s: "l%d_%s" % (l, s)
    wt_q, wt_kv, wt_kr, wt_r = _split_in(w["w_in"])
    cq = _mm(x, wt_q, tb=True, name=nm("cq"))
    ckv = _mm(x, wt_kv, tb=True, name=nm("ckv"))
    kr = _mm(x, wt_kr, tb=True, name=nm("krope"))
    hR = _mm(x, wt_r, tb=True, name=nm("hR"))
    qn, qn_hat, q_rstd = _norm_fwd(cq, p["q_norm_g"], None, center=False, eps=RMS_EPS, name=nm("qnorm"))
    kvn, kvn_hat, kv_rstd = _norm_fwd(ckv, p["kv_norm_g"], None, center=False, eps=RMS_EPS, name=nm("kvnorm"))
    q = _mm(qn, w["w_uq"], tb=True, name=nm("uq"))
    kv = _mm(kvn, w["w_ukv"], tb=True, name=nm("ukv"))
    q_nope, t1, t2 = _mla_pack_rope(q, kr)
    o1, o2 = _rope(t1, t2, tabs["cos_m"], tabs["sin_m"], name=nm("rope"))
    q3, k_r = _mla_unpack_rope(q_nope, o1, o2)
    kv3 = kv.reshape(S, H, MLA_NOPE + MLA_V)
    qh = q3.transpose(1, 0, 2).astype(BF16)
    kh = jnp.concatenate([kv3[:, :, :MLA_NOPE], jnp.broadcast_to(k_r[:, None, :], (S, H, MLA_ROPE))],
                         axis=-1).transpose(1, 0, 2).astype(BF16)
    vh = kv3[:, :, MLA_NOPE:].transpose(1, 0, 2).astype(BF16)
    a, lse = _attn_fwd(qh, kh, vh, name=nm("attn"))
    o_ret, states = _ret_fwd(hR, tabs["cos_r"], tabs["sin_r"], tabs["ret"], name=nm("ret"))
    r = _gn_gate_fwd(o_ret, hR, p["ret_gn_g"], p["ret_gn_b"], name=nm("gn"))
    mix_in = jnp.concatenate([a, r], axis=1)
    mix = _mm(mix_in, w["w_out"], name=nm("out"))
    x1, x1_hat, rstd1 = _norm_fwd(x, p["ln1_g"], p["ln1_b"], res=mix, alpha=p["alpha"], eps=LN_EPS, name=nm("ln1"))
    g = _mm(x1, w["w_gate"], tb=True, name=nm("gate"))
    u = _mm(x1, w["w_up"], tb=True, name=nm("up"))
    act = _swiglu_fwd(g, u, name=nm("swiglu"))
    f = _mm(act, w["w_down"], name=nm("down"))
    x2, x2_hat, rstd2 = _norm_fwd(x1, p["ln2_g"], p["ln2_b"], res=f, alpha=p["alpha"], eps=LN_EPS, name=nm("ln2"))
    saved = dict(x=x, qn=qn, qn_hat=qn_hat, q_rstd=q_rstd, kvn=kvn, kvn_hat=kvn_hat, kv_rstd=kv_rstd,
                 qh=qh, kh=kh, vh=vh, a=a, lse=lse, hR=hR, o_ret=o_ret, states=states, mix_in=mix_in,
                 x1=x1, x1_hat=x1_hat, rstd1=rstd1, g=g, u=u, act=act, x2_hat=x2_hat, rstd2=rstd2)
    return x2, saved


def _layer_bwd(dx2, sv, w, p, tabs, l):
    S, D = dx2.shape
    H = MLA_HEADS
    nm = lambda s: "l%d_b_%s" % (l, s)
    alpha = p["alpha"]
    gw, gp = {}, {}
    dz2, gp["ln2_g"], gp["ln2_b"] = _norm_bwd(dx2, sv["x2_hat"], sv["rstd2"], p["ln2_g"], name=nm("ln2"))
    dact = _mm(dz2, w["w_down"], tb=True, name=nm("dact"))
    gw["w_down"] = _mm(sv["act"], dz2, ta=True, name=nm("wdown"))
    dg, du = _swiglu_bwd(dact, sv["g"], sv["u"], name=nm("swiglu"))
    gw["w_gate"] = _mm(dg, sv["x1"], ta=True, name=nm("wgate"))
    gw["w_up"] = _mm(du, sv["x1"], ta=True, name=nm("wup"))
    t = _mm(dg, w["w_gate"], add=dz2, add_scale=alpha, name=nm("dx1a"))
    dx1 = _mm(du, w["w_up"], add=t, name=nm("dx1b"))
    dz1, gp["ln1_g"], gp["ln1_b"] = _norm_bwd(dx1, sv["x1_hat"], sv["rstd1"], p["ln1_g"], name=nm("ln1"))
    dmix = _mm(dz1, w["w_out"], tb=True, name=nm("dmix"))
    gw["w_out"] = _mm(sv["mix_in"], dz1, ta=True, name=nm("wout"))
    ret_col0 = (H * MLA_V) // RET_V
    do_ret, drg, gp["ret_gn_g"], gp["ret_gn_b"] = _gn_gate_bwd(
        dmix, sv["o_ret"], sv["hR"], p["ret_gn_g"], p["ret_gn_b"], dr_col0=ret_col0, name=nm("gn"))
    drq, drk, drv = _ret_bwd(do_ret, sv["hR"], sv["states"], tabs["cos_r"], tabs["sin_r"], tabs["ret"], name=nm("ret"))
    dhR = jnp.concatenate([drq, drk, drv, drg], axis=1)
    dqh, dkh, dvh = _attn_bwd(sv["qh"], sv["kh"], sv["vh"], sv["a"], dmix, sv["lse"], do_col0=0, name=nm("attn"))
    dq3 = dqh.transpose(1, 0, 2)
    dk_r = _sum_heads(dkh, name=nm("dkr"))
    dq_nope, dt1, dt2 = _mla_pack_rope(dq3.reshape(S, H * MLA_QK), dk_r)
    di1, di2 = _rope(dt1, dt2, tabs["cos_m"], tabs["nsin_m"], name=nm("rope"))
    dq3, dkr = _mla_unpack_rope(dq_nope, di1, di2)
    dq = dq3.reshape(S, H * MLA_QK)
    dkv = jnp.concatenate([dkh[:, :, :MLA_NOPE].transpose(1, 0, 2), dvh.transpose(1, 0, 2)], axis=-1).reshape(S, -1)
    gw["w_uq"] = _mm(dq, sv["qn"], ta=True, name=nm("wuq"))
    dqn = _mm(dq, w["w_uq"], name=nm("dqn"))
    gw["w_ukv"] = _mm(dkv, sv["kvn"], ta=True, name=nm("wukv"))
    dkvn = _mm(dkv, w["w_ukv"], name=nm("dkvn"))
    dcq, gp["q_norm_g"], _ = _norm_bwd(dqn, sv["qn_hat"], sv["q_rstd"], p["q_norm_g"], center=False, name=nm("qnorm"))
    dckv, gp["kv_norm_g"], _ = _norm_bwd(dkvn, sv["kvn_hat"], sv["kv_rstd"], p["kv_norm_g"], center=False,
                                         name=nm("kvnorm"))
    x = sv["x"]
    gw["w_in"] = jnp.concatenate([
        _mm(dcq, x, ta=True, name=nm("win_q")), _mm(dckv, x, ta=True, name=nm("win_kv")),
        _mm(dkr, x, ta=True, name=nm("win_kr")), _mm(dhR, x, ta=True, name=nm("win_r"))], axis=0)
    wt_q, wt_kv, wt_kr, wt_r = _split_in(w["w_in"])
    t = _mm(dcq, wt_q, add=dz1, add_scale=alpha, name=nm("dx_q"))
    t = _mm(dckv, wt_kv, add=t, name=nm("dx_kv"))
    t = _mm(dkr, wt_kr, add=t, name=nm("dx_kr"))
    dx = _mm(dhR, wt_r, add=t, name=nm("dx_r"))
    return dx, gw, gp


def _sum_heads(dkh, *, name):
    H, S, DQ = dkh.shape
    bs = _pick(S, 512, 8)

    def body(d_ref, o_ref):
        acc = d_ref[0][:, MLA_NOPE:]
        for h in range(1, H):
            acc = acc + d_ref[h][:, MLA_NOPE:]
        o_ref[...] = acc

    return _pcall(body, name=name, grid=(S // bs,),
                  in_specs=[pl.BlockSpec((H, bs, DQ), lambda i: (0, i, 0))],
                  out_specs=pl.BlockSpec((bs, MLA_ROPE), lambda i: (i, 0)),
                  out_shape=jax.ShapeDtypeStruct((S, MLA_ROPE), F32), compiler_params=_params(("parallel",)))(dkh)


_SMALL = ("ln_in_g", "ln_in_b", "q_norm_g", "kv_norm_g", "ret_gn_g", "ret_gn_b", "ln1_g", "ln1_b", "ln2_g", "ln2_b")
SMALL_COLS = 128


def _local_step(x, target, pos, small, weights):
    depth = len(weights)
    alpha = (2 * depth) ** 0.25
    cos_m, sin_m = _rope_tables(pos, MLA_ROPE)
    cos_r, sin_r = _rope_tables(pos, RET_QK)
    reps = MLA_HEADS + 1
    tabs = dict(cos_m=jnp.tile(cos_m, (1, reps)), sin_m=jnp.tile(sin_m, (1, reps)),
                nsin_m=jnp.tile(-sin_m, (1, reps)), cos_r=cos_r, sin_r=sin_r, ret=_ret_tables())
    h, h_hat, h_rstd = _norm_fwd(x, small["ln_in_g"], small["ln_in_b"], eps=LN_EPS, name="ln_in")
    saved, ps = [], []
    for l in range(depth):
        p = {k: small[k][l] for k in _SMALL[2:]}
        p["alpha"] = alpha
        h, sv = _layer_fwd(h, weights[l], p, tabs, l)
        saved.append(sv)
        ps.append(p)
    dy, loss = _loss_head(h, target, name="loss")
    gws, gps = [None] * depth, [None] * depth
    for l in reversed(range(depth)):
        dy, gws[l], gps[l] = _layer_bwd(dy, saved[l], weights[l], ps[l], tabs, l)
    grad_x, g_in_g, g_in_b = _norm_bwd(dy, h_hat, h_rstd, small["ln_in_g"], name="b_ln_in")
    gsmall = {"ln_in_g": g_in_g, "ln_in_b": g_in_b}
    for k in _SMALL[2:]:
        gsmall[k] = jnp.stack([gps[l][k] for l in range(depth)])
    return loss, grad_x, gws, gsmall


def kernel(x, positions, ln_in_g, ln_in_b, w_in, q_norm_g, kv_norm_g, w_uq, w_ukv, ret_gn_g, ret_gn_b, w_out, ln1_g, ln1_b, w_gate, w_up, w_down, ln2_g, ln2_b, loss_target, m_ln_in_g, m_ln_in_b, m_w_in, m_q_norm_g, m_kv_norm_g, m_w_uq, m_w_ukv, m_ret_gn_g, m_ret_gn_b, m_w_out, m_ln1_g, m_ln1_b, m_w_gate, m_w_up, m_w_down, m_ln2_g, m_ln2_b, v_ln_in_g, v_ln_in_b, v_w_in, v_q_norm_g, v_kv_norm_g, v_w_uq, v_w_ukv, v_ret_gn_g, v_ret_gn_b, v_w_out, v_ln1_g, v_ln1_b, v_w_gate, v_w_up, v_w_down, v_ln2_g, v_ln2_b):
    names = ["ln_in_g", "ln_in_b", "w_in", "q_norm_g", "kv_norm_g", "w_uq", "w_ukv", "ret_gn_g", "ret_gn_b", "w_out",
             "ln1_g", "ln1_b", "w_gate", "w_up", "w_down", "ln2_g", "ln2_b"]
    wv = dict(zip(names, (ln_in_g, ln_in_b, w_in, q_norm_g, kv_norm_g, w_uq, w_ukv, ret_gn_g, ret_gn_b, w_out,
                          ln1_g, ln1_b, w_gate, w_up, w_down, ln2_g, ln2_b)))
    mv = dict(zip(names, (m_ln_in_g, m_ln_in_b, m_w_in, m_q_norm_g, m_kv_norm_g, m_w_uq, m_w_ukv, m_ret_gn_g,
                          m_ret_gn_b, m_w_out, m_ln1_g, m_ln1_b, m_w_gate, m_w_up, m_w_down, m_ln2_g, m_ln2_b)))
    vv = dict(zip(names, (v_ln_in_g, v_ln_in_b, v_w_in, v_q_norm_g, v_kv_norm_g, v_w_uq, v_w_ukv, v_ret_gn_g,
                          v_ret_gn_b, v_w_out, v_ln1_g, v_ln1_b, v_w_gate, v_w_up, v_w_down, v_ln2_g, v_ln2_b)))
    S, D = x.shape[1], x.shape[2]
    depth = w_in.shape[0]
    big = [n for n, _ in _PARTS]
    shards = {n: wv[n] for n in big}
    layout = _pack_layout(shards, D)

    gathered = _all_gather(_pack(shards, layout, D, depth, BF16), name="ag_weights")
    weights = _unpack_full(gathered, layout, depth)

    small = {n: wv[n] for n in _SMALL}
    loss, grad_x, gws, gsmall = _local_step(x[0], loss_target[0], positions[0], small, weights)
    loss = lax.psum(loss, MESH_AXES)

    recv = _all_to_all(_pack_full(gws, layout, D, depth, BF16), name="a2a_grads")
    gshard = _unpack_shard(_sum_slots(recv, name="sum_grads"), layout, depth)

    flat = jnp.concatenate([gsmall[n].reshape(-1) for n in _SMALL])
    n_small = flat.shape[0]
    rows = -(-n_small // (SMALL_COLS * 8)) * 8
    flat = jnp.pad(flat, (0, rows * SMALL_COLS - n_small)).reshape(rows, SMALL_COLS)
    tot = _sum_slots(_all_gather(flat, name="ag_small"), name="sum_small").reshape(-1)
    grads, off = dict(gshard), 0
    for n in _SMALL:
        grads[n] = tot[off:off + wv[n].size].reshape(wv[n].shape)
        off += wv[n].size

    delta, new_m, new_v = {}, {}, {}
    for n in names:
        w2 = wv[n] if wv[n].ndim > 1 else wv[n].reshape(1, -1)
        d, nm_, nv_ = _adamw(w2, grads[n].reshape(w2.shape), mv[n].reshape(w2.shape), vv[n].reshape(w2.shape),
                             name="adamw_" + n)
        delta[n], new_m[n], new_v[n] = d.reshape(wv[n].shape), nm_.reshape(wv[n].shape), nv_.reshape(wv[n].shape)

    return (loss, grad_x[None], *[grads[n] for n in names], *[delta[n] for n in names],
            *[new_m[n] for n in names], *[new_v[n] for n in names])
```

```python
import jax
import jax.numpy as jnp
from jax import lax
from jax.experimental import pallas as pl
from jax.experimental.pallas import tpu as pltpu

F32 = jnp.float32
BF16 = jnp.bfloat16

CHUNK = 64
CHUNK_SHIFT = 6
MLA_HEADS = 8
MLA_Q_LORA = 512
MLA_KV_LORA = 256
MLA_NOPE = 128
MLA_ROPE = 64
MLA_V = 128
MLA_QK = MLA_NOPE + MLA_ROPE
RET_HEADS = 4
RET_QK = 256
RET_V = 256
ROPE_THETA = 10000.0
LN_EPS = 1e-5
RMS_EPS = 1e-6
GN_EPS = 1e-5
ADAM_LR = 0.001
ADAM_B1 = 0.9
ADAM_B2 = 0.999
ADAM_EPS = 1e-08
ADAM_WD = 0.01
ADAM_STEP = 10

N_DEV = 8
MESH_AXES = ("x", "y", "c")
VMEM_LIMIT_BYTES = 56 * 1024 * 1024
MM_VMEM_BUDGET = 36 * 1024 * 1024
ATTN_BLOCK = 256


def _pick(n, pref, mult):
    best = None
    d = mult
    while d <= min(n, pref):
        if n % d == 0:
            best = d
        d += mult
    return n if best is None else best


def _divisors(n, mult, cap):
    ds = [d for d in range(mult, min(n, cap) + 1, mult) if n % d == 0]
    return ds or [n]


def _pcall(body, **kw):
    return pl.pallas_call(body, **kw)


def _params(sem):
    return pltpu.CompilerParams(dimension_semantics=sem, vmem_limit_bytes=VMEM_LIMIT_BYTES)


def _sds(shape, dtype):
    return jax.ShapeDtypeStruct(shape, dtype)


def _mm_tiles(M, N, K, ta, sa, sb, so, has_add):
    bk = _pick(K, 512 if ta else 2048, 128)
    nk = K // bk
    best = None
    for bm in _divisors(M, 128 if ta else 16, 1024):
        for bn in _divisors(N, 128, 1024):
            vmem = 2 * (bm * bk * sa + bk * bn * sb) + 2 * bm * bn * so
            vmem += (2 * bm * bn * 4 if has_add else 0) + (bm * bn * 4 if nk > 1 else 0)
            if vmem > MM_VMEM_BUDGET:
                continue
            score = (bm * bn / (bm + bn), bm)
            if best is None or score > best[0]:
                best = (score, bm, bn)
    assert best is not None, (M, N, K)
    return best[1], best[2], bk


def _mm(a, b, *, ta=False, tb=False, add=None, add_scale=1.0, out_dtype=F32, name):
    if ta:
        K, M = a.shape
    else:
        M, K = a.shape
    if tb:
        N, K2 = b.shape
    else:
        K2, N = b.shape
    assert K == K2, (a.shape, b.shape, ta, tb)
    has_add = add is not None
    bm, bn, bk = _mm_tiles(M, N, K, ta, a.dtype.itemsize, b.dtype.itemsize, jnp.dtype(out_dtype).itemsize, has_add)
    nk = K // bk
    a_spec = (pl.BlockSpec((bk, bm), lambda i, j, k: (k, i)) if ta
              else pl.BlockSpec((bm, bk), lambda i, j, k: (i, k)))
    b_spec = (pl.BlockSpec((bn, bk), lambda i, j, k: (j, k)) if tb
              else pl.BlockSpec((bk, bn), lambda i, j, k: (k, j)))
    o_spec = pl.BlockSpec((bm, bn), lambda i, j, k: (i, j))
    dims = (((0 if ta else 1,), (1 if tb else 0,)), ((), ()))

    def body(*refs):
        a_ref, b_ref = refs[0], refs[1]
        c_ref = refs[2] if has_add else None
        o_ref = refs[3] if has_add else refs[2]
        part = lax.dot_general(a_ref[...].astype(BF16), b_ref[...].astype(BF16), dims, preferred_element_type=F32)

        def finish(r):
            if has_add:
                r = r + add_scale * c_ref[...]
            o_ref[...] = r.astype(out_dtype)

        if nk == 1:
            finish(part)
        else:
            acc_ref = refs[-1]
            k = pl.program_id(2)

            @pl.when(k == 0)
            def _():
                acc_ref[...] = part

            @pl.when(k > 0)
            def _():
                acc_ref[...] += part

            @pl.when(k == nk - 1)
            def _():
                finish(acc_ref[...])

    in_specs = [a_spec, b_spec] + ([o_spec] if has_add else [])
    args = (a, b) + ((add,) if has_add else ())
    return _pcall(
        body, name=name, grid=(M // bm, N // bn, nk), in_specs=in_specs, out_specs=o_spec,
        out_shape=_sds((M, N), out_dtype),
        scratch_shapes=[pltpu.VMEM((bm, bn), F32)] if nk > 1 else [],
        compiler_params=_params(("parallel", "parallel", "arbitrary")),
    )(*args)


def _norm_fwd(x, g, b, *, res=None, alpha=1.0, center=True, eps, want_f32=True, want_bf16=True, name):
    S, W = x.shape
    bs = _pick(S, 256, 16)
    has_res, has_b = res is not None, b is not None

    def body(*refs):
        it = iter(refs)
        x_ref = next(it)
        res_ref = next(it) if has_res else None
        g_ref = next(it)
        b_ref = next(it) if has_b else None
        y_ref = next(it) if want_f32 else None
        yb_ref = next(it) if want_bf16 else None
        xh_ref, r_ref = next(it), next(it)
        z = x_ref[...]
        if has_res:
            z = alpha * z + res_ref[...]
        if center:
            z = z - jnp.mean(z, axis=-1, keepdims=True)
        rstd = lax.rsqrt(jnp.mean(z * z, axis=-1, keepdims=True) + eps)
        xh = z * rstd
        y = xh * g_ref[...]
        if has_b:
            y = y + b_ref[...]
        if want_f32:
            y_ref[...] = y
        if want_bf16:
            yb_ref[...] = y.astype(BF16)
        xh_ref[...] = xh
        r_ref[...] = rstd

    row = pl.BlockSpec((bs, W), lambda i: (i, 0))
    vec = pl.BlockSpec((1, W), lambda i: (0, 0))
    in_specs = [row] + ([row] if has_res else []) + [vec] + ([vec] if has_b else [])
    args = (x,) + ((res,) if has_res else ()) + (g.reshape(1, W),) + ((b.reshape(1, W),) if has_b else ())
    out_specs = ([row] if want_f32 else []) + ([row] if want_bf16 else []) + [row, pl.BlockSpec((bs, 1), lambda i: (i, 0))]
    out_shape = (([_sds((S, W), F32)] if want_f32 else []) + ([_sds((S, W), BF16)] if want_bf16 else [])
                 + [_sds((S, W), F32), _sds((S, 1), F32)])
    outs = list(_pcall(body, name=name, grid=(S // bs,), in_specs=in_specs, out_specs=out_specs, out_shape=out_shape,
                       compiler_params=_params(("parallel",)))(*args))
    y = outs.pop(0) if want_f32 else None
    yb = outs.pop(0) if want_bf16 else None
    return y, yb, outs[0], outs[1]


def _norm_bwd(dy, xh, rstd, g, *, center=True, want_f32=True, want_bf16=True, name):
    S, W = dy.shape
    bs = _pick(S, 256, 16)

    def body(*refs):
        dy_ref, xh_ref, r_ref, g_ref = refs[:4]
        it = iter(refs[4:])
        dz_ref = next(it) if want_f32 else None
        dzb_ref = next(it) if want_bf16 else None
        dg_ref, db_ref = next(it), next(it)

        @pl.when(pl.program_id(0) == 0)
        def _():
            dg_ref[...] = jnp.zeros_like(dg_ref)
            db_ref[...] = jnp.zeros_like(db_ref)

        dyv = dy_ref[...]
        xhv = xh_ref[...]
        dyg = dyv * g_ref[...]
        m2 = jnp.mean(dyg * xhv, axis=-1, keepdims=True)
        t = dyg - xhv * m2
        if center:
            t = t - jnp.mean(dyg, axis=-1, keepdims=True)
        dz = r_ref[...] * t
        if want_f32:
            dz_ref[...] = dz
        if want_bf16:
            dzb_ref[...] = dz.astype(BF16)
        dg_ref[...] += jnp.sum(dyv * xhv, axis=0, keepdims=True)
        db_ref[...] += jnp.sum(dyv, axis=0, keepdims=True)

    row = pl.BlockSpec((bs, W), lambda i: (i, 0))
    vec = pl.BlockSpec((1, W), lambda i: (0, 0))
    out_specs = ([row] if want_f32 else []) + ([row] if want_bf16 else []) + [vec, vec]
    out_shape = (([_sds((S, W), F32)] if want_f32 else []) + ([_sds((S, W), BF16)] if want_bf16 else [])
                 + [_sds((1, W), F32), _sds((1, W), F32)])
    outs = list(_pcall(body, name=name, grid=(S // bs,),
                       in_specs=[row, row, pl.BlockSpec((bs, 1), lambda i: (i, 0)), vec],
                       out_specs=out_specs, out_shape=out_shape,
                       compiler_params=_params(("arbitrary",)))(dy, xh, rstd, g.reshape(1, W)))
    dz = outs.pop(0) if want_f32 else None
    dzb = outs.pop(0) if want_bf16 else None
    return dz, dzb, outs[0][0], outs[1][0]


def _rope(t1, t2, cos, sin, *, name):
    S, C = t1.shape
    bs = _pick(S, 512, 8)

    def body(t1_ref, t2_ref, c_ref, s_ref, o1_ref, o2_ref):
        a, b, c, s = t1_ref[...], t2_ref[...], c_ref[...], s_ref[...]
        o1_ref[...] = a * c - b * s
        o2_ref[...] = b * c + a * s

    row = pl.BlockSpec((bs, C), lambda i: (i, 0))
    return _pcall(
        body, name=name, grid=(S // bs,), in_specs=[row] * 4, out_specs=[row, row],
        out_shape=[_sds((S, C), F32)] * 2, compiler_params=_params(("parallel",)),
    )(t1, t2, cos, sin)


def _diag_mask(B):
    rows = lax.shift_right_logical(lax.broadcasted_iota(jnp.int32, (B, B), 0), CHUNK_SHIFT)
    cols = lax.shift_right_logical(lax.broadcasted_iota(jnp.int32, (B, B), 1), CHUNK_SHIFT)
    return rows >= cols


def _nt(a, b):
    return lax.dot_general(a, b, (((1,), (1,)), ((), ())), preferred_element_type=F32)


def _nn(a, b):
    return lax.dot_general(a, b, (((1,), (0,)), ((), ())), preferred_element_type=F32)


def _tn(a, b):
    return lax.dot_general(a, b, (((0,), (0,)), ((), ())), preferred_element_type=F32)


def _attn_fwd(qh, kh, vh, *, name):
    H, S, DQ = qh.shape
    DV = vh.shape[-1]
    B = _pick(S, ATTN_BLOCK, CHUNK)
    scale = float(DQ) ** -0.5
    neg = float(jnp.finfo(jnp.float32).min)

    def body(q_ref, k_ref, v_ref, o_ref, lse_ref):
        qi = pl.program_id(1)
        q = q_ref[...]

        def step(j, carry, masked):
            m, l, acc = carry
            rows = pl.ds(pl.multiple_of(j * B, B), B)
            s = _nt(q, k_ref[rows, :]) * scale
            if masked:
                s = jnp.where(_diag_mask(B), s, neg)
            m_new = jnp.maximum(m, jnp.max(s, axis=-1, keepdims=True))
            corr = jnp.exp(m - m_new)
            p = jnp.exp(s - m_new)
            l = corr * l + jnp.sum(p, axis=-1, keepdims=True)
            acc = corr * acc + _nn(p.astype(BF16), v_ref[rows, :])
            return m_new, l, acc

        init = (jnp.full((B, 1), -1e30, F32), jnp.zeros((B, 1), F32), jnp.zeros((B, DV), F32))
        carry = lax.fori_loop(0, qi, lambda j, c: step(j, c, False), init)
        m, l, acc = step(qi, carry, True)
        o_ref[...] = acc / l
        lse_ref[...] = m + jnp.log(l)

    return _pcall(
        body, name=name, grid=(H, S // B),
        in_specs=[pl.BlockSpec((None, B, DQ), lambda h, i: (h, i, 0)),
                  pl.BlockSpec((None, S, DQ), lambda h, i: (h, 0, 0)),
                  pl.BlockSpec((None, S, DV), lambda h, i: (h, 0, 0))],
        out_specs=[pl.BlockSpec((B, DV), lambda h, i: (i, h)),
                   pl.BlockSpec((None, B, 1), lambda h, i: (h, i, 0))],
        out_shape=[_sds((S, H * DV), F32), _sds((H, S, 1), F32)],
        compiler_params=_params(("parallel", "parallel")),
    )(qh, kh, vh)


def _attn_bwd(qh, kh, vh, o, do, lse, *, do_col0, name):
    H, S, DQ = qh.shape
    DV = vh.shape[-1]
    B = _pick(S, ATTN_BLOCK, CHUNK)
    scale = float(DQ) ** -0.5

    def body(q_ref, k_ref, v_ref, o_ref, do_ref, lse_ref, dq_ref, dk_ref, dv_ref):
        qi = pl.program_id(1)

        @pl.when(qi == 0)
        def _():
            dk_ref[...] = jnp.zeros_like(dk_ref)
            dv_ref[...] = jnp.zeros_like(dv_ref)

        q = q_ref[...]
        dov = do_ref[...]
        dob = dov.astype(BF16)
        lse_v = lse_ref[...]
        dsum = jnp.sum(dov * o_ref[...], axis=-1, keepdims=True)

        def step(j, dq, masked):
            rows = pl.ds(pl.multiple_of(j * B, B), B)
            k = k_ref[rows, :]
            v = v_ref[rows, :]
            p = jnp.exp(_nt(q, k) * scale - lse_v)
            if masked:
                p = jnp.where(_diag_mask(B), p, 0.0)
            ds = (p * (_nt(dob, v) - dsum) * scale).astype(BF16)
            dv_ref[rows, :] += _tn(p.astype(BF16), dob)
            dk_ref[rows, :] += _tn(ds, q)
            return dq + _nn(ds, k)

        dq = lax.fori_loop(0, qi, lambda j, c: step(j, c, False), jnp.zeros((B, DQ), F32))
        dq_ref[...] = step(qi, dq, True)

    return _pcall(
        body, name=name, grid=(H, S // B),
        in_specs=[pl.BlockSpec((None, B, DQ), lambda h, i: (h, i, 0)),
                  pl.BlockSpec((None, S, DQ), lambda h, i: (h, 0, 0)),
                  pl.BlockSpec((None, S, DV), lambda h, i: (h, 0, 0)),
                  pl.BlockSpec((B, DV), lambda h, i: (i, h)),
                  pl.BlockSpec((B, DV), lambda h, i: (i, do_col0 + h)),
                  pl.BlockSpec((None, B, 1), lambda h, i: (h, i, 0))],
        out_specs=[pl.BlockSpec((None, B, DQ), lambda h, i: (h, i, 0)),
                   pl.BlockSpec((None, S, DQ), lambda h, i: (h, 0, 0)),
                   pl.BlockSpec((None, S, DV), lambda h, i: (h, 0, 0))],
        out_shape=[_sds((H, S, DQ), F32), _sds((H, S, DQ), F32), _sds((H, S, DV), F32)],
        compiler_params=_params(("parallel", "arbitrary")),
    )(qh, kh, vh, o, do, lse)


def _ret_tables():
    H, L = RET_HEADS, CHUNK
    log_gamma = jnp.log1p(-jnp.exp2(-5.0 - jnp.arange(H, dtype=F32)))
    idx = jnp.arange(L, dtype=F32)
    intra = jnp.exp(log_gamma[:, None, None] * jnp.abs(idx[:, None] - idx[None, :]))
    qd = jnp.exp(log_gamma[:, None] * (idx + 1.0))[:, :, None]
    kd = jnp.exp(log_gamma[:, None] * (L - 1.0 - idx))[:, :, None]
    cd = jnp.exp(log_gamma * L)[:, None, None]
    return intra, qd, kd, cd


def _rot(t, c, s):
    half = t.shape[-1] // 2
    t1, t2 = t[:, :half], t[:, half:]
    return jnp.concatenate([t1 * c - t2 * s, t2 * c + t1 * s], axis=-1)


def _rot_t(t, c, s):
    half = t.shape[-1] // 2
    t1, t2 = t[:, :half], t[:, half:]
    return jnp.concatenate([t1 * c + t2 * s, t2 * c - t1 * s], axis=-1)


def _dot(a, b, ca, cb):
    return lax.dot_general(a.astype(BF16), b.astype(BF16), (((ca,), (cb,)), ((), ())), preferred_element_type=F32)


def _ret_fwd(hR, cos, sin, tables, *, name):
    S = hR.shape[0]
    H, L, DK, DV = RET_HEADS, CHUNK, RET_QK, RET_V
    NC = S // L
    qscale = float(DK) ** -0.5
    intra, qd, kd, cd = tables

    def body(q_ref, k_ref, v_ref, c_ref, s_ref, in_ref, qd_ref, kd_ref, cd_ref, o_ref, st_ref, state):
        @pl.when(pl.program_id(1) == 0)
        def _():
            state[...] = jnp.zeros_like(state)

        c, s = c_ref[...], s_ref[...]
        q = _rot(q_ref[...], c, s) * qscale
        k = _rot(k_ref[...], c, s)
        v = v_ref[...]
        st = state[...]
        st_ref[...] = st.astype(BF16)
        scores = _dot(q, k, 1, 1) * in_ref[...]
        o_ref[...] = _dot(scores, v, 1, 0) + _dot(q * qd_ref[...], st, 1, 0)
        state[...] = st * cd_ref[...] + _dot(k * kd_ref[...], v, 0, 0)

    blk = lambda off: pl.BlockSpec((L, DK), lambda h, c: (c, off + h))
    rope = pl.BlockSpec((L, DK // 2), lambda h, c: (c, 0))
    return _pcall(
        body, name=name, grid=(H, NC),
        in_specs=[blk(0), blk(H), blk(2 * H), rope, rope,
                  pl.BlockSpec((None, L, L), lambda h, c: (h, 0, 0)),
                  pl.BlockSpec((None, L, 1), lambda h, c: (h, 0, 0)),
                  pl.BlockSpec((None, L, 1), lambda h, c: (h, 0, 0)),
                  pl.BlockSpec((None, 1, 1), lambda h, c: (h, 0, 0))],
        out_specs=[pl.BlockSpec((L, DV), lambda h, c: (c, h)),
                   pl.BlockSpec((None, None, DK, DV), lambda h, c: (h, c, 0, 0))],
        out_shape=[_sds((S, H * DV), F32), _sds((H, NC, DK, DV), BF16)],
        scratch_shapes=[pltpu.VMEM((DK, DV), F32)],
        compiler_params=_params(("arbitrary", "arbitrary")),
    )(hR, hR, hR, cos, sin, intra, qd, kd, cd)


def _ret_bwd(do, hR, states, cos, sin, tables, *, name):
    S = hR.shape[0]
    H, L, DK, DV = RET_HEADS, CHUNK, RET_QK, RET_V
    NC = S // L
    qscale = float(DK) ** -0.5
    intra, qd, kd, cd = tables

    def body(do_ref, q_ref, k_ref, v_ref, st_ref, c_ref, s_ref, in_ref, qd_ref, kd_ref, cd_ref,
             dq_ref, dk_ref, dv_ref, dstate):
        @pl.when(pl.program_id(1) == 0)
        def _():
            dstate[...] = jnp.zeros_like(dstate)

        c, s = c_ref[...], s_ref[...]
        q = _rot(q_ref[...], c, s) * qscale
        k = _rot(k_ref[...], c, s)
        v = v_ref[...]
        dov = do_ref[...]
        dst = dstate[...]
        dec = in_ref[...]
        qdv, kdv = qd_ref[...], kd_ref[...]
        scores = _dot(q, k, 1, 1) * dec
        da = _dot(dov, v, 1, 1) * dec
        dv_ref[...] = (_dot(scores, dov, 0, 0) + _dot(k * kdv, dst, 1, 0)).astype(BF16)
        dq = _dot(da, k, 1, 0) + _dot(dov, st_ref[...], 1, 1) * qdv
        dk = _dot(da, q, 0, 0) + _dot(v, dst, 1, 1) * kdv
        dq_ref[...] = _rot_t(dq * qscale, c, s).astype(BF16)
        dk_ref[...] = _rot_t(dk, c, s).astype(BF16)
        dstate[...] = dst * cd_ref[...] + _dot(q * qdv, dov, 0, 0)

    rev = lambda c: NC - 1 - c
    blk = lambda off: pl.BlockSpec((L, DK), lambda h, c: (rev(c), off + h))
    rope = pl.BlockSpec((L, DK // 2), lambda h, c: (rev(c), 0))
    out = pl.BlockSpec((L, DK), lambda h, c: (rev(c), h))
    return _pcall(
        body, name=name, grid=(H, NC),
        in_specs=[out, blk(0), blk(H), blk(2 * H),
                  pl.BlockSpec((None, None, DK, DV), lambda h, c: (h, rev(c), 0, 0)), rope, rope,
                  pl.BlockSpec((None, L, L), lambda h, c: (h, 0, 0)),
                  pl.BlockSpec((None, L, 1), lambda h, c: (h, 0, 0)),
                  pl.BlockSpec((None, L, 1), lambda h, c: (h, 0, 0)),
                  pl.BlockSpec((None, 1, 1), lambda h, c: (h, 0, 0))],
        out_specs=[out, out, out],
        out_shape=[_sds((S, H * DK), BF16)] * 3,
        scratch_shapes=[pltpu.VMEM((DK, DV), F32)],
        compiler_params=_params(("arbitrary", "arbitrary")),
    )(do, hR, hR, hR, states, cos, sin, intra, qd, kd, cd)


def _gn_gate_fwd(o, hR, g, b, *, name):
    S = o.shape[0]
    H, DV = RET_HEADS, RET_V
    bs = _pick(S, 512, 16)

    def body(o_ref, rg_ref, g_ref, b_ref, r_ref):
        z = o_ref[...]
        z = z - jnp.mean(z, axis=-1, keepdims=True)
        xh = z * lax.rsqrt(jnp.mean(z * z, axis=-1, keepdims=True) + GN_EPS)
        rg = rg_ref[...]
        r_ref[...] = ((rg * jax.nn.sigmoid(rg)) * (xh * g_ref[...] + b_ref[...])).astype(BF16)

    row = pl.BlockSpec((bs, DV), lambda i, h: (i, h))
    vec = pl.BlockSpec((1, DV), lambda i, h: (0, h))
    return _pcall(
        body, name=name, grid=(S // bs, H),
        in_specs=[row, pl.BlockSpec((bs, DV), lambda i, h: (i, 3 * H + h)), vec, vec],
        out_specs=row, out_shape=_sds((S, H * DV), BF16),
        compiler_params=_params(("parallel", "parallel")),
    )(o, hR, g.reshape(1, H * DV), b.reshape(1, H * DV))


def _gn_gate_bwd(dr, o, hR, g, b, *, dr_col0, name):
    S = o.shape[0]
    H, DV = RET_HEADS, RET_V
    bs = _pick(S, 512, 16)

    def body(dr_ref, o_ref, rg_ref, g_ref, b_ref, do_ref, drg_ref, dg_ref, db_ref):
        @pl.when(pl.program_id(1) == 0)
        def _():
            dg_ref[...] = jnp.zeros_like(dg_ref)
            db_ref[...] = jnp.zeros_like(db_ref)

        z = o_ref[...]
        z = z - jnp.mean(z, axis=-1, keepdims=True)
        rstd = lax.rsqrt(jnp.mean(z * z, axis=-1, keepdims=True) + GN_EPS)
        xh = z * rstd
        gv = g_ref[...]
        y = xh * gv + b_ref[...]
        rg = rg_ref[...]
        sg = jax.nn.sigmoid(rg)
        drv = dr_ref[...]
        dy = drv * (rg * sg)
        drg_ref[...] = (drv * y * (sg * (1.0 + rg * (1.0 - sg)))).astype(BF16)
        dg_ref[...] += jnp.sum(dy * xh, axis=0, keepdims=True)
        db_ref[...] += jnp.sum(dy, axis=0, keepdims=True)
        dxh = dy * gv
        do_ref[...] = rstd * (dxh - jnp.mean(dxh, axis=-1, keepdims=True)
                              - xh * jnp.mean(dxh * xh, axis=-1, keepdims=True))

    row = pl.BlockSpec((bs, DV), lambda h, i: (i, h))
    vec = pl.BlockSpec((1, DV), lambda h, i: (0, h))
    do, drg, dg, db = _pcall(
        body, name=name, grid=(H, S // bs),
        in_specs=[pl.BlockSpec((bs, DV), lambda h, i: (i, dr_col0 + h)), row,
                  pl.BlockSpec((bs, DV), lambda h, i: (i, 3 * H + h)), vec, vec],
        out_specs=[row, row, vec, vec],
        out_shape=[_sds((S, H * DV), F32), _sds((S, H * DV), BF16), _sds((1, H * DV), F32), _sds((1, H * DV), F32)],
        compiler_params=_params(("arbitrary", "arbitrary")),
    )(dr, o, hR, g.reshape(1, H * DV), b.reshape(1, H * DV))
    return do, drg, dg[0], db[0]


def _swiglu_fwd(g, u, *, name):
    S, F = g.shape
    bs, bf = _pick(S, 512, 16), _pick(F, 1408, 128)

    def body(g_ref, u_ref, a_ref):
        gv = g_ref[...]
        a_ref[...] = ((gv * jax.nn.sigmoid(gv)) * u_ref[...]).astype(BF16)

    blk = pl.BlockSpec((bs, bf), lambda i, j: (i, j))
    return _pcall(body, name=name, grid=(S // bs, F // bf), in_specs=[blk, blk], out_specs=blk,
                  out_shape=_sds((S, F), BF16), compiler_params=_params(("parallel", "parallel")))(g, u)


def _swiglu_bwd(da, g, u, *, name):
    S, F = g.shape
    bs, bf = _pick(S, 512, 16), _pick(F, 1408, 128)

    def body(da_ref, g_ref, u_ref, dg_ref, du_ref):
        gv, dav = g_ref[...], da_ref[...]
        sg = jax.nn.sigmoid(gv)
        du_ref[...] = (dav * (gv * sg)).astype(BF16)
        dg_ref[...] = (dav * u_ref[...] * (sg * (1.0 + gv * (1.0 - sg)))).astype(BF16)

    blk = pl.BlockSpec((bs, bf), lambda i, j: (i, j))
    return _pcall(body, name=name, grid=(S // bs, F // bf), in_specs=[blk] * 3, out_specs=[blk, blk],
                  out_shape=[_sds((S, F), BF16)] * 2, compiler_params=_params(("parallel", "parallel")))(da, g, u)


def _loss_head(y, t, *, name):
    S, D = y.shape
    bs = _pick(S, 256, 8)
    inv_d = 1.0 / D

    def body(y_ref, t_ref, dy_ref, l_ref):
        @pl.when(pl.program_id(0) == 0)
        def _():
            l_ref[...] = jnp.zeros_like(l_ref)

        e = y_ref[...] - t_ref[...]
        dy_ref[...] = e * inv_d
        l_ref[...] += 0.5 * jnp.sum(jnp.mean(e * e, axis=-1, keepdims=True), axis=0, keepdims=True)

    row = pl.BlockSpec((bs, D), lambda i: (i, 0))
    dy, l = _pcall(
        body, name=name, grid=(S // bs,), in_specs=[row, row],
        out_specs=[row, pl.BlockSpec((1, 1), lambda i: (0, 0))],
        out_shape=[_sds((S, D), F32), _sds((1, 1), F32)],
        compiler_params=_params(("arbitrary",)),
    )(y, t)
    return dy, l[0, 0]


def _adamw(w, g, m, v, *, name):
    shape = w.shape
    C = shape[-1]
    R = w.size // C
    br = _pick(R, 512, 8)

    def body(w_ref, g_ref, m_ref, v_ref, d_ref, nm_ref, nv_ref):
        gv = g_ref[...]
        mn = ADAM_B1 * m_ref[...] + (1.0 - ADAM_B1) * gv
        vn = ADAM_B2 * v_ref[...] + (1.0 - ADAM_B2) * (gv * gv)
        m_hat = mn / (1.0 - ADAM_B1 ** ADAM_STEP)
        v_hat = vn / (1.0 - ADAM_B2 ** ADAM_STEP)
        d_ref[...] = -ADAM_LR * (m_hat / (jnp.sqrt(v_hat) + ADAM_EPS) + ADAM_WD * w_ref[...])
        nm_ref[...] = mn
        nv_ref[...] = vn

    blk = pl.BlockSpec((br, C), lambda i: (i, 0))
    outs = _pcall(body, name=name, grid=(R // br,), in_specs=[blk] * 4, out_specs=[blk] * 3,
                  out_shape=[_sds((R, C), F32)] * 3,
                  compiler_params=_params(("parallel",)))(*[a.reshape(R, C) for a in (w, g, m, v)])
    return tuple(o.reshape(shape) for o in outs)


def _sum_slots(x, *, name):
    shape = x.shape[1:]
    total = x[0].size
    C = next(c for c in (2048, 1024, 512, 256, 128) if total % (16 * c) == 0)
    R = total // C
    x = x.reshape(N_DEV, R, C)
    br = _pick(R, 256, 16)

    def body(x_ref, o_ref):
        acc = x_ref[0].astype(F32)
        for s in range(1, N_DEV):
            acc = acc + x_ref[s].astype(F32)
        o_ref[...] = acc

    return _pcall(body, name=name, grid=(R // br,),
                  in_specs=[pl.BlockSpec((N_DEV, br, C), lambda i: (0, i, 0))],
                  out_specs=pl.BlockSpec((br, C), lambda i: (i, 0)),
                  out_shape=_sds((R, C), F32), compiler_params=_params(("parallel",)))(x).reshape(shape)


def _sum_heads(dkh, *, name):
    H, S, DQ = dkh.shape
    bs = _pick(S, 512, 8)

    def body(d_ref, o_ref):
        acc = d_ref[0][:, MLA_NOPE:]
        for h in range(1, H):
            acc = acc + d_ref[h][:, MLA_NOPE:]
        o_ref[...] = acc

    return _pcall(body, name=name, grid=(S // bs,),
                  in_specs=[pl.BlockSpec((H, bs, DQ), lambda i: (0, i, 0))],
                  out_specs=pl.BlockSpec((bs, MLA_ROPE), lambda i: (i, 0)),
                  out_shape=_sds((S, MLA_ROPE), F32), compiler_params=_params(("parallel",)))(dkh)


N_PEER = N_DEV - 1


def _coords():
    return lax.axis_index("x"), lax.axis_index("y"), lax.axis_index("c")


def _exchange_call(body, ins, out_shapes, name):
    n = len(ins)
    any_spec = pl.BlockSpec(memory_space=pl.ANY)
    return _pcall(
        body, name=name, out_shape=out_shapes, in_specs=[any_spec] * n, out_specs=[any_spec] * n,
        scratch_shapes=[pltpu.SemaphoreType.DMA((n * N_PEER,)), pltpu.SemaphoreType.DMA((n * N_PEER,)),
                        pltpu.SemaphoreType.DMA((n,))],
    )(*ins)


def _all_gather(xs, *, name):
    n = len(xs)

    def body(*refs):
        x_refs, out_refs = refs[:n], refs[n:2 * n]
        send_sems, recv_sems, local_sems = refs[2 * n:]
        x, y, c = _coords()
        me, sibling = (x, y, c), (x, y, 1 - c)
        chips = [(1 - x, y), (x, 1 - y), (1 - x, 1 - y)]

        def copy(a, k, block, to, src=None):
            px, py, pc = block
            dst = out_refs[a].at[4 * px + 2 * py + pc]
            return pltpu.make_async_remote_copy(
                src_ref=dst if src is None else src, dst_ref=dst,
                send_sem=send_sems.at[a * N_PEER + k], recv_sem=recv_sems.at[a * N_PEER + k],
                device_id=to, device_id_type=pl.DeviceIdType.MESH)

        mine = [pltpu.make_async_copy(x_refs[a], out_refs[a].at[4 * x + 2 * y + c], local_sems.at[a]) for a in range(n)]
        for cp in mine:
            cp.start()
        first = []
        for a in range(n):
            first.append(copy(a, 0, me, sibling, src=x_refs[a]))
            first += [copy(a, 1 + j, me, (*chip, c), src=x_refs[a]) for j, chip in enumerate(chips)]
        for cp in first:
            cp.start()
        passed = []
        for j, chip in enumerate(chips):
            for a in range(n):
                copy(a, 1 + j, (*chip, c), me).wait_recv()
                fwd = copy(a, 4 + j, (*chip, c), sibling)
                fwd.start()
                passed.append(fwd)
        for a in range(n):
            copy(a, 0, sibling, me).wait_recv()
            for j, chip in enumerate(chips):
                copy(a, 4 + j, (*chip, 1 - c), me).wait_recv()
        for cp in first + passed:
            cp.wait_send()
        for cp in mine:
            cp.wait()

    return _exchange_call(body, xs, [_sds((N_DEV,) + x.shape, x.dtype) for x in xs], name)


def _all_to_all(gs, *, name):
    n = len(gs)

    def body(*refs):
        g_refs, out_refs = refs[:n], refs[n:2 * n]
        send_sems, recv_sems, local_sems = refs[2 * n:]
        x, y, c = _coords()
        me = 4 * x + 2 * y + c
        mine = [pltpu.make_async_copy(g_refs[a].at[me], out_refs[a].at[me], local_sems.at[a]) for a in range(n)]
        for cp in mine:
            cp.start()
        copies = []
        for k in range(1, N_DEV):
            px = 1 - x if k & 4 else x
            py = 1 - y if k & 2 else y
            pc = 1 - c if k & 1 else c
            peer = 4 * px + 2 * py + pc
            for a in range(n):
                sem = a * N_PEER + k - 1
                send = pltpu.make_async_remote_copy(
                    src_ref=g_refs[a].at[peer], dst_ref=out_refs[a].at[me], send_sem=send_sems.at[sem],
                    recv_sem=recv_sems.at[sem], device_id=(px, py, pc), device_id_type=pl.DeviceIdType.MESH)
                recv = pltpu.make_async_remote_copy(
                    src_ref=g_refs[a].at[peer], dst_ref=out_refs[a].at[peer], send_sem=send_sems.at[sem],
                    recv_sem=recv_sems.at[sem], device_id=(px, py, pc), device_id_type=pl.DeviceIdType.MESH)
                send.start()
                copies.append((send, recv))
        for send, recv in copies:
            recv.wait_recv()
        for send, recv in copies:
            send.wait_send()
        for cp in mine:
            cp.wait()

    return _exchange_call(body, gs, [_sds(g.shape, g.dtype) for g in gs], name)


_TRANSPOSED = {"w_in": True, "w_out": False, "w_gate": True, "w_up": True, "w_down": False, "w_uq": True, "w_ukv": True}
_BIG = tuple(_TRANSPOSED)
_SMALL = ("ln_in_g", "ln_in_b", "q_norm_g", "kv_norm_g", "ret_gn_g", "ret_gn_b", "ln1_g", "ln1_b", "ln2_g", "ln2_b")
SMALL_COLS = 128


def _rope_tables(pos, dim):
    inv_freq = ROPE_THETA ** (-jnp.arange(0, dim, 2, dtype=F32) / dim)
    ang = pos.astype(F32)[:, None] * inv_freq
    return jnp.cos(ang), jnp.sin(ang)


def _split_in(wt_in):
    a, b, c = MLA_Q_LORA, MLA_Q_LORA + MLA_KV_LORA, MLA_Q_LORA + MLA_KV_LORA + MLA_ROPE
    return wt_in[:a], wt_in[a:b], wt_in[b:c], wt_in[c:]


def _mla_pack_rope(q, kr):
    S = q.shape[0]
    H, hr = MLA_HEADS, MLA_ROPE // 2
    q3 = q.reshape(S, H, MLA_QK)
    t1 = jnp.concatenate([q3[:, :, MLA_NOPE:MLA_NOPE + hr].reshape(S, H * hr), kr[:, :hr]], axis=1)
    t2 = jnp.concatenate([q3[:, :, MLA_NOPE + hr:].reshape(S, H * hr), kr[:, hr:]], axis=1)
    return q3[:, :, :MLA_NOPE], t1, t2


def _mla_unpack_rope(q_nope, o1, o2):
    S = o1.shape[0]
    H, hr = MLA_HEADS, MLA_ROPE // 2
    q3 = jnp.concatenate([q_nope, o1[:, :H * hr].reshape(S, H, hr), o2[:, :H * hr].reshape(S, H, hr)], axis=-1)
    kr = jnp.concatenate([o1[:, H * hr:], o2[:, H * hr:]], axis=1)
    return q3, kr


def _layer_fwd(x, xb, w, p, tabs, l):
    S, D = x.shape
    H = MLA_HEADS
    nm = lambda s: "l%d_%s" % (l, s)
    wt_q, wt_kv, wt_kr, wt_r = _split_in(w["w_in"])
    cq = _mm(xb, wt_q, tb=True, name=nm("cq"))
    ckv = _mm(xb, wt_kv, tb=True, name=nm("ckv"))
    kr = _mm(xb, wt_kr, tb=True, name=nm("krope"))
    hR = _mm(xb, wt_r, tb=True, name=nm("hR"))
    _, qn, qn_hat, q_rstd = _norm_fwd(cq, p["q_norm_g"], None, center=False, eps=RMS_EPS, want_f32=False,
                                      name=nm("qnorm"))
    _, kvn, kvn_hat, kv_rstd = _norm_fwd(ckv, p["kv_norm_g"], None, center=False, eps=RMS_EPS, want_f32=False,
                                         name=nm("kvnorm"))
    q = _mm(qn, w["w_uq"], tb=True, name=nm("uq"))
    kv = _mm(kvn, w["w_ukv"], tb=True, name=nm("ukv"))
    q_nope, t1, t2 = _mla_pack_rope(q, kr)
    o1, o2 = _rope(t1, t2, tabs["cos_m"], tabs["sin_m"], name=nm("rope"))
    q3, k_r = _mla_unpack_rope(q_nope, o1, o2)
    kv3 = kv.reshape(S, H, MLA_NOPE + MLA_V)
    qh = q3.transpose(1, 0, 2).astype(BF16)
    kh = jnp.concatenate([kv3[:, :, :MLA_NOPE], jnp.broadcast_to(k_r[:, None, :], (S, H, MLA_ROPE))],
                         axis=-1).transpose(1, 0, 2).astype(BF16)
    vh = kv3[:, :, MLA_NOPE:].transpose(1, 0, 2).astype(BF16)
    a, lse = _attn_fwd(qh, kh, vh, name=nm("attn"))
    o_ret, states = _ret_fwd(hR, tabs["cos_r"], tabs["sin_r"], tabs["ret"], name=nm("ret"))
    r = _gn_gate_fwd(o_ret, hR, p["ret_gn_g"], p["ret_gn_b"], name=nm("gn"))
    mix_in = jnp.concatenate([a.astype(BF16), r], axis=1)
    mix = _mm(mix_in, w["w_out"], name=nm("out"))
    x1, x1b, x1_hat, rstd1 = _norm_fwd(x, p["ln1_g"], p["ln1_b"], res=mix, alpha=p["alpha"], eps=LN_EPS, name=nm("ln1"))
    g = _mm(x1b, w["w_gate"], tb=True, name=nm("gate"))
    u = _mm(x1b, w["w_up"], tb=True, name=nm("up"))
    act = _swiglu_fwd(g, u, name=nm("swiglu"))
    f = _mm(act, w["w_down"], name=nm("down"))
    x2, x2b, x2_hat, rstd2 = _norm_fwd(x1, p["ln2_g"], p["ln2_b"], res=f, alpha=p["alpha"], eps=LN_EPS, name=nm("ln2"))
    saved = dict(xb=xb, qn=qn, qn_hat=qn_hat, q_rstd=q_rstd, kvn=kvn, kvn_hat=kvn_hat, kv_rstd=kv_rstd,
                 qh=qh, kh=kh, vh=vh, a=a, lse=lse, hR=hR, o_ret=o_ret, states=states, mix_in=mix_in,
                 x1b=x1b, x1_hat=x1_hat, rstd1=rstd1, g=g, u=u, act=act, x2_hat=x2_hat, rstd2=rstd2)
    return x2, x2b, saved


def _layer_bwd(dx2, sv, w, p, tabs, l):
    S, D = dx2.shape
    H = MLA_HEADS
    nm = lambda s: "l%d_b_%s" % (l, s)
    alpha = p["alpha"]
    gw, gp = {}, {}
    dz2, dz2b, gp["ln2_g"], gp["ln2_b"] = _norm_bwd(dx2, sv["x2_hat"], sv["rstd2"], p["ln2_g"], name=nm("ln2"))
    dact = _mm(dz2b, w["w_down"], tb=True, name=nm("dact"))
    gw["w_down"] = _mm(sv["act"], dz2b, ta=True, out_dtype=BF16, name=nm("wdown"))
    dg, du = _swiglu_bwd(dact, sv["g"], sv["u"], name=nm("swiglu"))
    gw["w_gate"] = _mm(dg, sv["x1b"], ta=True, out_dtype=BF16, name=nm("wgate"))
    gw["w_up"] = _mm(du, sv["x1b"], ta=True, out_dtype=BF16, name=nm("wup"))
    t = _mm(dg, w["w_gate"], add=dz2, add_scale=alpha, name=nm("dx1a"))
    dx1 = _mm(du, w["w_up"], add=t, name=nm("dx1b"))
    dz1, dz1b, gp["ln1_g"], gp["ln1_b"] = _norm_bwd(dx1, sv["x1_hat"], sv["rstd1"], p["ln1_g"], name=nm("ln1"))
    dmix = _mm(dz1b, w["w_out"], tb=True, name=nm("dmix"))
    gw["w_out"] = _mm(sv["mix_in"], dz1b, ta=True, out_dtype=BF16, name=nm("wout"))
    ret_col0 = (H * MLA_V) // RET_V
    do_ret, drg, gp["ret_gn_g"], gp["ret_gn_b"] = _gn_gate_bwd(
        dmix, sv["o_ret"], sv["hR"], p["ret_gn_g"], p["ret_gn_b"], dr_col0=ret_col0, name=nm("gn"))
    drq, drk, drv = _ret_bwd(do_ret, sv["hR"], sv["states"], tabs["cos_r"], tabs["sin_r"], tabs["ret"], name=nm("ret"))
    dqh, dkh, dvh = _attn_bwd(sv["qh"], sv["kh"], sv["vh"], sv["a"], dmix, sv["lse"], do_col0=0, name=nm("attn"))
    dq3 = dqh.transpose(1, 0, 2)
    dk_r = _sum_heads(dkh, name=nm("dkr"))
    dq_nope, dt1, dt2 = _mla_pack_rope(dq3.reshape(S, H * MLA_QK), dk_r)
    di1, di2 = _rope(dt1, dt2, tabs["cos_m"], tabs["nsin_m"], name=nm("rope"))
    dq3, dkr = _mla_unpack_rope(dq_nope, di1, di2)
    dq = dq3.reshape(S, H * MLA_QK).astype(BF16)
    dkv = jnp.concatenate([dkh[:, :, :MLA_NOPE].transpose(1, 0, 2), dvh.transpose(1, 0, 2)],
                          axis=-1).reshape(S, -1).astype(BF16)
    gw["w_uq"] = _mm(dq, sv["qn"], ta=True, out_dtype=BF16, name=nm("wuq"))
    dqn = _mm(dq, w["w_uq"], name=nm("dqn"))
    gw["w_ukv"] = _mm(dkv, sv["kvn"], ta=True, out_dtype=BF16, name=nm("wukv"))
    dkvn = _mm(dkv, w["w_ukv"], name=nm("dkvn"))
    _, dcq, gp["q_norm_g"], _ = _norm_bwd(dqn, sv["qn_hat"], sv["q_rstd"], p["q_norm_g"], center=False,
                                          want_f32=False, name=nm("qnorm"))
    _, dckv, gp["kv_norm_g"], _ = _norm_bwd(dkvn, sv["kvn_hat"], sv["kv_rstd"], p["kv_norm_g"], center=False,
                                            want_f32=False, name=nm("kvnorm"))
    dh = jnp.concatenate([dcq, dckv, dkr.astype(BF16), drq, drk, drv, drg], axis=1)
    gw["w_in"] = _mm(dh, sv["xb"], ta=True, out_dtype=BF16, name=nm("win"))
    dx = _mm(dh, w["w_in"], add=dz1, add_scale=alpha, name=nm("dx"))
    return dx, gw, gp


def _local_step(x, target, pos, small, weights):
    depth = len(weights)
    alpha = (2 * depth) ** 0.25
    cos_m, sin_m = _rope_tables(pos, MLA_ROPE)
    cos_r, sin_r = _rope_tables(pos, RET_QK)
    reps = MLA_HEADS + 1
    tabs = dict(cos_m=jnp.tile(cos_m, (1, reps)), sin_m=jnp.tile(sin_m, (1, reps)),
                nsin_m=jnp.tile(-sin_m, (1, reps)), cos_r=cos_r, sin_r=sin_r, ret=_ret_tables())
    h, hb, h_hat, h_rstd = _norm_fwd(x, small["ln_in_g"], small["ln_in_b"], eps=LN_EPS, name="ln_in")
    saved, ps = [], []
    for l in range(depth):
        p = {k: small[k][l] for k in _SMALL[2:]}
        p["alpha"] = alpha
        h, hb, sv = _layer_fwd(h, hb, weights[l], p, tabs, l)
        saved.append(sv)
        ps.append(p)
    dy, loss = _loss_head(h, target, name="loss")
    gws, gps = [None] * depth, [None] * depth
    for l in reversed(range(depth)):
        dy, gws[l], gps[l] = _layer_bwd(dy, saved[l], weights[l], ps[l], tabs, l)
    grad_x, _, g_in_g, g_in_b = _norm_bwd(dy, h_hat, h_rstd, small["ln_in_g"], want_bf16=False, name="b_ln_in")
    gsmall = {"ln_in_g": g_in_g, "ln_in_b": g_in_b}
    for k in _SMALL[2:]:
        gsmall[k] = jnp.stack([gps[l][k] for l in range(depth)])
    return loss, grad_x, gws, gsmall


def kernel(x, positions, ln_in_g, ln_in_b, w_in, q_norm_g, kv_norm_g, w_uq, w_ukv, ret_gn_g, ret_gn_b, w_out, ln1_g, ln1_b, w_gate, w_up, w_down, ln2_g, ln2_b, loss_target, m_ln_in_g, m_ln_in_b, m_w_in, m_q_norm_g, m_kv_norm_g, m_w_uq, m_w_ukv, m_ret_gn_g, m_ret_gn_b, m_w_out, m_ln1_g, m_ln1_b, m_w_gate, m_w_up, m_w_down, m_ln2_g, m_ln2_b, v_ln_in_g, v_ln_in_b, v_w_in, v_q_norm_g, v_kv_norm_g, v_w_uq, v_w_ukv, v_ret_gn_g, v_ret_gn_b, v_w_out, v_ln1_g, v_ln1_b, v_w_gate, v_w_up, v_w_down, v_ln2_g, v_ln2_b):
    names = ["ln_in_g", "ln_in_b", "w_in", "q_norm_g", "kv_norm_g", "w_uq", "w_ukv", "ret_gn_g", "ret_gn_b", "w_out",
             "ln1_g", "ln1_b", "w_gate", "w_up", "w_down", "ln2_g", "ln2_b"]
    wv = dict(zip(names, (ln_in_g, ln_in_b, w_in, q_norm_g, kv_norm_g, w_uq, w_ukv, ret_gn_g, ret_gn_b, w_out,
                          ln1_g, ln1_b, w_gate, w_up, w_down, ln2_g, ln2_b)))
    mv = dict(zip(names, (m_ln_in_g, m_ln_in_b, m_w_in, m_q_norm_g, m_kv_norm_g, m_w_uq, m_w_ukv, m_ret_gn_g,
                          m_ret_gn_b, m_w_out, m_ln1_g, m_ln1_b, m_w_gate, m_w_up, m_w_down, m_ln2_g, m_ln2_b)))
    vv = dict(zip(names, (v_ln_in_g, v_ln_in_b, v_w_in, v_q_norm_g, v_kv_norm_g, v_w_uq, v_w_ukv, v_ret_gn_g,
                          v_ret_gn_b, v_w_out, v_ln1_g, v_ln1_b, v_w_gate, v_w_up, v_w_down, v_ln2_g, v_ln2_b)))
    depth = w_in.shape[0]

    keys = [(l, n) for l in range(depth) for n in _BIG]
    local = [(wv[n][l].T if _TRANSPOSED[n] else wv[n][l]).astype(BF16) for l, n in keys]
    gathered = _all_gather(local, name="ag_weights")
    weights = [dict() for _ in range(depth)]
    for (l, n), g in zip(keys, gathered):
        weights[l][n] = g.reshape(N_DEV * g.shape[1], g.shape[2])

    small = {n: wv[n] for n in _SMALL}
    loss, grad_x, gws, gsmall = _local_step(x[0], loss_target[0], positions[0], small, weights)
    loss = lax.psum(loss, MESH_AXES)

    shard_shapes = [sh.shape for sh in local]
    recv = _all_to_all([gws[l][n].reshape((N_DEV,) + s) for (l, n), s in zip(keys, shard_shapes)], name="a2a_grads")
    gshard = {n: [None] * depth for n in _BIG}
    for (l, n), r, s in zip(keys, recv, shard_shapes):
        tot = _sum_slots(r, name="sum_l%d_%s" % (l, n))
        gshard[n][l] = tot.T if _TRANSPOSED[n] else tot
    grads = {n: jnp.stack(v) for n, v in gshard.items()}

    flat = jnp.concatenate([gsmall[n].reshape(-1) for n in _SMALL])
    n_small = flat.shape[0]
    rows = -(-n_small // (SMALL_COLS * 8)) * 8
    flat = jnp.pad(flat, (0, rows * SMALL_COLS - n_small)).reshape(rows, SMALL_COLS)
    tot = _sum_slots(_all_gather([flat], name="ag_small")[0], name="sum_small").reshape(-1)
    off = 0
    for n in _SMALL:
        grads[n] = tot[off:off + wv[n].size].reshape(wv[n].shape)
        off += wv[n].size

    delta, new_m, new_v = {}, {}, {}
    for n in names:
        w2 = wv[n] if wv[n].ndim > 1 else wv[n].reshape(1, -1)
        d, nm_, nv_ = _adamw(w2, grads[n].reshape(w2.shape), mv[n].reshape(w2.shape), vv[n].reshape(w2.shape),
                             name="adamw_" + n)
        delta[n], new_m[n], new_v[n] = d.reshape(wv[n].shape), nm_.reshape(wv[n].shape), nv_.reshape(wv[n].shape)

    return (loss, grad_x[None], *[grads[n] for n in names], *[delta[n] for n in names],
            *[new_m[n] for n in names], *[new_v[n] for n in names])
```

```python
import jax
import jax.numpy as jnp
from jax import lax
from jax.experimental import pallas as pl
from jax.experimental.pallas import tpu as pltpu

F32 = jnp.float32
BF16 = jnp.bfloat16

CHUNK = 64
CHUNK_SHIFT = 6
MLA_HEADS = 8
MLA_Q_LORA = 512
MLA_KV_LORA = 256
MLA_NOPE = 128
MLA_ROPE = 64
MLA_V = 128
MLA_QK = MLA_NOPE + MLA_ROPE
RET_HEADS = 4
RET_QK = 256
RET_V = 256
ROPE_THETA = 10000.0
LN_EPS = 1e-5
RMS_EPS = 1e-6
GN_EPS = 1e-5
ADAM_LR = 0.001
ADAM_B1 = 0.9
ADAM_B2 = 0.999
ADAM_EPS = 1e-08
ADAM_WD = 0.01
ADAM_STEP = 10

N_DEV = 8
MESH_AXES = ("x", "y", "c")
VMEM_LIMIT_BYTES = 56 * 1024 * 1024
MM_VMEM_BUDGET = 40 * 1024 * 1024
MM_ACC_PENALTY = 0.85
ATTN_BLOCK = 256
ATTN_KEY_STEP = 512


def _pick(n, pref, mult):
    best = None
    d = mult
    while d <= min(n, pref):
        if n % d == 0:
            best = d
        d += mult
    return n if best is None else best


def _divisors(n, mult, cap):
    ds = [d for d in range(mult, min(n, cap) + 1, mult) if n % d == 0]
    return ds or [n]


def _pcall(body, **kw):
    return pl.pallas_call(body, **kw)


def _params(sem):
    return pltpu.CompilerParams(dimension_semantics=sem, vmem_limit_bytes=VMEM_LIMIT_BYTES)


def _sds(shape, dtype):
    return jax.ShapeDtypeStruct(shape, dtype)


def _mm_tiles(M, N, K, ta, sa, sb, so, has_add):
    best = None
    for bk in _divisors(K, 128, 8192):
        nk = K // bk
        for bm in _divisors(M, 128 if ta else 16, 1024):
            for bn in _divisors(N, 128, 1024):
                vmem = 2 * (bm * bk * sa + bk * bn * sb) + 2 * bm * bn * so
                vmem += (2 * bm * bn * 4 if has_add else 0) + (bm * bn * 4 if nk > 1 else 0)
                if vmem > MM_VMEM_BUDGET:
                    continue
                flops_per_byte = 1.0 / (1.0 / bm + (1.0 / max(N, bn) if nk == 1 else 1.0 / bn))
                score = (flops_per_byte * (1.0 if nk == 1 else MM_ACC_PENALTY), bn, bm)
                if best is None or score > best[0]:
                    best = (score, bm, bn, bk)
    assert best is not None, (M, N, K)
    return best[1:]


def _mm(a, b, *, ta=False, tb=False, add=None, add_scale=1.0, out_dtype=F32, job=None, name):
    if ta:
        K, M = a.shape
    else:
        M, K = a.shape
    if tb:
        N, K2 = b.shape
    else:
        K2, N = b.shape
    assert K == K2, (a.shape, b.shape, ta, tb)
    has_add = add is not None
    bm, bn, bk = _mm_tiles(M, N, K, ta, a.dtype.itemsize, b.dtype.itemsize, jnp.dtype(out_dtype).itemsize, has_add)
    nk = K // bk
    a_spec = (pl.BlockSpec((bk, bm), lambda i, j, k: (k, i)) if ta
              else pl.BlockSpec((bm, bk), lambda i, j, k: (i, k)))
    b_spec = (pl.BlockSpec((bn, bk), lambda i, j, k: (j, k)) if tb
              else pl.BlockSpec((bk, bn), lambda i, j, k: (k, j)))
    o_spec = pl.BlockSpec((bm, bn), lambda i, j, k: (i, j))
    dims = (((0 if ta else 1,), (1 if tb else 0,)), ((), ()))

    def body(*refs):
        a_ref, b_ref = refs[0], refs[1]
        c_ref = refs[2] if has_add else None
        o_ref = refs[3] if has_add else refs[2]
        part = lax.dot_general(a_ref[...].astype(BF16), b_ref[...].astype(BF16), dims, preferred_element_type=F32)

        def finish(r):
            if has_add:
                r = r + add_scale * c_ref[...]
            o_ref[...] = r.astype(out_dtype)

        if nk == 1:
            finish(part)
        else:
            acc_ref = refs[-1]
            k = pl.program_id(2)

            @pl.when(k == 0)
            def _():
                acc_ref[...] = part

            @pl.when(k > 0)
            def _():
                acc_ref[...] += part

            @pl.when(k == nk - 1)
            def _():
                finish(acc_ref[...])

    in_specs = [a_spec, b_spec] + ([o_spec] if has_add else [])
    args = (a, b) + ((add,) if has_add else ())
    outs, jouts = _hosted_call(
        body, job, name=name, grid=(M // bm, N // bn, nk), in_specs=in_specs, out_specs=[o_spec],
        out_shape=[_sds((M, N), out_dtype)], scratch_shapes=[pltpu.VMEM((bm, bn), F32)] if nk > 1 else [],
        semantics=("parallel", "parallel", "arbitrary"), args=args)
    return outs[0] if job is None else (outs[0], jouts)


def _norm_fwd(x, g, b, *, res=None, alpha=1.0, center=True, eps, want_f32=True, want_bf16=True, name):
    S, W = x.shape
    bs = _pick(S, 256, 16)
    has_res, has_b = res is not None, b is not None

    def body(*refs):
        it = iter(refs)
        x_ref = next(it)
        res_ref = next(it) if has_res else None
        g_ref = next(it)
        b_ref = next(it) if has_b else None
        y_ref = next(it) if want_f32 else None
        yb_ref = next(it) if want_bf16 else None
        xh_ref, r_ref = next(it), next(it)
        z = x_ref[...]
        if has_res:
            z = alpha * z + res_ref[...]
        if center:
            z = z - jnp.mean(z, axis=-1, keepdims=True)
        rstd = lax.rsqrt(jnp.mean(z * z, axis=-1, keepdims=True) + eps)
        xh = z * rstd
        y = xh * g_ref[...]
        if has_b:
            y = y + b_ref[...]
        if want_f32:
            y_ref[...] = y
        if want_bf16:
            yb_ref[...] = y.astype(BF16)
        xh_ref[...] = xh
        r_ref[...] = rstd

    row = pl.BlockSpec((bs, W), lambda i: (i, 0))
    vec = pl.BlockSpec((1, W), lambda i: (0, 0))
    in_specs = [row] + ([row] if has_res else []) + [vec] + ([vec] if has_b else [])
    args = (x,) + ((res,) if has_res else ()) + (g.reshape(1, W),) + ((b.reshape(1, W),) if has_b else ())
    out_specs = ([row] if want_f32 else []) + ([row] if want_bf16 else []) + [row, pl.BlockSpec((bs, 1), lambda i: (i, 0))]
    out_shape = (([_sds((S, W), F32)] if want_f32 else []) + ([_sds((S, W), BF16)] if want_bf16 else [])
                 + [_sds((S, W), F32), _sds((S, 1), F32)])
    outs = list(_pcall(body, name=name, grid=(S // bs,), in_specs=in_specs, out_specs=out_specs, out_shape=out_shape,
                       compiler_params=_params(("parallel",)))(*args))
    y = outs.pop(0) if want_f32 else None
    yb = outs.pop(0) if want_bf16 else None
    return y, yb, outs[0], outs[1]


def _norm_bwd(dy, xh, rstd, g, *, center=True, want_f32=True, want_bf16=True, name):
    S, W = dy.shape
    bs = _pick(S, 256, 16)

    def body(*refs):
        dy_ref, xh_ref, r_ref, g_ref = refs[:4]
        it = iter(refs[4:])
        dz_ref = next(it) if want_f32 else None
        dzb_ref = next(it) if want_bf16 else None
        dg_ref, db_ref = next(it), next(it)

        @pl.when(pl.program_id(0) == 0)
        def _():
            dg_ref[...] = jnp.zeros_like(dg_ref)
            db_ref[...] = jnp.zeros_like(db_ref)

        dyv = dy_ref[...]
        xhv = xh_ref[...]
        dyg = dyv * g_ref[...]
        m2 = jnp.mean(dyg * xhv, axis=-1, keepdims=True)
        t = dyg - xhv * m2
        if center:
            t = t - jnp.mean(dyg, axis=-1, keepdims=True)
        dz = r_ref[...] * t
        if want_f32:
            dz_ref[...] = dz
        if want_bf16:
            dzb_ref[...] = dz.astype(BF16)
        dg_ref[...] += jnp.sum(dyv * xhv, axis=0, keepdims=True)
        db_ref[...] += jnp.sum(dyv, axis=0, keepdims=True)

    row = pl.BlockSpec((bs, W), lambda i: (i, 0))
    vec = pl.BlockSpec((1, W), lambda i: (0, 0))
    out_specs = ([row] if want_f32 else []) + ([row] if want_bf16 else []) + [vec, vec]
    out_shape = (([_sds((S, W), F32)] if want_f32 else []) + ([_sds((S, W), BF16)] if want_bf16 else [])
                 + [_sds((1, W), F32), _sds((1, W), F32)])
    outs = list(_pcall(body, name=name, grid=(S // bs,),
                       in_specs=[row, row, pl.BlockSpec((bs, 1), lambda i: (i, 0)), vec],
                       out_specs=out_specs, out_shape=out_shape,
                       compiler_params=_params(("arbitrary",)))(dy, xh, rstd, g.reshape(1, W)))
    dz = outs.pop(0) if want_f32 else None
    dzb = outs.pop(0) if want_bf16 else None
    return dz, dzb, outs[0][0], outs[1][0]


def _rope(t1, t2, cos, sin, *, name):
    S, C = t1.shape
    bs = _pick(S, 512, 8)

    def body(t1_ref, t2_ref, c_ref, s_ref, o1_ref, o2_ref):
        a, b, c, s = t1_ref[...], t2_ref[...], c_ref[...], s_ref[...]
        o1_ref[...] = a * c - b * s
        o2_ref[...] = b * c + a * s

    row = pl.BlockSpec((bs, C), lambda i: (i, 0))
    return _pcall(
        body, name=name, grid=(S // bs,), in_specs=[row] * 4, out_specs=[row, row],
        out_shape=[_sds((S, C), F32)] * 2, compiler_params=_params(("parallel",)),
    )(t1, t2, cos, sin)


def _chunk_mask(row0, B, L):
    rows = lax.shift_right_logical(row0 + lax.broadcasted_iota(jnp.int32, (B, L), 0), CHUNK_SHIFT)
    cols = lax.shift_right_logical(lax.broadcasted_iota(jnp.int32, (B, L), 1), CHUNK_SHIFT)
    return rows >= cols


def _for_key_prefix(qi, B, G, S, fn):
    per = G // B
    for b in range(S // G):
        pl.when(qi // per == b)(lambda b=b: fn((b + 1) * G))


def _nt(a, b):
    return lax.dot_general(a, b, (((1,), (1,)), ((), ())), preferred_element_type=F32)


def _nn(a, b):
    return lax.dot_general(a, b, (((1,), (0,)), ((), ())), preferred_element_type=F32)


def _tn(a, b):
    return lax.dot_general(a, b, (((0,), (0,)), ((), ())), preferred_element_type=F32)


def _attn_fwd(qh, kh, vh, *, job=None, name):
    H, S, DQ = qh.shape
    DV = vh.shape[-1]
    B = _pick(S, ATTN_BLOCK, CHUNK)
    G = _pick(S, ATTN_KEY_STEP, B)
    scale = float(DQ) ** -0.5
    neg = float(jnp.finfo(jnp.float32).min)

    def body(q_ref, k_ref, v_ref, o_ref, lse_ref):
        qi = pl.program_id(1)

        def run(L):
            s = _nt(q_ref[...], k_ref[0:L, :]) * scale
            s = jnp.where(_chunk_mask(qi * B, B, L), s, neg)
            m = jnp.max(s, axis=-1, keepdims=True)
            e = jnp.exp(s - m)
            l = jnp.sum(e, axis=-1, keepdims=True)
            o_ref[...] = _nn((e / l).astype(BF16), v_ref[0:L, :])
            lse_ref[...] = m + jnp.log(l)

        _for_key_prefix(qi, B, G, S, run)

    outs, jouts = _hosted_call(
        body, job, name=name, grid=(H, S // B),
        in_specs=[pl.BlockSpec((None, B, DQ), lambda h, i: (h, i, 0)),
                  pl.BlockSpec((None, S, DQ), lambda h, i: (h, 0, 0)),
                  pl.BlockSpec((None, S, DV), lambda h, i: (h, 0, 0))],
        out_specs=[pl.BlockSpec((B, DV), lambda h, i: (i, h)),
                   pl.BlockSpec((None, B, 1), lambda h, i: (h, i, 0))],
        out_shape=[_sds((S, H * DV), F32), _sds((H, S, 1), F32)], scratch_shapes=[],
        semantics=("parallel", "parallel"), args=(qh, kh, vh))
    return (outs[0], outs[1]) if job is None else (outs[0], outs[1], jouts)


def _attn_bwd(qh, kh, vh, o, do, lse, *, do_col0, job=None, name):
    H, S, DQ = qh.shape
    DV = vh.shape[-1]
    B = _pick(S, ATTN_BLOCK, CHUNK)
    G = _pick(S, ATTN_KEY_STEP, B)
    scale = float(DQ) ** -0.5

    def body(q_ref, k_ref, v_ref, o_ref, do_ref, lse_ref, dq_ref, dk_ref, dv_ref):
        qi = pl.program_id(1)

        @pl.when(qi == 0)
        def _():
            dk_ref[...] = jnp.zeros_like(dk_ref)
            dv_ref[...] = jnp.zeros_like(dv_ref)

        def run(L):
            q = q_ref[...]
            k = k_ref[0:L, :]
            dov = do_ref[...]
            dob = dov.astype(BF16)
            p = jnp.exp(_nt(q, k) * scale - lse_ref[...])
            p = jnp.where(_chunk_mask(qi * B, B, L), p, 0.0)
            dsum = jnp.sum(dov * o_ref[...], axis=-1, keepdims=True)
            ds = (p * (_nt(dob, v_ref[0:L, :]) - dsum) * scale).astype(BF16)
            dv_ref[0:L, :] += _tn(p.astype(BF16), dob)
            dk_ref[0:L, :] += _tn(ds, q)
            dq_ref[...] = _nn(ds, k)

        _for_key_prefix(qi, B, G, S, run)

    outs, jouts = _hosted_call(
        body, job, name=name, grid=(H, S // B),
        in_specs=[pl.BlockSpec((None, B, DQ), lambda h, i: (h, i, 0)),
                  pl.BlockSpec((None, S, DQ), lambda h, i: (h, 0, 0)),
                  pl.BlockSpec((None, S, DV), lambda h, i: (h, 0, 0)),
                  pl.BlockSpec((B, DV), lambda h, i: (i, h)),
                  pl.BlockSpec((B, DV), lambda h, i: (i, do_col0 + h)),
                  pl.BlockSpec((None, B, 1), lambda h, i: (h, i, 0))],
        out_specs=[pl.BlockSpec((None, B, DQ), lambda h, i: (h, i, 0)),
                   pl.BlockSpec((None, S, DQ), lambda h, i: (h, 0, 0)),
                   pl.BlockSpec((None, S, DV), lambda h, i: (h, 0, 0))],
        out_shape=[_sds((H, S, DQ), F32), _sds((H, S, DQ), F32), _sds((H, S, DV), F32)], scratch_shapes=[],
        semantics=("parallel", "arbitrary"), args=(qh, kh, vh, o, do, lse))
    return tuple(outs) if job is None else (*outs, jouts)


def _ret_tables():
    H, L = RET_HEADS, CHUNK
    log_gamma = jnp.log1p(-jnp.exp2(-5.0 - jnp.arange(H, dtype=F32)))
    idx = jnp.arange(L, dtype=F32)
    intra = jnp.exp(log_gamma[:, None, None] * jnp.abs(idx[:, None] - idx[None, :]))
    qd = jnp.exp(log_gamma[:, None] * (idx + 1.0))[:, :, None]
    kd = jnp.exp(log_gamma[:, None] * (L - 1.0 - idx))[:, :, None]
    cd = jnp.exp(log_gamma * L)[:, None, None]
    return intra, qd, kd, cd


def _rot(t, c, s):
    half = t.shape[-1] // 2
    t1, t2 = t[:, :half], t[:, half:]
    return jnp.concatenate([t1 * c - t2 * s, t2 * c + t1 * s], axis=-1)


def _rot_t(t, c, s):
    half = t.shape[-1] // 2
    t1, t2 = t[:, :half], t[:, half:]
    return jnp.concatenate([t1 * c + t2 * s, t2 * c - t1 * s], axis=-1)


def _dot(a, b, ca, cb):
    return lax.dot_general(a.astype(BF16), b.astype(BF16), (((ca,), (cb,)), ((), ())), preferred_element_type=F32)


def _ret_fwd(hR, cos, sin, tables, *, name):
    S = hR.shape[0]
    H, L, DK, DV = RET_HEADS, CHUNK, RET_QK, RET_V
    NC = S // L
    qscale = float(DK) ** -0.5
    intra, qd, kd, cd = tables

    def body(q_ref, k_ref, v_ref, c_ref, s_ref, in_ref, qd_ref, kd_ref, cd_ref, o_ref, st_ref, state):
        @pl.when(pl.program_id(1) == 0)
        def _():
            state[...] = jnp.zeros_like(state)

        c, s = c_ref[...], s_ref[...]
        q = _rot(q_ref[...], c, s) * qscale
        k = _rot(k_ref[...], c, s)
        v = v_ref[...]
        st = state[...]
        st_ref[...] = st.astype(BF16)
        scores = _dot(q, k, 1, 1) * in_ref[...]
        o_ref[...] = _dot(scores, v, 1, 0) + _dot(q * qd_ref[...], st, 1, 0)
        state[...] = st * cd_ref[...] + _dot(k * kd_ref[...], v, 0, 0)

    blk = lambda off: pl.BlockSpec((L, DK), lambda h, c: (c, off + h))
    rope = pl.BlockSpec((L, DK // 2), lambda h, c: (c, 0))
    return _pcall(
        body, name=name, grid=(H, NC),
        in_specs=[blk(0), blk(H), blk(2 * H), rope, rope,
                  pl.BlockSpec((None, L, L), lambda h, c: (h, 0, 0)),
                  pl.BlockSpec((None, L, 1), lambda h, c: (h, 0, 0)),
                  pl.BlockSpec((None, L, 1), lambda h, c: (h, 0, 0)),
                  pl.BlockSpec((None, 1, 1), lambda h, c: (h, 0, 0))],
        out_specs=[pl.BlockSpec((L, DV), lambda h, c: (c, h)),
                   pl.BlockSpec((None, None, DK, DV), lambda h, c: (h, c, 0, 0))],
        out_shape=[_sds((S, H * DV), F32), _sds((H, NC, DK, DV), BF16)],
        scratch_shapes=[pltpu.VMEM((DK, DV), F32)],
        compiler_params=_params(("arbitrary", "arbitrary")),
    )(hR, hR, hR, cos, sin, intra, qd, kd, cd)


def _ret_bwd(do, hR, states, cos, sin, tables, *, name):
    S = hR.shape[0]
    H, L, DK, DV = RET_HEADS, CHUNK, RET_QK, RET_V
    NC = S // L
    qscale = float(DK) ** -0.5
    intra, qd, kd, cd = tables

    def body(do_ref, q_ref, k_ref, v_ref, st_ref, c_ref, s_ref, in_ref, qd_ref, kd_ref, cd_ref,
             dq_ref, dk_ref, dv_ref, dstate):
        @pl.when(pl.program_id(1) == 0)
        def _():
            dstate[...] = jnp.zeros_like(dstate)

        c, s = c_ref[...], s_ref[...]
        q = _rot(q_ref[...], c, s) * qscale
        k = _rot(k_ref[...], c, s)
        v = v_ref[...]
        dov = do_ref[...]
        dst = dstate[...]
        dec = in_ref[...]
        qdv, kdv = qd_ref[...], kd_ref[...]
        scores = _dot(q, k, 1, 1) * dec
        da = _dot(dov, v, 1, 1) * dec
        dv_ref[...] = (_dot(scores, dov, 0, 0) + _dot(k * kdv, dst, 1, 0)).astype(BF16)
        dq = _dot(da, k, 1, 0) + _dot(dov, st_ref[...], 1, 1) * qdv
        dk = _dot(da, q, 0, 0) + _dot(v, dst, 1, 1) * kdv
        dq_ref[...] = _rot_t(dq * qscale, c, s).astype(BF16)
        dk_ref[...] = _rot_t(dk, c, s).astype(BF16)
        dstate[...] = dst * cd_ref[...] + _dot(q * qdv, dov, 0, 0)

    rev = lambda c: NC - 1 - c
    blk = lambda off: pl.BlockSpec((L, DK), lambda h, c: (rev(c), off + h))
    rope = pl.BlockSpec((L, DK // 2), lambda h, c: (rev(c), 0))
    out = pl.BlockSpec((L, DK), lambda h, c: (rev(c), h))
    return _pcall(
        body, name=name, grid=(H, NC),
        in_specs=[out, blk(0), blk(H), blk(2 * H),
                  pl.BlockSpec((None, None, DK, DV), lambda h, c: (h, rev(c), 0, 0)), rope, rope,
                  pl.BlockSpec((None, L, L), lambda h, c: (h, 0, 0)),
                  pl.BlockSpec((None, L, 1), lambda h, c: (h, 0, 0)),
                  pl.BlockSpec((None, L, 1), lambda h, c: (h, 0, 0)),
                  pl.BlockSpec((None, 1, 1), lambda h, c: (h, 0, 0))],
        out_specs=[out, out, out],
        out_shape=[_sds((S, H * DK), BF16)] * 3,
        scratch_shapes=[pltpu.VMEM((DK, DV), F32)],
        compiler_params=_params(("arbitrary", "arbitrary")),
    )(do, hR, hR, hR, states, cos, sin, intra, qd, kd, cd)


def _gn_gate_fwd(o, hR, g, b, *, name):
    S = o.shape[0]
    H, DV = RET_HEADS, RET_V
    bs = _pick(S, 512, 16)

    def body(o_ref, rg_ref, g_ref, b_ref, r_ref):
        z = o_ref[...]
        z = z - jnp.mean(z, axis=-1, keepdims=True)
        xh = z * lax.rsqrt(jnp.mean(z * z, axis=-1, keepdims=True) + GN_EPS)
        rg = rg_ref[...]
        r_ref[...] = ((rg * jax.nn.sigmoid(rg)) * (xh * g_ref[...] + b_ref[...])).astype(BF16)

    row = pl.BlockSpec((bs, DV), lambda i, h: (i, h))
    vec = pl.BlockSpec((1, DV), lambda i, h: (0, h))
    return _pcall(
        body, name=name, grid=(S // bs, H),
        in_specs=[row, pl.BlockSpec((bs, DV), lambda i, h: (i, 3 * H + h)), vec, vec],
        out_specs=row, out_shape=_sds((S, H * DV), BF16),
        compiler_params=_params(("parallel", "parallel")),
    )(o, hR, g.reshape(1, H * DV), b.reshape(1, H * DV))


def _gn_gate_bwd(dr, o, hR, g, b, *, dr_col0, name):
    S = o.shape[0]
    H, DV = RET_HEADS, RET_V
    bs = _pick(S, 512, 16)

    def body(dr_ref, o_ref, rg_ref, g_ref, b_ref, do_ref, drg_ref, dg_ref, db_ref):
        @pl.when(pl.program_id(1) == 0)
        def _():
            dg_ref[...] = jnp.zeros_like(dg_ref)
            db_ref[...] = jnp.zeros_like(db_ref)

        z = o_ref[...]
        z = z - jnp.mean(z, axis=-1, keepdims=True)
        rstd = lax.rsqrt(jnp.mean(z * z, axis=-1, keepdims=True) + GN_EPS)
        xh = z * rstd
        gv = g_ref[...]
        y = xh * gv + b_ref[...]
        rg = rg_ref[...]
        sg = jax.nn.sigmoid(rg)
        drv = dr_ref[...]
        dy = drv * (rg * sg)
        drg_ref[...] = (drv * y * (sg * (1.0 + rg * (1.0 - sg)))).astype(BF16)
        dg_ref[...] += jnp.sum(dy * xh, axis=0, keepdims=True)
        db_ref[...] += jnp.sum(dy, axis=0, keepdims=True)
        dxh = dy * gv
        do_ref[...] = rstd * (dxh - jnp.mean(dxh, axis=-1, keepdims=True)
                              - xh * jnp.mean(dxh * xh, axis=-1, keepdims=True))

    row = pl.BlockSpec((bs, DV), lambda h, i: (i, h))
    vec = pl.BlockSpec((1, DV), lambda h, i: (0, h))
    do, drg, dg, db = _pcall(
        body, name=name, grid=(H, S // bs),
        in_specs=[pl.BlockSpec((bs, DV), lambda h, i: (i, dr_col0 + h)), row,
                  pl.BlockSpec((bs, DV), lambda h, i: (i, 3 * H + h)), vec, vec],
        out_specs=[row, row, vec, vec],
        out_shape=[_sds((S, H * DV), F32), _sds((S, H * DV), BF16), _sds((1, H * DV), F32), _sds((1, H * DV), F32)],
        compiler_params=_params(("arbitrary", "arbitrary")),
    )(dr, o, hR, g.reshape(1, H * DV), b.reshape(1, H * DV))
    return do, drg, dg[0], db[0]


def _swiglu_fwd(g, u, *, name):
    S, F = g.shape
    bs, bf = _pick(S, 512, 16), _pick(F, 1408, 128)

    def body(g_ref, u_ref, a_ref):
        gv = g_ref[...]
        a_ref[...] = ((gv * jax.nn.sigmoid(gv)) * u_ref[...]).astype(BF16)

    blk = pl.BlockSpec((bs, bf), lambda i, j: (i, j))
    return _pcall(body, name=name, grid=(S // bs, F // bf), in_specs=[blk, blk], out_specs=blk,
                  out_shape=_sds((S, F), BF16), compiler_params=_params(("parallel", "parallel")))(g, u)


def _swiglu_bwd(da, g, u, *, name):
    S, F = g.shape
    bs, bf = _pick(S, 512, 16), _pick(F, 1408, 128)

    def body(da_ref, g_ref, u_ref, dg_ref, du_ref):
        gv, dav = g_ref[...], da_ref[...]
        sg = jax.nn.sigmoid(gv)
        du_ref[...] = (dav * (gv * sg)).astype(BF16)
        dg_ref[...] = (dav * u_ref[...] * (sg * (1.0 + gv * (1.0 - sg)))).astype(BF16)

    blk = pl.BlockSpec((bs, bf), lambda i, j: (i, j))
    return _pcall(body, name=name, grid=(S // bs, F // bf), in_specs=[blk] * 3, out_specs=[blk, blk],
                  out_shape=[_sds((S, F), BF16)] * 2, compiler_params=_params(("parallel", "parallel")))(da, g, u)


def _loss_head(y, t, *, name):
    S, D = y.shape
    bs = _pick(S, 256, 8)
    inv_d = 1.0 / D

    def body(y_ref, t_ref, dy_ref, l_ref):
        @pl.when(pl.program_id(0) == 0)
        def _():
            l_ref[...] = jnp.zeros_like(l_ref)

        e = y_ref[...] - t_ref[...]
        dy_ref[...] = e * inv_d
        l_ref[...] += 0.5 * jnp.sum(jnp.mean(e * e, axis=-1, keepdims=True), axis=0, keepdims=True)

    row = pl.BlockSpec((bs, D), lambda i: (i, 0))
    dy, l = _pcall(
        body, name=name, grid=(S // bs,), in_specs=[row, row],
        out_specs=[row, pl.BlockSpec((1, 1), lambda i: (0, 0))],
        out_shape=[_sds((S, D), F32), _sds((1, 1), F32)],
        compiler_params=_params(("arbitrary",)),
    )(y, t)
    return dy, l[0, 0]


def _adamw(w, g, m, v, *, name):
    shape = w.shape
    C = shape[-1]
    R = w.size // C
    br = _pick(R, 512, 8)

    def body(w_ref, g_ref, m_ref, v_ref, d_ref, nm_ref, nv_ref):
        gv = g_ref[...]
        mn = ADAM_B1 * m_ref[...] + (1.0 - ADAM_B1) * gv
        vn = ADAM_B2 * v_ref[...] + (1.0 - ADAM_B2) * (gv * gv)
        m_hat = mn / (1.0 - ADAM_B1 ** ADAM_STEP)
        v_hat = vn / (1.0 - ADAM_B2 ** ADAM_STEP)
        d_ref[...] = -ADAM_LR * (m_hat / (jnp.sqrt(v_hat) + ADAM_EPS) + ADAM_WD * w_ref[...])
        nm_ref[...] = mn
        nv_ref[...] = vn

    blk = pl.BlockSpec((br, C), lambda i: (i, 0))
    outs = _pcall(body, name=name, grid=(R // br,), in_specs=[blk] * 4, out_specs=[blk] * 3,
                  out_shape=[_sds((R, C), F32)] * 3,
                  compiler_params=_params(("parallel",)))(*[a.reshape(R, C) for a in (w, g, m, v)])
    return tuple(o.reshape(shape) for o in outs)


def _sum_slots(x, *, name):
    shape = x.shape[1:]
    total = x[0].size
    C = next(c for c in (2048, 1024, 512, 256, 128) if total % (16 * c) == 0)
    R = total // C
    x = x.reshape(N_DEV, R, C)
    br = _pick(R, 256, 16)

    def body(x_ref, o_ref):
        acc = x_ref[0].astype(F32)
        for s in range(1, N_DEV):
            acc = acc + x_ref[s].astype(F32)
        o_ref[...] = acc

    return _pcall(body, name=name, grid=(R // br,),
                  in_specs=[pl.BlockSpec((N_DEV, br, C), lambda i: (0, i, 0))],
                  out_specs=pl.BlockSpec((br, C), lambda i: (i, 0)),
                  out_shape=_sds((R, C), F32), compiler_params=_params(("parallel",)))(x).reshape(shape)


def _sum_heads(dkh, *, name):
    H, S, DQ = dkh.shape
    bs = _pick(S, 512, 8)

    def body(d_ref, o_ref):
        acc = d_ref[0][:, MLA_NOPE:]
        for h in range(1, H):
            acc = acc + d_ref[h][:, MLA_NOPE:]
        o_ref[...] = acc

    return _pcall(body, name=name, grid=(S // bs,),
                  in_specs=[pl.BlockSpec((H, bs, DQ), lambda i: (0, i, 0))],
                  out_specs=pl.BlockSpec((bs, MLA_ROPE), lambda i: (i, 0)),
                  out_shape=_sds((S, MLA_ROPE), F32), compiler_params=_params(("parallel",)))(dkh)


N_PEER = N_DEV - 1


def _coords():
    return lax.axis_index("x"), lax.axis_index("y"), lax.axis_index("c")


def _ag_phase(phase, x_refs, out_refs, send_sems, recv_sems, local_sems):
    n = len(x_refs)
    x, y, c = _coords()
    me, sibling = (x, y, c), (x, y, 1 - c)
    chips = [(1 - x, y), (x, 1 - y), (1 - x, 1 - y)]

    def copy(a, k, block, to, src=None):
        px, py, pc = block
        dst = out_refs[a].at[4 * px + 2 * py + pc]
        return pltpu.make_async_remote_copy(
            src_ref=dst if src is None else src, dst_ref=dst,
            send_sem=send_sems.at[a * N_PEER + k], recv_sem=recv_sems.at[a * N_PEER + k],
            device_id=to, device_id_type=pl.DeviceIdType.MESH)

    def local(a):
        return pltpu.make_async_copy(x_refs[a], out_refs[a].at[4 * x + 2 * y + c], local_sems.at[a])

    def first(a):
        return ([copy(a, 0, me, sibling, src=x_refs[a])]
                + [copy(a, 1 + j, me, (*chip, c), src=x_refs[a]) for j, chip in enumerate(chips)])

    def passed(a, j):
        return copy(a, 4 + j, (*chips[j], c), sibling)

    if phase == 0:
        for a in range(n):
            local(a).start()
            for cp in first(a):
                cp.start()
    elif phase == 1:
        for j in range(len(chips)):
            for a in range(n):
                copy(a, 1 + j, (*chips[j], c), me).wait_recv()
                passed(a, j).start()
    else:
        for a in range(n):
            copy(a, 0, sibling, me).wait_recv()
            for j in range(len(chips)):
                copy(a, 4 + j, (*chips[j], 1 - c), me).wait_recv()
        for a in range(n):
            for cp in first(a):
                cp.wait_send()
            for j in range(len(chips)):
                passed(a, j).wait_send()
            local(a).wait()


def _a2a_phase(phase, g_refs, out_refs, send_sems, recv_sems, local_sems):
    n = len(g_refs)
    x, y, c = _coords()
    me = 4 * x + 2 * y + c

    def local(a):
        return pltpu.make_async_copy(g_refs[a].at[me], out_refs[a].at[me], local_sems.at[a])

    def pair(a, k):
        px = 1 - x if k & 4 else x
        py = 1 - y if k & 2 else y
        pc = 1 - c if k & 1 else c
        peer = 4 * px + 2 * py + pc
        sem = a * N_PEER + k - 1
        send = pltpu.make_async_remote_copy(
            src_ref=g_refs[a].at[peer], dst_ref=out_refs[a].at[me], send_sem=send_sems.at[sem],
            recv_sem=recv_sems.at[sem], device_id=(px, py, pc), device_id_type=pl.DeviceIdType.MESH)
        recv = pltpu.make_async_remote_copy(
            src_ref=g_refs[a].at[peer], dst_ref=out_refs[a].at[peer], send_sem=send_sems.at[sem],
            recv_sem=recv_sems.at[sem], device_id=(px, py, pc), device_id_type=pl.DeviceIdType.MESH)
        return send, recv

    if phase == 0:
        for a in range(n):
            local(a).start()
        for k in range(1, N_DEV):
            for a in range(n):
                pair(a, k)[0].start()
    elif phase == 2:
        for k in range(1, N_DEV):
            for a in range(n):
                pair(a, k)[1].wait_recv()
        for k in range(1, N_DEV):
            for a in range(n):
                pair(a, k)[0].wait_send()
        for a in range(n):
            local(a).wait()


def _job(kind, arrays):
    arrays = list(arrays)
    if kind == "ag":
        return dict(phase=_ag_phase, ins=arrays, outs=[_sds((N_DEV,) + a.shape, a.dtype) for a in arrays])
    return dict(phase=_a2a_phase, ins=arrays, outs=[_sds(a.shape, a.dtype) for a in arrays])


def _hosted_call(body, job, *, name, grid, in_specs, out_specs, out_shape, scratch_shapes, semantics, args):
    if job is None:
        outs = _pcall(body, name=name, grid=grid, in_specs=in_specs, out_specs=out_specs, out_shape=out_shape,
                      scratch_shapes=scratch_shapes, compiler_params=_params(semantics))(*args)
        return outs, []
    n_in, n_out, n_scr, nj = len(in_specs), len(out_specs), len(scratch_shapes), len(job["ins"])
    total = 1
    for g in grid:
        total *= g

    def wrapped(*refs):
        ins, refs = refs[:n_in], refs[n_in:]
        jins, refs = refs[:nj], refs[nj:]
        outs, refs = refs[:n_out], refs[n_out:]
        jouts, refs = refs[:nj], refs[nj:]
        scr, sems = refs[:n_scr], refs[n_scr:]
        step = 0
        for d, g in enumerate(grid):
            step = step * g + pl.program_id(d)

        @pl.when(step == 0)
        def _():
            job["phase"](0, jins, jouts, *sems)

        @pl.when(step == total // 2)
        def _():
            job["phase"](1, jins, jouts, *sems)

        body(*ins, *outs, *scr)

        @pl.when(step == total - 1)
        def _():
            job["phase"](2, jins, jouts, *sems)

    any_spec = pl.BlockSpec(memory_space=pl.ANY)
    sems = [pltpu.SemaphoreType.DMA((nj * N_PEER,)), pltpu.SemaphoreType.DMA((nj * N_PEER,)),
            pltpu.SemaphoreType.DMA((nj,))]
    outs = _pcall(
        wrapped, name=name, grid=grid, in_specs=list(in_specs) + [any_spec] * nj,
        out_specs=list(out_specs) + [any_spec] * nj, out_shape=list(out_shape) + job["outs"],
        scratch_shapes=list(scratch_shapes) + sems,
        compiler_params=_params(("arbitrary",) * len(grid)),
    )(*args, *job["ins"])
    return outs[:n_out], outs[n_out:]


def _exchange(job, *, name):
    outs, jouts = _hosted_call(lambda: None, job, name=name, grid=(1,), in_specs=[], out_specs=[], out_shape=[],
                               scratch_shapes=[], semantics=("arbitrary",), args=())
    return jouts


_TRANSPOSED = {"w_in": True, "w_out": False, "w_gate": True, "w_up": True, "w_down": False, "w_uq": True, "w_ukv": True}
_BIG = tuple(_TRANSPOSED)
_SMALL = ("ln_in_g", "ln_in_b", "q_norm_g", "kv_norm_g", "ret_gn_g", "ret_gn_b", "ln1_g", "ln1_b", "ln2_g", "ln2_b")
SMALL_COLS = 128


def _rope_tables(pos, dim):
    inv_freq = ROPE_THETA ** (-jnp.arange(0, dim, 2, dtype=F32) / dim)
    ang = pos.astype(F32)[:, None] * inv_freq
    return jnp.cos(ang), jnp.sin(ang)


def _split_in(wt_in):
    a, b, c = MLA_Q_LORA, MLA_Q_LORA + MLA_KV_LORA, MLA_Q_LORA + MLA_KV_LORA + MLA_ROPE
    return wt_in[:a], wt_in[a:b], wt_in[b:c], wt_in[c:]


def _mla_pack_rope(q, kr):
    S = q.shape[0]
    H, hr = MLA_HEADS, MLA_ROPE // 2
    q3 = q.reshape(S, H, MLA_QK)
    t1 = jnp.concatenate([q3[:, :, MLA_NOPE:MLA_NOPE + hr].reshape(S, H * hr), kr[:, :hr]], axis=1)
    t2 = jnp.concatenate([q3[:, :, MLA_NOPE + hr:].reshape(S, H * hr), kr[:, hr:]], axis=1)
    return q3[:, :, :MLA_NOPE], t1, t2


def _mla_unpack_rope(q_nope, o1, o2):
    S = o1.shape[0]
    H, hr = MLA_HEADS, MLA_ROPE // 2
    q3 = jnp.concatenate([q_nope, o1[:, :H * hr].reshape(S, H, hr), o2[:, :H * hr].reshape(S, H, hr)], axis=-1)
    kr = jnp.concatenate([o1[:, H * hr:], o2[:, H * hr:]], axis=1)
    return q3, kr


class _Plan:
    AG_FIRST = ((0, "w_in"), (0, "w_uq"), (0, "w_ukv"), (0, "w_out"))
    AG_HOSTS = {
        "l0_hR": ((0, "w_gate"),), "l0_attn": ((0, "w_up"), (0, "w_down")),
        "l0_gate": ((1, "w_in"), (1, "w_uq"), (1, "w_ukv")), "l0_up": ((1, "w_gate"),), "l0_down": ((1, "w_up"),),
        "l1_hR": ((1, "w_out"),), "l1_attn": ((1, "w_down"),)}
    A2A_HOSTS = {
        "l1_b_wgate": ((1, "w_down"),), "l1_b_dx1a": ((1, "w_gate"),), "l1_b_dx1b": ((1, "w_up"),),
        "l1_b_attn": ((1, "w_out"),), "l1_b_win": ((1, "w_uq"), (1, "w_ukv")), "l0_b_dact": ((1, "w_in"),),
        "l0_b_wgate": ((0, "w_down"),), "l0_b_dx1a": ((0, "w_gate"),), "l0_b_dx1b": ((0, "w_up"),),
        "l0_b_attn": ((0, "w_out"),), "l0_b_win": ((0, "w_uq"), (0, "w_ukv"))}
    A2A_LAST = ((0, "w_in"),)

    def __init__(self, local):
        self.local = local
        self.full = {}
        self.grads = {}
        self.recv = {}

    def _ag_job(self, keys):
        return _job("ag", [self.local[k] for k in keys])

    def _ag_done(self, keys, outs):
        for k, g in zip(keys, outs):
            self.full[k] = g.reshape(N_DEV * g.shape[1], g.shape[2])

    def _a2a_job(self, keys):
        return _job("a2a", [self.grads[k].reshape((N_DEV,) + self.local[k].shape) for k in keys])

    def _a2a_done(self, keys, outs):
        self.recv.update(zip(keys, outs))

    def gather_first(self):
        self._ag_done(self.AG_FIRST, _exchange(self._ag_job(self.AG_FIRST), name="ag_first"))

    def send_last(self):
        self._a2a_done(self.A2A_LAST, _exchange(self._a2a_job(self.A2A_LAST), name="a2a_last"))

    def call(self, fn, name, *args, **kw):
        if name in self.AG_HOSTS:
            keys, make, done = self.AG_HOSTS[name], self._ag_job, self._ag_done
        elif name in self.A2A_HOSTS:
            keys, make, done = self.A2A_HOSTS[name], self._a2a_job, self._a2a_done
        else:
            return fn(*args, name=name, **kw)
        *outs, jouts = fn(*args, name=name, job=make(keys), **kw)
        done(keys, jouts)
        return outs[0] if len(outs) == 1 else tuple(outs)


def _layer_fwd(x, xb, plan, p, tabs, l):
    S, D = x.shape
    H = MLA_HEADS
    nm = lambda s: "l%d_%s" % (l, s)
    w = lambda n: plan.full[(l, n)]
    mm = lambda name, *a, **kw: plan.call(_mm, nm(name), *a, **kw)
    wt_q, wt_kv, wt_kr, wt_r = _split_in(w("w_in"))
    cq = mm("cq", xb, wt_q, tb=True)
    ckv = mm("ckv", xb, wt_kv, tb=True)
    kr = mm("krope", xb, wt_kr, tb=True)
    hR = mm("hR", xb, wt_r, tb=True)
    _, qn, qn_hat, q_rstd = _norm_fwd(cq, p["q_norm_g"], None, center=False, eps=RMS_EPS, want_f32=False,
                                      name=nm("qnorm"))
    _, kvn, kvn_hat, kv_rstd = _norm_fwd(ckv, p["kv_norm_g"], None, center=False, eps=RMS_EPS, want_f32=False,
                                         name=nm("kvnorm"))
    q = mm("uq", qn, w("w_uq"), tb=True)
    kv = mm("ukv", kvn, w("w_ukv"), tb=True)
    q_nope, t1, t2 = _mla_pack_rope(q, kr)
    o1, o2 = _rope(t1, t2, tabs["cos_m"], tabs["sin_m"], name=nm("rope"))
    q3, k_r = _mla_unpack_rope(q_nope, o1, o2)
    kv3 = kv.reshape(S, H, MLA_NOPE + MLA_V)
    qh = q3.transpose(1, 0, 2).astype(BF16)
    kh = jnp.concatenate([kv3[:, :, :MLA_NOPE], jnp.broadcast_to(k_r[:, None, :], (S, H, MLA_ROPE))],
                         axis=-1).transpose(1, 0, 2).astype(BF16)
    vh = kv3[:, :, MLA_NOPE:].transpose(1, 0, 2).astype(BF16)
    a, lse = plan.call(_attn_fwd, nm("attn"), qh, kh, vh)
    o_ret, states = _ret_fwd(hR, tabs["cos_r"], tabs["sin_r"], tabs["ret"], name=nm("ret"))
    r = _gn_gate_fwd(o_ret, hR, p["ret_gn_g"], p["ret_gn_b"], name=nm("gn"))
    mix_in = jnp.concatenate([a.astype(BF16), r], axis=1)
    mix = mm("out", mix_in, w("w_out"))
    x1, x1b, x1_hat, rstd1 = _norm_fwd(x, p["ln1_g"], p["ln1_b"], res=mix, alpha=p["alpha"], eps=LN_EPS, name=nm("ln1"))
    g = mm("gate", x1b, w("w_gate"), tb=True)
    u = mm("up", x1b, w("w_up"), tb=True)
    act = _swiglu_fwd(g, u, name=nm("swiglu"))
    f = mm("down", act, w("w_down"))
    x2, x2b, x2_hat, rstd2 = _norm_fwd(x1, p["ln2_g"], p["ln2_b"], res=f, alpha=p["alpha"], eps=LN_EPS, name=nm("ln2"))
    saved = dict(xb=xb, qn=qn, qn_hat=qn_hat, q_rstd=q_rstd, kvn=kvn, kvn_hat=kvn_hat, kv_rstd=kv_rstd,
                 qh=qh, kh=kh, vh=vh, a=a, lse=lse, hR=hR, o_ret=o_ret, states=states, mix_in=mix_in,
                 x1b=x1b, x1_hat=x1_hat, rstd1=rstd1, g=g, u=u, act=act, x2_hat=x2_hat, rstd2=rstd2)
    return x2, x2b, saved


def _layer_bwd(dx2, sv, plan, p, tabs, l):
    S, D = dx2.shape
    H = MLA_HEADS
    nm = lambda s: "l%d_b_%s" % (l, s)
    w = lambda n: plan.full[(l, n)]
    mm = lambda name, *a, **kw: plan.call(_mm, nm(name), *a, **kw)
    alpha = p["alpha"]
    gw, gp = plan.grads, {}
    dz2, dz2b, gp["ln2_g"], gp["ln2_b"] = _norm_bwd(dx2, sv["x2_hat"], sv["rstd2"], p["ln2_g"], name=nm("ln2"))
    dact = mm("dact", dz2b, w("w_down"), tb=True)
    gw[(l, "w_down")] = mm("wdown", sv["act"], dz2b, ta=True, out_dtype=BF16)
    dg, du = _swiglu_bwd(dact, sv["g"], sv["u"], name=nm("swiglu"))
    gw[(l, "w_gate")] = mm("wgate", dg, sv["x1b"], ta=True, out_dtype=BF16)
    gw[(l, "w_up")] = mm("wup", du, sv["x1b"], ta=True, out_dtype=BF16)
    t = mm("dx1a", dg, w("w_gate"), add=dz2, add_scale=alpha)
    dx1 = mm("dx1b", du, w("w_up"), add=t)
    dz1, dz1b, gp["ln1_g"], gp["ln1_b"] = _norm_bwd(dx1, sv["x1_hat"], sv["rstd1"], p["ln1_g"], name=nm("ln1"))
    dmix = mm("dmix", dz1b, w("w_out"), tb=True)
    gw[(l, "w_out")] = mm("wout", sv["mix_in"], dz1b, ta=True, out_dtype=BF16)
    ret_col0 = (H * MLA_V) // RET_V
    do_ret, drg, gp["ret_gn_g"], gp["ret_gn_b"] = _gn_gate_bwd(
        dmix, sv["o_ret"], sv["hR"], p["ret_gn_g"], p["ret_gn_b"], dr_col0=ret_col0, name=nm("gn"))
    drq, drk, drv = _ret_bwd(do_ret, sv["hR"], sv["states"], tabs["cos_r"], tabs["sin_r"], tabs["ret"], name=nm("ret"))
    dqh, dkh, dvh = plan.call(_attn_bwd, nm("attn"), sv["qh"], sv["kh"], sv["vh"], sv["a"], dmix, sv["lse"], do_col0=0)
    dq3 = dqh.transpose(1, 0, 2)
    dk_r = _sum_heads(dkh, name=nm("dkr"))
    dq_nope, dt1, dt2 = _mla_pack_rope(dq3.reshape(S, H * MLA_QK), dk_r)
    di1, di2 = _rope(dt1, dt2, tabs["cos_m"], tabs["nsin_m"], name=nm("rope"))
    dq3, dkr = _mla_unpack_rope(dq_nope, di1, di2)
    dq = dq3.reshape(S, H * MLA_QK).astype(BF16)
    dkv = jnp.concatenate([dkh[:, :, :MLA_NOPE].transpose(1, 0, 2), dvh.transpose(1, 0, 2)],
                          axis=-1).reshape(S, -1).astype(BF16)
    gw[(l, "w_uq")] = mm("wuq", dq, sv["qn"], ta=True, out_dtype=BF16)
    dqn = mm("dqn", dq, w("w_uq"))
    gw[(l, "w_ukv")] = mm("wukv", dkv, sv["kvn"], ta=True, out_dtype=BF16)
    dkvn = mm("dkvn", dkv, w("w_ukv"))
    _, dcq, gp["q_norm_g"], _ = _norm_bwd(dqn, sv["qn_hat"], sv["q_rstd"], p["q_norm_g"], center=False,
                                          want_f32=False, name=nm("qnorm"))
    _, dckv, gp["kv_norm_g"], _ = _norm_bwd(dkvn, sv["kvn_hat"], sv["kv_rstd"], p["kv_norm_g"], center=False,
                                            want_f32=False, name=nm("kvnorm"))
    dh = jnp.concatenate([dcq, dckv, dkr.astype(BF16), drq, drk, drv, drg], axis=1)
    gw[(l, "w_in")] = mm("win", dh, sv["xb"], ta=True, out_dtype=BF16)
    dx = mm("dx", dh, w("w_in"), add=dz1, add_scale=alpha)
    return dx, gp


def _local_step(x, target, pos, small, plan, depth):
    alpha = (2 * depth) ** 0.25
    cos_m, sin_m = _rope_tables(pos, MLA_ROPE)
    cos_r, sin_r = _rope_tables(pos, RET_QK)
    reps = MLA_HEADS + 1
    tabs = dict(cos_m=jnp.tile(cos_m, (1, reps)), sin_m=jnp.tile(sin_m, (1, reps)),
                nsin_m=jnp.tile(-sin_m, (1, reps)), cos_r=cos_r, sin_r=sin_r, ret=_ret_tables())
    h, hb, h_hat, h_rstd = _norm_fwd(x, small["ln_in_g"], small["ln_in_b"], eps=LN_EPS, name="ln_in")
    saved, ps = [], []
    for l in range(depth):
        p = {k: small[k][l] for k in _SMALL[2:]}
        p["alpha"] = alpha
        h, hb, sv = _layer_fwd(h, hb, plan, p, tabs, l)
        saved.append(sv)
        ps.append(p)
    dy, loss = _loss_head(h, target, name="loss")
    gps = [None] * depth
    for l in reversed(range(depth)):
        dy, gps[l] = _layer_bwd(dy, saved[l], plan, ps[l], tabs, l)
    grad_x, _, g_in_g, g_in_b = _norm_bwd(dy, h_hat, h_rstd, small["ln_in_g"], want_bf16=False, name="b_ln_in")
    gsmall = {"ln_in_g": g_in_g, "ln_in_b": g_in_b}
    for k in _SMALL[2:]:
        gsmall[k] = jnp.stack([gps[l][k] for l in range(depth)])
    return loss, grad_x, gsmall


def kernel(x, positions, ln_in_g, ln_in_b, w_in, q_norm_g, kv_norm_g, w_uq, w_ukv, ret_gn_g, ret_gn_b, w_out, ln1_g, ln1_b, w_gate, w_up, w_down, ln2_g, ln2_b, loss_target, m_ln_in_g, m_ln_in_b, m_w_in, m_q_norm_g, m_kv_norm_g, m_w_uq, m_w_ukv, m_ret_gn_g, m_ret_gn_b, m_w_out, m_ln1_g, m_ln1_b, m_w_gate, m_w_up, m_w_down, m_ln2_g, m_ln2_b, v_ln_in_g, v_ln_in_b, v_w_in, v_q_norm_g, v_kv_norm_g, v_w_uq, v_w_ukv, v_ret_gn_g, v_ret_gn_b, v_w_out, v_ln1_g, v_ln1_b, v_w_gate, v_w_up, v_w_down, v_ln2_g, v_ln2_b):
    names = ["ln_in_g", "ln_in_b", "w_in", "q_norm_g", "kv_norm_g", "w_uq", "w_ukv", "ret_gn_g", "ret_gn_b", "w_out",
             "ln1_g", "ln1_b", "w_gate", "w_up", "w_down", "ln2_g", "ln2_b"]
    wv = dict(zip(names, (ln_in_g, ln_in_b, w_in, q_norm_g, kv_norm_g, w_uq, w_ukv, ret_gn_g, ret_gn_b, w_out,
                          ln1_g, ln1_b, w_gate, w_up, w_down, ln2_g, ln2_b)))
    mv = dict(zip(names, (m_ln_in_g, m_ln_in_b, m_w_in, m_q_norm_g, m_kv_norm_g, m_w_uq, m_w_ukv, m_ret_gn_g,
                          m_ret_gn_b, m_w_out, m_ln1_g, m_ln1_b, m_w_gate, m_w_up, m_w_down, m_ln2_g, m_ln2_b)))
    vv = dict(zip(names, (v_ln_in_g, v_ln_in_b, v_w_in, v_q_norm_g, v_kv_norm_g, v_w_uq, v_w_ukv, v_ret_gn_g,
                          v_ret_gn_b, v_w_out, v_ln1_g, v_ln1_b, v_w_gate, v_w_up, v_w_down, v_ln2_g, v_ln2_b)))
    depth = w_in.shape[0]
    assert depth == 2, "the exchange plan is written for two layers"

    keys = [(l, n) for l in range(depth) for n in _BIG]
    plan = _Plan({(l, n): (wv[n][l].T if _TRANSPOSED[n] else wv[n][l]).astype(BF16) for l, n in keys})
    plan.gather_first()

    small = {n: wv[n] for n in _SMALL}
    loss, grad_x, gsmall = _local_step(x[0], loss_target[0], positions[0], small, plan, depth)
    loss = lax.psum(loss, MESH_AXES)
    plan.send_last()

    gshard = {n: [None] * depth for n in _BIG}
    for l, n in keys:
        tot = _sum_slots(plan.recv[(l, n)], name="sum_l%d_%s" % (l, n))
        gshard[n][l] = tot.T if _TRANSPOSED[n] else tot
    grads = {n: jnp.stack(v) for n, v in gshard.items()}

    flat = jnp.concatenate([gsmall[n].reshape(-1) for n in _SMALL])
    n_small = flat.shape[0]
    rows = -(-n_small // (SMALL_COLS * 8)) * 8
    flat = jnp.pad(flat, (0, rows * SMALL_COLS - n_small)).reshape(rows, SMALL_COLS)
    tot = _sum_slots(_exchange(_job("ag", [flat]), name="ag_small")[0], name="sum_small").reshape(-1)
    off = 0
    for n in _SMALL:
        grads[n] = tot[off:off + wv[n].size].reshape(wv[n].shape)
        off += wv[n].size

    delta, new_m, new_v = {}, {}, {}
    for n in names:
        w2 = wv[n] if wv[n].ndim > 1 else wv[n].reshape(1, -1)
        d, nm_, nv_ = _adamw(w2, grads[n].reshape(w2.shape), mv[n].reshape(w2.shape), vv[n].reshape(w2.shape),
                             name="adamw_" + n)
        delta[n], new_m[n], new_v[n] = d.reshape(wv[n].shape), nm_.reshape(wv[n].shape), nv_.reshape(wv[n].shape)

    return (loss, grad_x[None], *[grads[n] for n in names], *[delta[n] for n in names],
            *[new_m[n] for n in names], *[new_v[n] for n in names])
```

```python
import jax
import jax.numpy as jnp
from jax import lax
from jax.experimental import pallas as pl
from jax.experimental.pallas import tpu as pltpu

F32 = jnp.float32
BF16 = jnp.bfloat16

CHUNK = 64
CHUNK_SHIFT = 6
MLA_HEADS = 8
MLA_Q_LORA = 512
MLA_KV_LORA = 256
MLA_NOPE = 128
MLA_ROPE = 64
MLA_V = 128
MLA_QK = MLA_NOPE + MLA_ROPE
RET_HEADS = 4
RET_QK = 256
RET_V = 256
ROPE_THETA = 10000.0
LN_EPS = 1e-5
RMS_EPS = 1e-6
GN_EPS = 1e-5
ADAM_LR = 0.001
ADAM_B1 = 0.9
ADAM_B2 = 0.999
ADAM_EPS = 1e-08
ADAM_WD = 0.01
ADAM_STEP = 10

N_DEV = 8
MESH_AXES = ("x", "y", "c")
VMEM_LIMIT_BYTES = 56 * 1024 * 1024
MM_VMEM_BUDGET = 40 * 1024 * 1024
MM_ACC_PENALTY = 0.85
ATTN_BLOCK = 256
ATTN_KEY_STEP = 512
RET_BLOCK = 256


def _pick(n, pref, mult):
    best = None
    d = mult
    while d <= min(n, pref):
        if n % d == 0:
            best = d
        d += mult
    return n if best is None else best


def _divisors(n, mult, cap):
    ds = [d for d in range(mult, min(n, cap) + 1, mult) if n % d == 0]
    return ds or [n]


def _pcall(body, **kw):
    return pl.pallas_call(body, **kw)


def _params(sem):
    return pltpu.CompilerParams(dimension_semantics=sem, vmem_limit_bytes=VMEM_LIMIT_BYTES)


def _sds(shape, dtype):
    return jax.ShapeDtypeStruct(shape, dtype)


def _mm_tiles(M, N, K, ta, sa, sb, so, has_add):
    best = None
    for bk in _divisors(K, 128, 8192):
        nk = K // bk
        for bm in _divisors(M, 128 if ta else 16, 1024):
            for bn in _divisors(N, 128, 1024):
                vmem = 2 * (bm * bk * sa + bk * bn * sb) + 2 * bm * bn * so
                vmem += (2 * bm * bn * 4 if has_add else 0) + (bm * bn * 4 if nk > 1 else 0)
                if vmem > MM_VMEM_BUDGET:
                    continue
                flops_per_byte = 1.0 / (1.0 / bm + (1.0 / max(N, bn) if nk == 1 else 1.0 / bn))
                score = (flops_per_byte * (1.0 if nk == 1 else MM_ACC_PENALTY), bn, bm)
                if best is None or score > best[0]:
                    best = (score, bm, bn, bk)
    assert best is not None, (M, N, K)
    return best[1:]


def _mm(a, b, *, ta=False, tb=False, add=None, add_scale=1.0, out_dtype=F32, job=None, name):
    if ta:
        K, M = a.shape
    else:
        M, K = a.shape
    if tb:
        N, K2 = b.shape
    else:
        K2, N = b.shape
    assert K == K2, (a.shape, b.shape, ta, tb)
    has_add = add is not None
    bm, bn, bk = _mm_tiles(M, N, K, ta, a.dtype.itemsize, b.dtype.itemsize, jnp.dtype(out_dtype).itemsize, has_add)
    nk = K // bk
    a_spec = (pl.BlockSpec((bk, bm), lambda i, j, k: (k, i)) if ta
              else pl.BlockSpec((bm, bk), lambda i, j, k: (i, k)))
    b_spec = (pl.BlockSpec((bn, bk), lambda i, j, k: (j, k)) if tb
              else pl.BlockSpec((bk, bn), lambda i, j, k: (k, j)))
    o_spec = pl.BlockSpec((bm, bn), lambda i, j, k: (i, j))
    dims = (((0 if ta else 1,), (1 if tb else 0,)), ((), ()))

    def body(*refs):
        a_ref, b_ref = refs[0], refs[1]
        c_ref = refs[2] if has_add else None
        o_ref = refs[3] if has_add else refs[2]
        part = lax.dot_general(a_ref[...].astype(BF16), b_ref[...].astype(BF16), dims, preferred_element_type=F32)

        def finish(r):
            if has_add:
                r = r + add_scale * c_ref[...]
            o_ref[...] = r.astype(out_dtype)

        if nk == 1:
            finish(part)
        else:
            acc_ref = refs[-1]
            k = pl.program_id(2)

            @pl.when(k == 0)
            def _():
                acc_ref[...] = part

            @pl.when(k > 0)
            def _():
                acc_ref[...] += part

            @pl.when(k == nk - 1)
            def _():
                finish(acc_ref[...])

    in_specs = [a_spec, b_spec] + ([o_spec] if has_add else [])
    args = (a, b) + ((add,) if has_add else ())
    outs, jouts = _hosted_call(
        body, job, name=name, grid=(M // bm, N // bn, nk), in_specs=in_specs, out_specs=[o_spec],
        out_shape=[_sds((M, N), out_dtype)], scratch_shapes=[pltpu.VMEM((bm, bn), F32)] if nk > 1 else [],
        semantics=("parallel", "parallel", "arbitrary"), args=args)
    return outs[0] if job is None else (outs[0], jouts)


def _norm_fwd(x, g, b, *, res=None, alpha=1.0, center=True, eps, want_f32=True, want_bf16=True, name):
    S, W = x.shape
    bs = _pick(S, 256, 16)
    has_res, has_b = res is not None, b is not None

    def body(*refs):
        it = iter(refs)
        x_ref = next(it)
        res_ref = next(it) if has_res else None
        g_ref = next(it)
        b_ref = next(it) if has_b else None
        y_ref = next(it) if want_f32 else None
        yb_ref = next(it) if want_bf16 else None
        xh_ref, r_ref = next(it), next(it)
        z = x_ref[...]
        if has_res:
            z = alpha * z + res_ref[...]
        if center:
            z = z - jnp.mean(z, axis=-1, keepdims=True)
        rstd = lax.rsqrt(jnp.mean(z * z, axis=-1, keepdims=True) + eps)
        xh = z * rstd
        y = xh * g_ref[...]
        if has_b:
            y = y + b_ref[...]
        if want_f32:
            y_ref[...] = y
        if want_bf16:
            yb_ref[...] = y.astype(BF16)
        xh_ref[...] = xh
        r_ref[...] = rstd

    row = pl.BlockSpec((bs, W), lambda i: (i, 0))
    vec = pl.BlockSpec((1, W), lambda i: (0, 0))
    in_specs = [row] + ([row] if has_res else []) + [vec] + ([vec] if has_b else [])
    args = (x,) + ((res,) if has_res else ()) + (g.reshape(1, W),) + ((b.reshape(1, W),) if has_b else ())
    out_specs = ([row] if want_f32 else []) + ([row] if want_bf16 else []) + [row, pl.BlockSpec((bs, 1), lambda i: (i, 0))]
    out_shape = (([_sds((S, W), F32)] if want_f32 else []) + ([_sds((S, W), BF16)] if want_bf16 else [])
                 + [_sds((S, W), F32), _sds((S, 1), F32)])
    outs = list(_pcall(body, name=name, grid=(S // bs,), in_specs=in_specs, out_specs=out_specs, out_shape=out_shape,
                       compiler_params=_params(("parallel",)))(*args))
    y = outs.pop(0) if want_f32 else None
    yb = outs.pop(0) if want_bf16 else None
    return y, yb, outs[0], outs[1]


def _norm_bwd(dy, xh, rstd, g, *, center=True, want_f32=True, want_bf16=True, name):
    S, W = dy.shape
    bs = _pick(S, 256, 16)

    def body(*refs):
        dy_ref, xh_ref, r_ref, g_ref = refs[:4]
        it = iter(refs[4:])
        dz_ref = next(it) if want_f32 else None
        dzb_ref = next(it) if want_bf16 else None
        dg_ref, db_ref = next(it), next(it)

        @pl.when(pl.program_id(0) == 0)
        def _():
            dg_ref[...] = jnp.zeros_like(dg_ref)
            db_ref[...] = jnp.zeros_like(db_ref)

        dyv = dy_ref[...]
        xhv = xh_ref[...]
        dyg = dyv * g_ref[...]
        m2 = jnp.mean(dyg * xhv, axis=-1, keepdims=True)
        t = dyg - xhv * m2
        if center:
            t = t - jnp.mean(dyg, axis=-1, keepdims=True)
        dz = r_ref[...] * t
        if want_f32:
            dz_ref[...] = dz
        if want_bf16:
            dzb_ref[...] = dz.astype(BF16)
        dg_ref[...] += jnp.sum(dyv * xhv, axis=0, keepdims=True)
        db_ref[...] += jnp.sum(dyv, axis=0, keepdims=True)

    row = pl.BlockSpec((bs, W), lambda i: (i, 0))
    vec = pl.BlockSpec((1, W), lambda i: (0, 0))
    out_specs = ([row] if want_f32 else []) + ([row] if want_bf16 else []) + [vec, vec]
    out_shape = (([_sds((S, W), F32)] if want_f32 else []) + ([_sds((S, W), BF16)] if want_bf16 else [])
                 + [_sds((1, W), F32), _sds((1, W), F32)])
    outs = list(_pcall(body, name=name, grid=(S // bs,),
                       in_specs=[row, row, pl.BlockSpec((bs, 1), lambda i: (i, 0)), vec],
                       out_specs=out_specs, out_shape=out_shape,
                       compiler_params=_params(("arbitrary",)))(dy, xh, rstd, g.reshape(1, W)))
    dz = outs.pop(0) if want_f32 else None
    dzb = outs.pop(0) if want_bf16 else None
    return dz, dzb, outs[0][0], outs[1][0]


def _rope(t1, t2, cos, sin, *, name):
    S, C = t1.shape
    bs = _pick(S, 512, 8)

    def body(t1_ref, t2_ref, c_ref, s_ref, o1_ref, o2_ref):
        a, b, c, s = t1_ref[...], t2_ref[...], c_ref[...], s_ref[...]
        o1_ref[...] = a * c - b * s
        o2_ref[...] = b * c + a * s

    row = pl.BlockSpec((bs, C), lambda i: (i, 0))
    return _pcall(
        body, name=name, grid=(S // bs,), in_specs=[row] * 4, out_specs=[row, row],
        out_shape=[_sds((S, C), F32)] * 2, compiler_params=_params(("parallel",)),
    )(t1, t2, cos, sin)


def _chunk_mask(row0, B, L):
    rows = lax.shift_right_logical(row0 + lax.broadcasted_iota(jnp.int32, (B, L), 0), CHUNK_SHIFT)
    cols = lax.shift_right_logical(lax.broadcasted_iota(jnp.int32, (B, L), 1), CHUNK_SHIFT)
    return rows >= cols


def _for_key_prefix(qi, B, G, S, fn):
    per = G // B
    for b in range(S // G):
        pl.when(qi // per == b)(lambda b=b: fn((b + 1) * G))


def _nt(a, b):
    return lax.dot_general(a, b, (((1,), (1,)), ((), ())), preferred_element_type=F32)


def _nn(a, b):
    return lax.dot_general(a, b, (((1,), (0,)), ((), ())), preferred_element_type=F32)


def _tn(a, b):
    return lax.dot_general(a, b, (((0,), (0,)), ((), ())), preferred_element_type=F32)


def _attn_fwd(qh, kh, vh, *, job=None, name):
    H, S, DQ = qh.shape
    DV = vh.shape[-1]
    B = _pick(S, ATTN_BLOCK, CHUNK)
    G = _pick(S, ATTN_KEY_STEP, B)
    scale = float(DQ) ** -0.5
    neg = float(jnp.finfo(jnp.float32).min)

    def body(q_ref, k_ref, v_ref, o_ref, lse_ref):
        qi = pl.program_id(1)

        def run(L):
            s = _nt(q_ref[...], k_ref[0:L, :]) * scale
            s = jnp.where(_chunk_mask(qi * B, B, L), s, neg)
            m = jnp.max(s, axis=-1, keepdims=True)
            e = jnp.exp(s - m)
            l = jnp.sum(e, axis=-1, keepdims=True)
            o_ref[...] = _nn((e / l).astype(BF16), v_ref[0:L, :])
            lse_ref[...] = m + jnp.log(l)

        _for_key_prefix(qi, B, G, S, run)

    outs, jouts = _hosted_call(
        body, job, name=name, grid=(H, S // B),
        in_specs=[pl.BlockSpec((None, B, DQ), lambda h, i: (h, i, 0)),
                  pl.BlockSpec((None, S, DQ), lambda h, i: (h, 0, 0)),
                  pl.BlockSpec((None, S, DV), lambda h, i: (h, 0, 0))],
        out_specs=[pl.BlockSpec((B, DV), lambda h, i: (i, h)),
                   pl.BlockSpec((None, B, 1), lambda h, i: (h, i, 0))],
        out_shape=[_sds((S, H * DV), F32), _sds((H, S, 1), F32)], scratch_shapes=[],
        semantics=("parallel", "parallel"), args=(qh, kh, vh))
    return (outs[0], outs[1]) if job is None else (outs[0], outs[1], jouts)


def _attn_bwd(qh, kh, vh, o, do, lse, *, do_col0, job=None, name):
    H, S, DQ = qh.shape
    DV = vh.shape[-1]
    B = _pick(S, ATTN_BLOCK, CHUNK)
    G = _pick(S, ATTN_KEY_STEP, B)
    scale = float(DQ) ** -0.5

    def body(q_ref, k_ref, v_ref, o_ref, do_ref, lse_ref, dq_ref, dk_ref, dv_ref):
        qi = pl.program_id(1)

        @pl.when(qi == 0)
        def _():
            dk_ref[...] = jnp.zeros_like(dk_ref)
            dv_ref[...] = jnp.zeros_like(dv_ref)

        def run(L):
            q = q_ref[...]
            k = k_ref[0:L, :]
            dov = do_ref[...]
            dob = dov.astype(BF16)
            p = jnp.exp(_nt(q, k) * scale - lse_ref[...])
            p = jnp.where(_chunk_mask(qi * B, B, L), p, 0.0)
            dsum = jnp.sum(dov * o_ref[...], axis=-1, keepdims=True)
            ds = (p * (_nt(dob, v_ref[0:L, :]) - dsum) * scale).astype(BF16)
            dv_ref[0:L, :] += _tn(p.astype(BF16), dob)
            dk_ref[0:L, :] += _tn(ds, q)
            dq_ref[...] = _nn(ds, k)

        _for_key_prefix(qi, B, G, S, run)

    outs, jouts = _hosted_call(
        body, job, name=name, grid=(H, S // B),
        in_specs=[pl.BlockSpec((None, B, DQ), lambda h, i: (h, i, 0)),
                  pl.BlockSpec((None, S, DQ), lambda h, i: (h, 0, 0)),
                  pl.BlockSpec((None, S, DV), lambda h, i: (h, 0, 0)),
                  pl.BlockSpec((B, DV), lambda h, i: (i, h)),
                  pl.BlockSpec((B, DV), lambda h, i: (i, do_col0 + h)),
                  pl.BlockSpec((None, B, 1), lambda h, i: (h, i, 0))],
        out_specs=[pl.BlockSpec((None, B, DQ), lambda h, i: (h, i, 0)),
                   pl.BlockSpec((None, S, DQ), lambda h, i: (h, 0, 0)),
                   pl.BlockSpec((None, S, DV), lambda h, i: (h, 0, 0))],
        out_shape=[_sds((H, S, DQ), F32), _sds((H, S, DQ), F32), _sds((H, S, DV), F32)], scratch_shapes=[],
        semantics=("parallel", "arbitrary"), args=(qh, kh, vh, o, do, lse))
    return tuple(outs) if job is None else (*outs, jouts)


def _ret_tables(T):
    H = RET_HEADS
    log_gamma = jnp.log1p(-jnp.exp2(-5.0 - jnp.arange(H, dtype=F32)))
    idx = jnp.arange(T, dtype=F32)
    chunk = jnp.arange(T) // CHUNK
    visible = chunk[:, None] >= chunk[None, :]
    intra = jnp.where(visible[None], jnp.exp(log_gamma[:, None, None] * jnp.abs(idx[:, None] - idx[None, :])), 0.0)
    qd = jnp.exp(log_gamma[:, None] * (idx + 1.0))[:, :, None]
    kd = jnp.exp(log_gamma[:, None] * (T - 1.0 - idx))[:, :, None]
    cd = jnp.exp(log_gamma * T)[:, None, None]
    return intra, qd, kd, cd


def _rot(t, c, s):
    half = t.shape[-1] // 2
    t1, t2 = t[:, :half], t[:, half:]
    return jnp.concatenate([t1 * c - t2 * s, t2 * c + t1 * s], axis=-1)


def _rot_t(t, c, s):
    half = t.shape[-1] // 2
    t1, t2 = t[:, :half], t[:, half:]
    return jnp.concatenate([t1 * c + t2 * s, t2 * c - t1 * s], axis=-1)


def _dot(a, b, ca, cb):
    return lax.dot_general(a.astype(BF16), b.astype(BF16), (((ca,), (cb,)), ((), ())), preferred_element_type=F32)


def _ret_fwd(hR, cos, sin, tables, *, name):
    S = hR.shape[0]
    H, L, DK, DV = RET_HEADS, tables[0].shape[-1], RET_QK, RET_V
    NC = S // L
    qscale = float(DK) ** -0.5
    intra, qd, kd, cd = tables

    def body(q_ref, k_ref, v_ref, c_ref, s_ref, in_ref, qd_ref, kd_ref, cd_ref, o_ref, st_ref, state):
        @pl.when(pl.program_id(1) == 0)
        def _():
            state[...] = jnp.zeros_like(state)

        c, s = c_ref[...], s_ref[...]
        q = _rot(q_ref[...], c, s) * qscale
        k = _rot(k_ref[...], c, s)
        v = v_ref[...]
        st = state[...]
        st_ref[...] = st.astype(BF16)
        scores = _dot(q, k, 1, 1) * in_ref[...]
        o_ref[...] = _dot(scores, v, 1, 0) + _dot(q * qd_ref[...], st, 1, 0)
        state[...] = st * cd_ref[...] + _dot(k * kd_ref[...], v, 0, 0)

    blk = lambda off: pl.BlockSpec((L, DK), lambda h, c: (c, off + h))
    rope = pl.BlockSpec((L, DK // 2), lambda h, c: (c, 0))
    return _pcall(
        body, name=name, grid=(H, NC),
        in_specs=[blk(0), blk(H), blk(2 * H), rope, rope,
                  pl.BlockSpec((None, L, L), lambda h, c: (h, 0, 0)),
                  pl.BlockSpec((None, L, 1), lambda h, c: (h, 0, 0)),
                  pl.BlockSpec((None, L, 1), lambda h, c: (h, 0, 0)),
                  pl.BlockSpec((None, 1, 1), lambda h, c: (h, 0, 0))],
        out_specs=[pl.BlockSpec((L, DV), lambda h, c: (c, h)),
                   pl.BlockSpec((None, None, DK, DV), lambda h, c: (h, c, 0, 0))],
        out_shape=[_sds((S, H * DV), F32), _sds((H, NC, DK, DV), BF16)],
        scratch_shapes=[pltpu.VMEM((DK, DV), F32)],
        compiler_params=_params(("arbitrary", "arbitrary")),
    )(hR, hR, hR, cos, sin, intra, qd, kd, cd)


def _ret_bwd(do, hR, states, cos, sin, tables, *, name):
    S = hR.shape[0]
    H, L, DK, DV = RET_HEADS, tables[0].shape[-1], RET_QK, RET_V
    NC = S // L
    qscale = float(DK) ** -0.5
    intra, qd, kd, cd = tables

    def body(do_ref, q_ref, k_ref, v_ref, st_ref, c_ref, s_ref, in_ref, qd_ref, kd_ref, cd_ref,
             dq_ref, dk_ref, dv_ref, dstate):
        @pl.when(pl.program_id(1) == 0)
        def _():
            dstate[...] = jnp.zeros_like(dstate)

        c, s = c_ref[...], s_ref[...]
        q = _rot(q_ref[...], c, s) * qscale
        k = _rot(k_ref[...], c, s)
        v = v_ref[...]
        dov = do_ref[...]
        dst = dstate[...]
        dec = in_ref[...]
        qdv, kdv = qd_ref[...], kd_ref[...]
        scores = _dot(q, k, 1, 1) * dec
        da = _dot(dov, v, 1, 1) * dec
        dv_ref[...] = (_dot(scores, dov, 0, 0) + _dot(k * kdv, dst, 1, 0)).astype(BF16)
        dq = _dot(da, k, 1, 0) + _dot(dov, st_ref[...], 1, 1) * qdv
        dk = _dot(da, q, 0, 0) + _dot(v, dst, 1, 1) * kdv
        dq_ref[...] = _rot_t(dq * qscale, c, s).astype(BF16)
        dk_ref[...] = _rot_t(dk, c, s).astype(BF16)
        dstate[...] = dst * cd_ref[...] + _dot(q * qdv, dov, 0, 0)

    rev = lambda c: NC - 1 - c
    blk = lambda off: pl.BlockSpec((L, DK), lambda h, c: (rev(c), off + h))
    rope = pl.BlockSpec((L, DK // 2), lambda h, c: (rev(c), 0))
    out = pl.BlockSpec((L, DK), lambda h, c: (rev(c), h))
    return _pcall(
        body, name=name, grid=(H, NC),
        in_specs=[out, blk(0), blk(H), blk(2 * H),
                  pl.BlockSpec((None, None, DK, DV), lambda h, c: (h, rev(c), 0, 0)), rope, rope,
                  pl.BlockSpec((None, L, L), lambda h, c: (h, 0, 0)),
                  pl.BlockSpec((None, L, 1), lambda h, c: (h, 0, 0)),
                  pl.BlockSpec((None, L, 1), lambda h, c: (h, 0, 0)),
                  pl.BlockSpec((None, 1, 1), lambda h, c: (h, 0, 0))],
        out_specs=[out, out, out],
        out_shape=[_sds((S, H * DK), BF16)] * 3,
        scratch_shapes=[pltpu.VMEM((DK, DV), F32)],
        compiler_params=_params(("arbitrary", "arbitrary")),
    )(do, hR, hR, hR, states, cos, sin, intra, qd, kd, cd)


def _gn_gate_fwd(o, hR, g, b, *, name):
    S = o.shape[0]
    H, DV = RET_HEADS, RET_V
    bs = _pick(S, 512, 16)

    def body(o_ref, rg_ref, g_ref, b_ref, r_ref):
        z = o_ref[...]
        z = z - jnp.mean(z, axis=-1, keepdims=True)
        xh = z * lax.rsqrt(jnp.mean(z * z, axis=-1, keepdims=True) + GN_EPS)
        rg = rg_ref[...]
        r_ref[...] = ((rg * jax.nn.sigmoid(rg)) * (xh * g_ref[...] + b_ref[...])).astype(BF16)

    row = pl.BlockSpec((bs, DV), lambda i, h: (i, h))
    vec = pl.BlockSpec((1, DV), lambda i, h: (0, h))
    return _pcall(
        body, name=name, grid=(S // bs, H),
        in_specs=[row, pl.BlockSpec((bs, DV), lambda i, h: (i, 3 * H + h)), vec, vec],
        out_specs=row, out_shape=_sds((S, H * DV), BF16),
        compiler_params=_params(("parallel", "parallel")),
    )(o, hR, g.reshape(1, H * DV), b.reshape(1, H * DV))


def _gn_gate_bwd(dr, o, hR, g, b, *, dr_col0, name):
    S = o.shape[0]
    H, DV = RET_HEADS, RET_V
    bs = _pick(S, 512, 16)

    def body(dr_ref, o_ref, rg_ref, g_ref, b_ref, do_ref, drg_ref, dg_ref, db_ref):
        @pl.when(pl.program_id(1) == 0)
        def _():
            dg_ref[...] = jnp.zeros_like(dg_ref)
            db_ref[...] = jnp.zeros_like(db_ref)

        z = o_ref[...]
        z = z - jnp.mean(z, axis=-1, keepdims=True)
        rstd = lax.rsqrt(jnp.mean(z * z, axis=-1, keepdims=True) + GN_EPS)
        xh = z * rstd
        gv = g_ref[...]
        y = xh * gv + b_ref[...]
        rg = rg_ref[...]
        sg = jax.nn.sigmoid(rg)
        drv = dr_ref[...]
        dy = drv * (rg * sg)
        drg_ref[...] = (drv * y * (sg * (1.0 + rg * (1.0 - sg)))).astype(BF16)
        dg_ref[...] += jnp.sum(dy * xh, axis=0, keepdims=True)
        db_ref[...] += jnp.sum(dy, axis=0, keepdims=True)
        dxh = dy * gv
        do_ref[...] = rstd * (dxh - jnp.mean(dxh, axis=-1, keepdims=True)
                              - xh * jnp.mean(dxh * xh, axis=-1, keepdims=True))

    row = pl.BlockSpec((bs, DV), lambda h, i: (i, h))
    vec = pl.BlockSpec((1, DV), lambda h, i: (0, h))
    do, drg, dg, db = _pcall(
        body, name=name, grid=(H, S // bs),
        in_specs=[pl.BlockSpec((bs, DV), lambda h, i: (i, dr_col0 + h)), row,
                  pl.BlockSpec((bs, DV), lambda h, i: (i, 3 * H + h)), vec, vec],
        out_specs=[row, row, vec, vec],
        out_shape=[_sds((S, H * DV), F32), _sds((S, H * DV), BF16), _sds((1, H * DV), F32), _sds((1, H * DV), F32)],
        compiler_params=_params(("arbitrary", "arbitrary")),
    )(dr, o, hR, g.reshape(1, H * DV), b.reshape(1, H * DV))
    return do, drg, dg[0], db[0]


def _swiglu_fwd(g, u, *, name):
    S, F = g.shape
    bs, bf = _pick(S, 512, 16), _pick(F, 1408, 128)

    def body(g_ref, u_ref, a_ref):
        gv = g_ref[...]
        a_ref[...] = ((gv * jax.nn.sigmoid(gv)) * u_ref[...]).astype(BF16)

    blk = pl.BlockSpec((bs, bf), lambda i, j: (i, j))
    return _pcall(body, name=name, grid=(S // bs, F // bf), in_specs=[blk, blk], out_specs=blk,
                  out_shape=_sds((S, F), BF16), compiler_params=_params(("parallel", "parallel")))(g, u)


def _swiglu_bwd(da, g, u, *, name):
    S, F = g.shape
    bs, bf = _pick(S, 512, 16), _pick(F, 1408, 128)

    def body(da_ref, g_ref, u_ref, dg_ref, du_ref):
        gv, dav = g_ref[...], da_ref[...]
        sg = jax.nn.sigmoid(gv)
        du_ref[...] = (dav * (gv * sg)).astype(BF16)
        dg_ref[...] = (dav * u_ref[...] * (sg * (1.0 + gv * (1.0 - sg)))).astype(BF16)

    blk = pl.BlockSpec((bs, bf), lambda i, j: (i, j))
    return _pcall(body, name=name, grid=(S // bs, F // bf), in_specs=[blk] * 3, out_specs=[blk, blk],
                  out_shape=[_sds((S, F), BF16)] * 2, compiler_params=_params(("parallel", "parallel")))(da, g, u)


def _loss_head(y, t, *, name):
    S, D = y.shape
    bs = _pick(S, 256, 8)
    inv_d = 1.0 / D

    def body(y_ref, t_ref, dy_ref, l_ref):
        @pl.when(pl.program_id(0) == 0)
        def _():
            l_ref[...] = jnp.zeros_like(l_ref)

        e = y_ref[...] - t_ref[...]
        dy_ref[...] = e * inv_d
        l_ref[...] += 0.5 * jnp.sum(jnp.mean(e * e, axis=-1, keepdims=True), axis=0, keepdims=True)

    row = pl.BlockSpec((bs, D), lambda i: (i, 0))
    dy, l = _pcall(
        body, name=name, grid=(S // bs,), in_specs=[row, row],
        out_specs=[row, pl.BlockSpec((1, 1), lambda i: (0, 0))],
        out_shape=[_sds((S, D), F32), _sds((1, 1), F32)],
        compiler_params=_params(("arbitrary",)),
    )(y, t)
    return dy, l[0, 0]


def _adamw(w, g, m, v, *, name):
    shape = w.shape
    C = shape[-1]
    R = w.size // C
    br = _pick(R, 512, 8)

    def body(w_ref, g_ref, m_ref, v_ref, d_ref, nm_ref, nv_ref):
        gv = g_ref[...]
        mn = ADAM_B1 * m_ref[...] + (1.0 - ADAM_B1) * gv
        vn = ADAM_B2 * v_ref[...] + (1.0 - ADAM_B2) * (gv * gv)
        m_hat = mn / (1.0 - ADAM_B1 ** ADAM_STEP)
        v_hat = vn / (1.0 - ADAM_B2 ** ADAM_STEP)
        d_ref[...] = -ADAM_LR * (m_hat / (jnp.sqrt(v_hat) + ADAM_EPS) + ADAM_WD * w_ref[...])
        nm_ref[...] = mn
        nv_ref[...] = vn

    blk = pl.BlockSpec((br, C), lambda i: (i, 0))
    outs = _pcall(body, name=name, grid=(R // br,), in_specs=[blk] * 4, out_specs=[blk] * 3,
                  out_shape=[_sds((R, C), F32)] * 3,
                  compiler_params=_params(("parallel",)))(*[a.reshape(R, C) for a in (w, g, m, v)])
    return tuple(o.reshape(shape) for o in outs)


def _lane_dense(total):
    C = next(c for c in (2048, 1024, 512, 256, 128) if total % (16 * c) == 0)
    return total // C, C


def _sum_slots(x, *, name):
    ns, shape = x.shape[0], x.shape[1:]
    R, C = _lane_dense(x[0].size)
    x = x.reshape(ns, R, C)
    br = _pick(R, 256, 16)

    def body(x_ref, o_ref):
        acc = x_ref[0].astype(F32)
        for s in range(1, ns):
            acc = acc + x_ref[s].astype(F32)
        o_ref[...] = acc

    return _pcall(body, name=name, grid=(R // br,),
                  in_specs=[pl.BlockSpec((ns, br, C), lambda i: (0, i, 0))],
                  out_specs=pl.BlockSpec((br, C), lambda i: (i, 0)),
                  out_shape=_sds((R, C), F32), compiler_params=_params(("parallel",)))(x).reshape(shape)


def _pair_sum(g, r, *, name):
    shape = r.shape
    R, C = _lane_dense(r[0].size)
    br = _pick(R, 512, 16)
    core = lax.axis_index("c").astype(jnp.int32).reshape(1)

    def body(c_ref, g_ref, r_ref, o_ref):
        o_ref[...] = (g_ref[...].astype(F32) + r_ref[...].astype(F32)).astype(o_ref.dtype)

    blk = pl.BlockSpec((None, br, C), lambda i, j, c_ref: (i, j, 0))
    return _pcall(
        body, name=name, out_shape=_sds((N_CHIP, R, C), g.dtype),
        grid_spec=pltpu.PrefetchScalarGridSpec(
            num_scalar_prefetch=1, grid=(N_CHIP, R // br),
            in_specs=[pl.BlockSpec((None, br, C), lambda i, j, c_ref: (2 * i + c_ref[0], j, 0)), blk],
            out_specs=blk),
        compiler_params=_params(("parallel", "parallel")),
    )(core, g.reshape(N_DEV, R, C), r.reshape(N_CHIP, R, C)).reshape(shape)


def _sum_heads(dkh, *, name):
    H, S, DQ = dkh.shape
    bs = _pick(S, 512, 8)

    def body(d_ref, o_ref):
        acc = d_ref[0][:, MLA_NOPE:]
        for h in range(1, H):
            acc = acc + d_ref[h][:, MLA_NOPE:]
        o_ref[...] = acc

    return _pcall(body, name=name, grid=(S // bs,),
                  in_specs=[pl.BlockSpec((H, bs, DQ), lambda i: (0, i, 0))],
                  out_specs=pl.BlockSpec((bs, MLA_ROPE), lambda i: (i, 0)),
                  out_shape=_sds((S, MLA_ROPE), F32), compiler_params=_params(("parallel",)))(dkh)


N_PEER = N_DEV - 1
N_CHIP = N_DEV // 2


def _coords():
    return lax.axis_index("x"), lax.axis_index("y"), lax.axis_index("c")


def _ag_phase(phase, x_refs, out_refs, send_sems, recv_sems, local_sems):
    n = len(x_refs)
    x, y, c = _coords()
    me, sibling = (x, y, c), (x, y, 1 - c)
    chips = [(1 - x, y), (x, 1 - y), (1 - x, 1 - y)]

    def copy(a, k, block, to, src=None):
        px, py, pc = block
        dst = out_refs[a].at[4 * px + 2 * py + pc]
        return pltpu.make_async_remote_copy(
            src_ref=dst if src is None else src, dst_ref=dst,
            send_sem=send_sems.at[a * N_PEER + k], recv_sem=recv_sems.at[a * N_PEER + k],
            device_id=to, device_id_type=pl.DeviceIdType.MESH)

    def local(a):
        return pltpu.make_async_copy(x_refs[a], out_refs[a].at[4 * x + 2 * y + c], local_sems.at[a])

    def first(a):
        return ([copy(a, 0, me, sibling, src=x_refs[a])]
                + [copy(a, 1 + j, me, (*chip, c), src=x_refs[a]) for j, chip in enumerate(chips)])

    def passed(a, j):
        return copy(a, 4 + j, (*chips[j], c), sibling)

    if phase == 0:
        for a in range(n):
            local(a).start()
            for cp in first(a):
                cp.start()
    elif phase == 1:
        for j in range(len(chips)):
            for a in range(n):
                copy(a, 1 + j, (*chips[j], c), me).wait_recv()
                passed(a, j).start()
    else:
        for a in range(n):
            copy(a, 0, sibling, me).wait_recv()
            for j in range(len(chips)):
                copy(a, 4 + j, (*chips[j], 1 - c), me).wait_recv()
        for a in range(n):
            for cp in first(a):
                cp.wait_send()
            for j in range(len(chips)):
                passed(a, j).wait_send()
            local(a).wait()


def _pair_phase(phase, g_refs, out_refs, send_sems, recv_sems, local_sems):
    n = len(g_refs)
    x, y, c = _coords()

    def copy(a, i):
        return pltpu.make_async_remote_copy(
            src_ref=g_refs[a].at[2 * i + (1 - c)], dst_ref=out_refs[a].at[i],
            send_sem=send_sems.at[a * N_PEER + i], recv_sem=recv_sems.at[a * N_PEER + i],
            device_id=(x, y, 1 - c), device_id_type=pl.DeviceIdType.MESH)

    if phase == 0:
        for a in range(n):
            for i in range(N_CHIP):
                copy(a, i).start()
    elif phase == 2:
        for a in range(n):
            for i in range(N_CHIP):
                copy(a, i).wait_recv()
        for a in range(n):
            for i in range(N_CHIP):
                copy(a, i).wait_send()


def _cross_phase(phase, p_refs, out_refs, send_sems, recv_sems, local_sems):
    n = len(p_refs)
    x, y, c = _coords()
    mine = 2 * x + y

    def local(a):
        return pltpu.make_async_copy(p_refs[a].at[mine], out_refs[a].at[mine], local_sems.at[a])

    def pair(a, k):
        px = 1 - x if k & 2 else x
        py = 1 - y if k & 1 else y
        peer = 2 * px + py
        sem = a * N_PEER + k - 1
        send = pltpu.make_async_remote_copy(
            src_ref=p_refs[a].at[peer], dst_ref=out_refs[a].at[mine], send_sem=send_sems.at[sem],
            recv_sem=recv_sems.at[sem], device_id=(px, py, c), device_id_type=pl.DeviceIdType.MESH)
        recv = pltpu.make_async_remote_copy(
            src_ref=p_refs[a].at[peer], dst_ref=out_refs[a].at[peer], send_sem=send_sems.at[sem],
            recv_sem=recv_sems.at[sem], device_id=(px, py, c), device_id_type=pl.DeviceIdType.MESH)
        return send, recv

    if phase == 0:
        for a in range(n):
            local(a).start()
        for k in range(1, N_CHIP):
            for a in range(n):
                pair(a, k)[0].start()
    elif phase == 2:
        for k in range(1, N_CHIP):
            for a in range(n):
                pair(a, k)[1].wait_recv()
        for k in range(1, N_CHIP):
            for a in range(n):
                pair(a, k)[0].wait_send()
        for a in range(n):
            local(a).wait()


_PHASES = {"ag": _ag_phase, "pair": _pair_phase, "cross": _cross_phase}


def _job(kind, arrays):
    arrays = list(arrays)
    if kind == "ag":
        shapes = [(N_DEV,) + a.shape for a in arrays]
    elif kind == "pair":
        shapes = [(N_CHIP,) + a.shape[1:] for a in arrays]
    else:
        shapes = [a.shape for a in arrays]
    return dict(parts=[(kind, len(arrays))], ins=arrays, outs=[_sds(s, a.dtype) for s, a in zip(shapes, arrays)])


def _join(jobs):
    jobs = [j for j in jobs if j is not None]
    if not jobs:
        return None
    return dict(parts=[p for j in jobs for p in j["parts"]], ins=[a for j in jobs for a in j["ins"]],
                outs=[o for j in jobs for o in j["outs"]])


def _hosted_call(body, job, *, name, grid, in_specs, out_specs, out_shape, scratch_shapes, semantics, args):
    if job is None:
        outs = _pcall(body, name=name, grid=grid, in_specs=in_specs, out_specs=out_specs, out_shape=out_shape,
                      scratch_shapes=scratch_shapes, compiler_params=_params(semantics))(*args)
        return outs, []
    n_in, n_out, n_scr, nj = len(in_specs), len(out_specs), len(scratch_shapes), len(job["ins"])
    parts = job["parts"]
    total = 1
    for g in grid:
        total *= g

    def wrapped(*refs):
        ins, refs = refs[:n_in], refs[n_in:]
        jins, refs = refs[:nj], refs[nj:]
        outs, refs = refs[:n_out], refs[n_out:]
        jouts, refs = refs[:nj], refs[nj:]
        scr, sems = refs[:n_scr], refs[n_scr:]
        step = 0
        for d, g in enumerate(grid):
            step = step * g + pl.program_id(d)

        def run(phase):
            off = 0
            for i, (kind, n) in enumerate(parts):
                _PHASES[kind](phase, jins[off:off + n], jouts[off:off + n], *sems[3 * i:3 * i + 3])
                off += n

        pl.when(step == 0)(lambda: run(0))
        pl.when(step == total // 2)(lambda: run(1))
        body(*ins, *outs, *scr)
        pl.when(step == total - 1)(lambda: run(2))

    any_spec = pl.BlockSpec(memory_space=pl.ANY)
    sems = []
    for kind, n in parts:
        sems += [pltpu.SemaphoreType.DMA((n * N_PEER,)), pltpu.SemaphoreType.DMA((n * N_PEER,)),
                 pltpu.SemaphoreType.DMA((n,))]
    outs = _pcall(
        wrapped, name=name, grid=grid, in_specs=list(in_specs) + [any_spec] * nj,
        out_specs=list(out_specs) + [any_spec] * nj, out_shape=list(out_shape) + job["outs"],
        scratch_shapes=list(scratch_shapes) + sems,
        compiler_params=_params(("arbitrary",) * len(grid)),
    )(*args, *job["ins"])
    return outs[:n_out], outs[n_out:]


def _exchange(job, *, name):
    outs, jouts = _hosted_call(lambda: None, job, name=name, grid=(1,), in_specs=[], out_specs=[], out_shape=[],
                               scratch_shapes=[], semantics=("arbitrary",), args=())
    return jouts


_TRANSPOSED = {"w_in": True, "w_out": False, "w_gate": True, "w_up": True, "w_down": False, "w_uq": True, "w_ukv": True}
_BIG = tuple(_TRANSPOSED)
_SMALL = ("ln_in_g", "ln_in_b", "q_norm_g", "kv_norm_g", "ret_gn_g", "ret_gn_b", "ln1_g", "ln1_b", "ln2_g", "ln2_b")
SMALL_COLS = 128


def _rope_tables(pos, dim):
    inv_freq = ROPE_THETA ** (-jnp.arange(0, dim, 2, dtype=F32) / dim)
    ang = pos.astype(F32)[:, None] * inv_freq
    return jnp.cos(ang), jnp.sin(ang)


def _split_in(wt_in):
    a, b, c = MLA_Q_LORA, MLA_Q_LORA + MLA_KV_LORA, MLA_Q_LORA + MLA_KV_LORA + MLA_ROPE
    return wt_in[:a], wt_in[a:b], wt_in[b:c], wt_in[c:]


def _mla_pack_rope(q, kr):
    S = q.shape[0]
    H, hr = MLA_HEADS, MLA_ROPE // 2
    q3 = q.reshape(S, H, MLA_QK)
    t1 = jnp.concatenate([q3[:, :, MLA_NOPE:MLA_NOPE + hr].reshape(S, H * hr), kr[:, :hr]], axis=1)
    t2 = jnp.concatenate([q3[:, :, MLA_NOPE + hr:].reshape(S, H * hr), kr[:, hr:]], axis=1)
    return q3[:, :, :MLA_NOPE], t1, t2


def _mla_unpack_rope(q_nope, o1, o2):
    S = o1.shape[0]
    H, hr = MLA_HEADS, MLA_ROPE // 2
    q3 = jnp.concatenate([q_nope, o1[:, :H * hr].reshape(S, H, hr), o2[:, :H * hr].reshape(S, H, hr)], axis=-1)
    kr = jnp.concatenate([o1[:, H * hr:], o2[:, H * hr:]], axis=1)
    return q3, kr


class _Plan:
    FIRST = [("ag", ((0, "w_in"), (0, "w_uq"), (0, "w_ukv"), (0, "w_out")))]
    LAST = [("cross", ((0, "w_in"),))]
    HOSTS = {
        "l0_hR": [("ag", ((0, "w_gate"),))], "l0_attn": [("ag", ((0, "w_up"),))],
        "l0_gate": [("ag", ((0, "w_down"),))], "l0_up": [("ag", ((1, "w_in"), (1, "w_uq"), (1, "w_ukv")))],
        "l0_down": [("ag", ((1, "w_gate"),))], "l1_hR": [("ag", ((1, "w_out"),))],
        "l1_attn": [("ag", ((1, "w_up"),))], "l1_gate": [("ag", ((1, "w_down"),))],
        "l0_b_dact": [("cross", ((1, "w_in"),))]}
    for _l in (0, 1):
        HOSTS.update({
            "l%d_b_wgate" % _l: [("pair", ((_l, "w_down"),))],
            "l%d_b_wup" % _l: [("pair", ((_l, "w_gate"),)), ("cross", ((_l, "w_down"),))],
            "l%d_b_dx1a" % _l: [("pair", ((_l, "w_up"),)), ("cross", ((_l, "w_gate"),))],
            "l%d_b_dx1b" % _l: [("cross", ((_l, "w_up"),))],
            "l%d_b_attn" % _l: [("pair", ((_l, "w_out"),))],
            "l%d_b_win" % _l: [("pair", ((_l, "w_uq"), (_l, "w_ukv"))), ("cross", ((_l, "w_out"),))],
            "l%d_b_dx" % _l: [("pair", ((_l, "w_in"),)), ("cross", ((_l, "w_uq"), (_l, "w_ukv")))]})

    def __init__(self, local):
        self.local = local
        self.full = {}
        self.grads = {}
        self.paired = {}
        self.recv = {}

    def _by_device(self, k):
        return self.grads[k].reshape((N_DEV,) + self.local[k].shape)

    def _make(self, kind, keys):
        if kind == "ag":
            return _job(kind, [self.local[k] for k in keys])
        if kind == "pair":
            return _job(kind, [self._by_device(k) for k in keys])
        return _job(kind, [_pair_sum(self._by_device(k), self.paired[k], name="pairsum_l%d_%s" % k) for k in keys])

    def _done(self, kind, keys, outs):
        for k, o in zip(keys, outs):
            if kind == "ag":
                self.full[k] = o.reshape(N_DEV * o.shape[1], o.shape[2])
            elif kind == "pair":
                self.paired[k] = o
            else:
                self.recv[k] = o

    def _run(self, todo, call):
        outs = call(_join([self._make(kind, keys) for kind, keys in todo]))
        for kind, keys in todo:
            self._done(kind, keys, outs[:len(keys)])
            outs = outs[len(keys):]

    def gather_first(self):
        self._run(self.FIRST, lambda job: _exchange(job, name="ag_first"))

    def send_last(self):
        self._run(self.LAST, lambda job: _exchange(job, name="rs_last"))

    def call(self, fn, name, *args, **kw):
        if name not in self.HOSTS:
            return fn(*args, name=name, **kw)
        res = []

        def run(job):
            *outs, jouts = fn(*args, name=name, job=job, **kw)
            res.extend(outs)
            return jouts

        self._run(self.HOSTS[name], run)
        return res[0] if len(res) == 1 else tuple(res)


def _layer_fwd(x, xb, plan, p, tabs, l):
    S, D = x.shape
    H = MLA_HEADS
    nm = lambda s: "l%d_%s" % (l, s)
    w = lambda n: plan.full[(l, n)]
    mm = lambda name, *a, **kw: plan.call(_mm, nm(name), *a, **kw)
    wt_q, wt_kv, wt_kr, wt_r = _split_in(w("w_in"))
    cq = mm("cq", xb, wt_q, tb=True)
    ckv = mm("ckv", xb, wt_kv, tb=True)
    kr = mm("krope", xb, wt_kr, tb=True)
    hR = mm("hR", xb, wt_r, tb=True)
    _, qn, qn_hat, q_rstd = _norm_fwd(cq, p["q_norm_g"], None, center=False, eps=RMS_EPS, want_f32=False,
                                      name=nm("qnorm"))
    _, kvn, kvn_hat, kv_rstd = _norm_fwd(ckv, p["kv_norm_g"], None, center=False, eps=RMS_EPS, want_f32=False,
                                         name=nm("kvnorm"))
    q = mm("uq", qn, w("w_uq"), tb=True)
    kv = mm("ukv", kvn, w("w_ukv"), tb=True)
    q_nope, t1, t2 = _mla_pack_rope(q, kr)
    o1, o2 = _rope(t1, t2, tabs["cos_m"], tabs["sin_m"], name=nm("rope"))
    q3, k_r = _mla_unpack_rope(q_nope, o1, o2)
    kv3 = kv.reshape(S, H, MLA_NOPE + MLA_V)
    qh = q3.transpose(1, 0, 2).astype(BF16)
    kh = jnp.concatenate([kv3[:, :, :MLA_NOPE], jnp.broadcast_to(k_r[:, None, :], (S, H, MLA_ROPE))],
                         axis=-1).transpose(1, 0, 2).astype(BF16)
    vh = kv3[:, :, MLA_NOPE:].transpose(1, 0, 2).astype(BF16)
    a, lse = plan.call(_attn_fwd, nm("attn"), qh, kh, vh)
    o_ret, states = _ret_fwd(hR, tabs["cos_r"], tabs["sin_r"], tabs["ret"], name=nm("ret"))
    r = _gn_gate_fwd(o_ret, hR, p["ret_gn_g"], p["ret_gn_b"], name=nm("gn"))
    mix_in = jnp.concatenate([a.astype(BF16), r], axis=1)
    mix = mm("out", mix_in, w("w_out"))
    x1, x1b, x1_hat, rstd1 = _norm_fwd(x, p["ln1_g"], p["ln1_b"], res=mix, alpha=p["alpha"], eps=LN_EPS, name=nm("ln1"))
    g = mm("gate", x1b, w("w_gate"), tb=True)
    u = mm("up", x1b, w("w_up"), tb=True)
    act = _swiglu_fwd(g, u, name=nm("swiglu"))
    f = mm("down", act, w("w_down"))
    x2, x2b, x2_hat, rstd2 = _norm_fwd(x1, p["ln2_g"], p["ln2_b"], res=f, alpha=p["alpha"], eps=LN_EPS, name=nm("ln2"))
    saved = dict(xb=xb, qn=qn, qn_hat=qn_hat, q_rstd=q_rstd, kvn=kvn, kvn_hat=kvn_hat, kv_rstd=kv_rstd,
                 qh=qh, kh=kh, vh=vh, a=a, lse=lse, hR=hR, o_ret=o_ret, states=states, mix_in=mix_in,
                 x1b=x1b, x1_hat=x1_hat, rstd1=rstd1, g=g, u=u, act=act, x2_hat=x2_hat, rstd2=rstd2)
    return x2, x2b, saved


def _layer_bwd(dx2, sv, plan, p, tabs, l):
    S, D = dx2.shape
    H = MLA_HEADS
    nm = lambda s: "l%d_b_%s" % (l, s)
    w = lambda n: plan.full[(l, n)]
    mm = lambda name, *a, **kw: plan.call(_mm, nm(name), *a, **kw)
    alpha = p["alpha"]
    gw, gp = plan.grads, {}
    dz2, dz2b, gp["ln2_g"], gp["ln2_b"] = _norm_bwd(dx2, sv["x2_hat"], sv["rstd2"], p["ln2_g"], name=nm("ln2"))
    dact = mm("dact", dz2b, w("w_down"), tb=True)
    gw[(l, "w_down")] = mm("wdown", sv["act"], dz2b, ta=True, out_dtype=BF16)
    dg, du = _swiglu_bwd(dact, sv["g"], sv["u"], name=nm("swiglu"))
    gw[(l, "w_gate")] = mm("wgate", dg, sv["x1b"], ta=True, out_dtype=BF16)
    gw[(l, "w_up")] = mm("wup", du, sv["x1b"], ta=True, out_dtype=BF16)
    t = mm("dx1a", dg, w("w_gate"), add=dz2, add_scale=alpha)
    dx1 = mm("dx1b", du, w("w_up"), add=t)
    dz1, dz1b, gp["ln1_g"], gp["ln1_b"] = _norm_bwd(dx1, sv["x1_hat"], sv["rstd1"], p["ln1_g"], name=nm("ln1"))
    dmix = mm("dmix", dz1b, w("w_out"), tb=True)
    gw[(l, "w_out")] = mm("wout", sv["mix_in"], dz1b, ta=True, out_dtype=BF16)
    ret_col0 = (H * MLA_V) // RET_V
    do_ret, drg, gp["ret_gn_g"], gp["ret_gn_b"] = _gn_gate_bwd(
        dmix, sv["o_ret"], sv["hR"], p["ret_gn_g"], p["ret_gn_b"], dr_col0=ret_col0, name=nm("gn"))
    drq, drk, drv = _ret_bwd(do_ret, sv["hR"], sv["states"], tabs["cos_r"], tabs["sin_r"], tabs["ret"], name=nm("ret"))
    dqh, dkh, dvh = plan.call(_attn_bwd, nm("attn"), sv["qh"], sv["kh"], sv["vh"], sv["a"], dmix, sv["lse"], do_col0=0)
    dq3 = dqh.transpose(1, 0, 2)
    dk_r = _sum_heads(dkh, name=nm("dkr"))
    dq_nope, dt1, dt2 = _mla_pack_rope(dq3.reshape(S, H * MLA_QK), dk_r)
    di1, di2 = _rope(dt1, dt2, tabs["cos_m"], tabs["nsin_m"], name=nm("rope"))
    dq3, dkr = _mla_unpack_rope(dq_nope, di1, di2)
    dq = dq3.reshape(S, H * MLA_QK).astype(BF16)
    dkv = jnp.concatenate([dkh[:, :, :MLA_NOPE].transpose(1, 0, 2), dvh.transpose(1, 0, 2)],
                          axis=-1).reshape(S, -1).astype(BF16)
    gw[(l, "w_uq")] = mm("wuq", dq, sv["qn"], ta=True, out_dtype=BF16)
    dqn = mm("dqn", dq, w("w_uq"))
    gw[(l, "w_ukv")] = mm("wukv", dkv, sv["kvn"], ta=True, out_dtype=BF16)
    dkvn = mm("dkvn", dkv, w("w_ukv"))
    _, dcq, gp["q_norm_g"], _ = _norm_bwd(dqn, sv["qn_hat"], sv["q_rstd"], p["q_norm_g"], center=False,
                                          want_f32=False, name=nm("qnorm"))
    _, dckv, gp["kv_norm_g"], _ = _norm_bwd(dkvn, sv["kvn_hat"], sv["kv_rstd"], p["kv_norm_g"], center=False,
                                            want_f32=False, name=nm("kvnorm"))
    dh = jnp.concatenate([dcq, dckv, dkr.astype(BF16), drq, drk, drv, drg], axis=1)
    gw[(l, "w_in")] = mm("win", dh, sv["xb"], ta=True, out_dtype=BF16)
    dx = mm("dx", dh, w("w_in"), add=dz1, add_scale=alpha)
    return dx, gp


def _local_step(x, target, pos, small, plan, depth):
    alpha = (2 * depth) ** 0.25
    cos_m, sin_m = _rope_tables(pos, MLA_ROPE)
    cos_r, sin_r = _rope_tables(pos, RET_QK)
    reps = MLA_HEADS + 1
    tabs = dict(cos_m=jnp.tile(cos_m, (1, reps)), sin_m=jnp.tile(sin_m, (1, reps)),
                nsin_m=jnp.tile(-sin_m, (1, reps)), cos_r=cos_r, sin_r=sin_r, ret=_ret_tables(_pick(x.shape[0], RET_BLOCK, CHUNK)))
    h, hb, h_hat, h_rstd = _norm_fwd(x, small["ln_in_g"], small["ln_in_b"], eps=LN_EPS, name="ln_in")
    saved, ps = [], []
    for l in range(depth):
        p = {k: small[k][l] for k in _SMALL[2:]}
        p["alpha"] = alpha
        h, hb, sv = _layer_fwd(h, hb, plan, p, tabs, l)
        saved.append(sv)
        ps.append(p)
    dy, loss = _loss_head(h, target, name="loss")
    gps = [None] * depth
    for l in reversed(range(depth)):
        dy, gps[l] = _layer_bwd(dy, saved[l], plan, ps[l], tabs, l)
    grad_x, _, g_in_g, g_in_b = _norm_bwd(dy, h_hat, h_rstd, small["ln_in_g"], want_bf16=False, name="b_ln_in")
    gsmall = {"ln_in_g": g_in_g, "ln_in_b": g_in_b}
    for k in _SMALL[2:]:
        gsmall[k] = jnp.stack([gps[l][k] for l in range(depth)])
    return loss, grad_x, gsmall


def kernel(x, positions, ln_in_g, ln_in_b, w_in, q_norm_g, kv_norm_g, w_uq, w_ukv, ret_gn_g, ret_gn_b, w_out, ln1_g, ln1_b, w_gate, w_up, w_down, ln2_g, ln2_b, loss_target, m_ln_in_g, m_ln_in_b, m_w_in, m_q_norm_g, m_kv_norm_g, m_w_uq, m_w_ukv, m_ret_gn_g, m_ret_gn_b, m_w_out, m_ln1_g, m_ln1_b, m_w_gate, m_w_up, m_w_down, m_ln2_g, m_ln2_b, v_ln_in_g, v_ln_in_b, v_w_in, v_q_norm_g, v_kv_norm_g, v_w_uq, v_w_ukv, v_ret_gn_g, v_ret_gn_b, v_w_out, v_ln1_g, v_ln1_b, v_w_gate, v_w_up, v_w_down, v_ln2_g, v_ln2_b):
    names = ["ln_in_g", "ln_in_b", "w_in", "q_norm_g", "kv_norm_g", "w_uq", "w_ukv", "ret_gn_g", "ret_gn_b", "w_out",
             "ln1_g", "ln1_b", "w_gate", "w_up", "w_down", "ln2_g", "ln2_b"]
    wv = dict(zip(names, (ln_in_g, ln_in_b, w_in, q_norm_g, kv_norm_g, w_uq, w_ukv, ret_gn_g, ret_gn_b, w_out,
                          ln1_g, ln1_b, w_gate, w_up, w_down, ln2_g, ln2_b)))
    mv = dict(zip(names, (m_ln_in_g, m_ln_in_b, m_w_in, m_q_norm_g, m_kv_norm_g, m_w_uq, m_w_ukv, m_ret_gn_g,
                          m_ret_gn_b, m_w_out, m_ln1_g, m_ln1_b, m_w_gate, m_w_up, m_w_down, m_ln2_g, m_ln2_b)))
    vv = dict(zip(names, (v_ln_in_g, v_ln_in_b, v_w_in, v_q_norm_g, v_kv_norm_g, v_w_uq, v_w_ukv, v_ret_gn_g,
                          v_ret_gn_b, v_w_out, v_ln1_g, v_ln1_b, v_w_gate, v_w_up, v_w_down, v_ln2_g, v_ln2_b)))
    depth = w_in.shape[0]
    assert depth == 2, "the exchange plan is written for two layers"

    keys = [(l, n) for l in range(depth) for n in _BIG]
    plan = _Plan({(l, n): (wv[n][l].T if _TRANSPOSED[n] else wv[n][l]).astype(BF16) for l, n in keys})
    plan.gather_first()

    small = {n: wv[n] for n in _SMALL}
    loss, grad_x, gsmall = _local_step(x[0], loss_target[0], positions[0], small, plan, depth)
    loss = lax.psum(loss, MESH_AXES)
    plan.send_last()

    gshard = {n: [None] * depth for n in _BIG}
    for l, n in keys:
        tot = _sum_slots(plan.recv[(l, n)], name="sum_l%d_%s" % (l, n))
        gshard[n][l] = tot.T if _TRANSPOSED[n] else tot
    grads = {n: jnp.stack(v) for n, v in gshard.items()}

    flat = jnp.concatenate([gsmall[n].reshape(-1) for n in _SMALL])
    n_small = flat.shape[0]
    rows = -(-n_small // (SMALL_COLS * 8)) * 8
    flat = jnp.pad(flat, (0, rows * SMALL_COLS - n_small)).reshape(rows, SMALL_COLS)
    tot = _sum_slots(_exchange(_job("ag", [flat]), name="ag_small")[0], name="sum_small").reshape(-1)
    off = 0
    for n in _SMALL:
        grads[n] = tot[off:off + wv[n].size].reshape(wv[n].shape)
        off += wv[n].size

    delta, new_m, new_v = {}, {}, {}
    for n in names:
        w2 = wv[n] if wv[n].ndim > 1 else wv[n].reshape(1, -1)
        d, nm_, nv_ = _adamw(w2, grads[n].reshape(w2.shape), mv[n].reshape(w2.shape), vv[n].reshape(w2.shape),
                             name="adamw_" + n)
        delta[n], new_m[n], new_v[n] = d.reshape(wv[n].shape), nm_.reshape(wv[n].shape), nv_.reshape(wv[n].shape)

    return (loss, grad_x[None], *[grads[n] for n in names], *[delta[n] for n in names],
            *[new_m[n] for n in names], *[new_v[n] for n in names])
```

```python
import jax
import jax.numpy as jnp
from jax import lax
from jax.experimental import pallas as pl
from jax.experimental.pallas import tpu as pltpu

F32 = jnp.float32
BF16 = jnp.bfloat16

CHUNK = 64
CHUNK_SHIFT = 6
MLA_HEADS = 8
MLA_Q_LORA = 512
MLA_KV_LORA = 256
MLA_NOPE = 128
MLA_ROPE = 64
MLA_V = 128
MLA_QK = MLA_NOPE + MLA_ROPE
RET_HEADS = 4
RET_QK = 256
RET_V = 256
ROPE_THETA = 10000.0
LN_EPS = 1e-5
RMS_EPS = 1e-6
GN_EPS = 1e-5
ADAM_LR = 0.001
ADAM_B1 = 0.9
ADAM_B2 = 0.999
ADAM_EPS = 1e-08
ADAM_WD = 0.01
ADAM_STEP = 10

N_DEV = 8
MESH_AXES = ("x", "y", "c")
VMEM_LIMIT_BYTES = 56 * 1024 * 1024
MM_VMEM_BUDGET = 40 * 1024 * 1024
MM_ACC_PENALTY = 0.85
ATTN_BLOCK = 256
ATTN_KEY_STEP = 512
RET_BLOCK = 256


def _pick(n, pref, mult):
    best = None
    d = mult
    while d <= min(n, pref):
        if n % d == 0:
            best = d
        d += mult
    return n if best is None else best


def _divisors(n, mult, cap):
    ds = [d for d in range(mult, min(n, cap) + 1, mult) if n % d == 0]
    return ds or [n]


def _pcall(body, **kw):
    return pl.pallas_call(body, **kw)


def _params(sem):
    return pltpu.CompilerParams(dimension_semantics=sem, vmem_limit_bytes=VMEM_LIMIT_BYTES)


def _sds(shape, dtype):
    return jax.ShapeDtypeStruct(shape, dtype)


def _mm_tiles(M, N, K, ta, sa, sb, tile_bytes):
    best = None
    for bk in _divisors(K, 128, 8192):
        nk = K // bk
        for bm in _divisors(M, 128 if ta else 16, 1024):
            for bn in _divisors(N, 128, 1024):
                vmem = 2 * (bm * bk * sa + bk * bn * sb) + 2 * bm * bn * tile_bytes + (bm * bn * 4 if nk > 1 else 0)
                if vmem > MM_VMEM_BUDGET:
                    continue
                flops_per_byte = 1.0 / (1.0 / bm + (1.0 / max(N, bn) if nk == 1 else 1.0 / bn))
                score = (flops_per_byte * (1.0 if nk == 1 else MM_ACC_PENALTY), bk, bn, bm)
                if best is None or score > best[0]:
                    best = (score, bm, bn, bk)
    assert best is not None, (M, N, K)
    return best[1:]


def _mm(a, b, *, ta=False, tb=False, add=None, add_scale=1.0, out_dtype=F32, extras=(), epilogue=None,
        out_dtypes=None, job=None, name):
    if ta:
        K, M = a.shape
    else:
        M, K = a.shape
    if tb:
        N, K2 = b.shape
    else:
        K2, N = b.shape
    assert K == K2, (a.shape, b.shape, ta, tb)
    if epilogue is None:
        assert not extras and out_dtypes is None
        out_dtypes = (out_dtype,)
        if add is not None:
            extras, epilogue = (add,), lambda r, c: (r + add_scale * c,)
        else:
            epilogue = lambda r: (r,)
    n_ex, n_out = len(extras), len(out_dtypes)
    tile_bytes = sum(e.dtype.itemsize for e in extras) + sum(jnp.dtype(d).itemsize for d in out_dtypes)
    bm, bn, bk = _mm_tiles(M, N, K, ta, a.dtype.itemsize, b.dtype.itemsize, tile_bytes)
    nk = K // bk
    a_spec = (pl.BlockSpec((bk, bm), lambda i, j, k: (k, i)) if ta
              else pl.BlockSpec((bm, bk), lambda i, j, k: (i, k)))
    b_spec = (pl.BlockSpec((bn, bk), lambda i, j, k: (j, k)) if tb
              else pl.BlockSpec((bk, bn), lambda i, j, k: (k, j)))
    o_spec = pl.BlockSpec((bm, bn), lambda i, j, k: (i, j))
    dims = (((0 if ta else 1,), (1 if tb else 0,)), ((), ()))

    def body(*refs):
        a_ref, b_ref = refs[0], refs[1]
        ex_refs, o_refs = refs[2:2 + n_ex], refs[2 + n_ex:2 + n_ex + n_out]
        part = lax.dot_general(a_ref[...].astype(BF16), b_ref[...].astype(BF16), dims, preferred_element_type=F32)

        def finish(r):
            for o_ref, v in zip(o_refs, epilogue(r, *[e[...] for e in ex_refs])):
                o_ref[...] = v.astype(o_ref.dtype)

        if nk == 1:
            finish(part)
        else:
            acc_ref = refs[-1]
            k = pl.program_id(2)

            @pl.when(k == 0)
            def _():
                acc_ref[...] = part

            @pl.when(k > 0)
            def _():
                acc_ref[...] += part

            @pl.when(k == nk - 1)
            def _():
                finish(acc_ref[...])

    outs, jouts = _hosted_call(
        body, job, name=name, grid=(M // bm, N // bn, nk), in_specs=[a_spec, b_spec] + [o_spec] * n_ex,
        out_specs=[o_spec] * n_out, out_shape=[_sds((M, N), d) for d in out_dtypes],
        scratch_shapes=[pltpu.VMEM((bm, bn), F32)] if nk > 1 else [],
        semantics=("parallel", "parallel", "arbitrary"), args=(a, b, *extras))
    res = list(outs) + ([jouts] if job is not None else [])
    return res[0] if len(res) == 1 else tuple(res)


def _norm_fwd(x, g, b, *, res=None, alpha=1.0, center=True, eps, want_f32=True, want_bf16=True, name):
    S, W = x.shape
    bs = _pick(S, 256, 16)
    has_res, has_b = res is not None, b is not None

    def body(*refs):
        it = iter(refs)
        x_ref = next(it)
        res_ref = next(it) if has_res else None
        g_ref = next(it)
        b_ref = next(it) if has_b else None
        y_ref = next(it) if want_f32 else None
        yb_ref = next(it) if want_bf16 else None
        xh_ref, r_ref = next(it), next(it)
        z = x_ref[...]
        if has_res:
            z = alpha * z + res_ref[...]
        if center:
            z = z - jnp.mean(z, axis=-1, keepdims=True)
        rstd = lax.rsqrt(jnp.mean(z * z, axis=-1, keepdims=True) + eps)
        xh = z * rstd
        y = xh * g_ref[...]
        if has_b:
            y = y + b_ref[...]
        if want_f32:
            y_ref[...] = y
        if want_bf16:
            yb_ref[...] = y.astype(BF16)
        xh_ref[...] = xh
        r_ref[...] = rstd

    row = pl.BlockSpec((bs, W), lambda i: (i, 0))
    vec = pl.BlockSpec((1, W), lambda i: (0, 0))
    in_specs = [row] + ([row] if has_res else []) + [vec] + ([vec] if has_b else [])
    args = (x,) + ((res,) if has_res else ()) + (g.reshape(1, W),) + ((b.reshape(1, W),) if has_b else ())
    out_specs = ([row] if want_f32 else []) + ([row] if want_bf16 else []) + [row, pl.BlockSpec((bs, 1), lambda i: (i, 0))]
    out_shape = (([_sds((S, W), F32)] if want_f32 else []) + ([_sds((S, W), BF16)] if want_bf16 else [])
                 + [_sds((S, W), F32), _sds((S, 1), F32)])
    outs = list(_pcall(body, name=name, grid=(S // bs,), in_specs=in_specs, out_specs=out_specs, out_shape=out_shape,
                       compiler_params=_params(("parallel",)))(*args))
    y = outs.pop(0) if want_f32 else None
    yb = outs.pop(0) if want_bf16 else None
    return y, yb, outs[0], outs[1]


def _norm_bwd(dy, xh, rstd, g, *, center=True, want_f32=True, want_bf16=True, name):
    S, W = dy.shape
    bs = _pick(S, 256, 16)

    def body(*refs):
        dy_ref, xh_ref, r_ref, g_ref = refs[:4]
        it = iter(refs[4:])
        dz_ref = next(it) if want_f32 else None
        dzb_ref = next(it) if want_bf16 else None
        dg_ref, db_ref = next(it), next(it)

        @pl.when(pl.program_id(0) == 0)
        def _():
            dg_ref[...] = jnp.zeros_like(dg_ref)
            db_ref[...] = jnp.zeros_like(db_ref)

        dyv = dy_ref[...]
        xhv = xh_ref[...]
        dyg = dyv * g_ref[...]
        m2 = jnp.mean(dyg * xhv, axis=-1, keepdims=True)
        t = dyg - xhv * m2
        if center:
            t = t - jnp.mean(dyg, axis=-1, keepdims=True)
        dz = r_ref[...] * t
        if want_f32:
            dz_ref[...] = dz
        if want_bf16:
            dzb_ref[...] = dz.astype(BF16)
        dg_ref[...] += jnp.sum(dyv * xhv, axis=0, keepdims=True)
        db_ref[...] += jnp.sum(dyv, axis=0, keepdims=True)

    row = pl.BlockSpec((bs, W), lambda i: (i, 0))
    vec = pl.BlockSpec((1, W), lambda i: (0, 0))
    out_specs = ([row] if want_f32 else []) + ([row] if want_bf16 else []) + [vec, vec]
    out_shape = (([_sds((S, W), F32)] if want_f32 else []) + ([_sds((S, W), BF16)] if want_bf16 else [])
                 + [_sds((1, W), F32), _sds((1, W), F32)])
    outs = list(_pcall(body, name=name, grid=(S // bs,),
                       in_specs=[row, row, pl.BlockSpec((bs, 1), lambda i: (i, 0)), vec],
                       out_specs=out_specs, out_shape=out_shape,
                       compiler_params=_params(("arbitrary",)))(dy, xh, rstd, g.reshape(1, W)))
    dz = outs.pop(0) if want_f32 else None
    dzb = outs.pop(0) if want_bf16 else None
    return dz, dzb, outs[0][0], outs[1][0]


def _rope(t1, t2, cos, sin, *, name):
    S, C = t1.shape
    bs = _pick(S, 512, 8)

    def body(t1_ref, t2_ref, c_ref, s_ref, o1_ref, o2_ref):
        a, b, c, s = t1_ref[...], t2_ref[...], c_ref[...], s_ref[...]
        o1_ref[...] = a * c - b * s
        o2_ref[...] = b * c + a * s

    row = pl.BlockSpec((bs, C), lambda i: (i, 0))
    return _pcall(
        body, name=name, grid=(S // bs,), in_specs=[row] * 4, out_specs=[row, row],
        out_shape=[_sds((S, C), F32)] * 2, compiler_params=_params(("parallel",)),
    )(t1, t2, cos, sin)


def _chunk_mask(row0, B, L):
    rows = lax.shift_right_logical(row0 + lax.broadcasted_iota(jnp.int32, (B, L), 0), CHUNK_SHIFT)
    cols = lax.shift_right_logical(lax.broadcasted_iota(jnp.int32, (B, L), 1), CHUNK_SHIFT)
    return rows >= cols


def _for_key_prefix(qi, B, G, S, fn):
    per = G // B
    for b in range(S // G):
        pl.when(qi // per == b)(lambda b=b: fn((b + 1) * G))


def _nt(a, b):
    return lax.dot_general(a, b, (((1,), (1,)), ((), ())), preferred_element_type=F32)


def _nn(a, b):
    return lax.dot_general(a, b, (((1,), (0,)), ((), ())), preferred_element_type=F32)


def _tn(a, b):
    return lax.dot_general(a, b, (((0,), (0,)), ((), ())), preferred_element_type=F32)


def _attn_fwd(qh, kh, vh, *, job=None, name):
    H, S, DQ = qh.shape
    DV = vh.shape[-1]
    B = _pick(S, ATTN_BLOCK, CHUNK)
    G = _pick(S, ATTN_KEY_STEP, B)
    scale = float(DQ) ** -0.5
    neg = float(jnp.finfo(jnp.float32).min)

    def body(q_ref, k_ref, v_ref, o_ref, lse_ref):
        qi = pl.program_id(1)

        def run(L):
            s = _nt(q_ref[...], k_ref[0:L, :]) * scale
            s = jnp.where(_chunk_mask(qi * B, B, L), s, neg)
            m = jnp.max(s, axis=-1, keepdims=True)
            e = jnp.exp(s - m)
            l = jnp.sum(e, axis=-1, keepdims=True)
            o_ref[...] = _nn((e / l).astype(BF16), v_ref[0:L, :])
            lse_ref[...] = m + jnp.log(l)

        _for_key_prefix(qi, B, G, S, run)

    outs, jouts = _hosted_call(
        body, job, name=name, grid=(H, S // B),
        in_specs=[pl.BlockSpec((None, B, DQ), lambda h, i: (h, i, 0)),
                  pl.BlockSpec((None, S, DQ), lambda h, i: (h, 0, 0)),
                  pl.BlockSpec((None, S, DV), lambda h, i: (h, 0, 0))],
        out_specs=[pl.BlockSpec((B, DV), lambda h, i: (i, h)),
                   pl.BlockSpec((None, B, 1), lambda h, i: (h, i, 0))],
        out_shape=[_sds((S, H * DV), F32), _sds((H, S, 1), F32)], scratch_shapes=[],
        semantics=("parallel", "parallel"), args=(qh, kh, vh))
    return (outs[0], outs[1]) if job is None else (outs[0], outs[1], jouts)


def _attn_bwd(qh, kh, vh, o, do, lse, *, do_col0, job=None, name):
    H, S, DQ = qh.shape
    DV = vh.shape[-1]
    B = _pick(S, ATTN_BLOCK, CHUNK)
    G = _pick(S, ATTN_KEY_STEP, B)
    scale = float(DQ) ** -0.5

    def body(q_ref, k_ref, v_ref, o_ref, do_ref, lse_ref, dq_ref, dk_ref, dv_ref):
        qi = pl.program_id(1)

        @pl.when(qi == 0)
        def _():
            dk_ref[...] = jnp.zeros_like(dk_ref)
            dv_ref[...] = jnp.zeros_like(dv_ref)

        def run(L):
            q = q_ref[...]
            k = k_ref[0:L, :]
            dov = do_ref[...]
            dob = dov.astype(BF16)
            p = jnp.exp(_nt(q, k) * scale - lse_ref[...])
            p = jnp.where(_chunk_mask(qi * B, B, L), p, 0.0)
            dsum = jnp.sum(dov * o_ref[...], axis=-1, keepdims=True)
            ds = (p * (_nt(dob, v_ref[0:L, :]) - dsum) * scale).astype(BF16)
            dv_ref[0:L, :] += _tn(p.astype(BF16), dob)
            dk_ref[0:L, :] += _tn(ds, q)
            dq_ref[...] = _nn(ds, k)

        _for_key_prefix(qi, B, G, S, run)

    outs, jouts = _hosted_call(
        body, job, name=name, grid=(H, S // B),
        in_specs=[pl.BlockSpec((None, B, DQ), lambda h, i: (h, i, 0)),
                  pl.BlockSpec((None, S, DQ), lambda h, i: (h, 0, 0)),
                  pl.BlockSpec((None, S, DV), lambda h, i: (h, 0, 0)),
                  pl.BlockSpec((B, DV), lambda h, i: (i, h)),
                  pl.BlockSpec((B, DV), lambda h, i: (i, do_col0 + h)),
                  pl.BlockSpec((None, B, 1), lambda h, i: (h, i, 0))],
        out_specs=[pl.BlockSpec((None, B, DQ), lambda h, i: (h, i, 0)),
                   pl.BlockSpec((None, S, DQ), lambda h, i: (h, 0, 0)),
                   pl.BlockSpec((None, S, DV), lambda h, i: (h, 0, 0))],
        out_shape=[_sds((H, S, DQ), F32), _sds((H, S, DQ), F32), _sds((H, S, DV), F32)], scratch_shapes=[],
        semantics=("parallel", "arbitrary"), args=(qh, kh, vh, o, do, lse))
    return tuple(outs) if job is None else (*outs, jouts)


def _ret_tables(T):
    H = RET_HEADS
    log_gamma = jnp.log1p(-jnp.exp2(-5.0 - jnp.arange(H, dtype=F32)))
    idx = jnp.arange(T, dtype=F32)
    chunk = jnp.arange(T) // CHUNK
    visible = chunk[:, None] >= chunk[None, :]
    intra = jnp.where(visible[None], jnp.exp(log_gamma[:, None, None] * jnp.abs(idx[:, None] - idx[None, :])), 0.0)
    qd = jnp.exp(log_gamma[:, None] * (idx + 1.0))[:, :, None]
    kd = jnp.exp(log_gamma[:, None] * (T - 1.0 - idx))[:, :, None]
    cd = jnp.exp(log_gamma * T)[:, None, None]
    return intra, qd, kd, cd


def _rot(t, c, s):
    half = t.shape[-1] // 2
    t1, t2 = t[:, :half], t[:, half:]
    return jnp.concatenate([t1 * c - t2 * s, t2 * c + t1 * s], axis=-1)


def _rot_t(t, c, s):
    half = t.shape[-1] // 2
    t1, t2 = t[:, :half], t[:, half:]
    return jnp.concatenate([t1 * c + t2 * s, t2 * c - t1 * s], axis=-1)


def _dot(a, b, ca, cb):
    return lax.dot_general(a.astype(BF16), b.astype(BF16), (((ca,), (cb,)), ((), ())), preferred_element_type=F32)


def _ret_fwd(hR, cos, sin, tables, *, name):
    S = hR.shape[0]
    H, L, DK, DV = RET_HEADS, tables[0].shape[-1], RET_QK, RET_V
    NC = S // L
    qscale = float(DK) ** -0.5
    intra, qd, kd, cd = tables

    def body(q_ref, k_ref, v_ref, c_ref, s_ref, in_ref, qd_ref, kd_ref, cd_ref, o_ref, st_ref, state):
        @pl.when(pl.program_id(1) == 0)
        def _():
            state[...] = jnp.zeros_like(state)

        c, s = c_ref[...], s_ref[...]
        q = _rot(q_ref[...], c, s) * qscale
        k = _rot(k_ref[...], c, s)
        v = v_ref[...]
        st = state[...]
        st_ref[...] = st.astype(BF16)
        scores = _dot(q, k, 1, 1) * in_ref[...]
        o_ref[...] = _dot(scores, v, 1, 0) + _dot(q * qd_ref[...], st, 1, 0)
        state[...] = st * cd_ref[...] + _dot(k * kd_ref[...], v, 0, 0)

    blk = lambda off: pl.BlockSpec((L, DK), lambda h, c: (c, off + h))
    rope = pl.BlockSpec((L, DK // 2), lambda h, c: (c, 0))
    return _pcall(
        body, name=name, grid=(H, NC),
        in_specs=[blk(0), blk(H), blk(2 * H), rope, rope,
                  pl.BlockSpec((None, L, L), lambda h, c: (h, 0, 0)),
                  pl.BlockSpec((None, L, 1), lambda h, c: (h, 0, 0)),
                  pl.BlockSpec((None, L, 1), lambda h, c: (h, 0, 0)),
                  pl.BlockSpec((None, 1, 1), lambda h, c: (h, 0, 0))],
        out_specs=[pl.BlockSpec((L, DV), lambda h, c: (c, h)),
                   pl.BlockSpec((None, None, DK, DV), lambda h, c: (h, c, 0, 0))],
        out_shape=[_sds((S, H * DV), F32), _sds((H, NC, DK, DV), BF16)],
        scratch_shapes=[pltpu.VMEM((DK, DV), F32)],
        compiler_params=_params(("arbitrary", "arbitrary")),
    )(hR, hR, hR, cos, sin, intra, qd, kd, cd)


def _ret_bwd(do, hR, states, cos, sin, tables, *, name):
    S = hR.shape[0]
    H, L, DK, DV = RET_HEADS, tables[0].shape[-1], RET_QK, RET_V
    NC = S // L
    qscale = float(DK) ** -0.5
    intra, qd, kd, cd = tables

    def body(do_ref, q_ref, k_ref, v_ref, st_ref, c_ref, s_ref, in_ref, qd_ref, kd_ref, cd_ref,
             dq_ref, dk_ref, dv_ref, dstate):
        @pl.when(pl.program_id(1) == 0)
        def _():
            dstate[...] = jnp.zeros_like(dstate)

        c, s = c_ref[...], s_ref[...]
        q = _rot(q_ref[...], c, s) * qscale
        k = _rot(k_ref[...], c, s)
        v = v_ref[...]
        dov = do_ref[...]
        dst = dstate[...]
        dec = in_ref[...]
        qdv, kdv = qd_ref[...], kd_ref[...]
        scores = _dot(q, k, 1, 1) * dec
        da = _dot(dov, v, 1, 1) * dec
        dv_ref[...] = (_dot(scores, dov, 0, 0) + _dot(k * kdv, dst, 1, 0)).astype(BF16)
        dq = _dot(da, k, 1, 0) + _dot(dov, st_ref[...], 1, 1) * qdv
        dk = _dot(da, q, 0, 0) + _dot(v, dst, 1, 1) * kdv
        dq_ref[...] = _rot_t(dq * qscale, c, s).astype(BF16)
        dk_ref[...] = _rot_t(dk, c, s).astype(BF16)
        dstate[...] = dst * cd_ref[...] + _dot(q * qdv, dov, 0, 0)

    rev = lambda c: NC - 1 - c
    blk = lambda off: pl.BlockSpec((L, DK), lambda h, c: (rev(c), off + h))
    rope = pl.BlockSpec((L, DK // 2), lambda h, c: (rev(c), 0))
    out = pl.BlockSpec((L, DK), lambda h, c: (rev(c), h))
    return _pcall(
        body, name=name, grid=(H, NC),
        in_specs=[out, blk(0), blk(H), blk(2 * H),
                  pl.BlockSpec((None, None, DK, DV), lambda h, c: (h, rev(c), 0, 0)), rope, rope,
                  pl.BlockSpec((None, L, L), lambda h, c: (h, 0, 0)),
                  pl.BlockSpec((None, L, 1), lambda h, c: (h, 0, 0)),
                  pl.BlockSpec((None, L, 1), lambda h, c: (h, 0, 0)),
                  pl.BlockSpec((None, 1, 1), lambda h, c: (h, 0, 0))],
        out_specs=[out, out, out],
        out_shape=[_sds((S, H * DK), BF16)] * 3,
        scratch_shapes=[pltpu.VMEM((DK, DV), F32)],
        compiler_params=_params(("arbitrary", "arbitrary")),
    )(do, hR, hR, hR, states, cos, sin, intra, qd, kd, cd)


def _gn_gate_fwd(o, hR, g, b, *, name):
    S = o.shape[0]
    H, DV = RET_HEADS, RET_V
    bs = _pick(S, 512, 16)

    def body(o_ref, rg_ref, g_ref, b_ref, r_ref):
        z = o_ref[...]
        z = z - jnp.mean(z, axis=-1, keepdims=True)
        xh = z * lax.rsqrt(jnp.mean(z * z, axis=-1, keepdims=True) + GN_EPS)
        rg = rg_ref[...]
        r_ref[...] = ((rg * jax.nn.sigmoid(rg)) * (xh * g_ref[...] + b_ref[...])).astype(BF16)

    row = pl.BlockSpec((bs, DV), lambda i, h: (i, h))
    vec = pl.BlockSpec((1, DV), lambda i, h: (0, h))
    return _pcall(
        body, name=name, grid=(S // bs, H),
        in_specs=[row, pl.BlockSpec((bs, DV), lambda i, h: (i, 3 * H + h)), vec, vec],
        out_specs=row, out_shape=_sds((S, H * DV), BF16),
        compiler_params=_params(("parallel", "parallel")),
    )(o, hR, g.reshape(1, H * DV), b.reshape(1, H * DV))


def _gn_gate_bwd(dr, o, hR, g, b, *, dr_col0, name):
    S = o.shape[0]
    H, DV = RET_HEADS, RET_V
    bs = _pick(S, 512, 16)

    def body(dr_ref, o_ref, rg_ref, g_ref, b_ref, do_ref, drg_ref, dg_ref, db_ref):
        @pl.when(pl.program_id(1) == 0)
        def _():
            dg_ref[...] = jnp.zeros_like(dg_ref)
            db_ref[...] = jnp.zeros_like(db_ref)

        z = o_ref[...]
        z = z - jnp.mean(z, axis=-1, keepdims=True)
        rstd = lax.rsqrt(jnp.mean(z * z, axis=-1, keepdims=True) + GN_EPS)
        xh = z * rstd
        gv = g_ref[...]
        y = xh * gv + b_ref[...]
        rg = rg_ref[...]
        sg = jax.nn.sigmoid(rg)
        drv = dr_ref[...]
        dy = drv * (rg * sg)
        drg_ref[...] = (drv * y * (sg * (1.0 + rg * (1.0 - sg)))).astype(BF16)
        dg_ref[...] += jnp.sum(dy * xh, axis=0, keepdims=True)
        db_ref[...] += jnp.sum(dy, axis=0, keepdims=True)
        dxh = dy * gv
        do_ref[...] = rstd * (dxh - jnp.mean(dxh, axis=-1, keepdims=True)
                              - xh * jnp.mean(dxh * xh, axis=-1, keepdims=True))

    row = pl.BlockSpec((bs, DV), lambda h, i: (i, h))
    vec = pl.BlockSpec((1, DV), lambda h, i: (0, h))
    do, drg, dg, db = _pcall(
        body, name=name, grid=(H, S // bs),
        in_specs=[pl.BlockSpec((bs, DV), lambda h, i: (i, dr_col0 + h)), row,
                  pl.BlockSpec((bs, DV), lambda h, i: (i, 3 * H + h)), vec, vec],
        out_specs=[row, row, vec, vec],
        out_shape=[_sds((S, H * DV), F32), _sds((S, H * DV), BF16), _sds((1, H * DV), F32), _sds((1, H * DV), F32)],
        compiler_params=_params(("arbitrary", "arbitrary")),
    )(dr, o, hR, g.reshape(1, H * DV), b.reshape(1, H * DV))
    return do, drg, dg[0], db[0]


def _swiglu(u, g):
    return g, u, (g * jax.nn.sigmoid(g)) * u


def _swiglu_bwd(da, g, u):
    g, u = g.astype(F32), u.astype(F32)
    sg = jax.nn.sigmoid(g)
    return da * u * (sg * (1.0 + g * (1.0 - sg))), da * (g * sg)


def _loss_head(y, t, *, name):
    S, D = y.shape
    bs = _pick(S, 256, 8)
    inv_d = 1.0 / D

    def body(y_ref, t_ref, dy_ref, l_ref):
        @pl.when(pl.program_id(0) == 0)
        def _():
            l_ref[...] = jnp.zeros_like(l_ref)

        e = y_ref[...] - t_ref[...]
        dy_ref[...] = e * inv_d
        l_ref[...] += 0.5 * jnp.sum(jnp.mean(e * e, axis=-1, keepdims=True), axis=0, keepdims=True)

    row = pl.BlockSpec((bs, D), lambda i: (i, 0))
    dy, l = _pcall(
        body, name=name, grid=(S // bs,), in_specs=[row, row],
        out_specs=[row, pl.BlockSpec((1, 1), lambda i: (0, 0))],
        out_shape=[_sds((S, D), F32), _sds((1, 1), F32)],
        compiler_params=_params(("arbitrary",)),
    )(y, t)
    return dy, l[0, 0]


def _adamw(w, g, m, v, *, name):
    shape = w.shape
    C = shape[-1]
    R = w.size // C
    br = _pick(R, 512, 8)

    def body(w_ref, g_ref, m_ref, v_ref, d_ref, nm_ref, nv_ref):
        gv = g_ref[...]
        mn = ADAM_B1 * m_ref[...] + (1.0 - ADAM_B1) * gv
        vn = ADAM_B2 * v_ref[...] + (1.0 - ADAM_B2) * (gv * gv)
        m_hat = mn / (1.0 - ADAM_B1 ** ADAM_STEP)
        v_hat = vn / (1.0 - ADAM_B2 ** ADAM_STEP)
        d_ref[...] = -ADAM_LR * (m_hat / (jnp.sqrt(v_hat) + ADAM_EPS) + ADAM_WD * w_ref[...])
        nm_ref[...] = mn
        nv_ref[...] = vn

    blk = pl.BlockSpec((br, C), lambda i: (i, 0))
    outs = _pcall(body, name=name, grid=(R // br,), in_specs=[blk] * 4, out_specs=[blk] * 3,
                  out_shape=[_sds((R, C), F32)] * 3,
                  compiler_params=_params(("parallel",)))(*[a.reshape(R, C) for a in (w, g, m, v)])
    return tuple(o.reshape(shape) for o in outs)


def _slot_block(n, k):
    return (_pick(n, 256, 16) if n % 16 == 0 else n), _pick(k, 512, 128)


def _sum_slots(x, *, name):
    ns, n, k = x.shape
    br, bc = _slot_block(n, k)

    def body(x_ref, o_ref):
        acc = x_ref[0].astype(F32)
        for s in range(1, ns):
            acc = acc + x_ref[s].astype(F32)
        o_ref[...] = acc

    return _pcall(body, name=name, grid=(n // br, k // bc),
                  in_specs=[pl.BlockSpec((ns, br, bc), lambda i, j: (0, i, j))],
                  out_specs=pl.BlockSpec((br, bc), lambda i, j: (i, j)),
                  out_shape=_sds((n, k), F32), compiler_params=_params(("parallel", "parallel")))(x)


def _pair_sum(g, r, *, name):
    _, n, k = r.shape
    br, bc = _slot_block(n, k)
    core = lax.axis_index("c").astype(jnp.int32).reshape(1)

    def body(c_ref, g_ref, r_ref, o_ref):
        o_ref[...] = (g_ref[...].astype(F32) + r_ref[...].astype(F32)).astype(o_ref.dtype)

    blk = pl.BlockSpec((None, br, bc), lambda s, i, j, c_ref: (s, i, j))
    return _pcall(
        body, name=name, out_shape=_sds(r.shape, g.dtype),
        grid_spec=pltpu.PrefetchScalarGridSpec(
            num_scalar_prefetch=1, grid=(N_CHIP, n // br, k // bc),
            in_specs=[pl.BlockSpec((None, br, bc), lambda s, i, j, c_ref: (2 * s + c_ref[0], i, j)), blk],
            out_specs=blk),
        compiler_params=_params(("parallel", "parallel", "parallel")),
    )(core, g, r)


def _sum_heads(dkh, *, name):
    H, S, DQ = dkh.shape
    bs = _pick(S, 512, 8)

    def body(d_ref, o_ref):
        acc = d_ref[0][:, MLA_NOPE:]
        for h in range(1, H):
            acc = acc + d_ref[h][:, MLA_NOPE:]
        o_ref[...] = acc

    return _pcall(body, name=name, grid=(S // bs,),
                  in_specs=[pl.BlockSpec((H, bs, DQ), lambda i: (0, i, 0))],
                  out_specs=pl.BlockSpec((bs, MLA_ROPE), lambda i: (i, 0)),
                  out_shape=_sds((S, MLA_ROPE), F32), compiler_params=_params(("parallel",)))(dkh)


N_PEER = N_DEV - 1
N_CHIP = N_DEV // 2
HOST_TAIL_FRACTION = 8


def _coords():
    return lax.axis_index("x"), lax.axis_index("y"), lax.axis_index("c")


def _ag_phase(phase, x_refs, out_refs, send_sems, recv_sems, local_sems):
    n = len(x_refs)
    x, y, c = _coords()
    me, sibling = (x, y, c), (x, y, 1 - c)
    chips = [(1 - x, y), (x, 1 - y), (1 - x, 1 - y)]

    def copy(a, k, block, to, src=None):
        px, py, pc = block
        dst = out_refs[a].at[4 * px + 2 * py + pc]
        return pltpu.make_async_remote_copy(
            src_ref=dst if src is None else src, dst_ref=dst,
            send_sem=send_sems.at[a * N_PEER + k], recv_sem=recv_sems.at[a * N_PEER + k],
            device_id=to, device_id_type=pl.DeviceIdType.MESH)

    def local(a):
        return pltpu.make_async_copy(x_refs[a], out_refs[a].at[4 * x + 2 * y + c], local_sems.at[a])

    def first(a):
        return ([copy(a, 0, me, sibling, src=x_refs[a])]
                + [copy(a, 1 + j, me, (*chip, c), src=x_refs[a]) for j, chip in enumerate(chips)])

    def passed(a, j):
        return copy(a, 4 + j, (*chips[j], c), sibling)

    if phase == 0:
        for a in range(n):
            local(a).start()
            for cp in first(a):
                cp.start()
    elif phase == 1:
        for j in range(len(chips)):
            for a in range(n):
                copy(a, 1 + j, (*chips[j], c), me).wait_recv()
                passed(a, j).start()
    else:
        for a in range(n):
            copy(a, 0, sibling, me).wait_recv()
            for j in range(len(chips)):
                copy(a, 4 + j, (*chips[j], 1 - c), me).wait_recv()
        for a in range(n):
            for cp in first(a):
                cp.wait_send()
            for j in range(len(chips)):
                passed(a, j).wait_send()
            local(a).wait()


def _pair_phase(phase, g_refs, out_refs, send_sems, recv_sems, local_sems):
    n = len(g_refs)
    x, y, c = _coords()

    def copy(a, i):
        return pltpu.make_async_remote_copy(
            src_ref=g_refs[a].at[2 * i + (1 - c)], dst_ref=out_refs[a].at[i],
            send_sem=send_sems.at[a * N_PEER + i], recv_sem=recv_sems.at[a * N_PEER + i],
            device_id=(x, y, 1 - c), device_id_type=pl.DeviceIdType.MESH)

    if phase == 0:
        for a in range(n):
            for i in range(N_CHIP):
                copy(a, i).start()
    elif phase == 2:
        for a in range(n):
            for i in range(N_CHIP):
                copy(a, i).wait_recv()
        for a in range(n):
            for i in range(N_CHIP):
                copy(a, i).wait_send()


def _cross_phase(phase, p_refs, out_refs, send_sems, recv_sems, local_sems):
    n = len(p_refs)
    x, y, c = _coords()
    mine = 2 * x + y

    def local(a):
        return pltpu.make_async_copy(p_refs[a].at[mine], out_refs[a].at[mine], local_sems.at[a])

    def pair(a, k):
        px = 1 - x if k & 2 else x
        py = 1 - y if k & 1 else y
        peer = 2 * px + py
        sem = a * N_PEER + k - 1
        send = pltpu.make_async_remote_copy(
            src_ref=p_refs[a].at[peer], dst_ref=out_refs[a].at[mine], send_sem=send_sems.at[sem],
            recv_sem=recv_sems.at[sem], device_id=(px, py, c), device_id_type=pl.DeviceIdType.MESH)
        recv = pltpu.make_async_remote_copy(
            src_ref=p_refs[a].at[peer], dst_ref=out_refs[a].at[peer], send_sem=send_sems.at[sem],
            recv_sem=recv_sems.at[sem], device_id=(px, py, c), device_id_type=pl.DeviceIdType.MESH)
        return send, recv

    if phase == 0:
        for a in range(n):
            local(a).start()
        for k in range(1, N_CHIP):
            for a in range(n):
                pair(a, k)[0].start()
    elif phase == 2:
        for k in range(1, N_CHIP):
            for a in range(n):
                pair(a, k)[1].wait_recv()
        for k in range(1, N_CHIP):
            for a in range(n):
                pair(a, k)[0].wait_send()
        for a in range(n):
            local(a).wait()


_PHASES = {"ag": _ag_phase, "pair": _pair_phase, "cross": _cross_phase}


def _job(kind, arrays):
    arrays = list(arrays)
    if kind == "ag":
        shapes = [(N_DEV,) + a.shape for a in arrays]
    elif kind == "pair":
        shapes = [(N_CHIP,) + a.shape[1:] for a in arrays]
    else:
        shapes = [a.shape for a in arrays]
    return dict(parts=[(kind, len(arrays))], ins=arrays, outs=[_sds(s, a.dtype) for s, a in zip(shapes, arrays)])


def _join(jobs):
    jobs = [j for j in jobs if j is not None]
    if not jobs:
        return None
    return dict(parts=[p for j in jobs for p in j["parts"]], ins=[a for j in jobs for a in j["ins"]],
                outs=[o for j in jobs for o in j["outs"]])


def _hosted_call(body, job, *, name, grid, in_specs, out_specs, out_shape, scratch_shapes, semantics, args):
    if job is None:
        outs = _pcall(body, name=name, grid=grid, in_specs=in_specs, out_specs=out_specs, out_shape=out_shape,
                      scratch_shapes=scratch_shapes, compiler_params=_params(semantics))(*args)
        return outs, []
    n_in, n_out, n_scr, nj = len(in_specs), len(out_specs), len(scratch_shapes), len(job["ins"])
    parts = job["parts"]
    total = 1
    for g in grid:
        total *= g
    late = total - max(total // HOST_TAIL_FRACTION, 1) if total > 1 else 0

    def wrapped(*refs):
        ins, refs = refs[:n_in], refs[n_in:]
        jins, refs = refs[:nj], refs[nj:]
        outs, refs = refs[:n_out], refs[n_out:]
        jouts, refs = refs[:nj], refs[nj:]
        scr, sems = refs[:n_scr], refs[n_scr:]
        step = 0
        for d, g in enumerate(grid):
            step = step * g + pl.program_id(d)

        def run(phase):
            off = 0
            for i, (kind, n) in enumerate(parts):
                _PHASES[kind](phase, jins[off:off + n], jouts[off:off + n], *sems[3 * i:3 * i + 3])
                off += n

        pl.when(step == 0)(lambda: run(0))
        pl.when(step == late)(lambda: run(1))
        body(*ins, *outs, *scr)
        pl.when(step == total - 1)(lambda: run(2))

    any_spec = pl.BlockSpec(memory_space=pl.ANY)
    sems = []
    for kind, n in parts:
        sems += [pltpu.SemaphoreType.DMA((n * N_PEER,)), pltpu.SemaphoreType.DMA((n * N_PEER,)),
                 pltpu.SemaphoreType.DMA((n,))]
    outs = _pcall(
        wrapped, name=name, grid=grid, in_specs=list(in_specs) + [any_spec] * nj,
        out_specs=list(out_specs) + [any_spec] * nj, out_shape=list(out_shape) + job["outs"],
        scratch_shapes=list(scratch_shapes) + sems,
        compiler_params=_params(("arbitrary",) * len(grid)),
    )(*args, *job["ins"])
    return outs[:n_out], outs[n_out:]


def _exchange(job, *, name):
    outs, jouts = _hosted_call(lambda: None, job, name=name, grid=(1,), in_specs=[], out_specs=[], out_shape=[],
                               scratch_shapes=[], semantics=("arbitrary",), args=())
    return jouts


_TRANSPOSED = {"w_in": True, "w_out": False, "w_gate": True, "w_up": True, "w_down": False, "w_uq": True, "w_ukv": True}
_BIG = tuple(_TRANSPOSED)
_SMALL = ("ln_in_g", "ln_in_b", "q_norm_g", "kv_norm_g", "ret_gn_g", "ret_gn_b", "ln1_g", "ln1_b", "ln2_g", "ln2_b")
SMALL_COLS = 128


def _rope_tables(pos, dim):
    inv_freq = ROPE_THETA ** (-jnp.arange(0, dim, 2, dtype=F32) / dim)
    ang = pos.astype(F32)[:, None] * inv_freq
    return jnp.cos(ang), jnp.sin(ang)


def _split_in(wt_in):
    a, b, c = MLA_Q_LORA, MLA_Q_LORA + MLA_KV_LORA, MLA_Q_LORA + MLA_KV_LORA + MLA_ROPE
    return wt_in[:a], wt_in[a:b], wt_in[b:c], wt_in[c:]


def _mla_pack_rope(q, kr):
    S = q.shape[0]
    H, hr = MLA_HEADS, MLA_ROPE // 2
    q3 = q.reshape(S, H, MLA_QK)
    t1 = jnp.concatenate([q3[:, :, MLA_NOPE:MLA_NOPE + hr].reshape(S, H * hr), kr[:, :hr]], axis=1)
    t2 = jnp.concatenate([q3[:, :, MLA_NOPE + hr:].reshape(S, H * hr), kr[:, hr:]], axis=1)
    return q3[:, :, :MLA_NOPE], t1, t2


def _mla_unpack_rope(q_nope, o1, o2):
    S = o1.shape[0]
    H, hr = MLA_HEADS, MLA_ROPE // 2
    q3 = jnp.concatenate([q_nope, o1[:, :H * hr].reshape(S, H, hr), o2[:, :H * hr].reshape(S, H, hr)], axis=-1)
    kr = jnp.concatenate([o1[:, H * hr:], o2[:, H * hr:]], axis=1)
    return q3, kr


class _Plan:
    FIRST = [("ag", ((0, "w_in"),))]
    LAST = [("cross", ((0, "w_in"),))]
    HOSTS = {
        "l0_cq": [("ag", ((0, "w_uq"), (0, "w_ukv")))], "l0_hR": [("ag", ((0, "w_out"),))],
        "l0_attn": [("ag", ((0, "w_gate"),))], "l0_gate": [("ag", ((0, "w_up"),))],
        "l0_up": [("ag", ((0, "w_down"),))], "l0_down": [("ag", ((1, "w_in"), (1, "w_uq"), (1, "w_ukv")))],
        "l1_hR": [("ag", ((1, "w_out"),))], "l1_attn": [("ag", ((1, "w_gate"),))],
        "l1_gate": [("ag", ((1, "w_up"),))], "l1_up": [("ag", ((1, "w_down"),))],
        "l0_b_dact": [("cross", ((1, "w_in"),))]}
    for _l in (0, 1):
        HOSTS.update({
            "l%d_b_wgate" % _l: [("pair", ((_l, "w_down"),))],
            "l%d_b_wup" % _l: [("pair", ((_l, "w_gate"),)), ("cross", ((_l, "w_down"),))],
            "l%d_b_dx1a" % _l: [("pair", ((_l, "w_up"),)), ("cross", ((_l, "w_gate"),))],
            "l%d_b_dx1b" % _l: [("cross", ((_l, "w_up"),))],
            "l%d_b_attn" % _l: [("pair", ((_l, "w_out"),))],
            "l%d_b_win" % _l: [("pair", ((_l, "w_uq"), (_l, "w_ukv"))), ("cross", ((_l, "w_out"),))],
            "l%d_b_dx" % _l: [("pair", ((_l, "w_in"),)), ("cross", ((_l, "w_uq"), (_l, "w_ukv")))]})

    def __init__(self, local):
        self.local = local
        self.full = {}
        self.grads = {}
        self.paired = {}
        self.recv = {}

    def _by_device(self, k):
        return self.grads[k].reshape((N_DEV,) + self.local[k].shape)

    def _make(self, kind, keys):
        if kind == "ag":
            return _job(kind, [self.local[k] for k in keys])
        if kind == "pair":
            return _job(kind, [self._by_device(k) for k in keys])
        return _job(kind, [_pair_sum(self._by_device(k), self.paired[k], name="pairsum_l%d_%s" % k) for k in keys])

    def _done(self, kind, keys, outs):
        for k, o in zip(keys, outs):
            if kind == "ag":
                self.full[k] = o.reshape(N_DEV * o.shape[1], o.shape[2])
            elif kind == "pair":
                self.paired[k] = o
            else:
                self.recv[k] = o

    def _run(self, todo, call):
        outs = call(_join([self._make(kind, keys) for kind, keys in todo]))
        for kind, keys in todo:
            self._done(kind, keys, outs[:len(keys)])
            outs = outs[len(keys):]

    def gather_first(self):
        self._run(self.FIRST, lambda job: _exchange(job, name="ag_first"))

    def send_last(self):
        self._run(self.LAST, lambda job: _exchange(job, name="rs_last"))

    def call(self, fn, name, *args, **kw):
        if name not in self.HOSTS:
            return fn(*args, name=name, **kw)
        res = []

        def run(job):
            *outs, jouts = fn(*args, name=name, job=job, **kw)
            res.extend(outs)
            return jouts

        self._run(self.HOSTS[name], run)
        return res[0] if len(res) == 1 else tuple(res)


def _layer_fwd(x, xb, plan, p, tabs, l):
    S, D = x.shape
    H = MLA_HEADS
    nm = lambda s: "l%d_%s" % (l, s)
    w = lambda n: plan.full[(l, n)]
    mm = lambda name, *a, **kw: plan.call(_mm, nm(name), *a, **kw)
    wt_q, wt_kv, wt_kr, wt_r = _split_in(w("w_in"))
    cq = mm("cq", xb, wt_q, tb=True)
    ckv = mm("ckv", xb, wt_kv, tb=True)
    kr = mm("krope", xb, wt_kr, tb=True)
    hR = mm("hR", xb, wt_r, tb=True)
    _, qn, qn_hat, q_rstd = _norm_fwd(cq, p["q_norm_g"], None, center=False, eps=RMS_EPS, want_f32=False,
                                      name=nm("qnorm"))
    _, kvn, kvn_hat, kv_rstd = _norm_fwd(ckv, p["kv_norm_g"], None, center=False, eps=RMS_EPS, want_f32=False,
                                         name=nm("kvnorm"))
    q = mm("uq", qn, w("w_uq"), tb=True)
    kv = mm("ukv", kvn, w("w_ukv"), tb=True)
    q_nope, t1, t2 = _mla_pack_rope(q, kr)
    o1, o2 = _rope(t1, t2, tabs["cos_m"], tabs["sin_m"], name=nm("rope"))
    q3, k_r = _mla_unpack_rope(q_nope, o1, o2)
    kv3 = kv.reshape(S, H, MLA_NOPE + MLA_V)
    qh = q3.transpose(1, 0, 2).astype(BF16)
    kh = jnp.concatenate([kv3[:, :, :MLA_NOPE], jnp.broadcast_to(k_r[:, None, :], (S, H, MLA_ROPE))],
                         axis=-1).transpose(1, 0, 2).astype(BF16)
    vh = kv3[:, :, MLA_NOPE:].transpose(1, 0, 2).astype(BF16)
    a, lse = plan.call(_attn_fwd, nm("attn"), qh, kh, vh)
    o_ret, states = _ret_fwd(hR, tabs["cos_r"], tabs["sin_r"], tabs["ret"], name=nm("ret"))
    r = _gn_gate_fwd(o_ret, hR, p["ret_gn_g"], p["ret_gn_b"], name=nm("gn"))
    mix_in = jnp.concatenate([a.astype(BF16), r], axis=1)
    mix = mm("out", mix_in, w("w_out"))
    x1, x1b, x1_hat, rstd1 = _norm_fwd(x, p["ln1_g"], p["ln1_b"], res=mix, alpha=p["alpha"], eps=LN_EPS, name=nm("ln1"))
    g = mm("gate", x1b, w("w_gate"), tb=True)
    gb, ub, act = mm("up", x1b, w("w_up"), tb=True, extras=(g,), epilogue=_swiglu, out_dtypes=(BF16, BF16, BF16))
    f = mm("down", act, w("w_down"))
    x2, x2b, x2_hat, rstd2 = _norm_fwd(x1, p["ln2_g"], p["ln2_b"], res=f, alpha=p["alpha"], eps=LN_EPS, name=nm("ln2"))
    saved = dict(xb=xb, qn=qn, qn_hat=qn_hat, q_rstd=q_rstd, kvn=kvn, kvn_hat=kvn_hat, kv_rstd=kv_rstd,
                 qh=qh, kh=kh, vh=vh, a=a, lse=lse, hR=hR, o_ret=o_ret, states=states, mix_in=mix_in,
                 x1b=x1b, x1_hat=x1_hat, rstd1=rstd1, gb=gb, ub=ub, act=act, x2_hat=x2_hat, rstd2=rstd2)
    return x2, x2b, saved


def _layer_bwd(dx2, sv, plan, p, tabs, l):
    S, D = dx2.shape
    H = MLA_HEADS
    nm = lambda s: "l%d_b_%s" % (l, s)
    w = lambda n: plan.full[(l, n)]
    mm = lambda name, *a, **kw: plan.call(_mm, nm(name), *a, **kw)
    alpha = p["alpha"]
    gw, gp = plan.grads, {}
    dz2, dz2b, gp["ln2_g"], gp["ln2_b"] = _norm_bwd(dx2, sv["x2_hat"], sv["rstd2"], p["ln2_g"], name=nm("ln2"))
    dg, du = mm("dact", dz2b, w("w_down"), tb=True, extras=(sv["gb"], sv["ub"]), epilogue=_swiglu_bwd,
                out_dtypes=(BF16, BF16))
    gw[(l, "w_down")] = mm("wdown", sv["act"], dz2b, ta=True, out_dtype=BF16)
    gw[(l, "w_gate")] = mm("wgate", dg, sv["x1b"], ta=True, out_dtype=BF16)
    gw[(l, "w_up")] = mm("wup", du, sv["x1b"], ta=True, out_dtype=BF16)
    t = mm("dx1a", dg, w("w_gate"), add=dz2, add_scale=alpha)
    dx1 = mm("dx1b", du, w("w_up"), add=t)
    dz1, dz1b, gp["ln1_g"], gp["ln1_b"] = _norm_bwd(dx1, sv["x1_hat"], sv["rstd1"], p["ln1_g"], name=nm("ln1"))
    dmix = mm("dmix", dz1b, w("w_out"), tb=True)
    gw[(l, "w_out")] = mm("wout", sv["mix_in"], dz1b, ta=True, out_dtype=BF16)
    ret_col0 = (H * MLA_V) // RET_V
    do_ret, drg, gp["ret_gn_g"], gp["ret_gn_b"] = _gn_gate_bwd(
        dmix, sv["o_ret"], sv["hR"], p["ret_gn_g"], p["ret_gn_b"], dr_col0=ret_col0, name=nm("gn"))
    drq, drk, drv = _ret_bwd(do_ret, sv["hR"], sv["states"], tabs["cos_r"], tabs["sin_r"], tabs["ret"], name=nm("ret"))
    dqh, dkh, dvh = plan.call(_attn_bwd, nm("attn"), sv["qh"], sv["kh"], sv["vh"], sv["a"], dmix, sv["lse"], do_col0=0)
    dq3 = dqh.transpose(1, 0, 2)
    dk_r = _sum_heads(dkh, name=nm("dkr"))
    dq_nope, dt1, dt2 = _mla_pack_rope(dq3.reshape(S, H * MLA_QK), dk_r)
    di1, di2 = _rope(dt1, dt2, tabs["cos_m"], tabs["nsin_m"], name=nm("rope"))
    dq3, dkr = _mla_unpack_rope(dq_nope, di1, di2)
    dq = dq3.reshape(S, H * MLA_QK).astype(BF16)
    dkv = jnp.concatenate([dkh[:, :, :MLA_NOPE].transpose(1, 0, 2), dvh.transpose(1, 0, 2)],
                          axis=-1).reshape(S, -1).astype(BF16)
    gw[(l, "w_uq")] = mm("wuq", dq, sv["qn"], ta=True, out_dtype=BF16)
    dqn = mm("dqn", dq, w("w_uq"))
    gw[(l, "w_ukv")] = mm("wukv", dkv, sv["kvn"], ta=True, out_dtype=BF16)
    dkvn = mm("dkvn", dkv, w("w_ukv"))
    _, dcq, gp["q_norm_g"], _ = _norm_bwd(dqn, sv["qn_hat"], sv["q_rstd"], p["q_norm_g"], center=False,
                                          want_f32=False, name=nm("qnorm"))
    _, dckv, gp["kv_norm_g"], _ = _norm_bwd(dkvn, sv["kvn_hat"], sv["kv_rstd"], p["kv_norm_g"], center=False,
                                            want_f32=False, name=nm("kvnorm"))
    dh = jnp.concatenate([dcq, dckv, dkr.astype(BF16), drq, drk, drv, drg], axis=1)
    gw[(l, "w_in")] = mm("win", dh, sv["xb"], ta=True, out_dtype=BF16)
    dx = mm("dx", dh, w("w_in"), add=dz1, add_scale=alpha)
    return dx, gp


def _local_step(x, target, pos, small, plan, depth):
    alpha = (2 * depth) ** 0.25
    cos_m, sin_m = _rope_tables(pos, MLA_ROPE)
    cos_r, sin_r = _rope_tables(pos, RET_QK)
    reps = MLA_HEADS + 1
    tabs = dict(cos_m=jnp.tile(cos_m, (1, reps)), sin_m=jnp.tile(sin_m, (1, reps)),
                nsin_m=jnp.tile(-sin_m, (1, reps)), cos_r=cos_r, sin_r=sin_r, ret=_ret_tables(_pick(x.shape[0], RET_BLOCK, CHUNK)))
    h, hb, h_hat, h_rstd = _norm_fwd(x, small["ln_in_g"], small["ln_in_b"], eps=LN_EPS, name="ln_in")
    saved, ps = [], []
    for l in range(depth):
        p = {k: small[k][l] for k in _SMALL[2:]}
        p["alpha"] = alpha
        h, hb, sv = _layer_fwd(h, hb, plan, p, tabs, l)
        saved.append(sv)
        ps.append(p)
    dy, loss = _loss_head(h, target, name="loss")
    gps = [None] * depth
    for l in reversed(range(depth)):
        dy, gps[l] = _layer_bwd(dy, saved[l], plan, ps[l], tabs, l)
    grad_x, _, g_in_g, g_in_b = _norm_bwd(dy, h_hat, h_rstd, small["ln_in_g"], want_bf16=False, name="b_ln_in")
    gsmall = {"ln_in_g": g_in_g, "ln_in_b": g_in_b}
    for k in _SMALL[2:]:
        gsmall[k] = jnp.stack([gps[l][k] for l in range(depth)])
    return loss, grad_x, gsmall


def kernel(x, positions, ln_in_g, ln_in_b, w_in, q_norm_g, kv_norm_g, w_uq, w_ukv, ret_gn_g, ret_gn_b, w_out, ln1_g, ln1_b, w_gate, w_up, w_down, ln2_g, ln2_b, loss_target, m_ln_in_g, m_ln_in_b, m_w_in, m_q_norm_g, m_kv_norm_g, m_w_uq, m_w_ukv, m_ret_gn_g, m_ret_gn_b, m_w_out, m_ln1_g, m_ln1_b, m_w_gate, m_w_up, m_w_down, m_ln2_g, m_ln2_b, v_ln_in_g, v_ln_in_b, v_w_in, v_q_norm_g, v_kv_norm_g, v_w_uq, v_w_ukv, v_ret_gn_g, v_ret_gn_b, v_w_out, v_ln1_g, v_ln1_b, v_w_gate, v_w_up, v_w_down, v_ln2_g, v_ln2_b):
    names = ["ln_in_g", "ln_in_b", "w_in", "q_norm_g", "kv_norm_g", "w_uq", "w_ukv", "ret_gn_g", "ret_gn_b", "w_out",
             "ln1_g", "ln1_b", "w_gate", "w_up", "w_down", "ln2_g", "ln2_b"]
    wv = dict(zip(names, (ln_in_g, ln_in_b, w_in, q_norm_g, kv_norm_g, w_uq, w_ukv, ret_gn_g, ret_gn_b, w_out,
                          ln1_g, ln1_b, w_gate, w_up, w_down, ln2_g, ln2_b)))
    mv = dict(zip(names, (m_ln_in_g, m_ln_in_b, m_w_in, m_q_norm_g, m_kv_norm_g, m_w_uq, m_w_ukv, m_ret_gn_g,
                          m_ret_gn_b, m_w_out, m_ln1_g, m_ln1_b, m_w_gate, m_w_up, m_w_down, m_ln2_g, m_ln2_b)))
    vv = dict(zip(names, (v_ln_in_g, v_ln_in_b, v_w_in, v_q_norm_g, v_kv_norm_g, v_w_uq, v_w_ukv, v_ret_gn_g,
                          v_ret_gn_b, v_w_out, v_ln1_g, v_ln1_b, v_w_gate, v_w_up, v_w_down, v_ln2_g, v_ln2_b)))
    depth = w_in.shape[0]
    assert depth == 2, "the exchange plan is written for two layers"

    keys = [(l, n) for l in range(depth) for n in _BIG]
    plan = _Plan({(l, n): (wv[n][l].T if _TRANSPOSED[n] else wv[n][l]).astype(BF16) for l, n in keys})
    plan.gather_first()

    small = {n: wv[n] for n in _SMALL}
    loss, grad_x, gsmall = _local_step(x[0], loss_target[0], positions[0], small, plan, depth)
    loss = lax.psum(loss, MESH_AXES)
    plan.send_last()

    gshard = {n: [None] * depth for n in _BIG}
    for l, n in keys:
        tot = _sum_slots(plan.recv[(l, n)], name="sum_l%d_%s" % (l, n))
        gshard[n][l] = tot.T if _TRANSPOSED[n] else tot
    grads = {n: jnp.stack(v) for n, v in gshard.items()}

    flat = jnp.concatenate([gsmall[n].reshape(-1) for n in _SMALL])
    n_small = flat.shape[0]
    rows = -(-n_small // (SMALL_COLS * 8)) * 8
    flat = jnp.pad(flat, (0, rows * SMALL_COLS - n_small)).reshape(rows, SMALL_COLS)
    tot = _sum_slots(_exchange(_job("ag", [flat]), name="ag_small")[0], name="sum_small").reshape(-1)
    off = 0
    for n in _SMALL:
        grads[n] = tot[off:off + wv[n].size].reshape(wv[n].shape)
        off += wv[n].size

    delta, new_m, new_v = {}, {}, {}
    for n in names:
        w2 = wv[n] if wv[n].ndim > 1 else wv[n].reshape(1, -1)
        d, nm_, nv_ = _adamw(w2, grads[n].reshape(w2.shape), mv[n].reshape(w2.shape), vv[n].reshape(w2.shape),
                             name="adamw_" + n)
        delta[n], new_m[n], new_v[n] = d.reshape(wv[n].shape), nm_.reshape(wv[n].shape), nv_.reshape(wv[n].shape)

    return (loss, grad_x[None], *[grads[n] for n in names], *[delta[n] for n in names],
            *[new_m[n] for n in names], *[new_v[n] for n in names])
```

```python
import jax
import jax.numpy as jnp
from jax import lax
from jax.experimental import pallas as pl
from jax.experimental.pallas import tpu as pltpu

F32 = jnp.float32
BF16 = jnp.bfloat16

CHUNK = 64
CHUNK_SHIFT = 6
MLA_HEADS = 8
MLA_Q_LORA = 512
MLA_KV_LORA = 256
MLA_NOPE = 128
MLA_ROPE = 64
MLA_V = 128
MLA_QK = MLA_NOPE + MLA_ROPE
RET_HEADS = 4
RET_QK = 256
RET_V = 256
ROPE_THETA = 10000.0
LN_EPS = 1e-5
RMS_EPS = 1e-6
GN_EPS = 1e-5
ADAM_LR = 0.001
ADAM_B1 = 0.9
ADAM_B2 = 0.999
ADAM_EPS = 1e-08
ADAM_WD = 0.01
ADAM_STEP = 10

LOG2_E = 1.4426950408889634

N_DEV = 8
MESH_AXES = ("x", "y", "c")
VMEM_LIMIT_BYTES = 56 * 1024 * 1024
MM_VMEM_BUDGET = 40 * 1024 * 1024
MM_ACC_PENALTY = 0.85
ATTN_BLOCK = 256
ATTN_KEY_STEP = 512
RET_BLOCK = 256


def _pick(n, pref, mult):
    best = None
    d = mult
    while d <= min(n, pref):
        if n % d == 0:
            best = d
        d += mult
    return n if best is None else best


def _divisors(n, mult, cap):
    ds = [d for d in range(mult, min(n, cap) + 1, mult) if n % d == 0]
    return ds or [n]


def _pcall(body, **kw):
    return pl.pallas_call(body, **kw)


def _params(sem):
    return pltpu.CompilerParams(dimension_semantics=sem, vmem_limit_bytes=VMEM_LIMIT_BYTES)


def _sds(shape, dtype):
    return jax.ShapeDtypeStruct(shape, dtype)


def _mm_tiles(M, N, K, ta, sa, sb, tile_bytes):
    best = None
    for bk in _divisors(K, 128, 8192):
        nk = K // bk
        for bm in _divisors(M, 128 if ta else 16, 1024):
            for bn in _divisors(N, 128, 1024):
                vmem = 2 * (bm * bk * sa + bk * bn * sb) + 2 * bm * bn * tile_bytes + (bm * bn * 4 if nk > 1 else 0)
                if vmem > MM_VMEM_BUDGET:
                    continue
                flops_per_byte = 1.0 / (1.0 / bm + (1.0 / max(N, bn) if nk == 1 else 1.0 / bn))
                score = (flops_per_byte * (1.0 if nk == 1 else MM_ACC_PENALTY), bk, bn, bm)
                if best is None or score > best[0]:
                    best = (score, bm, bn, bk)
    assert best is not None, (M, N, K)
    return best[1:]


def _mm(a, b, *, ta=False, tb=False, add=None, add_scale=1.0, out_dtype=F32, extras=(), epilogue=None,
        out_dtypes=None, job=None, name):
    if ta:
        K, M = a.shape
    else:
        M, K = a.shape
    if tb:
        N, K2 = b.shape
    else:
        K2, N = b.shape
    assert K == K2, (a.shape, b.shape, ta, tb)
    if epilogue is None:
        assert not extras and out_dtypes is None
        out_dtypes = (out_dtype,)
        if add is not None:
            extras, epilogue = (add,), lambda r, c: (r + add_scale * c,)
        else:
            epilogue = lambda r: (r,)
    n_ex, n_out = len(extras), len(out_dtypes)
    tile_bytes = sum(e.dtype.itemsize for e in extras) + sum(jnp.dtype(d).itemsize for d in out_dtypes)
    bm, bn, bk = _mm_tiles(M, N, K, ta, a.dtype.itemsize, b.dtype.itemsize, tile_bytes)
    nk = K // bk
    a_spec = (pl.BlockSpec((bk, bm), lambda i, j, k: (k, i)) if ta
              else pl.BlockSpec((bm, bk), lambda i, j, k: (i, k)))
    b_spec = (pl.BlockSpec((bn, bk), lambda i, j, k: (j, k)) if tb
              else pl.BlockSpec((bk, bn), lambda i, j, k: (k, j)))
    o_spec = pl.BlockSpec((bm, bn), lambda i, j, k: (i, j))
    dims = (((0 if ta else 1,), (1 if tb else 0,)), ((), ()))

    def body(*refs):
        a_ref, b_ref = refs[0], refs[1]
        ex_refs, o_refs = refs[2:2 + n_ex], refs[2 + n_ex:2 + n_ex + n_out]
        part = lax.dot_general(a_ref[...].astype(BF16), b_ref[...].astype(BF16), dims, preferred_element_type=F32)

        def finish(r):
            for o_ref, v in zip(o_refs, epilogue(r, *[e[...] for e in ex_refs])):
                o_ref[...] = v.astype(o_ref.dtype)

        if nk == 1:
            finish(part)
        else:
            acc_ref = refs[-1]
            k = pl.program_id(2)

            @pl.when(k == 0)
            def _():
                acc_ref[...] = part

            @pl.when(k > 0)
            def _():
                acc_ref[...] += part

            @pl.when(k == nk - 1)
            def _():
                finish(acc_ref[...])

    outs, jouts = _hosted_call(
        body, job, name=name, grid=(M // bm, N // bn, nk), in_specs=[a_spec, b_spec] + [o_spec] * n_ex,
        out_specs=[o_spec] * n_out, out_shape=[_sds((M, N), d) for d in out_dtypes],
        scratch_shapes=[pltpu.VMEM((bm, bn), F32)] if nk > 1 else [],
        semantics=("parallel", "parallel", "arbitrary"), args=(a, b, *extras))
    res = list(outs) + ([jouts] if job is not None else [])
    return res[0] if len(res) == 1 else tuple(res)


def _norm_fwd(x, g, b, *, res=None, alpha=1.0, center=True, eps, want_f32=True, want_bf16=True, name):
    S, W = x.shape
    bs = _pick(S, 256, 16)
    has_res, has_b = res is not None, b is not None

    def body(*refs):
        it = iter(refs)
        x_ref = next(it)
        res_ref = next(it) if has_res else None
        g_ref = next(it)
        b_ref = next(it) if has_b else None
        y_ref = next(it) if want_f32 else None
        yb_ref = next(it) if want_bf16 else None
        xh_ref, r_ref = next(it), next(it)
        z = x_ref[...]
        if has_res:
            z = alpha * z + res_ref[...]
        if center:
            z = z - jnp.mean(z, axis=-1, keepdims=True)
        rstd = lax.rsqrt(jnp.mean(z * z, axis=-1, keepdims=True) + eps)
        xh = z * rstd
        y = xh * g_ref[...]
        if has_b:
            y = y + b_ref[...]
        if want_f32:
            y_ref[...] = y
        if want_bf16:
            yb_ref[...] = y.astype(BF16)
        xh_ref[...] = xh
        r_ref[...] = rstd

    row = pl.BlockSpec((bs, W), lambda i: (i, 0))
    vec = pl.BlockSpec((1, W), lambda i: (0, 0))
    in_specs = [row] + ([row] if has_res else []) + [vec] + ([vec] if has_b else [])
    args = (x,) + ((res,) if has_res else ()) + (g.reshape(1, W),) + ((b.reshape(1, W),) if has_b else ())
    out_specs = ([row] if want_f32 else []) + ([row] if want_bf16 else []) + [row, pl.BlockSpec((bs, 1), lambda i: (i, 0))]
    out_shape = (([_sds((S, W), F32)] if want_f32 else []) + ([_sds((S, W), BF16)] if want_bf16 else [])
                 + [_sds((S, W), F32), _sds((S, 1), F32)])
    outs = list(_pcall(body, name=name, grid=(S // bs,), in_specs=in_specs, out_specs=out_specs, out_shape=out_shape,
                       compiler_params=_params(("parallel",)))(*args))
    y = outs.pop(0) if want_f32 else None
    yb = outs.pop(0) if want_bf16 else None
    return y, yb, outs[0], outs[1]


def _norm_bwd(dy, xh, rstd, g, *, center=True, want_f32=True, want_bf16=True, name):
    S, W = dy.shape
    bs = _pick(S, 256, 16)

    def body(*refs):
        dy_ref, xh_ref, r_ref, g_ref = refs[:4]
        it = iter(refs[4:])
        dz_ref = next(it) if want_f32 else None
        dzb_ref = next(it) if want_bf16 else None
        dg_ref, db_ref = next(it), next(it)

        @pl.when(pl.program_id(0) == 0)
        def _():
            dg_ref[...] = jnp.zeros_like(dg_ref)
            db_ref[...] = jnp.zeros_like(db_ref)

        dyv = dy_ref[...]
        xhv = xh_ref[...]
        dyg = dyv * g_ref[...]
        m2 = jnp.mean(dyg * xhv, axis=-1, keepdims=True)
        t = dyg - xhv * m2
        if center:
            t = t - jnp.mean(dyg, axis=-1, keepdims=True)
        dz = r_ref[...] * t
        if want_f32:
            dz_ref[...] = dz
        if want_bf16:
            dzb_ref[...] = dz.astype(BF16)
        dg_ref[...] += jnp.sum(dyv * xhv, axis=0, keepdims=True)
        db_ref[...] += jnp.sum(dyv, axis=0, keepdims=True)

    row = pl.BlockSpec((bs, W), lambda i: (i, 0))
    vec = pl.BlockSpec((1, W), lambda i: (0, 0))
    out_specs = ([row] if want_f32 else []) + ([row] if want_bf16 else []) + [vec, vec]
    out_shape = (([_sds((S, W), F32)] if want_f32 else []) + ([_sds((S, W), BF16)] if want_bf16 else [])
                 + [_sds((1, W), F32), _sds((1, W), F32)])
    outs = list(_pcall(body, name=name, grid=(S // bs,),
                       in_specs=[row, row, pl.BlockSpec((bs, 1), lambda i: (i, 0)), vec],
                       out_specs=out_specs, out_shape=out_shape,
                       compiler_params=_params(("arbitrary",)))(dy, xh, rstd, g.reshape(1, W)))
    dz = outs.pop(0) if want_f32 else None
    dzb = outs.pop(0) if want_bf16 else None
    return dz, dzb, outs[0][0], outs[1][0]


def _rope(t1, t2, cos, sin, *, name):
    S, C = t1.shape
    bs = _pick(S, 512, 8)

    def body(t1_ref, t2_ref, c_ref, s_ref, o1_ref, o2_ref):
        a, b, c, s = t1_ref[...], t2_ref[...], c_ref[...], s_ref[...]
        o1_ref[...] = a * c - b * s
        o2_ref[...] = b * c + a * s

    row = pl.BlockSpec((bs, C), lambda i: (i, 0))
    return _pcall(
        body, name=name, grid=(S // bs,), in_specs=[row] * 4, out_specs=[row, row],
        out_shape=[_sds((S, C), F32)] * 2, compiler_params=_params(("parallel",)),
    )(t1, t2, cos, sin)


def _chunk_mask(row0, col0, B, G):
    rows = lax.shift_right_logical(row0 + lax.broadcasted_iota(jnp.int32, (B, G), 0), CHUNK_SHIFT)
    cols = lax.shift_right_logical(col0 + lax.broadcasted_iota(jnp.int32, (B, G), 1), CHUNK_SHIFT)
    return rows >= cols


def _mask_tail(x, qi, B, G, fill):
    L = x.shape[1]
    tail = jnp.where(_chunk_mask(qi * B, L - G, B, G), x[:, L - G:], fill)
    return tail if L == G else jnp.concatenate([x[:, :L - G], tail], axis=1)


def _for_key_prefix(qi, B, G, S, fn):
    per = G // B
    for b in range(S // G):
        pl.when(qi // per == b)(lambda b=b: fn((b + 1) * G))


def _nt(a, b):
    return lax.dot_general(a, b, (((1,), (1,)), ((), ())), preferred_element_type=F32)


def _nn(a, b):
    return lax.dot_general(a, b, (((1,), (0,)), ((), ())), preferred_element_type=F32)


def _tn(a, b):
    return lax.dot_general(a, b, (((0,), (0,)), ((), ())), preferred_element_type=F32)


def _attn_fwd(qh, kh, vh, *, job=None, name):
    H, S, DQ = qh.shape
    DV = vh.shape[-1]
    B = _pick(S, ATTN_BLOCK, CHUNK)
    G = _pick(S, ATTN_KEY_STEP, B)
    scale = float(DQ) ** -0.5
    neg = float(jnp.finfo(jnp.float32).min)

    def body(q_ref, k_ref, v_ref, o_ref, lse_ref):
        qi = pl.program_id(1)

        def run(L):
            raw = _mask_tail(_nt(q_ref[...], k_ref[0:L, :]), qi, B, G, neg)
            m = jnp.max(raw, axis=-1, keepdims=True)
            e = jnp.exp2((raw - m) * (scale * LOG2_E))
            l = jnp.sum(e, axis=-1, keepdims=True)
            o_ref[...] = _nn((e * (1.0 / l)).astype(BF16), v_ref[0:L, :])
            lse_ref[...] = m * scale + jnp.log(l)

        _for_key_prefix(qi, B, G, S, run)

    outs, jouts = _hosted_call(
        body, job, name=name, grid=(H, S // B),
        in_specs=[pl.BlockSpec((None, B, DQ), lambda h, i: (h, i, 0)),
                  pl.BlockSpec((None, S, DQ), lambda h, i: (h, 0, 0)),
                  pl.BlockSpec((None, S, DV), lambda h, i: (h, 0, 0))],
        out_specs=[pl.BlockSpec((B, DV), lambda h, i: (i, h)),
                   pl.BlockSpec((None, B, 1), lambda h, i: (h, i, 0))],
        out_shape=[_sds((S, H * DV), F32), _sds((H, S, 1), F32)], scratch_shapes=[],
        semantics=("parallel", "parallel"), args=(qh, kh, vh))
    return (outs[0], outs[1]) if job is None else (outs[0], outs[1], jouts)


def _attn_bwd(qh, kh, vh, o, do, lse, *, do_col0, job=None, name):
    H, S, DQ = qh.shape
    DV = vh.shape[-1]
    B = _pick(S, ATTN_BLOCK, CHUNK)
    G = _pick(S, ATTN_KEY_STEP, B)
    scale = float(DQ) ** -0.5

    def body(q_ref, k_ref, v_ref, o_ref, do_ref, lse_ref, dq_ref, dk_ref, dv_ref):
        qi = pl.program_id(1)

        @pl.when(qi == 0)
        def _():
            dk_ref[...] = jnp.zeros_like(dk_ref)
            dv_ref[...] = jnp.zeros_like(dv_ref)

        def run(L):
            q = q_ref[...]
            k = k_ref[0:L, :]
            dov = do_ref[...]
            dob = dov.astype(BF16)
            p = jnp.exp2(_nt(q, k) * (scale * LOG2_E) - lse_ref[...] * LOG2_E)
            p = _mask_tail(p, qi, B, G, 0.0)
            dsum = jnp.sum(dov * o_ref[...], axis=-1, keepdims=True)
            ds = (p * (_nt(dob, v_ref[0:L, :]) - dsum)).astype(BF16)
            dv_ref[0:L, :] += _tn(p.astype(BF16), dob)
            dk_ref[0:L, :] += _tn(ds, q)
            dq_ref[...] = _nn(ds, k) * scale

        _for_key_prefix(qi, B, G, S, run)

        @pl.when(qi == S // B - 1)
        def _():
            dk_ref[...] = dk_ref[...] * scale

    outs, jouts = _hosted_call(
        body, job, name=name, grid=(H, S // B),
        in_specs=[pl.BlockSpec((None, B, DQ), lambda h, i: (h, i, 0)),
                  pl.BlockSpec((None, S, DQ), lambda h, i: (h, 0, 0)),
                  pl.BlockSpec((None, S, DV), lambda h, i: (h, 0, 0)),
                  pl.BlockSpec((B, DV), lambda h, i: (i, h)),
                  pl.BlockSpec((B, DV), lambda h, i: (i, do_col0 + h)),
                  pl.BlockSpec((None, B, 1), lambda h, i: (h, i, 0))],
        out_specs=[pl.BlockSpec((None, B, DQ), lambda h, i: (h, i, 0)),
                   pl.BlockSpec((None, S, DQ), lambda h, i: (h, 0, 0)),
                   pl.BlockSpec((None, S, DV), lambda h, i: (h, 0, 0))],
        out_shape=[_sds((H, S, DQ), F32), _sds((H, S, DQ), F32), _sds((H, S, DV), F32)], scratch_shapes=[],
        semantics=("parallel", "arbitrary"), args=(qh, kh, vh, o, do, lse))
    return tuple(outs) if job is None else (*outs, jouts)


def _ret_tables(T):
    H = RET_HEADS
    log_gamma = jnp.log1p(-jnp.exp2(-5.0 - jnp.arange(H, dtype=F32)))
    idx = jnp.arange(T, dtype=F32)
    chunk = jnp.arange(T) // CHUNK
    visible = chunk[:, None] >= chunk[None, :]
    intra = jnp.where(visible[None], jnp.exp(log_gamma[:, None, None] * jnp.abs(idx[:, None] - idx[None, :])), 0.0)
    qd = jnp.exp(log_gamma[:, None] * (idx + 1.0))[:, :, None]
    kd = jnp.exp(log_gamma[:, None] * (T - 1.0 - idx))[:, :, None]
    cd = jnp.exp(log_gamma * T)[:, None, None]
    return intra, qd, kd, cd


def _rot(t, c, s):
    half = t.shape[-1] // 2
    t1, t2 = t[:, :half], t[:, half:]
    return jnp.concatenate([t1 * c - t2 * s, t2 * c + t1 * s], axis=-1)


def _rot_t(t, c, s):
    half = t.shape[-1] // 2
    t1, t2 = t[:, :half], t[:, half:]
    return jnp.concatenate([t1 * c + t2 * s, t2 * c - t1 * s], axis=-1)


def _dot(a, b, ca, cb):
    return lax.dot_general(a.astype(BF16), b.astype(BF16), (((ca,), (cb,)), ((), ())), preferred_element_type=F32)


def _ret_fwd(hR, cos, sin, tables, *, name):
    S = hR.shape[0]
    H, L, DK, DV = RET_HEADS, tables[0].shape[-1], RET_QK, RET_V
    NC = S // L
    qscale = float(DK) ** -0.5
    intra, qd, kd, cd = tables

    def body(q_ref, k_ref, v_ref, c_ref, s_ref, in_ref, qd_ref, kd_ref, cd_ref, o_ref, st_ref, state):
        @pl.when(pl.program_id(1) == 0)
        def _():
            state[...] = jnp.zeros_like(state)

        c, s = c_ref[...], s_ref[...]
        q = _rot(q_ref[...], c, s) * qscale
        k = _rot(k_ref[...], c, s)
        v = v_ref[...]
        st = state[...]
        st_ref[...] = st.astype(BF16)
        scores = _dot(q, k, 1, 1) * in_ref[...]
        o_ref[...] = _dot(scores, v, 1, 0) + _dot(q * qd_ref[...], st, 1, 0)
        state[...] = st * cd_ref[...] + _dot(k * kd_ref[...], v, 0, 0)

    blk = lambda off: pl.BlockSpec((L, DK), lambda h, c: (c, off + h))
    rope = pl.BlockSpec((L, DK // 2), lambda h, c: (c, 0))
    return _pcall(
        body, name=name, grid=(H, NC),
        in_specs=[blk(0), blk(H), blk(2 * H), rope, rope,
                  pl.BlockSpec((None, L, L), lambda h, c: (h, 0, 0)),
                  pl.BlockSpec((None, L, 1), lambda h, c: (h, 0, 0)),
                  pl.BlockSpec((None, L, 1), lambda h, c: (h, 0, 0)),
                  pl.BlockSpec((None, 1, 1), lambda h, c: (h, 0, 0))],
        out_specs=[pl.BlockSpec((L, DV), lambda h, c: (c, h)),
                   pl.BlockSpec((None, None, DK, DV), lambda h, c: (h, c, 0, 0))],
        out_shape=[_sds((S, H * DV), F32), _sds((H, NC, DK, DV), BF16)],
        scratch_shapes=[pltpu.VMEM((DK, DV), F32)],
        compiler_params=_params(("arbitrary", "arbitrary")),
    )(hR, hR, hR, cos, sin, intra, qd, kd, cd)


def _ret_bwd(do, hR, states, cos, sin, tables, *, name):
    S = hR.shape[0]
    H, L, DK, DV = RET_HEADS, tables[0].shape[-1], RET_QK, RET_V
    NC = S // L
    qscale = float(DK) ** -0.5
    intra, qd, kd, cd = tables

    def body(do_ref, q_ref, k_ref, v_ref, st_ref, c_ref, s_ref, in_ref, qd_ref, kd_ref, cd_ref,
             dq_ref, dk_ref, dv_ref, dstate):
        @pl.when(pl.program_id(1) == 0)
        def _():
            dstate[...] = jnp.zeros_like(dstate)

        c, s = c_ref[...], s_ref[...]
        q = _rot(q_ref[...], c, s) * qscale
        k = _rot(k_ref[...], c, s)
        v = v_ref[...]
        dov = do_ref[...]
        dst = dstate[...]
        dec = in_ref[...]
        qdv, kdv = qd_ref[...], kd_ref[...]
        scores = _dot(q, k, 1, 1) * dec
        da = _dot(dov, v, 1, 1) * dec
        dv_ref[...] = (_dot(scores, dov, 0, 0) + _dot(k * kdv, dst, 1, 0)).astype(BF16)
        dq = _dot(da, k, 1, 0) + _dot(dov, st_ref[...], 1, 1) * qdv
        dk = _dot(da, q, 0, 0) + _dot(v, dst, 1, 1) * kdv
        dq_ref[...] = _rot_t(dq * qscale, c, s).astype(BF16)
        dk_ref[...] = _rot_t(dk, c, s).astype(BF16)
        dstate[...] = dst * cd_ref[...] + _dot(q * qdv, dov, 0, 0)

    rev = lambda c: NC - 1 - c
    blk = lambda off: pl.BlockSpec((L, DK), lambda h, c: (rev(c), off + h))
    rope = pl.BlockSpec((L, DK // 2), lambda h, c: (rev(c), 0))
    out = pl.BlockSpec((L, DK), lambda h, c: (rev(c), h))
    return _pcall(
        body, name=name, grid=(H, NC),
        in_specs=[out, blk(0), blk(H), blk(2 * H),
                  pl.BlockSpec((None, None, DK, DV), lambda h, c: (h, rev(c), 0, 0)), rope, rope,
                  pl.BlockSpec((None, L, L), lambda h, c: (h, 0, 0)),
                  pl.BlockSpec((None, L, 1), lambda h, c: (h, 0, 0)),
                  pl.BlockSpec((None, L, 1), lambda h, c: (h, 0, 0)),
                  pl.BlockSpec((None, 1, 1), lambda h, c: (h, 0, 0))],
        out_specs=[out, out, out],
        out_shape=[_sds((S, H * DK), BF16)] * 3,
        scratch_shapes=[pltpu.VMEM((DK, DV), F32)],
        compiler_params=_params(("arbitrary", "arbitrary")),
    )(do, hR, hR, hR, states, cos, sin, intra, qd, kd, cd)


def _gn_gate_fwd(o, hR, g, b, *, name):
    S = o.shape[0]
    H, DV = RET_HEADS, RET_V
    bs = _pick(S, 512, 16)

    def body(o_ref, rg_ref, g_ref, b_ref, r_ref):
        z = o_ref[...]
        z = z - jnp.mean(z, axis=-1, keepdims=True)
        xh = z * lax.rsqrt(jnp.mean(z * z, axis=-1, keepdims=True) + GN_EPS)
        rg = rg_ref[...]
        r_ref[...] = ((rg * jax.nn.sigmoid(rg)) * (xh * g_ref[...] + b_ref[...])).astype(BF16)

    row = pl.BlockSpec((bs, DV), lambda i, h: (i, h))
    vec = pl.BlockSpec((1, DV), lambda i, h: (0, h))
    return _pcall(
        body, name=name, grid=(S // bs, H),
        in_specs=[row, pl.BlockSpec((bs, DV), lambda i, h: (i, 3 * H + h)), vec, vec],
        out_specs=row, out_shape=_sds((S, H * DV), BF16),
        compiler_params=_params(("parallel", "parallel")),
    )(o, hR, g.reshape(1, H * DV), b.reshape(1, H * DV))


def _gn_gate_bwd(dr, o, hR, g, b, *, dr_col0, name):
    S = o.shape[0]
    H, DV = RET_HEADS, RET_V
    bs = _pick(S, 512, 16)

    def body(dr_ref, o_ref, rg_ref, g_ref, b_ref, do_ref, drg_ref, dg_ref, db_ref):
        @pl.when(pl.program_id(1) == 0)
        def _():
            dg_ref[...] = jnp.zeros_like(dg_ref)
            db_ref[...] = jnp.zeros_like(db_ref)

        z = o_ref[...]
        z = z - jnp.mean(z, axis=-1, keepdims=True)
        rstd = lax.rsqrt(jnp.mean(z * z, axis=-1, keepdims=True) + GN_EPS)
        xh = z * rstd
        gv = g_ref[...]
        y = xh * gv + b_ref[...]
        rg = rg_ref[...]
        sg = jax.nn.sigmoid(rg)
        drv = dr_ref[...]
        dy = drv * (rg * sg)
        drg_ref[...] = (drv * y * (sg * (1.0 + rg * (1.0 - sg)))).astype(BF16)
        dg_ref[...] += jnp.sum(dy * xh, axis=0, keepdims=True)
        db_ref[...] += jnp.sum(dy, axis=0, keepdims=True)
        dxh = dy * gv
        do_ref[...] = rstd * (dxh - jnp.mean(dxh, axis=-1, keepdims=True)
                              - xh * jnp.mean(dxh * xh, axis=-1, keepdims=True))

    row = pl.BlockSpec((bs, DV), lambda h, i: (i, h))
    vec = pl.BlockSpec((1, DV), lambda h, i: (0, h))
    do, drg, dg, db = _pcall(
        body, name=name, grid=(H, S // bs),
        in_specs=[pl.BlockSpec((bs, DV), lambda h, i: (i, dr_col0 + h)), row,
                  pl.BlockSpec((bs, DV), lambda h, i: (i, 3 * H + h)), vec, vec],
        out_specs=[row, row, vec, vec],
        out_shape=[_sds((S, H * DV), F32), _sds((S, H * DV), BF16), _sds((1, H * DV), F32), _sds((1, H * DV), F32)],
        compiler_params=_params(("arbitrary", "arbitrary")),
    )(dr, o, hR, g.reshape(1, H * DV), b.reshape(1, H * DV))
    return do, drg, dg[0], db[0]


def _swiglu(u, g):
    return g, u, (g * jax.nn.sigmoid(g)) * u


def _swiglu_bwd(da, g, u):
    g, u = g.astype(F32), u.astype(F32)
    sg = jax.nn.sigmoid(g)
    return da * u * (sg * (1.0 + g * (1.0 - sg))), da * (g * sg)


def _loss_head(y, t, *, name):
    S, D = y.shape
    bs = _pick(S, 256, 8)
    inv_d = 1.0 / D

    def body(y_ref, t_ref, dy_ref, l_ref):
        @pl.when(pl.program_id(0) == 0)
        def _():
            l_ref[...] = jnp.zeros_like(l_ref)

        e = y_ref[...] - t_ref[...]
        dy_ref[...] = e * inv_d
        l_ref[...] += 0.5 * jnp.sum(jnp.mean(e * e, axis=-1, keepdims=True), axis=0, keepdims=True)

    row = pl.BlockSpec((bs, D), lambda i: (i, 0))
    dy, l = _pcall(
        body, name=name, grid=(S // bs,), in_specs=[row, row],
        out_specs=[row, pl.BlockSpec((1, 1), lambda i: (0, 0))],
        out_shape=[_sds((S, D), F32), _sds((1, 1), F32)],
        compiler_params=_params(("arbitrary",)),
    )(y, t)
    return dy, l[0, 0]


def _adamw(w, g, m, v, *, name):
    shape = w.shape
    C = shape[-1]
    R = w.size // C
    br = _pick(R, 512, 8)

    def body(w_ref, g_ref, m_ref, v_ref, d_ref, nm_ref, nv_ref):
        gv = g_ref[...]
        mn = ADAM_B1 * m_ref[...] + (1.0 - ADAM_B1) * gv
        vn = ADAM_B2 * v_ref[...] + (1.0 - ADAM_B2) * (gv * gv)
        m_hat = mn / (1.0 - ADAM_B1 ** ADAM_STEP)
        v_hat = vn / (1.0 - ADAM_B2 ** ADAM_STEP)
        d_ref[...] = -ADAM_LR * (m_hat / (jnp.sqrt(v_hat) + ADAM_EPS) + ADAM_WD * w_ref[...])
        nm_ref[...] = mn
        nv_ref[...] = vn

    blk = pl.BlockSpec((br, C), lambda i: (i, 0))
    outs = _pcall(body, name=name, grid=(R // br,), in_specs=[blk] * 4, out_specs=[blk] * 3,
                  out_shape=[_sds((R, C), F32)] * 3,
                  compiler_params=_params(("parallel",)))(*[a.reshape(R, C) for a in (w, g, m, v)])
    return tuple(o.reshape(shape) for o in outs)


def _slot_block(n, k):
    return (_pick(n, 1024, 16) if n % 16 == 0 else n), _pick(k, 1024, 128)


def _sum_slots(x, *, name):
    ns, n, k = x.shape
    br, bc = _slot_block(n, k)

    def body(x_ref, o_ref):
        acc = x_ref[0].astype(F32)
        for s in range(1, ns):
            acc = acc + x_ref[s].astype(F32)
        o_ref[...] = acc

    return _pcall(body, name=name, grid=(n // br, k // bc),
                  in_specs=[pl.BlockSpec((ns, br, bc), lambda i, j: (0, i, j))],
                  out_specs=pl.BlockSpec((br, bc), lambda i, j: (i, j)),
                  out_shape=_sds((n, k), F32), compiler_params=_params(("parallel", "parallel")))(x)


def _pair_sum(g, r, *, name):
    _, n, k = r.shape
    br, bc = _slot_block(n, k)
    core = lax.axis_index("c").astype(jnp.int32).reshape(1)

    def body(c_ref, g_ref, r_ref, o_ref):
        o_ref[...] = (g_ref[...].astype(F32) + r_ref[...].astype(F32)).astype(o_ref.dtype)

    blk = pl.BlockSpec((None, br, bc), lambda s, i, j, c_ref: (s, i, j))
    return _pcall(
        body, name=name, out_shape=_sds(r.shape, g.dtype),
        grid_spec=pltpu.PrefetchScalarGridSpec(
            num_scalar_prefetch=1, grid=(N_CHIP, n // br, k // bc),
            in_specs=[pl.BlockSpec((None, br, bc), lambda s, i, j, c_ref: (2 * s + c_ref[0], i, j)), blk],
            out_specs=blk),
        compiler_params=_params(("parallel", "parallel", "parallel")),
    )(core, g, r)


def _sum_heads(dkh, *, name):
    H, S, DQ = dkh.shape
    bs = _pick(S, 512, 8)

    def body(d_ref, o_ref):
        acc = d_ref[0][:, MLA_NOPE:]
        for h in range(1, H):
            acc = acc + d_ref[h][:, MLA_NOPE:]
        o_ref[...] = acc

    return _pcall(body, name=name, grid=(S // bs,),
                  in_specs=[pl.BlockSpec((H, bs, DQ), lambda i: (0, i, 0))],
                  out_specs=pl.BlockSpec((bs, MLA_ROPE), lambda i: (i, 0)),
                  out_shape=_sds((S, MLA_ROPE), F32), compiler_params=_params(("parallel",)))(dkh)


N_PEER = N_DEV - 1
N_CHIP = N_DEV // 2
HOST_TAIL_FRACTION = 8


def _coords():
    return lax.axis_index("x"), lax.axis_index("y"), lax.axis_index("c")


def _ag_phase(phase, x_refs, out_refs, send_sems, recv_sems, local_sems):
    n = len(x_refs)
    x, y, c = _coords()
    me, sibling = (x, y, c), (x, y, 1 - c)
    chips = [(1 - x, y), (x, 1 - y), (1 - x, 1 - y)]

    def copy(a, k, block, to, src=None):
        px, py, pc = block
        dst = out_refs[a].at[4 * px + 2 * py + pc]
        return pltpu.make_async_remote_copy(
            src_ref=dst if src is None else src, dst_ref=dst,
            send_sem=send_sems.at[a * N_PEER + k], recv_sem=recv_sems.at[a * N_PEER + k],
            device_id=to, device_id_type=pl.DeviceIdType.MESH)

    def local(a):
        return pltpu.make_async_copy(x_refs[a], out_refs[a].at[4 * x + 2 * y + c], local_sems.at[a])

    def first(a):
        return ([copy(a, 0, me, sibling, src=x_refs[a])]
                + [copy(a, 1 + j, me, (*chip, c), src=x_refs[a]) for j, chip in enumerate(chips)])

    def passed(a, j):
        return copy(a, 4 + j, (*chips[j], c), sibling)

    if phase == 0:
        for a in range(n):
            local(a).start()
            for cp in first(a):
                cp.start()
    elif phase == 1:
        for j in range(len(chips)):
            for a in range(n):
                copy(a, 1 + j, (*chips[j], c), me).wait_recv()
                passed(a, j).start()
    else:
        for a in range(n):
            copy(a, 0, sibling, me).wait_recv()
            for j in range(len(chips)):
                copy(a, 4 + j, (*chips[j], 1 - c), me).wait_recv()
        for a in range(n):
            for cp in first(a):
                cp.wait_send()
            for j in range(len(chips)):
                passed(a, j).wait_send()
            local(a).wait()


def _pair_phase(phase, g_refs, out_refs, send_sems, recv_sems, local_sems):
    n = len(g_refs)
    x, y, c = _coords()

    def copy(a, i):
        return pltpu.make_async_remote_copy(
            src_ref=g_refs[a].at[2 * i + (1 - c)], dst_ref=out_refs[a].at[i],
            send_sem=send_sems.at[a * N_PEER + i], recv_sem=recv_sems.at[a * N_PEER + i],
            device_id=(x, y, 1 - c), device_id_type=pl.DeviceIdType.MESH)

    if phase == 0:
        for a in range(n):
            for i in range(N_CHIP):
                copy(a, i).start()
    elif phase == 2:
        for a in range(n):
            for i in range(N_CHIP):
                copy(a, i).wait_recv()
        for a in range(n):
            for i in range(N_CHIP):
                copy(a, i).wait_send()


def _cross_phase(phase, p_refs, out_refs, send_sems, recv_sems, local_sems):
    n = len(p_refs)
    x, y, c = _coords()
    mine = 2 * x + y

    def local(a):
        return pltpu.make_async_copy(p_refs[a].at[mine], out_refs[a].at[mine], local_sems.at[a])

    def pair(a, k):
        px = 1 - x if k & 2 else x
        py = 1 - y if k & 1 else y
        peer = 2 * px + py
        sem = a * N_PEER + k - 1
        send = pltpu.make_async_remote_copy(
            src_ref=p_refs[a].at[peer], dst_ref=out_refs[a].at[mine], send_sem=send_sems.at[sem],
            recv_sem=recv_sems.at[sem], device_id=(px, py, c), device_id_type=pl.DeviceIdType.MESH)
        recv = pltpu.make_async_remote_copy(
            src_ref=p_refs[a].at[peer], dst_ref=out_refs[a].at[peer], send_sem=send_sems.at[sem],
            recv_sem=recv_sems.at[sem], device_id=(px, py, c), device_id_type=pl.DeviceIdType.MESH)
        return send, recv

    if phase == 0:
        for a in range(n):
            local(a).start()
        for k in range(1, N_CHIP):
            for a in range(n):
                pair(a, k)[0].start()
    elif phase == 2:
        for k in range(1, N_CHIP):
            for a in range(n):
                pair(a, k)[1].wait_recv()
        for k in range(1, N_CHIP):
            for a in range(n):
                pair(a, k)[0].wait_send()
        for a in range(n):
            local(a).wait()


_PHASES = {"ag": _ag_phase, "pair": _pair_phase, "cross": _cross_phase}


def _job(kind, arrays):
    arrays = list(arrays)
    if kind == "ag":
        shapes = [(N_DEV,) + a.shape for a in arrays]
    elif kind == "pair":
        shapes = [(N_CHIP,) + a.shape[1:] for a in arrays]
    else:
        shapes = [a.shape for a in arrays]
    return dict(parts=[(kind, len(arrays))], ins=arrays, outs=[_sds(s, a.dtype) for s, a in zip(shapes, arrays)])


def _join(jobs):
    jobs = [j for j in jobs if j is not None]
    if not jobs:
        return None
    return dict(parts=[p for j in jobs for p in j["parts"]], ins=[a for j in jobs for a in j["ins"]],
                outs=[o for j in jobs for o in j["outs"]])


def _hosted_call(body, job, *, name, grid, in_specs, out_specs, out_shape, scratch_shapes, semantics, args):
    if job is None:
        outs = _pcall(body, name=name, grid=grid, in_specs=in_specs, out_specs=out_specs, out_shape=out_shape,
                      scratch_shapes=scratch_shapes, compiler_params=_params(semantics))(*args)
        return outs, []
    n_in, n_out, n_scr, nj = len(in_specs), len(out_specs), len(scratch_shapes), len(job["ins"])
    parts = job["parts"]
    total = 1
    for g in grid:
        total *= g
    late = total - max(total // HOST_TAIL_FRACTION, 1) if total > 1 else 0

    def wrapped(*refs):
        ins, refs = refs[:n_in], refs[n_in:]
        jins, refs = refs[:nj], refs[nj:]
        outs, refs = refs[:n_out], refs[n_out:]
        jouts, refs = refs[:nj], refs[nj:]
        scr, sems = refs[:n_scr], refs[n_scr:]
        step = 0
        for d, g in enumerate(grid):
            step = step * g + pl.program_id(d)

        def run(phase):
            off = 0
            for i, (kind, n) in enumerate(parts):
                _PHASES[kind](phase, jins[off:off + n], jouts[off:off + n], *sems[3 * i:3 * i + 3])
                off += n

        pl.when(step == 0)(lambda: run(0))
        pl.when(step == late)(lambda: run(1))
        body(*ins, *outs, *scr)
        pl.when(step == total - 1)(lambda: run(2))

    any_spec = pl.BlockSpec(memory_space=pl.ANY)
    sems = []
    for kind, n in parts:
        sems += [pltpu.SemaphoreType.DMA((n * N_PEER,)), pltpu.SemaphoreType.DMA((n * N_PEER,)),
                 pltpu.SemaphoreType.DMA((n,))]
    outs = _pcall(
        wrapped, name=name, grid=grid, in_specs=list(in_specs) + [any_spec] * nj,
        out_specs=list(out_specs) + [any_spec] * nj, out_shape=list(out_shape) + job["outs"],
        scratch_shapes=list(scratch_shapes) + sems,
        compiler_params=_params(("arbitrary",) * len(grid)),
    )(*args, *job["ins"])
    return outs[:n_out], outs[n_out:]


def _exchange(job, *, name):
    outs, jouts = _hosted_call(lambda: None, job, name=name, grid=(1,), in_specs=[], out_specs=[], out_shape=[],
                               scratch_shapes=[], semantics=("arbitrary",), args=())
    return jouts


_TRANSPOSED = {"w_in": True, "w_out": False, "w_gate": True, "w_up": True, "w_down": False, "w_uq": True, "w_ukv": True}
_BIG = tuple(_TRANSPOSED)
_SMALL = ("ln_in_g", "ln_in_b", "q_norm_g", "kv_norm_g", "ret_gn_g", "ret_gn_b", "ln1_g", "ln1_b", "ln2_g", "ln2_b")
SMALL_COLS = 128


def _rope_tables(pos, dim):
    inv_freq = ROPE_THETA ** (-jnp.arange(0, dim, 2, dtype=F32) / dim)
    ang = pos.astype(F32)[:, None] * inv_freq
    return jnp.cos(ang), jnp.sin(ang)


def _split_in(wt_in):
    a, b, c = MLA_Q_LORA, MLA_Q_LORA + MLA_KV_LORA, MLA_Q_LORA + MLA_KV_LORA + MLA_ROPE
    return wt_in[:a], wt_in[a:b], wt_in[b:c], wt_in[c:]


def _mla_pack_rope(q, kr):
    S = q.shape[0]
    H, hr = MLA_HEADS, MLA_ROPE // 2
    q3 = q.reshape(S, H, MLA_QK)
    t1 = jnp.concatenate([q3[:, :, MLA_NOPE:MLA_NOPE + hr].reshape(S, H * hr), kr[:, :hr]], axis=1)
    t2 = jnp.concatenate([q3[:, :, MLA_NOPE + hr:].reshape(S, H * hr), kr[:, hr:]], axis=1)
    return q3[:, :, :MLA_NOPE], t1, t2


def _mla_unpack_rope(q_nope, o1, o2):
    S = o1.shape[0]
    H, hr = MLA_HEADS, MLA_ROPE // 2
    q3 = jnp.concatenate([q_nope, o1[:, :H * hr].reshape(S, H, hr), o2[:, :H * hr].reshape(S, H, hr)], axis=-1)
    kr = jnp.concatenate([o1[:, H * hr:], o2[:, H * hr:]], axis=1)
    return q3, kr


class _Plan:
    FIRST = [("ag", ((0, "w_in"),))]
    LAST = [("cross", ((0, "w_in"),))]
    HOSTS = {
        "l0_cq": [("ag", ((0, "w_uq"), (0, "w_ukv")))], "l0_hR": [("ag", ((0, "w_out"),))],
        "l0_attn": [("ag", ((0, "w_gate"),))], "l0_gate": [("ag", ((0, "w_up"),))],
        "l0_up": [("ag", ((0, "w_down"),))], "l0_down": [("ag", ((1, "w_in"), (1, "w_uq"), (1, "w_ukv")))],
        "l1_hR": [("ag", ((1, "w_out"),))], "l1_attn": [("ag", ((1, "w_gate"),))],
        "l1_gate": [("ag", ((1, "w_up"),))], "l1_up": [("ag", ((1, "w_down"),))],
        "l0_b_dact": [("cross", ((1, "w_in"),))]}
    for _l in (0, 1):
        HOSTS.update({
            "l%d_b_wgate" % _l: [("pair", ((_l, "w_down"),))],
            "l%d_b_wup" % _l: [("pair", ((_l, "w_gate"),)), ("cross", ((_l, "w_down"),))],
            "l%d_b_dx1a" % _l: [("pair", ((_l, "w_up"),)), ("cross", ((_l, "w_gate"),))],
            "l%d_b_dx1b" % _l: [("cross", ((_l, "w_up"),))],
            "l%d_b_attn" % _l: [("pair", ((_l, "w_out"),))],
            "l%d_b_win" % _l: [("pair", ((_l, "w_uq"), (_l, "w_ukv"))), ("cross", ((_l, "w_out"),))],
            "l%d_b_dx" % _l: [("pair", ((_l, "w_in"),)), ("cross", ((_l, "w_uq"), (_l, "w_ukv")))]})

    def __init__(self, local):
        self.local = local
        self.full = {}
        self.grads = {}
        self.paired = {}
        self.recv = {}

    def _by_device(self, k):
        return self.grads[k].reshape((N_DEV,) + self.local[k].shape)

    def _make(self, kind, keys):
        if kind == "ag":
            return _job(kind, [self.local[k] for k in keys])
        if kind == "pair":
            return _job(kind, [self._by_device(k) for k in keys])
        return _job(kind, [_pair_sum(self._by_device(k), self.paired[k], name="pairsum_l%d_%s" % k) for k in keys])

    def _done(self, kind, keys, outs):
        for k, o in zip(keys, outs):
            if kind == "ag":
                self.full[k] = o.reshape(N_DEV * o.shape[1], o.shape[2])
            elif kind == "pair":
                self.paired[k] = o
            else:
                self.recv[k] = o

    def _run(self, todo, call):
        outs = call(_join([self._make(kind, keys) for kind, keys in todo]))
        for kind, keys in todo:
            self._done(kind, keys, outs[:len(keys)])
            outs = outs[len(keys):]

    def gather_first(self):
        self._run(self.FIRST, lambda job: _exchange(job, name="ag_first"))

    def send_last(self):
        self._run(self.LAST, lambda job: _exchange(job, name="rs_last"))

    def call(self, fn, name, *args, **kw):
        if name not in self.HOSTS:
            return fn(*args, name=name, **kw)
        res = []

        def run(job):
            *outs, jouts = fn(*args, name=name, job=job, **kw)
            res.extend(outs)
            return jouts

        self._run(self.HOSTS[name], run)
        return res[0] if len(res) == 1 else tuple(res)


def _layer_fwd(x, xb, plan, p, tabs, l):
    S, D = x.shape
    H = MLA_HEADS
    nm = lambda s: "l%d_%s" % (l, s)
    w = lambda n: plan.full[(l, n)]
    mm = lambda name, *a, **kw: plan.call(_mm, nm(name), *a, **kw)
    wt_q, wt_kv, wt_kr, wt_r = _split_in(w("w_in"))
    cq = mm("cq", xb, wt_q, tb=True)
    ckv = mm("ckv", xb, wt_kv, tb=True)
    kr = mm("krope", xb, wt_kr, tb=True)
    hR = mm("hR", xb, wt_r, tb=True)
    _, qn, qn_hat, q_rstd = _norm_fwd(cq, p["q_norm_g"], None, center=False, eps=RMS_EPS, want_f32=False,
                                      name=nm("qnorm"))
    _, kvn, kvn_hat, kv_rstd = _norm_fwd(ckv, p["kv_norm_g"], None, center=False, eps=RMS_EPS, want_f32=False,
                                         name=nm("kvnorm"))
    q = mm("uq", qn, w("w_uq"), tb=True)
    kv = mm("ukv", kvn, w("w_ukv"), tb=True)
    q_nope, t1, t2 = _mla_pack_rope(q, kr)
    o1, o2 = _rope(t1, t2, tabs["cos_m"], tabs["sin_m"], name=nm("rope"))
    q3, k_r = _mla_unpack_rope(q_nope, o1, o2)
    kv3 = kv.reshape(S, H, MLA_NOPE + MLA_V)
    qh = q3.transpose(1, 0, 2).astype(BF16)
    kh = jnp.concatenate([kv3[:, :, :MLA_NOPE], jnp.broadcast_to(k_r[:, None, :], (S, H, MLA_ROPE))],
                         axis=-1).transpose(1, 0, 2).astype(BF16)
    vh = kv3[:, :, MLA_NOPE:].transpose(1, 0, 2).astype(BF16)
    a, lse = plan.call(_attn_fwd, nm("attn"), qh, kh, vh)
    o_ret, states = _ret_fwd(hR, tabs["cos_r"], tabs["sin_r"], tabs["ret"], name=nm("ret"))
    r = _gn_gate_fwd(o_ret, hR, p["ret_gn_g"], p["ret_gn_b"], name=nm("gn"))
    mix_in = jnp.concatenate([a.astype(BF16), r], axis=1)
    mix = mm("out", mix_in, w("w_out"))
    x1, x1b, x1_hat, rstd1 = _norm_fwd(x, p["ln1_g"], p["ln1_b"], res=mix, alpha=p["alpha"], eps=LN_EPS, name=nm("ln1"))
    g = mm("gate", x1b, w("w_gate"), tb=True)
    gb, ub, act = mm("up", x1b, w("w_up"), tb=True, extras=(g,), epilogue=_swiglu, out_dtypes=(BF16, BF16, BF16))
    f = mm("down", act, w("w_down"))
    x2, x2b, x2_hat, rstd2 = _norm_fwd(x1, p["ln2_g"], p["ln2_b"], res=f, alpha=p["alpha"], eps=LN_EPS, name=nm("ln2"))
    saved = dict(xb=xb, qn=qn, qn_hat=qn_hat, q_rstd=q_rstd, kvn=kvn, kvn_hat=kvn_hat, kv_rstd=kv_rstd,
                 qh=qh, kh=kh, vh=vh, a=a, lse=lse, hR=hR, o_ret=o_ret, states=states, mix_in=mix_in,
                 x1b=x1b, x1_hat=x1_hat, rstd1=rstd1, gb=gb, ub=ub, act=act, x2_hat=x2_hat, rstd2=rstd2)
    return x2, x2b, saved


def _layer_bwd(dx2, sv, plan, p, tabs, l):
    S, D = dx2.shape
    H = MLA_HEADS
    nm = lambda s: "l%d_b_%s" % (l, s)
    w = lambda n: plan.full[(l, n)]
    mm = lambda name, *a, **kw: plan.call(_mm, nm(name), *a, **kw)
    alpha = p["alpha"]
    gw, gp = plan.grads, {}
    dz2, dz2b, gp["ln2_g"], gp["ln2_b"] = _norm_bwd(dx2, sv["x2_hat"], sv["rstd2"], p["ln2_g"], name=nm("ln2"))
    dg, du = mm("dact", dz2b, w("w_down"), tb=True, extras=(sv["gb"], sv["ub"]), epilogue=_swiglu_bwd,
                out_dtypes=(BF16, BF16))
    gw[(l, "w_down")] = mm("wdown", sv["act"], dz2b, ta=True, out_dtype=BF16)
    gw[(l, "w_gate")] = mm("wgate", dg, sv["x1b"], ta=True, out_dtype=BF16)
    gw[(l, "w_up")] = mm("wup", du, sv["x1b"], ta=True, out_dtype=BF16)
    t = mm("dx1a", dg, w("w_gate"), add=dz2, add_scale=alpha)
    dx1 = mm("dx1b", du, w("w_up"), add=t)
    dz1, dz1b, gp["ln1_g"], gp["ln1_b"] = _norm_bwd(dx1, sv["x1_hat"], sv["rstd1"], p["ln1_g"], name=nm("ln1"))
    dmix = mm("dmix", dz1b, w("w_out"), tb=True)
    gw[(l, "w_out")] = mm("wout", sv["mix_in"], dz1b, ta=True, out_dtype=BF16)
    ret_col0 = (H * MLA_V) // RET_V
    do_ret, drg, gp["ret_gn_g"], gp["ret_gn_b"] = _gn_gate_bwd(
        dmix, sv["o_ret"], sv["hR"], p["ret_gn_g"], p["ret_gn_b"], dr_col0=ret_col0, name=nm("gn"))
    drq, drk, drv = _ret_bwd(do_ret, sv["hR"], sv["states"], tabs["cos_r"], tabs["sin_r"], tabs["ret"], name=nm("ret"))
    dqh, dkh, dvh = plan.call(_attn_bwd, nm("attn"), sv["qh"], sv["kh"], sv["vh"], sv["a"], dmix, sv["lse"], do_col0=0)
    dq3 = dqh.transpose(1, 0, 2)
    dk_r = _sum_heads(dkh, name=nm("dkr"))
    dq_nope, dt1, dt2 = _mla_pack_rope(dq3.reshape(S, H * MLA_QK), dk_r)
    di1, di2 = _rope(dt1, dt2, tabs["cos_m"], tabs["nsin_m"], name=nm("rope"))
    dq3, dkr = _mla_unpack_rope(dq_nope, di1, di2)
    dq = dq3.reshape(S, H * MLA_QK).astype(BF16)
    dkv = jnp.concatenate([dkh[:, :, :MLA_NOPE].transpose(1, 0, 2), dvh.transpose(1, 0, 2)],
                          axis=-1).reshape(S, -1).astype(BF16)
    gw[(l, "w_uq")] = mm("wuq", dq, sv["qn"], ta=True, out_dtype=BF16)
    dqn = mm("dqn", dq, w("w_uq"))
    gw[(l, "w_ukv")] = mm("wukv", dkv, sv["kvn"], ta=True, out_dtype=BF16)
    dkvn = mm("dkvn", dkv, w("w_ukv"))
    _, dcq, gp["q_norm_g"], _ = _norm_bwd(dqn, sv["qn_hat"], sv["q_rstd"], p["q_norm_g"], center=False,
                                          want_f32=False, name=nm("qnorm"))
    _, dckv, gp["kv_norm_g"], _ = _norm_bwd(dkvn, sv["kvn_hat"], sv["kv_rstd"], p["kv_norm_g"], center=False,
                                            want_f32=False, name=nm("kvnorm"))
    dh = jnp.concatenate([dcq, dckv, dkr.astype(BF16), drq, drk, drv, drg], axis=1)
    gw[(l, "w_in")] = mm("win", dh, sv["xb"], ta=True, out_dtype=BF16)
    dx = mm("dx", dh, w("w_in"), add=dz1, add_scale=alpha)
    return dx, gp


def _local_step(x, target, pos, small, plan, depth):
    alpha = (2 * depth) ** 0.25
    cos_m, sin_m = _rope_tables(pos, MLA_ROPE)
    cos_r, sin_r = _rope_tables(pos, RET_QK)
    reps = MLA_HEADS + 1
    tabs = dict(cos_m=jnp.tile(cos_m, (1, reps)), sin_m=jnp.tile(sin_m, (1, reps)),
                nsin_m=jnp.tile(-sin_m, (1, reps)), cos_r=cos_r, sin_r=sin_r, ret=_ret_tables(_pick(x.shape[0], RET_BLOCK, CHUNK)))
    h, hb, h_hat, h_rstd = _norm_fwd(x, small["ln_in_g"], small["ln_in_b"], eps=LN_EPS, name="ln_in")
    saved, ps = [], []
    for l in range(depth):
        p = {k: small[k][l] for k in _SMALL[2:]}
        p["alpha"] = alpha
        h, hb, sv = _layer_fwd(h, hb, plan, p, tabs, l)
        saved.append(sv)
        ps.append(p)
    dy, loss = _loss_head(h, target, name="loss")
    gps = [None] * depth
    for l in reversed(range(depth)):
        dy, gps[l] = _layer_bwd(dy, saved[l], plan, ps[l], tabs, l)
    grad_x, _, g_in_g, g_in_b = _norm_bwd(dy, h_hat, h_rstd, small["ln_in_g"], want_bf16=False, name="b_ln_in")
    gsmall = {"ln_in_g": g_in_g, "ln_in_b": g_in_b}
    for k in _SMALL[2:]:
        gsmall[k] = jnp.stack([gps[l][k] for l in range(depth)])
    return loss, grad_x, gsmall


def kernel(x, positions, ln_in_g, ln_in_b, w_in, q_norm_g, kv_norm_g, w_uq, w_ukv, ret_gn_g, ret_gn_b, w_out, ln1_g, ln1_b, w_gate, w_up, w_down, ln2_g, ln2_b, loss_target, m_ln_in_g, m_ln_in_b, m_w_in, m_q_norm_g, m_kv_norm_g, m_w_uq, m_w_ukv, m_ret_gn_g, m_ret_gn_b, m_w_out, m_ln1_g, m_ln1_b, m_w_gate, m_w_up, m_w_down, m_ln2_g, m_ln2_b, v_ln_in_g, v_ln_in_b, v_w_in, v_q_norm_g, v_kv_norm_g, v_w_uq, v_w_ukv, v_ret_gn_g, v_ret_gn_b, v_w_out, v_ln1_g, v_ln1_b, v_w_gate, v_w_up, v_w_down, v_ln2_g, v_ln2_b):
    names = ["ln_in_g", "ln_in_b", "w_in", "q_norm_g", "kv_norm_g", "w_uq", "w_ukv", "ret_gn_g", "ret_gn_b", "w_out",
             "ln1_g", "ln1_b", "w_gate", "w_up", "w_down", "ln2_g", "ln2_b"]
    wv = dict(zip(names, (ln_in_g, ln_in_b, w_in, q_norm_g, kv_norm_g, w_uq, w_ukv, ret_gn_g, ret_gn_b, w_out,
                          ln1_g, ln1_b, w_gate, w_up, w_down, ln2_g, ln2_b)))
    mv = dict(zip(names, (m_ln_in_g, m_ln_in_b, m_w_in, m_q_norm_g, m_kv_norm_g, m_w_uq, m_w_ukv, m_ret_gn_g,
                          m_ret_gn_b, m_w_out, m_ln1_g, m_ln1_b, m_w_gate, m_w_up, m_w_down, m_ln2_g, m_ln2_b)))
    vv = dict(zip(names, (v_ln_in_g, v_ln_in_b, v_w_in, v_q_norm_g, v_kv_norm_g, v_w_uq, v_w_ukv, v_ret_gn_g,
                          v_ret_gn_b, v_w_out, v_ln1_g, v_ln1_b, v_w_gate, v_w_up, v_w_down, v_ln2_g, v_ln2_b)))
    depth = w_in.shape[0]
    assert depth == 2, "the exchange plan is written for two layers"

    keys = [(l, n) for l in range(depth) for n in _BIG]
    plan = _Plan({(l, n): (wv[n][l].T if _TRANSPOSED[n] else wv[n][l]).astype(BF16) for l, n in keys})
    plan.gather_first()

    small = {n: wv[n] for n in _SMALL}
    loss, grad_x, gsmall = _local_step(x[0], loss_target[0], positions[0], small, plan, depth)
    loss = lax.psum(loss, MESH_AXES)
    plan.send_last()

    gshard = {n: [None] * depth for n in _BIG}
    for l, n in keys:
        tot = _sum_slots(plan.recv[(l, n)], name="sum_l%d_%s" % (l, n))
        gshard[n][l] = tot.T if _TRANSPOSED[n] else tot
    grads = {n: jnp.stack(v) for n, v in gshard.items()}

    flat = jnp.concatenate([gsmall[n].reshape(-1) for n in _SMALL])
    n_small = flat.shape[0]
    rows = -(-n_small // (SMALL_COLS * 8)) * 8
    flat = jnp.pad(flat, (0, rows * SMALL_COLS - n_small)).reshape(rows, SMALL_COLS)
    tot = _sum_slots(_exchange(_job("ag", [flat]), name="ag_small")[0], name="sum_small").reshape(-1)
    off = 0
    for n in _SMALL:
        grads[n] = tot[off:off + wv[n].size].reshape(wv[n].shape)
        off += wv[n].size

    delta, new_m, new_v = {}, {}, {}
    for n in names:
        w2 = wv[n] if wv[n].ndim > 1 else wv[n].reshape(1, -1)
        d, nm_, nv_ = _adamw(w2, grads[n].reshape(w2.shape), mv[n].reshape(w2.shape), vv[n].reshape(w2.shape),
                             name="adamw_" + n)
        delta[n], new_m[n], new_v[n] = d.reshape(wv[n].shape), nm_.reshape(wv[n].shape), nv_.reshape(wv[n].shape)

    return (loss, grad_x[None], *[grads[n] for n in names], *[delta[n] for n in names],
            *[new_m[n] for n in names], *[new_v[n] for n in names])
```

```python
import jax
import jax.numpy as jnp
from jax import lax
from jax.experimental import pallas as pl
from jax.experimental.pallas import tpu as pltpu

F32 = jnp.float32
BF16 = jnp.bfloat16

CHUNK = 64
CHUNK_SHIFT = 6
MLA_HEADS = 8
MLA_Q_LORA = 512
MLA_KV_LORA = 256
MLA_NOPE = 128
MLA_ROPE = 64
MLA_V = 128
MLA_QK = MLA_NOPE + MLA_ROPE
LANES = 128
MLA_PAD = 2 * LANES
RET_HEADS = 4
RET_QK = 256
RET_V = 256
ROPE_THETA = 10000.0
LN_EPS = 1e-5
RMS_EPS = 1e-6
GN_EPS = 1e-5
ADAM_LR = 0.001
ADAM_B1 = 0.9
ADAM_B2 = 0.999
ADAM_EPS = 1e-08
ADAM_WD = 0.01
ADAM_STEP = 10

LOG2_E = 1.4426950408889634

N_DEV = 8
MESH_AXES = ("x", "y", "c")
VMEM_LIMIT_BYTES = 56 * 1024 * 1024
MM_VMEM_BUDGET = 40 * 1024 * 1024
MM_ACC_PENALTY = 0.85
ATTN_BLOCK = 256
ATTN_KEY_STEP = 512
RET_BLOCK = 256


def _pick(n, pref, mult):
    best = None
    d = mult
    while d <= min(n, pref):
        if n % d == 0:
            best = d
        d += mult
    return n if best is None else best


def _divisors(n, mult, cap):
    ds = [d for d in range(mult, min(n, cap) + 1, mult) if n % d == 0]
    return ds or [n]


def _pcall(body, **kw):
    return pl.pallas_call(body, **kw)


def _params(sem):
    return pltpu.CompilerParams(dimension_semantics=sem, vmem_limit_bytes=VMEM_LIMIT_BYTES)


def _sds(shape, dtype):
    return jax.ShapeDtypeStruct(shape, dtype)


def _mm_tiles(M, N, K, ta, sa, sb, tile_bytes):
    best = None
    for bk in _divisors(K, 128, 8192):
        nk = K // bk
        for bm in _divisors(M, 128 if ta else 16, 1024):
            for bn in _divisors(N, 128, 1024):
                vmem = 2 * (bm * bk * sa + bk * bn * sb) + 2 * bm * bn * tile_bytes + (bm * bn * 4 if nk > 1 else 0)
                if vmem > MM_VMEM_BUDGET:
                    continue
                flops_per_byte = 1.0 / (1.0 / bm + (1.0 / max(N, bn) if nk == 1 else 1.0 / bn))
                score = (flops_per_byte * (1.0 if nk == 1 else MM_ACC_PENALTY), bk, bn, bm)
                if best is None or score > best[0]:
                    best = (score, bm, bn, bk)
    assert best is not None, (M, N, K)
    return best[1:]


def _mm(a, b, *, ta=False, tb=False, add=None, add_scale=1.0, out_dtype=F32, extras=(), epilogue=None,
        out_dtypes=None, job=None, name):
    if ta:
        K, M = a.shape
    else:
        M, K = a.shape
    if tb:
        N, K2 = b.shape
    else:
        K2, N = b.shape
    assert K == K2, (a.shape, b.shape, ta, tb)
    if epilogue is None:
        assert not extras and out_dtypes is None
        out_dtypes = (out_dtype,)
        if add is not None:
            extras, epilogue = (add,), lambda r, c: (r + add_scale * c,)
        else:
            epilogue = lambda r: (r,)
    n_ex, n_out = len(extras), len(out_dtypes)
    tile_bytes = sum(e.dtype.itemsize for e in extras) + sum(jnp.dtype(d).itemsize for d in out_dtypes)
    bm, bn, bk = _mm_tiles(M, N, K, ta, a.dtype.itemsize, b.dtype.itemsize, tile_bytes)
    nk = K // bk
    a_spec = (pl.BlockSpec((bk, bm), lambda i, j, k: (k, i)) if ta
              else pl.BlockSpec((bm, bk), lambda i, j, k: (i, k)))
    b_spec = (pl.BlockSpec((bn, bk), lambda i, j, k: (j, k)) if tb
              else pl.BlockSpec((bk, bn), lambda i, j, k: (k, j)))
    o_spec = pl.BlockSpec((bm, bn), lambda i, j, k: (i, j))
    dims = (((0 if ta else 1,), (1 if tb else 0,)), ((), ()))

    def body(*refs):
        a_ref, b_ref = refs[0], refs[1]
        ex_refs, o_refs = refs[2:2 + n_ex], refs[2 + n_ex:2 + n_ex + n_out]
        part = lax.dot_general(a_ref[...].astype(BF16), b_ref[...].astype(BF16), dims, preferred_element_type=F32)

        def finish(r):
            for o_ref, v in zip(o_refs, epilogue(r, *[e[...] for e in ex_refs])):
                o_ref[...] = v.astype(o_ref.dtype)

        if nk == 1:
            finish(part)
        else:
            acc_ref = refs[-1]
            k = pl.program_id(2)

            @pl.when(k == 0)
            def _():
                acc_ref[...] = part

            @pl.when(k > 0)
            def _():
                acc_ref[...] += part

            @pl.when(k == nk - 1)
            def _():
                finish(acc_ref[...])

    outs, jouts = _hosted_call(
        body, job, name=name, grid=(M // bm, N // bn, nk), in_specs=[a_spec, b_spec] + [o_spec] * n_ex,
        out_specs=[o_spec] * n_out, out_shape=[_sds((M, N), d) for d in out_dtypes],
        scratch_shapes=[pltpu.VMEM((bm, bn), F32)] if nk > 1 else [],
        semantics=("parallel", "parallel", "arbitrary"), args=(a, b, *extras))
    res = list(outs) + ([jouts] if job is not None else [])
    return res[0] if len(res) == 1 else tuple(res)


def _norm_fwd(x, g, b, *, res=None, alpha=1.0, center=True, eps, want_f32=True, want_bf16=True, name):
    S, W = x.shape
    bs = _pick(S, 256, 16)
    has_res, has_b = res is not None, b is not None

    def body(*refs):
        it = iter(refs)
        x_ref = next(it)
        res_ref = next(it) if has_res else None
        g_ref = next(it)
        b_ref = next(it) if has_b else None
        y_ref = next(it) if want_f32 else None
        yb_ref = next(it) if want_bf16 else None
        xh_ref, r_ref = next(it), next(it)
        z = x_ref[...]
        if has_res:
            z = alpha * z + res_ref[...]
        if center:
            z = z - jnp.mean(z, axis=-1, keepdims=True)
        rstd = lax.rsqrt(jnp.mean(z * z, axis=-1, keepdims=True) + eps)
        xh = z * rstd
        y = xh * g_ref[...]
        if has_b:
            y = y + b_ref[...]
        if want_f32:
            y_ref[...] = y
        if want_bf16:
            yb_ref[...] = y.astype(BF16)
        xh_ref[...] = xh
        r_ref[...] = rstd

    row = pl.BlockSpec((bs, W), lambda i: (i, 0))
    vec = pl.BlockSpec((1, W), lambda i: (0, 0))
    in_specs = [row] + ([row] if has_res else []) + [vec] + ([vec] if has_b else [])
    args = (x,) + ((res,) if has_res else ()) + (g.reshape(1, W),) + ((b.reshape(1, W),) if has_b else ())
    out_specs = ([row] if want_f32 else []) + ([row] if want_bf16 else []) + [row, pl.BlockSpec((bs, 1), lambda i: (i, 0))]
    out_shape = (([_sds((S, W), F32)] if want_f32 else []) + ([_sds((S, W), BF16)] if want_bf16 else [])
                 + [_sds((S, W), F32), _sds((S, 1), F32)])
    outs = list(_pcall(body, name=name, grid=(S // bs,), in_specs=in_specs, out_specs=out_specs, out_shape=out_shape,
                       compiler_params=_params(("parallel",)))(*args))
    y = outs.pop(0) if want_f32 else None
    yb = outs.pop(0) if want_bf16 else None
    return y, yb, outs[0], outs[1]


def _norm_bwd(dy, xh, rstd, g, *, center=True, want_f32=True, want_bf16=True, name):
    S, W = dy.shape
    bs = _pick(S, 256, 16)

    def body(*refs):
        dy_ref, xh_ref, r_ref, g_ref = refs[:4]
        it = iter(refs[4:])
        dz_ref = next(it) if want_f32 else None
        dzb_ref = next(it) if want_bf16 else None
        dg_ref, db_ref = next(it), next(it)

        @pl.when(pl.program_id(0) == 0)
        def _():
            dg_ref[...] = jnp.zeros_like(dg_ref)
            db_ref[...] = jnp.zeros_like(db_ref)

        dyv = dy_ref[...]
        xhv = xh_ref[...]
        dyg = dyv * g_ref[...]
        m2 = jnp.mean(dyg * xhv, axis=-1, keepdims=True)
        t = dyg - xhv * m2
        if center:
            t = t - jnp.mean(dyg, axis=-1, keepdims=True)
        dz = r_ref[...] * t
        if want_f32:
            dz_ref[...] = dz
        if want_bf16:
            dzb_ref[...] = dz.astype(BF16)
        dg_ref[...] += jnp.sum(dyv * xhv, axis=0, keepdims=True)
        db_ref[...] += jnp.sum(dyv, axis=0, keepdims=True)

    row = pl.BlockSpec((bs, W), lambda i: (i, 0))
    vec = pl.BlockSpec((1, W), lambda i: (0, 0))
    out_specs = ([row] if want_f32 else []) + ([row] if want_bf16 else []) + [vec, vec]
    out_shape = (([_sds((S, W), F32)] if want_f32 else []) + ([_sds((S, W), BF16)] if want_bf16 else [])
                 + [_sds((1, W), F32), _sds((1, W), F32)])
    outs = list(_pcall(body, name=name, grid=(S // bs,),
                       in_specs=[row, row, pl.BlockSpec((bs, 1), lambda i: (i, 0)), vec],
                       out_specs=out_specs, out_shape=out_shape,
                       compiler_params=_params(("arbitrary",)))(dy, xh, rstd, g.reshape(1, W)))
    dz = outs.pop(0) if want_f32 else None
    dzb = outs.pop(0) if want_bf16 else None
    return dz, dzb, outs[0][0], outs[1][0]


def _rot_group(t, c, s):
    half = MLA_ROPE // 2
    lane = lax.broadcasted_iota(jnp.int32, t.shape, 1)
    swapped = jnp.where(lane < half, pltpu.roll(t, LANES - half, 1), pltpu.roll(t, half, 1))
    return t * c + swapped * s


def _mla_prep(q3, kv3, krp, c128, s128, *, name):
    H, S, DP = q3.shape
    bs = _pick(S, 512, 16)

    def body(q_ref, kv_ref, kr_ref, c_ref, s_ref, qh_ref, kh_ref, vh_ref):
        c, s = c_ref[...], s_ref[...]
        q, kv = q_ref[...], kv_ref[...]
        qh_ref[...] = jnp.concatenate([q[:, :LANES], _rot_group(q[:, LANES:], c, s)], axis=1).astype(BF16)
        kh_ref[...] = jnp.concatenate([kv[:, :LANES], _rot_group(kr_ref[...], c, s)], axis=1).astype(BF16)
        vh_ref[...] = kv[:, LANES:].astype(BF16)

    head = pl.BlockSpec((None, bs, DP), lambda h, i: (h, i, 0))
    tab = pl.BlockSpec((bs, LANES), lambda h, i: (i, 0))
    return _pcall(
        body, name=name, grid=(H, S // bs), in_specs=[head, head, tab, tab, tab],
        out_specs=[head, head, pl.BlockSpec((None, bs, LANES), lambda h, i: (h, i, 0))],
        out_shape=[_sds((H, S, DP), BF16), _sds((H, S, DP), BF16), _sds((H, S, LANES), BF16)],
        compiler_params=_params(("parallel", "parallel")),
    )(q3, kv3, krp, c128, s128)


def _mla_unprep(dqh, dkh, dvh, c128, s128, *, name):
    H, S, DP = dqh.shape
    bs = _pick(S, 512, 16)

    def body(dq_ref, dk_ref, dv_ref, c_ref, s_ref, oq_ref, okv_ref, okr_ref):
        c, ns = c_ref[...], -s_ref[...]
        dq, dk = dq_ref[...], dk_ref[...]
        oq_ref[...] = jnp.concatenate([dq[:, :LANES], _rot_group(dq[:, LANES:], c, ns)], axis=1).astype(BF16)
        okv_ref[...] = jnp.concatenate([dk[:, :LANES], dv_ref[...]], axis=1).astype(BF16)
        dkr = _rot_group(dk[:, LANES:], c, ns)

        @pl.when(pl.program_id(1) == 0)
        def _():
            okr_ref[...] = dkr

        @pl.when(pl.program_id(1) > 0)
        def _():
            okr_ref[...] += dkr

    head = pl.BlockSpec((None, bs, DP), lambda i, h: (h, i, 0))
    tab = pl.BlockSpec((bs, LANES), lambda i, h: (i, 0))
    return _pcall(
        body, name=name, grid=(S // bs, H),
        in_specs=[head, head, pl.BlockSpec((None, bs, LANES), lambda i, h: (h, i, 0)), tab, tab],
        out_specs=[head, head, tab],
        out_shape=[_sds((H, S, DP), BF16), _sds((H, S, DP), BF16), _sds((S, LANES), F32)],
        compiler_params=_params(("parallel", "arbitrary")),
    )(dqh, dkh, dvh, c128, s128)


def _mm_call(a, b, *, grid, a_spec, b_spec, o_spec, dims, nk, out_shape, name):
    def body(*refs):
        a_ref, b_ref, o_ref = refs[:3]
        part = lax.dot_general(a_ref[...].astype(BF16), b_ref[...].astype(BF16), dims, preferred_element_type=F32)
        if nk == 1:
            o_ref[...] = part.astype(o_ref.dtype)
        else:
            acc_ref = refs[3]
            k = pl.program_id(2)

            @pl.when(k == 0)
            def _():
                acc_ref[...] = part

            @pl.when(k > 0)
            def _():
                acc_ref[...] += part

            @pl.when(k == nk - 1)
            def _():
                o_ref[...] = acc_ref[...].astype(o_ref.dtype)

    return _pcall(body, name=name, grid=grid, in_specs=[a_spec, b_spec], out_specs=o_spec, out_shape=out_shape,
                  scratch_shapes=[pltpu.VMEM(o_spec.block_shape, F32)] if nk > 1 else [],
                  compiler_params=_params(("parallel", "parallel", "arbitrary")))(a, b)


def _mm_heads_out(a, w3, *, name):
    S, K = a.shape
    H, n, _ = w3.shape
    bm = _pick(S, 1024, 16)
    out = _mm_call(
        a, w3.reshape(H * n, K), grid=(S // bm, H, 1), nk=1, dims=(((1,), (1,)), ((), ())),
        a_spec=pl.BlockSpec((bm, K), lambda i, j, k: (i, 0)), b_spec=pl.BlockSpec((n, K), lambda i, j, k: (j, 0)),
        o_spec=pl.BlockSpec((bm, n), lambda i, j, k: (j * (S // bm) + i, 0)), out_shape=_sds((H * S, n), F32), name=name)
    return out.reshape(H, S, n)


def _mm_heads_sum(a3, w3, *, name):
    H, S, n = a3.shape
    N = w3.shape[2]
    bm, bn = _pick(S, 1024, 16), _pick(N, 512, 128)
    return _mm_call(
        a3.reshape(H * S, n), w3.reshape(H * n, N), grid=(S // bm, N // bn, H), nk=H, dims=(((1,), (0,)), ((), ())),
        a_spec=pl.BlockSpec((bm, n), lambda i, j, k: (k * (S // bm) + i, 0)),
        b_spec=pl.BlockSpec((n, bn), lambda i, j, k: (k, j)),
        o_spec=pl.BlockSpec((bm, bn), lambda i, j, k: (i, j)), out_shape=_sds((S, N), F32), name=name)


def _mm_heads_tn(a3, b, *, name):
    H, S, n = a3.shape
    N = b.shape[1]
    bk, bn = _pick(S, 4096, 128), _pick(N, 512, 128)
    nk = S // bk
    return _mm_call(
        a3.reshape(H * S, n), b, grid=(H, N // bn, nk), nk=nk, dims=(((0,), (0,)), ((), ())),
        a_spec=pl.BlockSpec((bk, n), lambda i, j, k: (i * nk + k, 0)),
        b_spec=pl.BlockSpec((bk, bn), lambda i, j, k: (k, j)),
        o_spec=pl.BlockSpec((n, bn), lambda i, j, k: (i, j)), out_shape=_sds((H * n, N), BF16), name=name)


def _chunk_mask(row0, col0, B, G):
    rows = lax.shift_right_logical(row0 + lax.broadcasted_iota(jnp.int32, (B, G), 0), CHUNK_SHIFT)
    cols = lax.shift_right_logical(col0 + lax.broadcasted_iota(jnp.int32, (B, G), 1), CHUNK_SHIFT)
    return rows >= cols


def _mask_tail(x, qi, B, G, fill):
    L = x.shape[1]
    tail = jnp.where(_chunk_mask(qi * B, L - G, B, G), x[:, L - G:], fill)
    return tail if L == G else jnp.concatenate([x[:, :L - G], tail], axis=1)


def _for_key_prefix(qi, B, G, S, fn):
    per = G // B
    for b in range(S // G):
        pl.when(qi // per == b)(lambda b=b: fn((b + 1) * G))


def _nt(a, b):
    return lax.dot_general(a, b, (((1,), (1,)), ((), ())), preferred_element_type=F32)


def _nn(a, b):
    return lax.dot_general(a, b, (((1,), (0,)), ((), ())), preferred_element_type=F32)


def _tn(a, b):
    return lax.dot_general(a, b, (((0,), (0,)), ((), ())), preferred_element_type=F32)


def _attn_fwd(qh, kh, vh, *, job=None, name):
    H, S, DQ = qh.shape
    DV = vh.shape[-1]
    B = _pick(S, ATTN_BLOCK, CHUNK)
    G = _pick(S, ATTN_KEY_STEP, B)
    scale = float(MLA_QK) ** -0.5
    neg = float(jnp.finfo(jnp.float32).min)

    def body(q_ref, k_ref, v_ref, o_ref, lse_ref):
        qi = pl.program_id(1)

        def run(L):
            raw = _mask_tail(_nt(q_ref[...], k_ref[0:L, :]), qi, B, G, neg)
            m = jnp.max(raw, axis=-1, keepdims=True)
            e = jnp.exp2((raw - m) * (scale * LOG2_E))
            l = jnp.sum(e, axis=-1, keepdims=True)
            o_ref[...] = _nn((e * (1.0 / l)).astype(BF16), v_ref[0:L, :])
            lse_ref[...] = m * scale + jnp.log(l)

        _for_key_prefix(qi, B, G, S, run)

    outs, jouts = _hosted_call(
        body, job, name=name, grid=(H, S // B),
        in_specs=[pl.BlockSpec((None, B, DQ), lambda h, i: (h, i, 0)),
                  pl.BlockSpec((None, S, DQ), lambda h, i: (h, 0, 0)),
                  pl.BlockSpec((None, S, DV), lambda h, i: (h, 0, 0))],
        out_specs=[pl.BlockSpec((B, DV), lambda h, i: (i, h)),
                   pl.BlockSpec((None, B, 1), lambda h, i: (h, i, 0))],
        out_shape=[_sds((S, H * DV), F32), _sds((H, S, 1), F32)], scratch_shapes=[],
        semantics=("parallel", "parallel"), args=(qh, kh, vh))
    return (outs[0], outs[1]) if job is None else (outs[0], outs[1], jouts)


def _attn_bwd(qh, kh, vh, o, do, lse, *, do_col0, job=None, name):
    H, S, DQ = qh.shape
    DV = vh.shape[-1]
    B = _pick(S, ATTN_BLOCK, CHUNK)
    G = _pick(S, ATTN_KEY_STEP, B)
    scale = float(MLA_QK) ** -0.5

    def body(q_ref, k_ref, v_ref, o_ref, do_ref, lse_ref, dq_ref, dk_ref, dv_ref):
        qi = pl.program_id(1)

        @pl.when(qi == 0)
        def _():
            dk_ref[...] = jnp.zeros_like(dk_ref)
            dv_ref[...] = jnp.zeros_like(dv_ref)

        def run(L):
            q = q_ref[...]
            k = k_ref[0:L, :]
            dov = do_ref[...]
            dob = dov.astype(BF16)
            p = jnp.exp2(_nt(q, k) * (scale * LOG2_E) - lse_ref[...] * LOG2_E)
            p = _mask_tail(p, qi, B, G, 0.0)
            dsum = jnp.sum(dov * o_ref[...], axis=-1, keepdims=True)
            ds = (p * (_nt(dob, v_ref[0:L, :]) - dsum)).astype(BF16)
            dv_ref[0:L, :] += _tn(p.astype(BF16), dob)
            dk_ref[0:L, :] += _tn(ds, q)
            dq_ref[...] = _nn(ds, k) * scale

        _for_key_prefix(qi, B, G, S, run)

        @pl.when(qi == S // B - 1)
        def _():
            dk_ref[...] = dk_ref[...] * scale

    outs, jouts = _hosted_call(
        body, job, name=name, grid=(H, S // B),
        in_specs=[pl.BlockSpec((None, B, DQ), lambda h, i: (h, i, 0)),
                  pl.BlockSpec((None, S, DQ), lambda h, i: (h, 0, 0)),
                  pl.BlockSpec((None, S, DV), lambda h, i: (h, 0, 0)),
                  pl.BlockSpec((B, DV), lambda h, i: (i, h)),
                  pl.BlockSpec((B, DV), lambda h, i: (i, do_col0 + h)),
                  pl.BlockSpec((None, B, 1), lambda h, i: (h, i, 0))],
        out_specs=[pl.BlockSpec((None, B, DQ), lambda h, i: (h, i, 0)),
                   pl.BlockSpec((None, S, DQ), lambda h, i: (h, 0, 0)),
                   pl.BlockSpec((None, S, DV), lambda h, i: (h, 0, 0))],
        out_shape=[_sds((H, S, DQ), F32), _sds((H, S, DQ), F32), _sds((H, S, DV), F32)], scratch_shapes=[],
        semantics=("parallel", "arbitrary"), args=(qh, kh, vh, o, do, lse))
    return tuple(outs) if job is None else (*outs, jouts)


def _ret_tables(T):
    H = RET_HEADS
    log_gamma = jnp.log1p(-jnp.exp2(-5.0 - jnp.arange(H, dtype=F32)))
    idx = jnp.arange(T, dtype=F32)
    chunk = jnp.arange(T) // CHUNK
    visible = chunk[:, None] >= chunk[None, :]
    intra = jnp.where(visible[None], jnp.exp(log_gamma[:, None, None] * jnp.abs(idx[:, None] - idx[None, :])), 0.0)
    qd = jnp.exp(log_gamma[:, None] * (idx + 1.0))[:, :, None]
    kd = jnp.exp(log_gamma[:, None] * (T - 1.0 - idx))[:, :, None]
    cd = jnp.exp(log_gamma * T)[:, None, None]
    return intra, qd, kd, cd


def _rot(t, c, s):
    half = t.shape[-1] // 2
    t1, t2 = t[:, :half], t[:, half:]
    return jnp.concatenate([t1 * c - t2 * s, t2 * c + t1 * s], axis=-1)


def _rot_t(t, c, s):
    half = t.shape[-1] // 2
    t1, t2 = t[:, :half], t[:, half:]
    return jnp.concatenate([t1 * c + t2 * s, t2 * c - t1 * s], axis=-1)


def _dot(a, b, ca, cb):
    return lax.dot_general(a.astype(BF16), b.astype(BF16), (((ca,), (cb,)), ((), ())), preferred_element_type=F32)


def _ret_fwd(hR, cos, sin, tables, *, name):
    S = hR.shape[0]
    H, L, DK, DV = RET_HEADS, tables[0].shape[-1], RET_QK, RET_V
    NC = S // L
    qscale = float(DK) ** -0.5
    intra, qd, kd, cd = tables

    def body(q_ref, k_ref, v_ref, c_ref, s_ref, in_ref, qd_ref, kd_ref, cd_ref, o_ref, st_ref, state):
        @pl.when(pl.program_id(1) == 0)
        def _():
            state[...] = jnp.zeros_like(state)

        c, s = c_ref[...], s_ref[...]
        q = _rot(q_ref[...], c, s) * qscale
        k = _rot(k_ref[...], c, s)
        v = v_ref[...]
        st = state[...]
        st_ref[...] = st.astype(BF16)
        scores = _dot(q, k, 1, 1) * in_ref[...]
        o_ref[...] = _dot(scores, v, 1, 0) + _dot(q * qd_ref[...], st, 1, 0)
        state[...] = st * cd_ref[...] + _dot(k * kd_ref[...], v, 0, 0)

    blk = lambda off: pl.BlockSpec((L, DK), lambda h, c: (c, off + h))
    rope = pl.BlockSpec((L, DK // 2), lambda h, c: (c, 0))
    return _pcall(
        body, name=name, grid=(H, NC),
        in_specs=[blk(0), blk(H), blk(2 * H), rope, rope,
                  pl.BlockSpec((None, L, L), lambda h, c: (h, 0, 0)),
                  pl.BlockSpec((None, L, 1), lambda h, c: (h, 0, 0)),
                  pl.BlockSpec((None, L, 1), lambda h, c: (h, 0, 0)),
                  pl.BlockSpec((None, 1, 1), lambda h, c: (h, 0, 0))],
        out_specs=[pl.BlockSpec((L, DV), lambda h, c: (c, h)),
                   pl.BlockSpec((None, None, DK, DV), lambda h, c: (h, c, 0, 0))],
        out_shape=[_sds((S, H * DV), F32), _sds((H, NC, DK, DV), BF16)],
        scratch_shapes=[pltpu.VMEM((DK, DV), F32)],
        compiler_params=_params(("arbitrary", "arbitrary")),
    )(hR, hR, hR, cos, sin, intra, qd, kd, cd)


def _ret_bwd(do, hR, states, cos, sin, tables, *, name):
    S = hR.shape[0]
    H, L, DK, DV = RET_HEADS, tables[0].shape[-1], RET_QK, RET_V
    NC = S // L
    qscale = float(DK) ** -0.5
    intra, qd, kd, cd = tables

    def body(do_ref, q_ref, k_ref, v_ref, st_ref, c_ref, s_ref, in_ref, qd_ref, kd_ref, cd_ref,
             dq_ref, dk_ref, dv_ref, dstate):
        @pl.when(pl.program_id(1) == 0)
        def _():
            dstate[...] = jnp.zeros_like(dstate)

        c, s = c_ref[...], s_ref[...]
        q = _rot(q_ref[...], c, s) * qscale
        k = _rot(k_ref[...], c, s)
        v = v_ref[...]
        dov = do_ref[...]
        dst = dstate[...]
        dec = in_ref[...]
        qdv, kdv = qd_ref[...], kd_ref[...]
        scores = _dot(q, k, 1, 1) * dec
        da = _dot(dov, v, 1, 1) * dec
        dv_ref[...] = (_dot(scores, dov, 0, 0) + _dot(k * kdv, dst, 1, 0)).astype(BF16)
        dq = _dot(da, k, 1, 0) + _dot(dov, st_ref[...], 1, 1) * qdv
        dk = _dot(da, q, 0, 0) + _dot(v, dst, 1, 1) * kdv
        dq_ref[...] = _rot_t(dq * qscale, c, s).astype(BF16)
        dk_ref[...] = _rot_t(dk, c, s).astype(BF16)
        dstate[...] = dst * cd_ref[...] + _dot(q * qdv, dov, 0, 0)

    rev = lambda c: NC - 1 - c
    blk = lambda off: pl.BlockSpec((L, DK), lambda h, c: (rev(c), off + h))
    rope = pl.BlockSpec((L, DK // 2), lambda h, c: (rev(c), 0))
    out = pl.BlockSpec((L, DK), lambda h, c: (rev(c), h))
    return _pcall(
        body, name=name, grid=(H, NC),
        in_specs=[out, blk(0), blk(H), blk(2 * H),
                  pl.BlockSpec((None, None, DK, DV), lambda h, c: (h, rev(c), 0, 0)), rope, rope,
                  pl.BlockSpec((None, L, L), lambda h, c: (h, 0, 0)),
                  pl.BlockSpec((None, L, 1), lambda h, c: (h, 0, 0)),
                  pl.BlockSpec((None, L, 1), lambda h, c: (h, 0, 0)),
                  pl.BlockSpec((None, 1, 1), lambda h, c: (h, 0, 0))],
        out_specs=[out, out, out],
        out_shape=[_sds((S, H * DK), BF16)] * 3,
        scratch_shapes=[pltpu.VMEM((DK, DV), F32)],
        compiler_params=_params(("arbitrary", "arbitrary")),
    )(do, hR, hR, hR, states, cos, sin, intra, qd, kd, cd)


def _gn_gate_fwd(o, hR, g, b, *, name):
    S = o.shape[0]
    H, DV = RET_HEADS, RET_V
    bs = _pick(S, 512, 16)

    def body(o_ref, rg_ref, g_ref, b_ref, r_ref):
        z = o_ref[...]
        z = z - jnp.mean(z, axis=-1, keepdims=True)
        xh = z * lax.rsqrt(jnp.mean(z * z, axis=-1, keepdims=True) + GN_EPS)
        rg = rg_ref[...]
        r_ref[...] = ((rg * jax.nn.sigmoid(rg)) * (xh * g_ref[...] + b_ref[...])).astype(BF16)

    row = pl.BlockSpec((bs, DV), lambda i, h: (i, h))
    vec = pl.BlockSpec((1, DV), lambda i, h: (0, h))
    return _pcall(
        body, name=name, grid=(S // bs, H),
        in_specs=[row, pl.BlockSpec((bs, DV), lambda i, h: (i, 3 * H + h)), vec, vec],
        out_specs=row, out_shape=_sds((S, H * DV), BF16),
        compiler_params=_params(("parallel", "parallel")),
    )(o, hR, g.reshape(1, H * DV), b.reshape(1, H * DV))


def _gn_gate_bwd(dr, o, hR, g, b, *, dr_col0, name):
    S = o.shape[0]
    H, DV = RET_HEADS, RET_V
    bs = _pick(S, 512, 16)

    def body(dr_ref, o_ref, rg_ref, g_ref, b_ref, do_ref, drg_ref, dg_ref, db_ref):
        @pl.when(pl.program_id(1) == 0)
        def _():
            dg_ref[...] = jnp.zeros_like(dg_ref)
            db_ref[...] = jnp.zeros_like(db_ref)

        z = o_ref[...]
        z = z - jnp.mean(z, axis=-1, keepdims=True)
        rstd = lax.rsqrt(jnp.mean(z * z, axis=-1, keepdims=True) + GN_EPS)
        xh = z * rstd
        gv = g_ref[...]
        y = xh * gv + b_ref[...]
        rg = rg_ref[...]
        sg = jax.nn.sigmoid(rg)
        drv = dr_ref[...]
        dy = drv * (rg * sg)
        drg_ref[...] = (drv * y * (sg * (1.0 + rg * (1.0 - sg)))).astype(BF16)
        dg_ref[...] += jnp.sum(dy * xh, axis=0, keepdims=True)
        db_ref[...] += jnp.sum(dy, axis=0, keepdims=True)
        dxh = dy * gv
        do_ref[...] = rstd * (dxh - jnp.mean(dxh, axis=-1, keepdims=True)
                              - xh * jnp.mean(dxh * xh, axis=-1, keepdims=True))

    row = pl.BlockSpec((bs, DV), lambda h, i: (i, h))
    vec = pl.BlockSpec((1, DV), lambda h, i: (0, h))
    do, drg, dg, db = _pcall(
        body, name=name, grid=(H, S // bs),
        in_specs=[pl.BlockSpec((bs, DV), lambda h, i: (i, dr_col0 + h)), row,
                  pl.BlockSpec((bs, DV), lambda h, i: (i, 3 * H + h)), vec, vec],
        out_specs=[row, row, vec, vec],
        out_shape=[_sds((S, H * DV), F32), _sds((S, H * DV), BF16), _sds((1, H * DV), F32), _sds((1, H * DV), F32)],
        compiler_params=_params(("arbitrary", "arbitrary")),
    )(dr, o, hR, g.reshape(1, H * DV), b.reshape(1, H * DV))
    return do, drg, dg[0], db[0]


def _swiglu(u, g):
    return g, u, (g * jax.nn.sigmoid(g)) * u


def _swiglu_bwd(da, g, u):
    g, u = g.astype(F32), u.astype(F32)
    sg = jax.nn.sigmoid(g)
    return da * u * (sg * (1.0 + g * (1.0 - sg))), da * (g * sg)


def _loss_head(y, t, *, name):
    S, D = y.shape
    bs = _pick(S, 256, 8)
    inv_d = 1.0 / D

    def body(y_ref, t_ref, dy_ref, l_ref):
        @pl.when(pl.program_id(0) == 0)
        def _():
            l_ref[...] = jnp.zeros_like(l_ref)

        e = y_ref[...] - t_ref[...]
        dy_ref[...] = e * inv_d
        l_ref[...] += 0.5 * jnp.sum(jnp.mean(e * e, axis=-1, keepdims=True), axis=0, keepdims=True)

    row = pl.BlockSpec((bs, D), lambda i: (i, 0))
    dy, l = _pcall(
        body, name=name, grid=(S // bs,), in_specs=[row, row],
        out_specs=[row, pl.BlockSpec((1, 1), lambda i: (0, 0))],
        out_shape=[_sds((S, D), F32), _sds((1, 1), F32)],
        compiler_params=_params(("arbitrary",)),
    )(y, t)
    return dy, l[0, 0]


def _adamw(w, g, m, v, *, name):
    shape = w.shape
    C = shape[-1]
    R = w.size // C
    br = _pick(R, 512, 8)

    def body(w_ref, g_ref, m_ref, v_ref, d_ref, nm_ref, nv_ref):
        gv = g_ref[...]
        mn = ADAM_B1 * m_ref[...] + (1.0 - ADAM_B1) * gv
        vn = ADAM_B2 * v_ref[...] + (1.0 - ADAM_B2) * (gv * gv)
        m_hat = mn / (1.0 - ADAM_B1 ** ADAM_STEP)
        v_hat = vn / (1.0 - ADAM_B2 ** ADAM_STEP)
        d_ref[...] = -ADAM_LR * (m_hat / (jnp.sqrt(v_hat) + ADAM_EPS) + ADAM_WD * w_ref[...])
        nm_ref[...] = mn
        nv_ref[...] = vn

    blk = pl.BlockSpec((br, C), lambda i: (i, 0))
    outs = _pcall(body, name=name, grid=(R // br,), in_specs=[blk] * 4, out_specs=[blk] * 3,
                  out_shape=[_sds((R, C), F32)] * 3,
                  compiler_params=_params(("parallel",)))(*[a.reshape(R, C) for a in (w, g, m, v)])
    return tuple(o.reshape(shape) for o in outs)


def _slot_block(n, k):
    return (_pick(n, 1024, 16) if n % 16 == 0 else n), _pick(k, 1024, 128)


def _sum_slots(x, *, name):
    ns, n, k = x.shape
    br, bc = _slot_block(n, k)

    def body(x_ref, o_ref):
        acc = x_ref[0].astype(F32)
        for s in range(1, ns):
            acc = acc + x_ref[s].astype(F32)
        o_ref[...] = acc

    return _pcall(body, name=name, grid=(n // br, k // bc),
                  in_specs=[pl.BlockSpec((ns, br, bc), lambda i, j: (0, i, j))],
                  out_specs=pl.BlockSpec((br, bc), lambda i, j: (i, j)),
                  out_shape=_sds((n, k), F32), compiler_params=_params(("parallel", "parallel")))(x)


def _pair_sum(g, r, *, name):
    _, n, k = r.shape
    br, bc = _slot_block(n, k)
    core = lax.axis_index("c").astype(jnp.int32).reshape(1)

    def body(c_ref, g_ref, r_ref, o_ref):
        o_ref[...] = (g_ref[...].astype(F32) + r_ref[...].astype(F32)).astype(o_ref.dtype)

    blk = pl.BlockSpec((None, br, bc), lambda s, i, j, c_ref: (s, i, j))
    return _pcall(
        body, name=name, out_shape=_sds(r.shape, g.dtype),
        grid_spec=pltpu.PrefetchScalarGridSpec(
            num_scalar_prefetch=1, grid=(N_CHIP, n // br, k // bc),
            in_specs=[pl.BlockSpec((None, br, bc), lambda s, i, j, c_ref: (2 * s + c_ref[0], i, j)), blk],
            out_specs=blk),
        compiler_params=_params(("parallel", "parallel", "parallel")),
    )(core, g, r)


N_PEER = N_DEV - 1
N_CHIP = N_DEV // 2
HOST_TAIL_FRACTION = 8


def _coords():
    return lax.axis_index("x"), lax.axis_index("y"), lax.axis_index("c")


def _ag_phase(phase, x_refs, out_refs, send_sems, recv_sems, local_sems):
    n = len(x_refs)
    x, y, c = _coords()
    me, sibling = (x, y, c), (x, y, 1 - c)
    chips = [(1 - x, y), (x, 1 - y), (1 - x, 1 - y)]

    def copy(a, k, block, to, src=None):
        px, py, pc = block
        dst = out_refs[a].at[4 * px + 2 * py + pc]
        return pltpu.make_async_remote_copy(
            src_ref=dst if src is None else src, dst_ref=dst,
            send_sem=send_sems.at[a * N_PEER + k], recv_sem=recv_sems.at[a * N_PEER + k],
            device_id=to, device_id_type=pl.DeviceIdType.MESH)

    def local(a):
        return pltpu.make_async_copy(x_refs[a], out_refs[a].at[4 * x + 2 * y + c], local_sems.at[a])

    def first(a):
        return ([copy(a, 0, me, sibling, src=x_refs[a])]
                + [copy(a, 1 + j, me, (*chip, c), src=x_refs[a]) for j, chip in enumerate(chips)])

    def passed(a, j):
        return copy(a, 4 + j, (*chips[j], c), sibling)

    if phase == 0:
        for a in range(n):
            local(a).start()
            for cp in first(a):
                cp.start()
    elif phase == 1:
        for j in range(len(chips)):
            for a in range(n):
                copy(a, 1 + j, (*chips[j], c), me).wait_recv()
                passed(a, j).start()
    else:
        for a in range(n):
            copy(a, 0, sibling, me).wait_recv()
            for j in range(len(chips)):
                copy(a, 4 + j, (*chips[j], 1 - c), me).wait_recv()
        for a in range(n):
            for cp in first(a):
                cp.wait_send()
            for j in range(len(chips)):
                passed(a, j).wait_send()
            local(a).wait()


def _pair_phase(phase, g_refs, out_refs, send_sems, recv_sems, local_sems):
    n = len(g_refs)
    x, y, c = _coords()

    def copy(a, i):
        return pltpu.make_async_remote_copy(
            src_ref=g_refs[a].at[2 * i + (1 - c)], dst_ref=out_refs[a].at[i],
            send_sem=send_sems.at[a * N_PEER + i], recv_sem=recv_sems.at[a * N_PEER + i],
            device_id=(x, y, 1 - c), device_id_type=pl.DeviceIdType.MESH)

    if phase == 0:
        for a in range(n):
            for i in range(N_CHIP):
                copy(a, i).start()
    elif phase == 2:
        for a in range(n):
            for i in range(N_CHIP):
                copy(a, i).wait_recv()
        for a in range(n):
            for i in range(N_CHIP):
                copy(a, i).wait_send()


def _cross_phase(phase, p_refs, out_refs, send_sems, recv_sems, local_sems):
    n = len(p_refs)
    x, y, c = _coords()
    mine = 2 * x + y

    def local(a):
        return pltpu.make_async_copy(p_refs[a].at[mine], out_refs[a].at[mine], local_sems.at[a])

    def pair(a, k):
        px = 1 - x if k & 2 else x
        py = 1 - y if k & 1 else y
        peer = 2 * px + py
        sem = a * N_PEER + k - 1
        send = pltpu.make_async_remote_copy(
            src_ref=p_refs[a].at[peer], dst_ref=out_refs[a].at[mine], send_sem=send_sems.at[sem],
            recv_sem=recv_sems.at[sem], device_id=(px, py, c), device_id_type=pl.DeviceIdType.MESH)
        recv = pltpu.make_async_remote_copy(
            src_ref=p_refs[a].at[peer], dst_ref=out_refs[a].at[peer], send_sem=send_sems.at[sem],
            recv_sem=recv_sems.at[sem], device_id=(px, py, c), device_id_type=pl.DeviceIdType.MESH)
        return send, recv

    if phase == 0:
        for a in range(n):
            local(a).start()
        for k in range(1, N_CHIP):
            for a in range(n):
                pair(a, k)[0].start()
    elif phase == 2:
        for k in range(1, N_CHIP):
            for a in range(n):
                pair(a, k)[1].wait_recv()
        for k in range(1, N_CHIP):
            for a in range(n):
                pair(a, k)[0].wait_send()
        for a in range(n):
            local(a).wait()


_PHASES = {"ag": _ag_phase, "pair": _pair_phase, "cross": _cross_phase}


def _job(kind, arrays):
    arrays = list(arrays)
    if kind == "ag":
        shapes = [(N_DEV,) + a.shape for a in arrays]
    elif kind == "pair":
        shapes = [(N_CHIP,) + a.shape[1:] for a in arrays]
    else:
        shapes = [a.shape for a in arrays]
    return dict(parts=[(kind, len(arrays))], ins=arrays, outs=[_sds(s, a.dtype) for s, a in zip(shapes, arrays)])


def _join(jobs):
    jobs = [j for j in jobs if j is not None]
    if not jobs:
        return None
    return dict(parts=[p for j in jobs for p in j["parts"]], ins=[a for j in jobs for a in j["ins"]],
                outs=[o for j in jobs for o in j["outs"]])


def _hosted_call(body, job, *, name, grid, in_specs, out_specs, out_shape, scratch_shapes, semantics, args):
    if job is None:
        outs = _pcall(body, name=name, grid=grid, in_specs=in_specs, out_specs=out_specs, out_shape=out_shape,
                      scratch_shapes=scratch_shapes, compiler_params=_params(semantics))(*args)
        return outs, []
    n_in, n_out, n_scr, nj = len(in_specs), len(out_specs), len(scratch_shapes), len(job["ins"])
    parts = job["parts"]
    total = 1
    for g in grid:
        total *= g
    late = total - max(total // HOST_TAIL_FRACTION, 1) if total > 1 else 0

    def wrapped(*refs):
        ins, refs = refs[:n_in], refs[n_in:]
        jins, refs = refs[:nj], refs[nj:]
        outs, refs = refs[:n_out], refs[n_out:]
        jouts, refs = refs[:nj], refs[nj:]
        scr, sems = refs[:n_scr], refs[n_scr:]
        step = 0
        for d, g in enumerate(grid):
            step = step * g + pl.program_id(d)

        def run(phase):
            off = 0
            for i, (kind, n) in enumerate(parts):
                _PHASES[kind](phase, jins[off:off + n], jouts[off:off + n], *sems[3 * i:3 * i + 3])
                off += n

        pl.when(step == 0)(lambda: run(0))
        pl.when(step == late)(lambda: run(1))
        body(*ins, *outs, *scr)
        pl.when(step == total - 1)(lambda: run(2))

    any_spec = pl.BlockSpec(memory_space=pl.ANY)
    sems = []
    for kind, n in parts:
        sems += [pltpu.SemaphoreType.DMA((n * N_PEER,)), pltpu.SemaphoreType.DMA((n * N_PEER,)),
                 pltpu.SemaphoreType.DMA((n,))]
    outs = _pcall(
        wrapped, name=name, grid=grid, in_specs=list(in_specs) + [any_spec] * nj,
        out_specs=list(out_specs) + [any_spec] * nj, out_shape=list(out_shape) + job["outs"],
        scratch_shapes=list(scratch_shapes) + sems,
        compiler_params=_params(("arbitrary",) * len(grid)),
    )(*args, *job["ins"])
    return outs[:n_out], outs[n_out:]


def _exchange(job, *, name):
    outs, jouts = _hosted_call(lambda: None, job, name=name, grid=(1,), in_specs=[], out_specs=[], out_shape=[],
                               scratch_shapes=[], semantics=("arbitrary",), args=())
    return jouts


_TRANSPOSED = {"w_in": True, "w_out": False, "w_gate": True, "w_up": True, "w_down": False, "w_uq": True, "w_ukv": True}
_BIG = tuple(_TRANSPOSED)
_SMALL = ("ln_in_g", "ln_in_b", "q_norm_g", "kv_norm_g", "ret_gn_g", "ret_gn_b", "ln1_g", "ln1_b", "ln2_g", "ln2_b")
SMALL_COLS = 128


def _rope_tables(pos, dim):
    inv_freq = ROPE_THETA ** (-jnp.arange(0, dim, 2, dtype=F32) / dim)
    ang = pos.astype(F32)[:, None] * inv_freq
    return jnp.cos(ang), jnp.sin(ang)


def _split_in(wt_in):
    a, b, c = MLA_Q_LORA, MLA_Q_LORA + MLA_KV_LORA, MLA_Q_LORA + MLA_KV_LORA + MLA_ROPE
    return wt_in[:a], wt_in[a:b], wt_in[b:c], wt_in[c:]


def _pad_heads(wt_uq):
    H = MLA_HEADS
    w3 = wt_uq.reshape(H, MLA_QK, wt_uq.shape[1])
    return jnp.pad(w3, ((0, 0), (0, MLA_PAD - MLA_QK), (0, 0)))


class _Plan:
    FIRST = [("ag", ((0, "w_in"),))]
    LAST = [("cross", ((0, "w_in"),))]
    HOSTS = {
        "l0_cq": [("ag", ((0, "w_uq"), (0, "w_ukv")))], "l0_hR": [("ag", ((0, "w_out"),))],
        "l0_attn": [("ag", ((0, "w_gate"),))], "l0_gate": [("ag", ((0, "w_up"),))],
        "l0_up": [("ag", ((0, "w_down"),))], "l0_down": [("ag", ((1, "w_in"), (1, "w_uq"), (1, "w_ukv")))],
        "l1_hR": [("ag", ((1, "w_out"),))], "l1_attn": [("ag", ((1, "w_gate"),))],
        "l1_gate": [("ag", ((1, "w_up"),))], "l1_up": [("ag", ((1, "w_down"),))],
        "l0_b_dact": [("cross", ((1, "w_in"),))]}
    for _l in (0, 1):
        HOSTS.update({
            "l%d_b_wgate" % _l: [("pair", ((_l, "w_down"),))],
            "l%d_b_wup" % _l: [("pair", ((_l, "w_gate"),)), ("cross", ((_l, "w_down"),))],
            "l%d_b_dx1a" % _l: [("pair", ((_l, "w_up"),)), ("cross", ((_l, "w_gate"),))],
            "l%d_b_dx1b" % _l: [("cross", ((_l, "w_up"),))],
            "l%d_b_attn" % _l: [("pair", ((_l, "w_out"),))],
            "l%d_b_win" % _l: [("pair", ((_l, "w_uq"), (_l, "w_ukv"))), ("cross", ((_l, "w_out"),))],
            "l%d_b_dx" % _l: [("pair", ((_l, "w_in"),)), ("cross", ((_l, "w_uq"), (_l, "w_ukv")))]})

    def __init__(self, local):
        self.local = local
        self.full = {}
        self.grads = {}
        self.paired = {}
        self.recv = {}

    def _by_device(self, k):
        return self.grads[k].reshape((N_DEV,) + self.local[k].shape)

    def _make(self, kind, keys):
        if kind == "ag":
            return _job(kind, [self.local[k] for k in keys])
        if kind == "pair":
            return _job(kind, [self._by_device(k) for k in keys])
        return _job(kind, [_pair_sum(self._by_device(k), self.paired[k], name="pairsum_l%d_%s" % k) for k in keys])

    def _done(self, kind, keys, outs):
        for k, o in zip(keys, outs):
            if kind == "ag":
                self.full[k] = o.reshape(N_DEV * o.shape[1], o.shape[2])
            elif kind == "pair":
                self.paired[k] = o
            else:
                self.recv[k] = o

    def _run(self, todo, call):
        outs = call(_join([self._make(kind, keys) for kind, keys in todo]))
        for kind, keys in todo:
            self._done(kind, keys, outs[:len(keys)])
            outs = outs[len(keys):]

    def gather_first(self):
        self._run(self.FIRST, lambda job: _exchange(job, name="ag_first"))

    def send_last(self):
        self._run(self.LAST, lambda job: _exchange(job, name="rs_last"))

    def call(self, fn, name, *args, **kw):
        if name not in self.HOSTS:
            return fn(*args, name=name, **kw)
        res = []

        def run(job):
            *outs, jouts = fn(*args, name=name, job=job, **kw)
            res.extend(outs)
            return jouts

        self._run(self.HOSTS[name], run)
        return res[0] if len(res) == 1 else tuple(res)


def _layer_fwd(x, xb, plan, p, tabs, l):
    S, D = x.shape
    H = MLA_HEADS
    nm = lambda s: "l%d_%s" % (l, s)
    w = lambda n: plan.full[(l, n)]
    mm = lambda name, *a, **kw: plan.call(_mm, nm(name), *a, **kw)
    wt_q, wt_kv, wt_kr, wt_r = _split_in(w("w_in"))
    cq = mm("cq", xb, wt_q, tb=True)
    ckv = mm("ckv", xb, wt_kv, tb=True)
    krp = mm("krope", xb, jnp.pad(wt_kr, ((0, LANES - MLA_ROPE), (0, 0))), tb=True)
    hR = mm("hR", xb, wt_r, tb=True)
    _, qn, qn_hat, q_rstd = _norm_fwd(cq, p["q_norm_g"], None, center=False, eps=RMS_EPS, want_f32=False,
                                      name=nm("qnorm"))
    _, kvn, kvn_hat, kv_rstd = _norm_fwd(ckv, p["kv_norm_g"], None, center=False, eps=RMS_EPS, want_f32=False,
                                         name=nm("kvnorm"))
    w_uq3 = _pad_heads(w("w_uq"))
    w_ukv3 = w("w_ukv").reshape(H, MLA_NOPE + MLA_V, -1)
    q3 = _mm_heads_out(qn, w_uq3, name=nm("uq"))
    kv3 = _mm_heads_out(kvn, w_ukv3, name=nm("ukv"))
    qh, kh, vh = _mla_prep(q3, kv3, krp, tabs["c128"], tabs["s128"], name=nm("rope"))
    a, lse = plan.call(_attn_fwd, nm("attn"), qh, kh, vh)
    o_ret, states = _ret_fwd(hR, tabs["cos_r"], tabs["sin_r"], tabs["ret"], name=nm("ret"))
    r = _gn_gate_fwd(o_ret, hR, p["ret_gn_g"], p["ret_gn_b"], name=nm("gn"))
    mix_in = jnp.concatenate([a.astype(BF16), r], axis=1)
    mix = mm("out", mix_in, w("w_out"))
    x1, x1b, x1_hat, rstd1 = _norm_fwd(x, p["ln1_g"], p["ln1_b"], res=mix, alpha=p["alpha"], eps=LN_EPS, name=nm("ln1"))
    g = mm("gate", x1b, w("w_gate"), tb=True)
    gb, ub, act = mm("up", x1b, w("w_up"), tb=True, extras=(g,), epilogue=_swiglu, out_dtypes=(BF16, BF16, BF16))
    f = mm("down", act, w("w_down"))
    x2, x2b, x2_hat, rstd2 = _norm_fwd(x1, p["ln2_g"], p["ln2_b"], res=f, alpha=p["alpha"], eps=LN_EPS, name=nm("ln2"))
    saved = dict(xb=xb, qn=qn, qn_hat=qn_hat, q_rstd=q_rstd, kvn=kvn, kvn_hat=kvn_hat, kv_rstd=kv_rstd,
                 qh=qh, kh=kh, vh=vh, a=a, lse=lse, hR=hR, o_ret=o_ret, states=states, mix_in=mix_in,
                 x1b=x1b, x1_hat=x1_hat, rstd1=rstd1, gb=gb, ub=ub, act=act, x2_hat=x2_hat, rstd2=rstd2)
    return x2, x2b, saved


def _layer_bwd(dx2, sv, plan, p, tabs, l):
    S, D = dx2.shape
    H = MLA_HEADS
    nm = lambda s: "l%d_b_%s" % (l, s)
    w = lambda n: plan.full[(l, n)]
    mm = lambda name, *a, **kw: plan.call(_mm, nm(name), *a, **kw)
    alpha = p["alpha"]
    gw, gp = plan.grads, {}
    dz2, dz2b, gp["ln2_g"], gp["ln2_b"] = _norm_bwd(dx2, sv["x2_hat"], sv["rstd2"], p["ln2_g"], name=nm("ln2"))
    dg, du = mm("dact", dz2b, w("w_down"), tb=True, extras=(sv["gb"], sv["ub"]), epilogue=_swiglu_bwd,
                out_dtypes=(BF16, BF16))
    gw[(l, "w_down")] = mm("wdown", sv["act"], dz2b, ta=True, out_dtype=BF16)
    gw[(l, "w_gate")] = mm("wgate", dg, sv["x1b"], ta=True, out_dtype=BF16)
    gw[(l, "w_up")] = mm("wup", du, sv["x1b"], ta=True, out_dtype=BF16)
    t = mm("dx1a", dg, w("w_gate"), add=dz2, add_scale=alpha)
    dx1 = mm("dx1b", du, w("w_up"), add=t)
    dz1, dz1b, gp["ln1_g"], gp["ln1_b"] = _norm_bwd(dx1, sv["x1_hat"], sv["rstd1"], p["ln1_g"], name=nm("ln1"))
    dmix = mm("dmix", dz1b, w("w_out"), tb=True)
    gw[(l, "w_out")] = mm("wout", sv["mix_in"], dz1b, ta=True, out_dtype=BF16)
    ret_col0 = (H * MLA_V) // RET_V
    do_ret, drg, gp["ret_gn_g"], gp["ret_gn_b"] = _gn_gate_bwd(
        dmix, sv["o_ret"], sv["hR"], p["ret_gn_g"], p["ret_gn_b"], dr_col0=ret_col0, name=nm("gn"))
    drq, drk, drv = _ret_bwd(do_ret, sv["hR"], sv["states"], tabs["cos_r"], tabs["sin_r"], tabs["ret"], name=nm("ret"))
    dqh, dkh, dvh = plan.call(_attn_bwd, nm("attn"), sv["qh"], sv["kh"], sv["vh"], sv["a"], dmix, sv["lse"], do_col0=0)
    dq3, dkv3, dkrp = _mla_unprep(dqh, dkh, dvh, tabs["c128"], tabs["s128"], name=nm("rope"))
    w_uq3 = _pad_heads(w("w_uq"))
    w_ukv3 = w("w_ukv").reshape(H, MLA_NOPE + MLA_V, -1)
    g_uq = _mm_heads_tn(dq3, sv["qn"], name=nm("wuq")).reshape(H, MLA_PAD, -1)
    gw[(l, "w_uq")] = g_uq[:, :MLA_QK].reshape(H * MLA_QK, -1)
    dqn = _mm_heads_sum(dq3, w_uq3, name=nm("dqn"))
    gw[(l, "w_ukv")] = _mm_heads_tn(dkv3, sv["kvn"], name=nm("wukv"))
    dkvn = _mm_heads_sum(dkv3, w_ukv3, name=nm("dkvn"))
    _, dcq, gp["q_norm_g"], _ = _norm_bwd(dqn, sv["qn_hat"], sv["q_rstd"], p["q_norm_g"], center=False,
                                          want_f32=False, name=nm("qnorm"))
    _, dckv, gp["kv_norm_g"], _ = _norm_bwd(dkvn, sv["kvn_hat"], sv["kv_rstd"], p["kv_norm_g"], center=False,
                                            want_f32=False, name=nm("kvnorm"))
    dh = jnp.concatenate([dcq, dckv, dkrp[:, :MLA_ROPE].astype(BF16), drq, drk, drv, drg], axis=1)
    gw[(l, "w_in")] = mm("win", dh, sv["xb"], ta=True, out_dtype=BF16)
    dx = mm("dx", dh, w("w_in"), add=dz1, add_scale=alpha)
    return dx, gp


def _local_step(x, target, pos, small, plan, depth):
    alpha = (2 * depth) ** 0.25
    cos_m, sin_m = _rope_tables(pos, MLA_ROPE)
    cos_r, sin_r = _rope_tables(pos, RET_QK)
    zeros = jnp.zeros((x.shape[0], LANES - MLA_ROPE), F32)
    tabs = dict(c128=jnp.concatenate([cos_m, cos_m, zeros], axis=1), s128=jnp.concatenate([-sin_m, sin_m, zeros], axis=1),
                cos_r=cos_r, sin_r=sin_r, ret=_ret_tables(_pick(x.shape[0], RET_BLOCK, CHUNK)))
    h, hb, h_hat, h_rstd = _norm_fwd(x, small["ln_in_g"], small["ln_in_b"], eps=LN_EPS, name="ln_in")
    saved, ps = [], []
    for l in range(depth):
        p = {k: small[k][l] for k in _SMALL[2:]}
        p["alpha"] = alpha
        h, hb, sv = _layer_fwd(h, hb, plan, p, tabs, l)
        saved.append(sv)
        ps.append(p)
    dy, loss = _loss_head(h, target, name="loss")
    gps = [None] * depth
    for l in reversed(range(depth)):
        dy, gps[l] = _layer_bwd(dy, saved[l], plan, ps[l], tabs, l)
    grad_x, _, g_in_g, g_in_b = _norm_bwd(dy, h_hat, h_rstd, small["ln_in_g"], want_bf16=False, name="b_ln_in")
    gsmall = {"ln_in_g": g_in_g, "ln_in_b": g_in_b}
    for k in _SMALL[2:]:
        gsmall[k] = jnp.stack([gps[l][k] for l in range(depth)])
    return loss, grad_x, gsmall


def kernel(x, positions, ln_in_g, ln_in_b, w_in, q_norm_g, kv_norm_g, w_uq, w_ukv, ret_gn_g, ret_gn_b, w_out, ln1_g, ln1_b, w_gate, w_up, w_down, ln2_g, ln2_b, loss_target, m_ln_in_g, m_ln_in_b, m_w_in, m_q_norm_g, m_kv_norm_g, m_w_uq, m_w_ukv, m_ret_gn_g, m_ret_gn_b, m_w_out, m_ln1_g, m_ln1_b, m_w_gate, m_w_up, m_w_down, m_ln2_g, m_ln2_b, v_ln_in_g, v_ln_in_b, v_w_in, v_q_norm_g, v_kv_norm_g, v_w_uq, v_w_ukv, v_ret_gn_g, v_ret_gn_b, v_w_out, v_ln1_g, v_ln1_b, v_w_gate, v_w_up, v_w_down, v_ln2_g, v_ln2_b):
    names = ["ln_in_g", "ln_in_b", "w_in", "q_norm_g", "kv_norm_g", "w_uq", "w_ukv", "ret_gn_g", "ret_gn_b", "w_out",
             "ln1_g", "ln1_b", "w_gate", "w_up", "w_down", "ln2_g", "ln2_b"]
    wv = dict(zip(names, (ln_in_g, ln_in_b, w_in, q_norm_g, kv_norm_g, w_uq, w_ukv, ret_gn_g, ret_gn_b, w_out,
                          ln1_g, ln1_b, w_gate, w_up, w_down, ln2_g, ln2_b)))
    mv = dict(zip(names, (m_ln_in_g, m_ln_in_b, m_w_in, m_q_norm_g, m_kv_norm_g, m_w_uq, m_w_ukv, m_ret_gn_g,
                          m_ret_gn_b, m_w_out, m_ln1_g, m_ln1_b, m_w_gate, m_w_up, m_w_down, m_ln2_g, m_ln2_b)))
    vv = dict(zip(names, (v_ln_in_g, v_ln_in_b, v_w_in, v_q_norm_g, v_kv_norm_g, v_w_uq, v_w_ukv, v_ret_gn_g,
                          v_ret_gn_b, v_w_out, v_ln1_g, v_ln1_b, v_w_gate, v_w_up, v_w_down, v_ln2_g, v_ln2_b)))
    depth = w_in.shape[0]
    assert depth == 2, "the exchange plan is written for two layers"

    keys = [(l, n) for l in range(depth) for n in _BIG]
    plan = _Plan({(l, n): (wv[n][l].T if _TRANSPOSED[n] else wv[n][l]).astype(BF16) for l, n in keys})
    plan.gather_first()

    small = {n: wv[n] for n in _SMALL}
    loss, grad_x, gsmall = _local_step(x[0], loss_target[0], positions[0], small, plan, depth)
    loss = lax.psum(loss, MESH_AXES)
    plan.send_last()

    gshard = {n: [None] * depth for n in _BIG}
    for l, n in keys:
        tot = _sum_slots(plan.recv[(l, n)], name="sum_l%d_%s" % (l, n))
        gshard[n][l] = tot.T if _TRANSPOSED[n] else tot
    grads = {n: jnp.stack(v) for n, v in gshard.items()}

    flat = jnp.concatenate([gsmall[n].reshape(-1) for n in _SMALL])
    n_small = flat.shape[0]
    rows = -(-n_small // (SMALL_COLS * 8)) * 8
    flat = jnp.pad(flat, (0, rows * SMALL_COLS - n_small)).reshape(rows, SMALL_COLS)
    tot = _sum_slots(_exchange(_job("ag", [flat]), name="ag_small")[0], name="sum_small").reshape(-1)
    off = 0
    for n in _SMALL:
        grads[n] = tot[off:off + wv[n].size].reshape(wv[n].shape)
        off += wv[n].size

    delta, new_m, new_v = {}, {}, {}
    for n in names:
        w2 = wv[n] if wv[n].ndim > 1 else wv[n].reshape(1, -1)
        d, nm_, nv_ = _adamw(w2, grads[n].reshape(w2.shape), mv[n].reshape(w2.shape), vv[n].reshape(w2.shape),
                             name="adamw_" + n)
        delta[n], new_m[n], new_v[n] = d.reshape(wv[n].shape), nm_.reshape(wv[n].shape), nv_.reshape(wv[n].shape)

    return (loss, grad_x[None], *[grads[n] for n in names], *[delta[n] for n in names],
            *[new_m[n] for n in names], *[new_v[n] for n in names])
```

```python
import jax
import jax.numpy as jnp
from jax import lax
from jax.experimental import pallas as pl
from jax.experimental.pallas import tpu as pltpu

F32 = jnp.float32
BF16 = jnp.bfloat16

CHUNK = 64
CHUNK_SHIFT = 6
MLA_HEADS = 8
MLA_Q_LORA = 512
MLA_KV_LORA = 256
MLA_NOPE = 128
MLA_ROPE = 64
MLA_V = 128
MLA_QK = MLA_NOPE + MLA_ROPE
LANES = 128
MLA_PAD = 2 * LANES
RET_HEADS = 4
RET_QK = 256
RET_V = 256
ROPE_THETA = 10000.0
LN_EPS = 1e-5
RMS_EPS = 1e-6
GN_EPS = 1e-5
ADAM_LR = 0.001
ADAM_B1 = 0.9
ADAM_B2 = 0.999
ADAM_EPS = 1e-08
ADAM_WD = 0.01
ADAM_STEP = 10

LOG2_E = 1.4426950408889634

N_DEV = 8
MESH_AXES = ("x", "y", "c")
VMEM_LIMIT_BYTES = 56 * 1024 * 1024
MM_VMEM_BUDGET = 40 * 1024 * 1024
MM_ACC_PENALTY = 0.85
ATTN_BLOCK = 256
ATTN_KEY_STEP = 512
RET_BLOCK = 256


def _pick(n, pref, mult):
    best = None
    d = mult
    while d <= min(n, pref):
        if n % d == 0:
            best = d
        d += mult
    return n if best is None else best


def _divisors(n, mult, cap):
    ds = [d for d in range(mult, min(n, cap) + 1, mult) if n % d == 0]
    return ds or [n]


def _pcall(body, **kw):
    return pl.pallas_call(body, **kw)


def _params(sem):
    return pltpu.CompilerParams(dimension_semantics=sem, vmem_limit_bytes=VMEM_LIMIT_BYTES)


def _sds(shape, dtype):
    return jax.ShapeDtypeStruct(shape, dtype)


def _mm_tiles(M, N, K, ta, sa, sb, tile_bytes):
    best = None
    for bk in _divisors(K, 128, 8192):
        nk = K // bk
        for bm in _divisors(M, 128 if ta else 16, 1024):
            for bn in _divisors(N, 128, 1024):
                vmem = 2 * (bm * bk * sa + bk * bn * sb) + 2 * bm * bn * tile_bytes + (bm * bn * 4 if nk > 1 else 0)
                if vmem > MM_VMEM_BUDGET:
                    continue
                flops_per_byte = 1.0 / (1.0 / bm + (1.0 / max(N, bn) if nk == 1 else 1.0 / bn))
                score = (flops_per_byte * (1.0 if nk == 1 else MM_ACC_PENALTY), bk, bn, bm)
                if best is None or score > best[0]:
                    best = (score, bm, bn, bk)
    assert best is not None, (M, N, K)
    return best[1:]


def _mm(a, b, *, ta=False, tb=False, add=None, add_scale=1.0, out_dtype=F32, extras=(), epilogue=None,
        out_dtypes=None, job=None, name):
    if ta:
        K, M = a.shape
    else:
        M, K = a.shape
    if tb:
        N, K2 = b.shape
    else:
        K2, N = b.shape
    assert K == K2, (a.shape, b.shape, ta, tb)
    if epilogue is None:
        assert not extras and out_dtypes is None
        out_dtypes = (out_dtype,)
        if add is not None:
            extras, epilogue = (add,), lambda r, c: (r + add_scale * c,)
        else:
            epilogue = lambda r: (r,)
    n_ex, n_out = len(extras), len(out_dtypes)
    tile_bytes = sum(e.dtype.itemsize for e in extras) + sum(jnp.dtype(d).itemsize for d in out_dtypes)
    bm, bn, bk = _mm_tiles(M, N, K, ta, a.dtype.itemsize, b.dtype.itemsize, tile_bytes)
    nk = K // bk
    a_spec = (pl.BlockSpec((bk, bm), lambda i, j, k: (k, i)) if ta
              else pl.BlockSpec((bm, bk), lambda i, j, k: (i, k)))
    b_spec = (pl.BlockSpec((bn, bk), lambda i, j, k: (j, k)) if tb
              else pl.BlockSpec((bk, bn), lambda i, j, k: (k, j)))
    o_spec = pl.BlockSpec((bm, bn), lambda i, j, k: (i, j))
    dims = (((0 if ta else 1,), (1 if tb else 0,)), ((), ()))

    def body(*refs):
        a_ref, b_ref = refs[0], refs[1]
        ex_refs, o_refs = refs[2:2 + n_ex], refs[2 + n_ex:2 + n_ex + n_out]
        part = lax.dot_general(a_ref[...].astype(BF16), b_ref[...].astype(BF16), dims, preferred_element_type=F32)

        def finish(r):
            for o_ref, v in zip(o_refs, epilogue(r, *[e[...] for e in ex_refs])):
                o_ref[...] = v.astype(o_ref.dtype)

        if nk == 1:
            finish(part)
        else:
            acc_ref = refs[-1]
            k = pl.program_id(2)

            @pl.when(k == 0)
            def _():
                acc_ref[...] = part

            @pl.when(k > 0)
            def _():
                acc_ref[...] += part

            @pl.when(k == nk - 1)
            def _():
                finish(acc_ref[...])

    outs, jouts = _hosted_call(
        body, job, name=name, grid=(M // bm, N // bn, nk), in_specs=[a_spec, b_spec] + [o_spec] * n_ex,
        out_specs=[o_spec] * n_out, out_shape=[_sds((M, N), d) for d in out_dtypes],
        scratch_shapes=[pltpu.VMEM((bm, bn), F32)] if nk > 1 else [],
        semantics=("parallel", "parallel", "arbitrary"), args=(a, b, *extras))
    res = list(outs) + ([jouts] if job is not None else [])
    return res[0] if len(res) == 1 else tuple(res)


def _norm_fwd(x, g, b, *, res=None, alpha=1.0, center=True, eps, want_f32=True, want_bf16=True, name):
    S, W = x.shape
    bs = _pick(S, 256, 16)
    has_res, has_b = res is not None, b is not None

    def body(*refs):
        it = iter(refs)
        x_ref = next(it)
        res_ref = next(it) if has_res else None
        g_ref = next(it)
        b_ref = next(it) if has_b else None
        y_ref = next(it) if want_f32 else None
        yb_ref = next(it) if want_bf16 else None
        xh_ref, r_ref = next(it), next(it)
        z = x_ref[...]
        if has_res:
            z = alpha * z + res_ref[...]
        if center:
            z = z - jnp.mean(z, axis=-1, keepdims=True)
        rstd = lax.rsqrt(jnp.mean(z * z, axis=-1, keepdims=True) + eps)
        xh = z * rstd
        y = xh * g_ref[...]
        if has_b:
            y = y + b_ref[...]
        if want_f32:
            y_ref[...] = y
        if want_bf16:
            yb_ref[...] = y.astype(BF16)
        xh_ref[...] = xh
        r_ref[...] = rstd

    row = pl.BlockSpec((bs, W), lambda i: (i, 0))
    vec = pl.BlockSpec((1, W), lambda i: (0, 0))
    in_specs = [row] + ([row] if has_res else []) + [vec] + ([vec] if has_b else [])
    args = (x,) + ((res,) if has_res else ()) + (g.reshape(1, W),) + ((b.reshape(1, W),) if has_b else ())
    out_specs = ([row] if want_f32 else []) + ([row] if want_bf16 else []) + [row, pl.BlockSpec((bs, 1), lambda i: (i, 0))]
    out_shape = (([_sds((S, W), F32)] if want_f32 else []) + ([_sds((S, W), BF16)] if want_bf16 else [])
                 + [_sds((S, W), F32), _sds((S, 1), F32)])
    outs = list(_pcall(body, name=name, grid=(S // bs,), in_specs=in_specs, out_specs=out_specs, out_shape=out_shape,
                       compiler_params=_params(("parallel",)))(*args))
    y = outs.pop(0) if want_f32 else None
    yb = outs.pop(0) if want_bf16 else None
    return y, yb, outs[0], outs[1]


def _norm_bwd(dy, xh, rstd, g, *, center=True, want_f32=True, want_bf16=True, name):
    S, W = dy.shape
    bs = _pick(S, 256, 16)

    def body(*refs):
        dy_ref, xh_ref, r_ref, g_ref = refs[:4]
        it = iter(refs[4:])
        dz_ref = next(it) if want_f32 else None
        dzb_ref = next(it) if want_bf16 else None
        dg_ref, db_ref = next(it), next(it)

        @pl.when(pl.program_id(0) == 0)
        def _():
            dg_ref[...] = jnp.zeros_like(dg_ref)
            db_ref[...] = jnp.zeros_like(db_ref)

        dyv = dy_ref[...]
        xhv = xh_ref[...]
        dyg = dyv * g_ref[...]
        m2 = jnp.mean(dyg * xhv, axis=-1, keepdims=True)
        t = dyg - xhv * m2
        if center:
            t = t - jnp.mean(dyg, axis=-1, keepdims=True)
        dz = r_ref[...] * t
        if want_f32:
            dz_ref[...] = dz
        if want_bf16:
            dzb_ref[...] = dz.astype(BF16)
        dg_ref[...] += jnp.sum(dyv * xhv, axis=0, keepdims=True)
        db_ref[...] += jnp.sum(dyv, axis=0, keepdims=True)

    row = pl.BlockSpec((bs, W), lambda i: (i, 0))
    vec = pl.BlockSpec((1, W), lambda i: (0, 0))
    out_specs = ([row] if want_f32 else []) + ([row] if want_bf16 else []) + [vec, vec]
    out_shape = (([_sds((S, W), F32)] if want_f32 else []) + ([_sds((S, W), BF16)] if want_bf16 else [])
                 + [_sds((1, W), F32), _sds((1, W), F32)])
    outs = list(_pcall(body, name=name, grid=(S // bs,),
                       in_specs=[row, row, pl.BlockSpec((bs, 1), lambda i: (i, 0)), vec],
                       out_specs=out_specs, out_shape=out_shape,
                       compiler_params=_params(("arbitrary",)))(dy, xh, rstd, g.reshape(1, W)))
    dz = outs.pop(0) if want_f32 else None
    dzb = outs.pop(0) if want_bf16 else None
    return dz, dzb, outs[0][0], outs[1][0]


def _rot_group(t, c, s):
    half = MLA_ROPE // 2
    lane = lax.broadcasted_iota(jnp.int32, t.shape, 1)
    swapped = jnp.where(lane < half, pltpu.roll(t, LANES - half, 1), pltpu.roll(t, half, 1))
    return t * c + swapped * s


def _mla_prep(q3, kv3, krp, c128, s128, *, name):
    H, S, DP = q3.shape
    bs = _pick(S, 2048, 16)

    def body(q_ref, kv_ref, kr_ref, c_ref, s_ref, qh_ref, kh_ref, vh_ref):
        c, s = c_ref[...], s_ref[...]
        q, kv = q_ref[...], kv_ref[...]
        qh_ref[...] = jnp.concatenate([q[:, :LANES], _rot_group(q[:, LANES:], c, s)], axis=1).astype(BF16)
        kh_ref[...] = jnp.concatenate([kv[:, :LANES], _rot_group(kr_ref[...], c, s)], axis=1).astype(BF16)
        vh_ref[...] = kv[:, LANES:].astype(BF16)

    head = pl.BlockSpec((None, bs, DP), lambda h, i: (h, i, 0))
    tab = pl.BlockSpec((bs, LANES), lambda h, i: (i, 0))
    return _pcall(
        body, name=name, grid=(H, S // bs), in_specs=[head, head, tab, tab, tab],
        out_specs=[head, head, pl.BlockSpec((None, bs, LANES), lambda h, i: (h, i, 0))],
        out_shape=[_sds((H, S, DP), BF16), _sds((H, S, DP), BF16), _sds((H, S, LANES), BF16)],
        compiler_params=_params(("parallel", "parallel")),
    )(q3, kv3, krp, c128, s128)


def _mla_unprep(dqh, dkh, dvh, c128, s128, *, name):
    H, S, DP = dqh.shape
    bs = _pick(S, 2048, 16)

    def body(dq_ref, dk_ref, dv_ref, c_ref, s_ref, oq_ref, okv_ref, okr_ref):
        c, ns = c_ref[...], -s_ref[...]
        dq, dk = dq_ref[...], dk_ref[...]
        oq_ref[...] = jnp.concatenate([dq[:, :LANES], _rot_group(dq[:, LANES:], c, ns)], axis=1).astype(BF16)
        okv_ref[...] = jnp.concatenate([dk[:, :LANES], dv_ref[...]], axis=1).astype(BF16)
        dkr = _rot_group(dk[:, LANES:], c, ns)

        @pl.when(pl.program_id(1) == 0)
        def _():
            okr_ref[...] = dkr

        @pl.when(pl.program_id(1) > 0)
        def _():
            okr_ref[...] += dkr

    head = pl.BlockSpec((None, bs, DP), lambda i, h: (h, i, 0))
    tab = pl.BlockSpec((bs, LANES), lambda i, h: (i, 0))
    return _pcall(
        body, name=name, grid=(S // bs, H),
        in_specs=[head, head, pl.BlockSpec((None, bs, LANES), lambda i, h: (h, i, 0)), tab, tab],
        out_specs=[head, head, tab],
        out_shape=[_sds((H, S, DP), BF16), _sds((H, S, DP), BF16), _sds((S, LANES), F32)],
        compiler_params=_params(("parallel", "arbitrary")),
    )(dqh, dkh, dvh, c128, s128)


def _mm_call(a, b, *, grid, a_spec, b_spec, o_spec, dims, nk, out_shape, name):
    def body(*refs):
        a_ref, b_ref, o_ref = refs[:3]
        part = lax.dot_general(a_ref[...].astype(BF16), b_ref[...].astype(BF16), dims, preferred_element_type=F32)
        if nk == 1:
            o_ref[...] = part.astype(o_ref.dtype)
        else:
            acc_ref = refs[3]
            k = pl.program_id(2)

            @pl.when(k == 0)
            def _():
                acc_ref[...] = part

            @pl.when(k > 0)
            def _():
                acc_ref[...] += part

            @pl.when(k == nk - 1)
            def _():
                o_ref[...] = acc_ref[...].astype(o_ref.dtype)

    return _pcall(body, name=name, grid=grid, in_specs=[a_spec, b_spec], out_specs=o_spec, out_shape=out_shape,
                  scratch_shapes=[pltpu.VMEM(o_spec.block_shape, F32)] if nk > 1 else [],
                  compiler_params=_params(("parallel", "parallel", "arbitrary")))(a, b)


def _mm_heads_out(a, w3, *, name):
    S, K = a.shape
    H, n, _ = w3.shape
    bm = _pick(S, 4096, 16)
    out = _mm_call(
        a, w3.reshape(H * n, K), grid=(S // bm, H, 1), nk=1, dims=(((1,), (1,)), ((), ())),
        a_spec=pl.BlockSpec((bm, K), lambda i, j, k: (i, 0)), b_spec=pl.BlockSpec((n, K), lambda i, j, k: (j, 0)),
        o_spec=pl.BlockSpec((bm, n), lambda i, j, k: (j * (S // bm) + i, 0)), out_shape=_sds((H * S, n), F32), name=name)
    return out.reshape(H, S, n)


def _mm_heads_sum(a3, w3, *, name):
    H, S, n = a3.shape
    N = w3.shape[2]
    bm, bn = _pick(S, 2048, 16), _pick(N, 512, 128)
    return _mm_call(
        a3.reshape(H * S, n), w3.reshape(H * n, N), grid=(S // bm, N // bn, H), nk=H, dims=(((1,), (0,)), ((), ())),
        a_spec=pl.BlockSpec((bm, n), lambda i, j, k: (k * (S // bm) + i, 0)),
        b_spec=pl.BlockSpec((n, bn), lambda i, j, k: (k, j)),
        o_spec=pl.BlockSpec((bm, bn), lambda i, j, k: (i, j)), out_shape=_sds((S, N), F32), name=name)


def _mm_heads_tn(a3, b, *, name):
    H, S, n = a3.shape
    N = b.shape[1]
    bk, bn = _pick(S, 4096, 128), _pick(N, 512, 128)
    nk = S // bk
    return _mm_call(
        a3.reshape(H * S, n), b, grid=(H, N // bn, nk), nk=nk, dims=(((0,), (0,)), ((), ())),
        a_spec=pl.BlockSpec((bk, n), lambda i, j, k: (i * nk + k, 0)),
        b_spec=pl.BlockSpec((bk, bn), lambda i, j, k: (k, j)),
        o_spec=pl.BlockSpec((n, bn), lambda i, j, k: (i, j)), out_shape=_sds((H * n, N), BF16), name=name)


def _chunk_mask(row0, col0, B, G):
    rows = lax.shift_right_logical(row0 + lax.broadcasted_iota(jnp.int32, (B, G), 0), CHUNK_SHIFT)
    cols = lax.shift_right_logical(col0 + lax.broadcasted_iota(jnp.int32, (B, G), 1), CHUNK_SHIFT)
    return rows >= cols


def _mask_tail(x, qi, B, G, fill):
    L = x.shape[1]
    tail = jnp.where(_chunk_mask(qi * B, L - G, B, G), x[:, L - G:], fill)
    return tail if L == G else jnp.concatenate([x[:, :L - G], tail], axis=1)


def _for_key_prefix(qi, B, G, S, fn):
    per = G // B
    for b in range(S // G):
        pl.when(qi // per == b)(lambda b=b: fn((b + 1) * G))


def _nt(a, b):
    return lax.dot_general(a, b, (((1,), (1,)), ((), ())), preferred_element_type=F32)


def _nn(a, b):
    return lax.dot_general(a, b, (((1,), (0,)), ((), ())), preferred_element_type=F32)


def _tn(a, b):
    return lax.dot_general(a, b, (((0,), (0,)), ((), ())), preferred_element_type=F32)


def _attn_fwd(qh, kh, vh, *, job=None, name):
    H, S, DQ = qh.shape
    DV = vh.shape[-1]
    B = _pick(S, ATTN_BLOCK, CHUNK)
    G = _pick(S, ATTN_KEY_STEP, B)
    scale = float(MLA_QK) ** -0.5
    neg = float(jnp.finfo(jnp.float32).min)

    def body(q_ref, k_ref, v_ref, o_ref, lse_ref):
        qi = pl.program_id(1)

        def run(L):
            raw = _mask_tail(_nt(q_ref[...], k_ref[0:L, :]), qi, B, G, neg)
            m = jnp.max(raw, axis=-1, keepdims=True)
            e = jnp.exp2((raw - m) * (scale * LOG2_E))
            l = jnp.sum(e, axis=-1, keepdims=True)
            o_ref[...] = _nn((e * (1.0 / l)).astype(BF16), v_ref[0:L, :])
            lse_ref[...] = m * scale + jnp.log(l)

        _for_key_prefix(qi, B, G, S, run)

    outs, jouts = _hosted_call(
        body, job, name=name, grid=(H, S // B),
        in_specs=[pl.BlockSpec((None, B, DQ), lambda h, i: (h, i, 0)),
                  pl.BlockSpec((None, S, DQ), lambda h, i: (h, 0, 0)),
                  pl.BlockSpec((None, S, DV), lambda h, i: (h, 0, 0))],
        out_specs=[pl.BlockSpec((B, DV), lambda h, i: (i, h)),
                   pl.BlockSpec((None, B, 1), lambda h, i: (h, i, 0))],
        out_shape=[_sds((S, H * DV), F32), _sds((H, S, 1), F32)], scratch_shapes=[],
        semantics=("parallel", "parallel"), args=(qh, kh, vh))
    return (outs[0], outs[1]) if job is None else (outs[0], outs[1], jouts)


def _attn_bwd(qh, kh, vh, o, do, lse, *, do_col0, job=None, name):
    H, S, DQ = qh.shape
    DV = vh.shape[-1]
    B = _pick(S, ATTN_BLOCK, CHUNK)
    G = _pick(S, ATTN_KEY_STEP, B)
    scale = float(MLA_QK) ** -0.5

    def body(q_ref, k_ref, v_ref, o_ref, do_ref, lse_ref, dq_ref, dk_ref, dv_ref):
        qi = pl.program_id(1)

        @pl.when(qi == 0)
        def _():
            dk_ref[...] = jnp.zeros_like(dk_ref)
            dv_ref[...] = jnp.zeros_like(dv_ref)

        def run(L):
            q = q_ref[...]
            k = k_ref[0:L, :]
            dov = do_ref[...]
            dob = dov.astype(BF16)
            p = jnp.exp2(_nt(q, k) * (scale * LOG2_E) - lse_ref[...] * LOG2_E)
            p = _mask_tail(p, qi, B, G, 0.0)
            dsum = jnp.sum(dov * o_ref[...], axis=-1, keepdims=True)
            ds = (p * (_nt(dob, v_ref[0:L, :]) - dsum)).astype(BF16)
            dv_ref[0:L, :] += _tn(p.astype(BF16), dob)
            dk_ref[0:L, :] += _tn(ds, q)
            dq_ref[...] = _nn(ds, k) * scale

        _for_key_prefix(qi, B, G, S, run)

        @pl.when(qi == S // B - 1)
        def _():
            dk_ref[...] = dk_ref[...] * scale

    outs, jouts = _hosted_call(
        body, job, name=name, grid=(H, S // B),
        in_specs=[pl.BlockSpec((None, B, DQ), lambda h, i: (h, i, 0)),
                  pl.BlockSpec((None, S, DQ), lambda h, i: (h, 0, 0)),
                  pl.BlockSpec((None, S, DV), lambda h, i: (h, 0, 0)),
                  pl.BlockSpec((B, DV), lambda h, i: (i, h)),
                  pl.BlockSpec((B, DV), lambda h, i: (i, do_col0 + h)),
                  pl.BlockSpec((None, B, 1), lambda h, i: (h, i, 0))],
        out_specs=[pl.BlockSpec((None, B, DQ), lambda h, i: (h, i, 0)),
                   pl.BlockSpec((None, S, DQ), lambda h, i: (h, 0, 0)),
                   pl.BlockSpec((None, S, DV), lambda h, i: (h, 0, 0))],
        out_shape=[_sds((H, S, DQ), F32), _sds((H, S, DQ), F32), _sds((H, S, DV), F32)], scratch_shapes=[],
        semantics=("parallel", "arbitrary"), args=(qh, kh, vh, o, do, lse))
    return tuple(outs) if job is None else (*outs, jouts)


def _ret_tables(T):
    H = RET_HEADS
    log_gamma = jnp.log1p(-jnp.exp2(-5.0 - jnp.arange(H, dtype=F32)))
    idx = jnp.arange(T, dtype=F32)
    chunk = jnp.arange(T) // CHUNK
    visible = chunk[:, None] >= chunk[None, :]
    intra = jnp.where(visible[None], jnp.exp(log_gamma[:, None, None] * jnp.abs(idx[:, None] - idx[None, :])), 0.0)
    qd = jnp.exp(log_gamma[:, None] * (idx + 1.0))[:, :, None]
    kd = jnp.exp(log_gamma[:, None] * (T - 1.0 - idx))[:, :, None]
    cd = jnp.exp(log_gamma * T)[:, None, None]
    return intra, qd, kd, cd


def _rot(t, c, s):
    half = t.shape[-1] // 2
    t1, t2 = t[:, :half], t[:, half:]
    return jnp.concatenate([t1 * c - t2 * s, t2 * c + t1 * s], axis=-1)


def _rot_t(t, c, s):
    half = t.shape[-1] // 2
    t1, t2 = t[:, :half], t[:, half:]
    return jnp.concatenate([t1 * c + t2 * s, t2 * c - t1 * s], axis=-1)


def _dot(a, b, ca, cb):
    return lax.dot_general(a.astype(BF16), b.astype(BF16), (((ca,), (cb,)), ((), ())), preferred_element_type=F32)


def _ret_fwd(hR, cos, sin, tables, *, name):
    S = hR.shape[0]
    H, L, DK, DV = RET_HEADS, tables[0].shape[-1], RET_QK, RET_V
    NC = S // L
    qscale = float(DK) ** -0.5
    intra, qd, kd, cd = tables

    def body(q_ref, k_ref, v_ref, c_ref, s_ref, in_ref, qd_ref, kd_ref, cd_ref, o_ref, st_ref, state):
        @pl.when(pl.program_id(1) == 0)
        def _():
            state[...] = jnp.zeros_like(state)

        c, s = c_ref[...], s_ref[...]
        q = _rot(q_ref[...], c, s) * qscale
        k = _rot(k_ref[...], c, s)
        v = v_ref[...]
        st = state[...]
        st_ref[...] = st.astype(BF16)
        scores = _dot(q, k, 1, 1) * in_ref[...]
        o_ref[...] = _dot(scores, v, 1, 0) + _dot(q * qd_ref[...], st, 1, 0)
        state[...] = st * cd_ref[...] + _dot(k * kd_ref[...], v, 0, 0)

    blk = lambda off: pl.BlockSpec((L, DK), lambda h, c: (c, off + h))
    rope = pl.BlockSpec((L, DK // 2), lambda h, c: (c, 0))
    return _pcall(
        body, name=name, grid=(H, NC),
        in_specs=[blk(0), blk(H), blk(2 * H), rope, rope,
                  pl.BlockSpec((None, L, L), lambda h, c: (h, 0, 0)),
                  pl.BlockSpec((None, L, 1), lambda h, c: (h, 0, 0)),
                  pl.BlockSpec((None, L, 1), lambda h, c: (h, 0, 0)),
                  pl.BlockSpec((None, 1, 1), lambda h, c: (h, 0, 0))],
        out_specs=[pl.BlockSpec((L, DV), lambda h, c: (c, h)),
                   pl.BlockSpec((None, None, DK, DV), lambda h, c: (h, c, 0, 0))],
        out_shape=[_sds((S, H * DV), F32), _sds((H, NC, DK, DV), BF16)],
        scratch_shapes=[pltpu.VMEM((DK, DV), F32)],
        compiler_params=_params(("arbitrary", "arbitrary")),
    )(hR, hR, hR, cos, sin, intra, qd, kd, cd)


def _ret_bwd(do, hR, states, cos, sin, tables, *, name):
    S = hR.shape[0]
    H, L, DK, DV = RET_HEADS, tables[0].shape[-1], RET_QK, RET_V
    NC = S // L
    qscale = float(DK) ** -0.5
    intra, qd, kd, cd = tables

    def body(do_ref, q_ref, k_ref, v_ref, st_ref, c_ref, s_ref, in_ref, qd_ref, kd_ref, cd_ref,
             dq_ref, dk_ref, dv_ref, dstate):
        @pl.when(pl.program_id(1) == 0)
        def _():
            dstate[...] = jnp.zeros_like(dstate)

        c, s = c_ref[...], s_ref[...]
        q = _rot(q_ref[...], c, s) * qscale
        k = _rot(k_ref[...], c, s)
        v = v_ref[...]
        dov = do_ref[...]
        dst = dstate[...]
        dec = in_ref[...]
        qdv, kdv = qd_ref[...], kd_ref[...]
        scores = _dot(q, k, 1, 1) * dec
        da = _dot(dov, v, 1, 1) * dec
        dv_ref[...] = (_dot(scores, dov, 0, 0) + _dot(k * kdv, dst, 1, 0)).astype(BF16)
        dq = _dot(da, k, 1, 0) + _dot(dov, st_ref[...], 1, 1) * qdv
        dk = _dot(da, q, 0, 0) + _dot(v, dst, 1, 1) * kdv
        dq_ref[...] = _rot_t(dq * qscale, c, s).astype(BF16)
        dk_ref[...] = _rot_t(dk, c, s).astype(BF16)
        dstate[...] = dst * cd_ref[...] + _dot(q * qdv, dov, 0, 0)

    rev = lambda c: NC - 1 - c
    blk = lambda off: pl.BlockSpec((L, DK), lambda h, c: (rev(c), off + h))
    rope = pl.BlockSpec((L, DK // 2), lambda h, c: (rev(c), 0))
    out = pl.BlockSpec((L, DK), lambda h, c: (rev(c), h))
    return _pcall(
        body, name=name, grid=(H, NC),
        in_specs=[out, blk(0), blk(H), blk(2 * H),
                  pl.BlockSpec((None, None, DK, DV), lambda h, c: (h, rev(c), 0, 0)), rope, rope,
                  pl.BlockSpec((None, L, L), lambda h, c: (h, 0, 0)),
                  pl.BlockSpec((None, L, 1), lambda h, c: (h, 0, 0)),
                  pl.BlockSpec((None, L, 1), lambda h, c: (h, 0, 0)),
                  pl.BlockSpec((None, 1, 1), lambda h, c: (h, 0, 0))],
        out_specs=[out, out, out],
        out_shape=[_sds((S, H * DK), BF16)] * 3,
        scratch_shapes=[pltpu.VMEM((DK, DV), F32)],
        compiler_params=_params(("arbitrary", "arbitrary")),
    )(do, hR, hR, hR, states, cos, sin, intra, qd, kd, cd)


def _gn_gate_fwd(o, hR, g, b, *, name):
    S = o.shape[0]
    H, DV = RET_HEADS, RET_V
    bs = _pick(S, 512, 16)

    def body(o_ref, rg_ref, g_ref, b_ref, r_ref):
        z = o_ref[...]
        z = z - jnp.mean(z, axis=-1, keepdims=True)
        xh = z * lax.rsqrt(jnp.mean(z * z, axis=-1, keepdims=True) + GN_EPS)
        rg = rg_ref[...]
        r_ref[...] = ((rg * jax.nn.sigmoid(rg)) * (xh * g_ref[...] + b_ref[...])).astype(BF16)

    row = pl.BlockSpec((bs, DV), lambda i, h: (i, h))
    vec = pl.BlockSpec((1, DV), lambda i, h: (0, h))
    return _pcall(
        body, name=name, grid=(S // bs, H),
        in_specs=[row, pl.BlockSpec((bs, DV), lambda i, h: (i, 3 * H + h)), vec, vec],
        out_specs=row, out_shape=_sds((S, H * DV), BF16),
        compiler_params=_params(("parallel", "parallel")),
    )(o, hR, g.reshape(1, H * DV), b.reshape(1, H * DV))


def _gn_gate_bwd(dr, o, hR, g, b, *, dr_col0, name):
    S = o.shape[0]
    H, DV = RET_HEADS, RET_V
    bs = _pick(S, 512, 16)

    def body(dr_ref, o_ref, rg_ref, g_ref, b_ref, do_ref, drg_ref, dg_ref, db_ref):
        @pl.when(pl.program_id(1) == 0)
        def _():
            dg_ref[...] = jnp.zeros_like(dg_ref)
            db_ref[...] = jnp.zeros_like(db_ref)

        z = o_ref[...]
        z = z - jnp.mean(z, axis=-1, keepdims=True)
        rstd = lax.rsqrt(jnp.mean(z * z, axis=-1, keepdims=True) + GN_EPS)
        xh = z * rstd
        gv = g_ref[...]
        y = xh * gv + b_ref[...]
        rg = rg_ref[...]
        sg = jax.nn.sigmoid(rg)
        drv = dr_ref[...]
        dy = drv * (rg * sg)
        drg_ref[...] = (drv * y * (sg * (1.0 + rg * (1.0 - sg)))).astype(BF16)
        dg_ref[...] += jnp.sum(dy * xh, axis=0, keepdims=True)
        db_ref[...] += jnp.sum(dy, axis=0, keepdims=True)
        dxh = dy * gv
        do_ref[...] = rstd * (dxh - jnp.mean(dxh, axis=-1, keepdims=True)
                              - xh * jnp.mean(dxh * xh, axis=-1, keepdims=True))

    row = pl.BlockSpec((bs, DV), lambda h, i: (i, h))
    vec = pl.BlockSpec((1, DV), lambda h, i: (0, h))
    do, drg, dg, db = _pcall(
        body, name=name, grid=(H, S // bs),
        in_specs=[pl.BlockSpec((bs, DV), lambda h, i: (i, dr_col0 + h)), row,
                  pl.BlockSpec((bs, DV), lambda h, i: (i, 3 * H + h)), vec, vec],
        out_specs=[row, row, vec, vec],
        out_shape=[_sds((S, H * DV), F32), _sds((S, H * DV), BF16), _sds((1, H * DV), F32), _sds((1, H * DV), F32)],
        compiler_params=_params(("arbitrary", "arbitrary")),
    )(dr, o, hR, g.reshape(1, H * DV), b.reshape(1, H * DV))
    return do, drg, dg[0], db[0]


def _swiglu(u, g):
    return g, u, (g * jax.nn.sigmoid(g)) * u


def _swiglu_bwd(da, g, u):
    g, u = g.astype(F32), u.astype(F32)
    sg = jax.nn.sigmoid(g)
    return da * u * (sg * (1.0 + g * (1.0 - sg))), da * (g * sg)


def _loss_head(y, t, *, name):
    S, D = y.shape
    bs = _pick(S, 256, 8)
    inv_d = 1.0 / D

    def body(y_ref, t_ref, dy_ref, l_ref):
        @pl.when(pl.program_id(0) == 0)
        def _():
            l_ref[...] = jnp.zeros_like(l_ref)

        e = y_ref[...] - t_ref[...]
        dy_ref[...] = e * inv_d
        l_ref[...] += 0.5 * jnp.sum(jnp.mean(e * e, axis=-1, keepdims=True), axis=0, keepdims=True)

    row = pl.BlockSpec((bs, D), lambda i: (i, 0))
    dy, l = _pcall(
        body, name=name, grid=(S // bs,), in_specs=[row, row],
        out_specs=[row, pl.BlockSpec((1, 1), lambda i: (0, 0))],
        out_shape=[_sds((S, D), F32), _sds((1, 1), F32)],
        compiler_params=_params(("arbitrary",)),
    )(y, t)
    return dy, l[0, 0]


def _adamw(w, g, m, v, *, name):
    shape = w.shape
    C = shape[-1]
    R = w.size // C
    br = _pick(R, 512, 8)

    def body(w_ref, g_ref, m_ref, v_ref, d_ref, nm_ref, nv_ref):
        gv = g_ref[...]
        mn = ADAM_B1 * m_ref[...] + (1.0 - ADAM_B1) * gv
        vn = ADAM_B2 * v_ref[...] + (1.0 - ADAM_B2) * (gv * gv)
        m_hat = mn / (1.0 - ADAM_B1 ** ADAM_STEP)
        v_hat = vn / (1.0 - ADAM_B2 ** ADAM_STEP)
        d_ref[...] = -ADAM_LR * (m_hat / (jnp.sqrt(v_hat) + ADAM_EPS) + ADAM_WD * w_ref[...])
        nm_ref[...] = mn
        nv_ref[...] = vn

    blk = pl.BlockSpec((br, C), lambda i: (i, 0))
    outs = _pcall(body, name=name, grid=(R // br,), in_specs=[blk] * 4, out_specs=[blk] * 3,
                  out_shape=[_sds((R, C), F32)] * 3,
                  compiler_params=_params(("parallel",)))(*[a.reshape(R, C) for a in (w, g, m, v)])
    return tuple(o.reshape(shape) for o in outs)


def _slot_block(n, k):
    return (_pick(n, 1024, 16) if n % 16 == 0 else n), _pick(k, 1024, 128)


def _sum_slots(x, *, name):
    ns, n, k = x.shape
    br, bc = _slot_block(n, k)

    def body(x_ref, o_ref):
        acc = x_ref[0].astype(F32)
        for s in range(1, ns):
            acc = acc + x_ref[s].astype(F32)
        o_ref[...] = acc

    return _pcall(body, name=name, grid=(n // br, k // bc),
                  in_specs=[pl.BlockSpec((ns, br, bc), lambda i, j: (0, i, j))],
                  out_specs=pl.BlockSpec((br, bc), lambda i, j: (i, j)),
                  out_shape=_sds((n, k), F32), compiler_params=_params(("parallel", "parallel")))(x)


def _pair_sum(g, r, *, name):
    _, n, k = r.shape
    br, bc = _slot_block(n, k)
    core = lax.axis_index("c").astype(jnp.int32).reshape(1)

    def body(c_ref, g_ref, r_ref, o_ref):
        o_ref[...] = (g_ref[...].astype(F32) + r_ref[...].astype(F32)).astype(o_ref.dtype)

    blk = pl.BlockSpec((None, br, bc), lambda s, i, j, c_ref: (s, i, j))
    return _pcall(
        body, name=name, out_shape=_sds(r.shape, g.dtype),
        grid_spec=pltpu.PrefetchScalarGridSpec(
            num_scalar_prefetch=1, grid=(N_CHIP, n // br, k // bc),
            in_specs=[pl.BlockSpec((None, br, bc), lambda s, i, j, c_ref: (2 * s + c_ref[0], i, j)), blk],
            out_specs=blk),
        compiler_params=_params(("parallel", "parallel", "parallel")),
    )(core, g, r)


N_PEER = N_DEV - 1
N_CHIP = N_DEV // 2
HOST_TAIL_FRACTION = 8


def _coords():
    return lax.axis_index("x"), lax.axis_index("y"), lax.axis_index("c")


def _ag_phase(phase, x_refs, out_refs, send_sems, recv_sems, local_sems):
    n = len(x_refs)
    x, y, c = _coords()
    me, sibling = (x, y, c), (x, y, 1 - c)
    chips = [(1 - x, y), (x, 1 - y), (1 - x, 1 - y)]

    def copy(a, k, block, to, src=None):
        px, py, pc = block
        dst = out_refs[a].at[4 * px + 2 * py + pc]
        return pltpu.make_async_remote_copy(
            src_ref=dst if src is None else src, dst_ref=dst,
            send_sem=send_sems.at[a * N_PEER + k], recv_sem=recv_sems.at[a * N_PEER + k],
            device_id=to, device_id_type=pl.DeviceIdType.MESH)

    def local(a):
        return pltpu.make_async_copy(x_refs[a], out_refs[a].at[4 * x + 2 * y + c], local_sems.at[a])

    def first(a):
        return ([copy(a, 0, me, sibling, src=x_refs[a])]
                + [copy(a, 1 + j, me, (*chip, c), src=x_refs[a]) for j, chip in enumerate(chips)])

    def passed(a, j):
        return copy(a, 4 + j, (*chips[j], c), sibling)

    if phase == 0:
        for a in range(n):
            local(a).start()
            for cp in first(a):
                cp.start()
    elif phase == 1:
        for j in range(len(chips)):
            for a in range(n):
                copy(a, 1 + j, (*chips[j], c), me).wait_recv()
                passed(a, j).start()
    else:
        for a in range(n):
            copy(a, 0, sibling, me).wait_recv()
            for j in range(len(chips)):
                copy(a, 4 + j, (*chips[j], 1 - c), me).wait_recv()
        for a in range(n):
            for cp in first(a):
                cp.wait_send()
            for j in range(len(chips)):
                passed(a, j).wait_send()
            local(a).wait()


def _ag1_phase(phase, x_refs, out_refs, send_sems, recv_sems, local_sems):
    n = len(x_refs)
    x, y, c = _coords()
    chips = [(1 - x, y), (x, 1 - y), (1 - x, 1 - y)]

    def local(a):
        return pltpu.make_async_copy(x_refs[a], out_refs[a].at[4 * x + 2 * y + c], local_sems.at[a])

    def pair(a, j):
        px, py = chips[j]
        sem = a * N_PEER + j
        send = pltpu.make_async_remote_copy(
            src_ref=x_refs[a], dst_ref=out_refs[a].at[4 * x + 2 * y + c], send_sem=send_sems.at[sem],
            recv_sem=recv_sems.at[sem], device_id=(px, py, c), device_id_type=pl.DeviceIdType.MESH)
        recv = pltpu.make_async_remote_copy(
            src_ref=x_refs[a], dst_ref=out_refs[a].at[4 * px + 2 * py + c], send_sem=send_sems.at[sem],
            recv_sem=recv_sems.at[sem], device_id=(px, py, c), device_id_type=pl.DeviceIdType.MESH)
        return send, recv

    if phase == 0:
        for a in range(n):
            local(a).start()
            for j in range(len(chips)):
                pair(a, j)[0].start()
    elif phase == 2:
        for a in range(n):
            for j in range(len(chips)):
                pair(a, j)[1].wait_recv()
        for a in range(n):
            for j in range(len(chips)):
                pair(a, j)[0].wait_send()
            local(a).wait()


def _ag2_phase(phase, buf_refs, out_refs, send_sems, recv_sems, local_sems):
    n = len(out_refs)
    x, y, c = _coords()

    def copy(a, i):
        return pltpu.make_async_remote_copy(
            src_ref=out_refs[a].at[2 * i + c], dst_ref=out_refs[a].at[2 * i + c],
            send_sem=send_sems.at[a * N_PEER + i], recv_sem=recv_sems.at[a * N_PEER + i],
            device_id=(x, y, 1 - c), device_id_type=pl.DeviceIdType.MESH)

    def arrival(a, i):
        return pltpu.make_async_remote_copy(
            src_ref=out_refs[a].at[2 * i + (1 - c)], dst_ref=out_refs[a].at[2 * i + (1 - c)],
            send_sem=send_sems.at[a * N_PEER + i], recv_sem=recv_sems.at[a * N_PEER + i],
            device_id=(x, y, 1 - c), device_id_type=pl.DeviceIdType.MESH)

    if phase == 0:
        for a in range(n):
            for i in range(N_CHIP):
                copy(a, i).start()
    elif phase == 2:
        for a in range(n):
            for i in range(N_CHIP):
                arrival(a, i).wait_recv()
        for a in range(n):
            for i in range(N_CHIP):
                copy(a, i).wait_send()


def _pair_phase(phase, g_refs, out_refs, send_sems, recv_sems, local_sems):
    n = len(g_refs)
    x, y, c = _coords()

    def copy(a, i):
        return pltpu.make_async_remote_copy(
            src_ref=g_refs[a].at[2 * i + (1 - c)], dst_ref=out_refs[a].at[i],
            send_sem=send_sems.at[a * N_PEER + i], recv_sem=recv_sems.at[a * N_PEER + i],
            device_id=(x, y, 1 - c), device_id_type=pl.DeviceIdType.MESH)

    if phase == 0:
        for a in range(n):
            for i in range(N_CHIP):
                copy(a, i).start()
    elif phase == 2:
        for a in range(n):
            for i in range(N_CHIP):
                copy(a, i).wait_recv()
        for a in range(n):
            for i in range(N_CHIP):
                copy(a, i).wait_send()


def _cross_phase(phase, p_refs, out_refs, send_sems, recv_sems, local_sems):
    n = len(p_refs)
    x, y, c = _coords()
    mine = 2 * x + y

    def local(a):
        return pltpu.make_async_copy(p_refs[a].at[mine], out_refs[a].at[mine], local_sems.at[a])

    def pair(a, k):
        px = 1 - x if k & 2 else x
        py = 1 - y if k & 1 else y
        peer = 2 * px + py
        sem = a * N_PEER + k - 1
        send = pltpu.make_async_remote_copy(
            src_ref=p_refs[a].at[peer], dst_ref=out_refs[a].at[mine], send_sem=send_sems.at[sem],
            recv_sem=recv_sems.at[sem], device_id=(px, py, c), device_id_type=pl.DeviceIdType.MESH)
        recv = pltpu.make_async_remote_copy(
            src_ref=p_refs[a].at[peer], dst_ref=out_refs[a].at[peer], send_sem=send_sems.at[sem],
            recv_sem=recv_sems.at[sem], device_id=(px, py, c), device_id_type=pl.DeviceIdType.MESH)
        return send, recv

    if phase == 0:
        for a in range(n):
            local(a).start()
        for k in range(1, N_CHIP):
            for a in range(n):
                pair(a, k)[0].start()
    elif phase == 2:
        for k in range(1, N_CHIP):
            for a in range(n):
                pair(a, k)[1].wait_recv()
        for k in range(1, N_CHIP):
            for a in range(n):
                pair(a, k)[0].wait_send()
        for a in range(n):
            local(a).wait()


_PHASES = {"ag": _ag_phase, "ag1": _ag1_phase, "ag2": _ag2_phase, "pair": _pair_phase, "cross": _cross_phase}


def _job(kind, arrays):
    arrays = list(arrays)
    if kind in ("ag", "ag1"):
        shapes = [(N_DEV,) + a.shape for a in arrays]
    elif kind == "pair":
        shapes = [(N_CHIP,) + a.shape[1:] for a in arrays]
    else:
        shapes = [a.shape for a in arrays]
    return dict(parts=[(kind, len(arrays))], ins=arrays, outs=[_sds(s, a.dtype) for s, a in zip(shapes, arrays)],
                in_place=[kind == "ag2"] * len(arrays))


def _join(jobs):
    jobs = [j for j in jobs if j is not None]
    if not jobs:
        return None
    return dict(parts=[p for j in jobs for p in j["parts"]], ins=[a for j in jobs for a in j["ins"]],
                outs=[o for j in jobs for o in j["outs"]], in_place=[f for j in jobs for f in j["in_place"]])


def _hosted_call(body, job, *, name, grid, in_specs, out_specs, out_shape, scratch_shapes, semantics, args):
    if job is None:
        outs = _pcall(body, name=name, grid=grid, in_specs=in_specs, out_specs=out_specs, out_shape=out_shape,
                      scratch_shapes=scratch_shapes, compiler_params=_params(semantics))(*args)
        return outs, []
    n_in, n_out, n_scr, nj = len(in_specs), len(out_specs), len(scratch_shapes), len(job["ins"])
    parts = job["parts"]
    total = 1
    for g in grid:
        total *= g
    late = total - max(total // HOST_TAIL_FRACTION, 1) if total > 1 else 0

    def wrapped(*refs):
        ins, refs = refs[:n_in], refs[n_in:]
        jins, refs = refs[:nj], refs[nj:]
        outs, refs = refs[:n_out], refs[n_out:]
        jouts, refs = refs[:nj], refs[nj:]
        scr, sems = refs[:n_scr], refs[n_scr:]
        step = 0
        for d, g in enumerate(grid):
            step = step * g + pl.program_id(d)

        def run(phase):
            off = 0
            for i, (kind, n) in enumerate(parts):
                _PHASES[kind](phase, jins[off:off + n], jouts[off:off + n], *sems[3 * i:3 * i + 3])
                off += n

        pl.when(step == 0)(lambda: run(0))
        pl.when(step == late)(lambda: run(1))
        body(*ins, *outs, *scr)
        pl.when(step == total - 1)(lambda: run(2))

    any_spec = pl.BlockSpec(memory_space=pl.ANY)
    sems = []
    for kind, n in parts:
        sems += [pltpu.SemaphoreType.DMA((n * N_PEER,)), pltpu.SemaphoreType.DMA((n * N_PEER,)),
                 pltpu.SemaphoreType.DMA((n,))]
    outs = _pcall(
        wrapped, name=name, grid=grid, in_specs=list(in_specs) + [any_spec] * nj,
        out_specs=list(out_specs) + [any_spec] * nj, out_shape=list(out_shape) + job["outs"],
        scratch_shapes=list(scratch_shapes) + sems,
        input_output_aliases={n_in + i: n_out + i for i in range(nj) if job["in_place"][i]},
        compiler_params=_params(("arbitrary",) * len(grid)),
    )(*args, *job["ins"])
    return outs[:n_out], outs[n_out:]


def _exchange(job, *, name):
    outs, jouts = _hosted_call(lambda: None, job, name=name, grid=(1,), in_specs=[], out_specs=[], out_shape=[],
                               scratch_shapes=[], semantics=("arbitrary",), args=())
    return jouts


_TRANSPOSED = {"w_in": True, "w_out": False, "w_gate": True, "w_up": True, "w_down": False, "w_uq": True, "w_ukv": True}
_BIG = tuple(_TRANSPOSED)
_SMALL = ("ln_in_g", "ln_in_b", "q_norm_g", "kv_norm_g", "ret_gn_g", "ret_gn_b", "ln1_g", "ln1_b", "ln2_g", "ln2_b")
SMALL_COLS = 128


def _rope_tables(pos, dim):
    inv_freq = ROPE_THETA ** (-jnp.arange(0, dim, 2, dtype=F32) / dim)
    ang = pos.astype(F32)[:, None] * inv_freq
    return jnp.cos(ang), jnp.sin(ang)


def _split_in(wt_in):
    a, b, c = MLA_Q_LORA, MLA_Q_LORA + MLA_KV_LORA, MLA_Q_LORA + MLA_KV_LORA + MLA_ROPE
    return wt_in[:a], wt_in[a:b], wt_in[b:c], wt_in[c:]


def _pad_heads(wt_uq):
    H = MLA_HEADS
    w3 = wt_uq.reshape(H, MLA_QK, wt_uq.shape[1])
    return jnp.pad(w3, ((0, 0), (0, MLA_PAD - MLA_QK), (0, 0)))


class _Plan:
    FIRST = [("ag", ((0, "w_in"),))]
    LAST = [("cross", ((0, "w_in"),))]
    _NEXT_IN = ((1, "w_in"), (1, "w_uq"), (1, "w_ukv"))
    HOSTS = {
        "l0_cq": [("ag", ((0, "w_uq"), (0, "w_ukv")))],
        "l0_hR": [("ag1", ((0, "w_out"),))],
        "l0_attn": [("ag2", ((0, "w_out"),)), ("ag1", ((0, "w_gate"), (0, "w_up")))],
        "l0_out": [("ag2", ((0, "w_gate"), (0, "w_up")))],
        "l0_gate": [("ag1", ((0, "w_down"),))],
        "l0_up": [("ag2", ((0, "w_down"),)), ("ag1", _NEXT_IN)],
        "l0_down": [("ag2", _NEXT_IN), ("ag1", ((1, "w_gate"),))],
        "l1_hR": [("ag2", ((1, "w_gate"),)), ("ag1", ((1, "w_out"),))],
        "l1_attn": [("ag2", ((1, "w_out"),)), ("ag1", ((1, "w_up"),))],
        "l1_out": [("ag2", ((1, "w_up"),))],
        "l1_gate": [("ag1", ((1, "w_down"),))],
        "l1_up": [("ag2", ((1, "w_down"),))],
        "l0_b_dact": [("cross", ((1, "w_in"),))]}
    for _l in (0, 1):
        HOSTS.update({
            "l%d_b_wgate" % _l: [("pair", ((_l, "w_down"),))],
            "l%d_b_wup" % _l: [("pair", ((_l, "w_gate"),)), ("cross", ((_l, "w_down"),))],
            "l%d_b_dx1a" % _l: [("pair", ((_l, "w_up"),)), ("cross", ((_l, "w_gate"),))],
            "l%d_b_dx1b" % _l: [("cross", ((_l, "w_up"),))],
            "l%d_b_attn" % _l: [("pair", ((_l, "w_out"),))],
            "l%d_b_win" % _l: [("pair", ((_l, "w_uq"), (_l, "w_ukv"))), ("cross", ((_l, "w_out"),))],
            "l%d_b_dx" % _l: [("pair", ((_l, "w_in"),)), ("cross", ((_l, "w_uq"), (_l, "w_ukv")))]})

    def __init__(self, local):
        self.local = local
        self.half = {}
        self.full = {}
        self.grads = {}
        self.paired = {}
        self.recv = {}

    def _by_device(self, k):
        return self.grads[k].reshape((N_DEV,) + self.local[k].shape)

    def _make(self, kind, keys):
        if kind in ("ag", "ag1"):
            return _job(kind, [self.local[k] for k in keys])
        if kind == "ag2":
            return _job(kind, [self.half[k] for k in keys])
        if kind == "pair":
            return _job(kind, [self._by_device(k) for k in keys])
        return _job(kind, [_pair_sum(self._by_device(k), self.paired[k], name="pairsum_l%d_%s" % k) for k in keys])

    def _done(self, kind, keys, outs):
        for k, o in zip(keys, outs):
            if kind in ("ag", "ag2"):
                self.full[k] = o.reshape(N_DEV * o.shape[1], o.shape[2])
            elif kind == "ag1":
                self.half[k] = o
            elif kind == "pair":
                self.paired[k] = o
            else:
                self.recv[k] = o

    def _run(self, todo, call):
        outs = call(_join([self._make(kind, keys) for kind, keys in todo]))
        for kind, keys in todo:
            self._done(kind, keys, outs[:len(keys)])
            outs = outs[len(keys):]

    def gather_first(self):
        self._run(self.FIRST, lambda job: _exchange(job, name="ag_first"))

    def send_last(self):
        self._run(self.LAST, lambda job: _exchange(job, name="rs_last"))

    def call(self, fn, name, *args, **kw):
        if name not in self.HOSTS:
            return fn(*args, name=name, **kw)
        res = []

        def run(job):
            *outs, jouts = fn(*args, name=name, job=job, **kw)
            res.extend(outs)
            return jouts

        self._run(self.HOSTS[name], run)
        return res[0] if len(res) == 1 else tuple(res)


def _layer_fwd(x, xb, plan, p, tabs, l):
    S, D = x.shape
    H = MLA_HEADS
    nm = lambda s: "l%d_%s" % (l, s)
    w = lambda n: plan.full[(l, n)]
    mm = lambda name, *a, **kw: plan.call(_mm, nm(name), *a, **kw)
    wt_q, wt_kv, wt_kr, wt_r = _split_in(w("w_in"))
    cq = mm("cq", xb, wt_q, tb=True)
    ckv = mm("ckv", xb, wt_kv, tb=True)
    krp = mm("krope", xb, jnp.pad(wt_kr, ((0, LANES - MLA_ROPE), (0, 0))), tb=True)
    hR = mm("hR", xb, wt_r, tb=True)
    _, qn, qn_hat, q_rstd = _norm_fwd(cq, p["q_norm_g"], None, center=False, eps=RMS_EPS, want_f32=False,
                                      name=nm("qnorm"))
    _, kvn, kvn_hat, kv_rstd = _norm_fwd(ckv, p["kv_norm_g"], None, center=False, eps=RMS_EPS, want_f32=False,
                                         name=nm("kvnorm"))
    w_uq3 = _pad_heads(w("w_uq"))
    w_ukv3 = w("w_ukv").reshape(H, MLA_NOPE + MLA_V, -1)
    q3 = _mm_heads_out(qn, w_uq3, name=nm("uq"))
    kv3 = _mm_heads_out(kvn, w_ukv3, name=nm("ukv"))
    qh, kh, vh = _mla_prep(q3, kv3, krp, tabs["c128"], tabs["s128"], name=nm("rope"))
    a, lse = plan.call(_attn_fwd, nm("attn"), qh, kh, vh)
    o_ret, states = _ret_fwd(hR, tabs["cos_r"], tabs["sin_r"], tabs["ret"], name=nm("ret"))
    r = _gn_gate_fwd(o_ret, hR, p["ret_gn_g"], p["ret_gn_b"], name=nm("gn"))
    mix_in = jnp.concatenate([a.astype(BF16), r], axis=1)
    mix = mm("out", mix_in, w("w_out"))
    x1, x1b, x1_hat, rstd1 = _norm_fwd(x, p["ln1_g"], p["ln1_b"], res=mix, alpha=p["alpha"], eps=LN_EPS, name=nm("ln1"))
    g = mm("gate", x1b, w("w_gate"), tb=True)
    gb, ub, act = mm("up", x1b, w("w_up"), tb=True, extras=(g,), epilogue=_swiglu, out_dtypes=(BF16, BF16, BF16))
    f = mm("down", act, w("w_down"))
    x2, x2b, x2_hat, rstd2 = _norm_fwd(x1, p["ln2_g"], p["ln2_b"], res=f, alpha=p["alpha"], eps=LN_EPS, name=nm("ln2"))
    saved = dict(xb=xb, qn=qn, qn_hat=qn_hat, q_rstd=q_rstd, kvn=kvn, kvn_hat=kvn_hat, kv_rstd=kv_rstd,
                 qh=qh, kh=kh, vh=vh, a=a, lse=lse, hR=hR, o_ret=o_ret, states=states, mix_in=mix_in,
                 x1b=x1b, x1_hat=x1_hat, rstd1=rstd1, gb=gb, ub=ub, act=act, x2_hat=x2_hat, rstd2=rstd2)
    return x2, x2b, saved


def _layer_bwd(dx2, sv, plan, p, tabs, l):
    S, D = dx2.shape
    H = MLA_HEADS
    nm = lambda s: "l%d_b_%s" % (l, s)
    w = lambda n: plan.full[(l, n)]
    mm = lambda name, *a, **kw: plan.call(_mm, nm(name), *a, **kw)
    alpha = p["alpha"]
    gw, gp = plan.grads, {}
    dz2, dz2b, gp["ln2_g"], gp["ln2_b"] = _norm_bwd(dx2, sv["x2_hat"], sv["rstd2"], p["ln2_g"], name=nm("ln2"))
    dg, du = mm("dact", dz2b, w("w_down"), tb=True, extras=(sv["gb"], sv["ub"]), epilogue=_swiglu_bwd,
                out_dtypes=(BF16, BF16))
    gw[(l, "w_down")] = mm("wdown", sv["act"], dz2b, ta=True, out_dtype=BF16)
    gw[(l, "w_gate")] = mm("wgate", dg, sv["x1b"], ta=True, out_dtype=BF16)
    gw[(l, "w_up")] = mm("wup", du, sv["x1b"], ta=True, out_dtype=BF16)
    t = mm("dx1a", dg, w("w_gate"), add=dz2, add_scale=alpha)
    dx1 = mm("dx1b", du, w("w_up"), add=t)
    dz1, dz1b, gp["ln1_g"], gp["ln1_b"] = _norm_bwd(dx1, sv["x1_hat"], sv["rstd1"], p["ln1_g"], name=nm("ln1"))
    dmix = mm("dmix", dz1b, w("w_out"), tb=True)
    gw[(l, "w_out")] = mm("wout", sv["mix_in"], dz1b, ta=True, out_dtype=BF16)
    ret_col0 = (H * MLA_V) // RET_V
    do_ret, drg, gp["ret_gn_g"], gp["ret_gn_b"] = _gn_gate_bwd(
        dmix, sv["o_ret"], sv["hR"], p["ret_gn_g"], p["ret_gn_b"], dr_col0=ret_col0, name=nm("gn"))
    drq, drk, drv = _ret_bwd(do_ret, sv["hR"], sv["states"], tabs["cos_r"], tabs["sin_r"], tabs["ret"], name=nm("ret"))
    dqh, dkh, dvh = plan.call(_attn_bwd, nm("attn"), sv["qh"], sv["kh"], sv["vh"], sv["a"], dmix, sv["lse"], do_col0=0)
    dq3, dkv3, dkrp = _mla_unprep(dqh, dkh, dvh, tabs["c128"], tabs["s128"], name=nm("rope"))
    w_uq3 = _pad_heads(w("w_uq"))
    w_ukv3 = w("w_ukv").reshape(H, MLA_NOPE + MLA_V, -1)
    g_uq = _mm_heads_tn(dq3, sv["qn"], name=nm("wuq")).reshape(H, MLA_PAD, -1)
    gw[(l, "w_uq")] = g_uq[:, :MLA_QK].reshape(H * MLA_QK, -1)
    dqn = _mm_heads_sum(dq3, w_uq3, name=nm("dqn"))
    gw[(l, "w_ukv")] = _mm_heads_tn(dkv3, sv["kvn"], name=nm("wukv"))
    dkvn = _mm_heads_sum(dkv3, w_ukv3, name=nm("dkvn"))
    _, dcq, gp["q_norm_g"], _ = _norm_bwd(dqn, sv["qn_hat"], sv["q_rstd"], p["q_norm_g"], center=False,
                                          want_f32=False, name=nm("qnorm"))
    _, dckv, gp["kv_norm_g"], _ = _norm_bwd(dkvn, sv["kvn_hat"], sv["kv_rstd"], p["kv_norm_g"], center=False,
                                            want_f32=False, name=nm("kvnorm"))
    dh = jnp.concatenate([dcq, dckv, dkrp[:, :MLA_ROPE].astype(BF16), drq, drk, drv, drg], axis=1)
    gw[(l, "w_in")] = mm("win", dh, sv["xb"], ta=True, out_dtype=BF16)
    dx = mm("dx", dh, w("w_in"), add=dz1, add_scale=alpha)
    return dx, gp


def _local_step(x, target, pos, small, plan, depth):
    alpha = (2 * depth) ** 0.25
    cos_m, sin_m = _rope_tables(pos, MLA_ROPE)
    cos_r, sin_r = _rope_tables(pos, RET_QK)
    zeros = jnp.zeros((x.shape[0], LANES - MLA_ROPE), F32)
    tabs = dict(c128=jnp.concatenate([cos_m, cos_m, zeros], axis=1), s128=jnp.concatenate([-sin_m, sin_m, zeros], axis=1),
                cos_r=cos_r, sin_r=sin_r, ret=_ret_tables(_pick(x.shape[0], RET_BLOCK, CHUNK)))
    h, hb, h_hat, h_rstd = _norm_fwd(x, small["ln_in_g"], small["ln_in_b"], eps=LN_EPS, name="ln_in")
    saved, ps = [], []
    for l in range(depth):
        p = {k: small[k][l] for k in _SMALL[2:]}
        p["alpha"] = alpha
        h, hb, sv = _layer_fwd(h, hb, plan, p, tabs, l)
        saved.append(sv)
        ps.append(p)
    dy, loss = _loss_head(h, target, name="loss")
    gps = [None] * depth
    for l in reversed(range(depth)):
        dy, gps[l] = _layer_bwd(dy, saved[l], plan, ps[l], tabs, l)
    grad_x, _, g_in_g, g_in_b = _norm_bwd(dy, h_hat, h_rstd, small["ln_in_g"], want_bf16=False, name="b_ln_in")
    gsmall = {"ln_in_g": g_in_g, "ln_in_b": g_in_b}
    for k in _SMALL[2:]:
        gsmall[k] = jnp.stack([gps[l][k] for l in range(depth)])
    return loss, grad_x, gsmall


def kernel(x, positions, ln_in_g, ln_in_b, w_in, q_norm_g, kv_norm_g, w_uq, w_ukv, ret_gn_g, ret_gn_b, w_out, ln1_g, ln1_b, w_gate, w_up, w_down, ln2_g, ln2_b, loss_target, m_ln_in_g, m_ln_in_b, m_w_in, m_q_norm_g, m_kv_norm_g, m_w_uq, m_w_ukv, m_ret_gn_g, m_ret_gn_b, m_w_out, m_ln1_g, m_ln1_b, m_w_gate, m_w_up, m_w_down, m_ln2_g, m_ln2_b, v_ln_in_g, v_ln_in_b, v_w_in, v_q_norm_g, v_kv_norm_g, v_w_uq, v_w_ukv, v_ret_gn_g, v_ret_gn_b, v_w_out, v_ln1_g, v_ln1_b, v_w_gate, v_w_up, v_w_down, v_ln2_g, v_ln2_b):
    names = ["ln_in_g", "ln_in_b", "w_in", "q_norm_g", "kv_norm_g", "w_uq", "w_ukv", "ret_gn_g", "ret_gn_b", "w_out",
             "ln1_g", "ln1_b", "w_gate", "w_up", "w_down", "ln2_g", "ln2_b"]
    wv = dict(zip(names, (ln_in_g, ln_in_b, w_in, q_norm_g, kv_norm_g, w_uq, w_ukv, ret_gn_g, ret_gn_b, w_out,
                          ln1_g, ln1_b, w_gate, w_up, w_down, ln2_g, ln2_b)))
    mv = dict(zip(names, (m_ln_in_g, m_ln_in_b, m_w_in, m_q_norm_g, m_kv_norm_g, m_w_uq, m_w_ukv, m_ret_gn_g,
                          m_ret_gn_b, m_w_out, m_ln1_g, m_ln1_b, m_w_gate, m_w_up, m_w_down, m_ln2_g, m_ln2_b)))
    vv = dict(zip(names, (v_ln_in_g, v_ln_in_b, v_w_in, v_q_norm_g, v_kv_norm_g, v_w_uq, v_w_ukv, v_ret_gn_g,
                          v_ret_gn_b, v_w_out, v_ln1_g, v_ln1_b, v_w_gate, v_w_up, v_w_down, v_ln2_g, v_ln2_b)))
    depth = w_in.shape[0]
    assert depth == 2, "the exchange plan is written for two layers"

    keys = [(l, n) for l in range(depth) for n in _BIG]
    plan = _Plan({(l, n): (wv[n][l].T if _TRANSPOSED[n] else wv[n][l]).astype(BF16) for l, n in keys})
    plan.gather_first()

    small = {n: wv[n] for n in _SMALL}
    loss, grad_x, gsmall = _local_step(x[0], loss_target[0], positions[0], small, plan, depth)
    loss = lax.psum(loss, MESH_AXES)
    plan.send_last()

    gshard = {n: [None] * depth for n in _BIG}
    for l, n in keys:
        tot = _sum_slots(plan.recv[(l, n)], name="sum_l%d_%s" % (l, n))
        gshard[n][l] = tot.T if _TRANSPOSED[n] else tot
    grads = {n: jnp.stack(v) for n, v in gshard.items()}

    flat = jnp.concatenate([gsmall[n].reshape(-1) for n in _SMALL])
    n_small = flat.shape[0]
    rows = -(-n_small // (SMALL_COLS * 8)) * 8
    flat = jnp.pad(flat, (0, rows * SMALL_COLS - n_small)).reshape(rows, SMALL_COLS)
    tot = _sum_slots(_exchange(_job("ag", [flat]), name="ag_small")[0], name="sum_small").reshape(-1)
    off = 0
    for n in _SMALL:
        grads[n] = tot[off:off + wv[n].size].reshape(wv[n].shape)
        off += wv[n].size

    delta, new_m, new_v = {}, {}, {}
    for n in names:
        w2 = wv[n] if wv[n].ndim > 1 else wv[n].reshape(1, -1)
        d, nm_, nv_ = _adamw(w2, grads[n].reshape(w2.shape), mv[n].reshape(w2.shape), vv[n].reshape(w2.shape),
                             name="adamw_" + n)
        delta[n], new_m[n], new_v[n] = d.reshape(wv[n].shape), nm_.reshape(wv[n].shape), nv_.reshape(wv[n].shape)

    return (loss, grad_x[None], *[grads[n] for n in names], *[delta[n] for n in names],
            *[new_m[n] for n in names], *[new_v[n] for n in names])
```

```python
import jax
import jax.numpy as jnp
from jax import lax
from jax.experimental import pallas as pl
from jax.experimental.pallas import tpu as pltpu

F32 = jnp.float32
BF16 = jnp.bfloat16

CHUNK = 64
CHUNK_SHIFT = 6
MLA_HEADS = 8
MLA_Q_LORA = 512
MLA_KV_LORA = 256
MLA_NOPE = 128
MLA_ROPE = 64
MLA_V = 128
MLA_QK = MLA_NOPE + MLA_ROPE
LANES = 128
MLA_PAD = 2 * LANES
RET_HEADS = 4
RET_QK = 256
RET_V = 256
ROPE_THETA = 10000.0
LN_EPS = 1e-5
RMS_EPS = 1e-6
GN_EPS = 1e-5
ADAM_LR = 0.001
ADAM_B1 = 0.9
ADAM_B2 = 0.999
ADAM_EPS = 1e-08
ADAM_WD = 0.01
ADAM_STEP = 10

LOG2_E = 1.4426950408889634

N_DEV = 8
MESH_AXES = ("x", "y", "c")
VMEM_LIMIT_BYTES = 56 * 1024 * 1024
MM_VMEM_BUDGET = 40 * 1024 * 1024
MM_ACC_PENALTY = 0.85
ATTN_BLOCK = 256
ATTN_BLOCK_FWD = 512
ATTN_KEY_STEP = 512
RET_BLOCK = 256


def _pick(n, pref, mult):
    best = None
    d = mult
    while d <= min(n, pref):
        if n % d == 0:
            best = d
        d += mult
    return n if best is None else best


def _divisors(n, mult, cap):
    ds = [d for d in range(mult, min(n, cap) + 1, mult) if n % d == 0]
    return ds or [n]


def _pcall(body, **kw):
    return pl.pallas_call(body, **kw)


def _params(sem):
    return pltpu.CompilerParams(dimension_semantics=sem, vmem_limit_bytes=VMEM_LIMIT_BYTES)


def _sds(shape, dtype):
    return jax.ShapeDtypeStruct(shape, dtype)


def _mm_tiles(M, N, K, ta, sa, sb, tile_bytes):
    best = None
    for bk in _divisors(K, 128, 8192):
        nk = K // bk
        for bm in _divisors(M, 128 if ta else 16, 1024):
            for bn in _divisors(N, 128, 1024):
                vmem = 2 * (bm * bk * sa + bk * bn * sb) + 2 * bm * bn * tile_bytes + (bm * bn * 4 if nk > 1 else 0)
                if vmem > MM_VMEM_BUDGET:
                    continue
                flops_per_byte = 1.0 / (1.0 / bm + (1.0 / max(N, bn) if nk == 1 else 1.0 / bn))
                score = (flops_per_byte * (1.0 if nk == 1 else MM_ACC_PENALTY), bk, bn, bm)
                if best is None or score > best[0]:
                    best = (score, bm, bn, bk)
    assert best is not None, (M, N, K)
    return best[1:]


def _mm(a, b, *, ta=False, tb=False, add=None, add_scale=1.0, out_dtype=F32, extras=(), epilogue=None,
        out_dtypes=None, job=None, name):
    if ta:
        K, M = a.shape
    else:
        M, K = a.shape
    if tb:
        N, K2 = b.shape
    else:
        K2, N = b.shape
    assert K == K2, (a.shape, b.shape, ta, tb)
    if epilogue is None:
        assert not extras and out_dtypes is None
        out_dtypes = (out_dtype,)
        if add is not None:
            extras, epilogue = (add,), lambda r, c: (r + add_scale * c,)
        else:
            epilogue = lambda r: (r,)
    n_ex, n_out = len(extras), len(out_dtypes)
    tile_bytes = sum(e.dtype.itemsize for e in extras) + sum(jnp.dtype(d).itemsize for d in out_dtypes)
    bm, bn, bk = _mm_tiles(M, N, K, ta, a.dtype.itemsize, b.dtype.itemsize, tile_bytes)
    nk = K // bk
    a_spec = (pl.BlockSpec((bk, bm), lambda i, j, k: (k, i)) if ta
              else pl.BlockSpec((bm, bk), lambda i, j, k: (i, k)))
    b_spec = (pl.BlockSpec((bn, bk), lambda i, j, k: (j, k)) if tb
              else pl.BlockSpec((bk, bn), lambda i, j, k: (k, j)))
    o_spec = pl.BlockSpec((bm, bn), lambda i, j, k: (i, j))
    dims = (((0 if ta else 1,), (1 if tb else 0,)), ((), ()))

    def body(*refs):
        a_ref, b_ref = refs[0], refs[1]
        ex_refs, o_refs = refs[2:2 + n_ex], refs[2 + n_ex:2 + n_ex + n_out]
        part = lax.dot_general(a_ref[...].astype(BF16), b_ref[...].astype(BF16), dims, preferred_element_type=F32)

        def finish(r):
            for o_ref, v in zip(o_refs, epilogue(r, *[e[...] for e in ex_refs])):
                o_ref[...] = v.astype(o_ref.dtype)

        if nk == 1:
            finish(part)
        else:
            acc_ref = refs[-1]
            k = pl.program_id(2)

            @pl.when(k == 0)
            def _():
                acc_ref[...] = part

            @pl.when(k > 0)
            def _():
                acc_ref[...] += part

            @pl.when(k == nk - 1)
            def _():
                finish(acc_ref[...])

    outs, jouts = _hosted_call(
        body, job, name=name, grid=(M // bm, N // bn, nk), in_specs=[a_spec, b_spec] + [o_spec] * n_ex,
        out_specs=[o_spec] * n_out, out_shape=[_sds((M, N), d) for d in out_dtypes],
        scratch_shapes=[pltpu.VMEM((bm, bn), F32)] if nk > 1 else [],
        semantics=("parallel", "parallel", "arbitrary"), args=(a, b, *extras))
    res = list(outs) + ([jouts] if job is not None else [])
    return res[0] if len(res) == 1 else tuple(res)


def _norm_fwd(x, g, b, *, res=None, alpha=1.0, center=True, eps, want_f32=True, want_bf16=True, col=0, name):
    S, W = x.shape[0], g.shape[-1]
    bs = _pick(S, 256, 16)
    has_res, has_b = res is not None, b is not None

    def body(*refs):
        it = iter(refs)
        x_ref = next(it)
        res_ref = next(it) if has_res else None
        g_ref = next(it)
        b_ref = next(it) if has_b else None
        y_ref = next(it) if want_f32 else None
        yb_ref = next(it) if want_bf16 else None
        xh_ref, r_ref = next(it), next(it)
        z = x_ref[...]
        if has_res:
            z = alpha * z + res_ref[...]
        if center:
            z = z - jnp.mean(z, axis=-1, keepdims=True)
        rstd = lax.rsqrt(jnp.mean(z * z, axis=-1, keepdims=True) + eps)
        xh = z * rstd
        y = xh * g_ref[...]
        if has_b:
            y = y + b_ref[...]
        if want_f32:
            y_ref[...] = y
        if want_bf16:
            yb_ref[...] = y.astype(BF16)
        xh_ref[...] = xh
        r_ref[...] = rstd

    row = pl.BlockSpec((bs, W), lambda i: (i, 0))
    vec = pl.BlockSpec((1, W), lambda i: (0, 0))
    in_specs = ([pl.BlockSpec((bs, W), lambda i: (i, col))] + ([row] if has_res else []) + [vec]
                + ([vec] if has_b else []))
    args = (x,) + ((res,) if has_res else ()) + (g.reshape(1, W),) + ((b.reshape(1, W),) if has_b else ())
    out_specs = ([row] if want_f32 else []) + ([row] if want_bf16 else []) + [row, pl.BlockSpec((bs, 1), lambda i: (i, 0))]
    out_shape = (([_sds((S, W), F32)] if want_f32 else []) + ([_sds((S, W), BF16)] if want_bf16 else [])
                 + [_sds((S, W), F32), _sds((S, 1), F32)])
    outs = list(_pcall(body, name=name, grid=(S // bs,), in_specs=in_specs, out_specs=out_specs, out_shape=out_shape,
                       compiler_params=_params(("parallel",)))(*args))
    y = outs.pop(0) if want_f32 else None
    yb = outs.pop(0) if want_bf16 else None
    return y, yb, outs[0], outs[1]


def _norm_bwd(dy, xh, rstd, g, *, center=True, want_f32=True, want_bf16=True, name):
    S, W = dy.shape
    bs = _pick(S, 256, 16)

    def body(*refs):
        dy_ref, xh_ref, r_ref, g_ref = refs[:4]
        it = iter(refs[4:])
        dz_ref = next(it) if want_f32 else None
        dzb_ref = next(it) if want_bf16 else None
        dg_ref, db_ref = next(it), next(it)

        @pl.when(pl.program_id(0) == 0)
        def _():
            dg_ref[...] = jnp.zeros_like(dg_ref)
            db_ref[...] = jnp.zeros_like(db_ref)

        dyv = dy_ref[...]
        xhv = xh_ref[...]
        dyg = dyv * g_ref[...]
        m2 = jnp.mean(dyg * xhv, axis=-1, keepdims=True)
        t = dyg - xhv * m2
        if center:
            t = t - jnp.mean(dyg, axis=-1, keepdims=True)
        dz = r_ref[...] * t
        if want_f32:
            dz_ref[...] = dz
        if want_bf16:
            dzb_ref[...] = dz.astype(BF16)
        dg_ref[...] += jnp.sum(dyv * xhv, axis=0, keepdims=True)
        db_ref[...] += jnp.sum(dyv, axis=0, keepdims=True)

    row = pl.BlockSpec((bs, W), lambda i: (i, 0))
    vec = pl.BlockSpec((1, W), lambda i: (0, 0))
    out_specs = ([row] if want_f32 else []) + ([row] if want_bf16 else []) + [vec, vec]
    out_shape = (([_sds((S, W), F32)] if want_f32 else []) + ([_sds((S, W), BF16)] if want_bf16 else [])
                 + [_sds((1, W), F32), _sds((1, W), F32)])
    outs = list(_pcall(body, name=name, grid=(S // bs,),
                       in_specs=[row, row, pl.BlockSpec((bs, 1), lambda i: (i, 0)), vec],
                       out_specs=out_specs, out_shape=out_shape,
                       compiler_params=_params(("arbitrary",)))(dy, xh, rstd, g.reshape(1, W)))
    dz = outs.pop(0) if want_f32 else None
    dzb = outs.pop(0) if want_bf16 else None
    return dz, dzb, outs[0][0], outs[1][0]


def _rot_group(t, c, s):
    half = MLA_ROPE // 2
    lane = lax.broadcasted_iota(jnp.int32, t.shape, 1)
    swapped = jnp.where(lane < half, pltpu.roll(t, LANES - half, 1), pltpu.roll(t, half, 1))
    return t * c + swapped * s


def _mla_prep(q3, kv3, krp, c128, s128, *, kr_col=0, name):
    H, S, DP = q3.shape
    bs = _pick(S, 2048, 16)

    def body(q_ref, kv_ref, kr_ref, c_ref, s_ref, qh_ref, kh_ref, vh_ref):
        c, s = c_ref[...], s_ref[...]
        q, kv = q_ref[...], kv_ref[...]
        qh_ref[...] = jnp.concatenate([q[:, :LANES], _rot_group(q[:, LANES:], c, s)], axis=1).astype(BF16)
        kh_ref[...] = jnp.concatenate([kv[:, :LANES], _rot_group(kr_ref[...], c, s)], axis=1).astype(BF16)
        vh_ref[...] = kv[:, LANES:].astype(BF16)

    head = pl.BlockSpec((None, bs, DP), lambda h, i: (h, i, 0))
    tab = pl.BlockSpec((bs, LANES), lambda h, i: (i, 0))
    return _pcall(
        body, name=name, grid=(H, S // bs),
        in_specs=[head, head, pl.BlockSpec((bs, LANES), lambda h, i: (i, kr_col)), tab, tab],
        out_specs=[head, head, pl.BlockSpec((None, bs, LANES), lambda h, i: (h, i, 0))],
        out_shape=[_sds((H, S, DP), BF16), _sds((H, S, DP), BF16), _sds((H, S, LANES), BF16)],
        compiler_params=_params(("parallel", "parallel")),
    )(q3, kv3, krp, c128, s128)


def _mla_unprep(dqh, dkh, dvh, c128, s128, *, name):
    H, S, DP = dqh.shape
    bs = _pick(S, 2048, 16)

    def body(dq_ref, dk_ref, dv_ref, c_ref, s_ref, oq_ref, okv_ref, okr_ref):
        c, ns = c_ref[...], -s_ref[...]
        dq, dk = dq_ref[...], dk_ref[...]
        oq_ref[...] = jnp.concatenate([dq[:, :LANES], _rot_group(dq[:, LANES:], c, ns)], axis=1).astype(BF16)
        okv_ref[...] = jnp.concatenate([dk[:, :LANES], dv_ref[...]], axis=1).astype(BF16)
        dkr = _rot_group(dk[:, LANES:], c, ns)

        @pl.when(pl.program_id(1) == 0)
        def _():
            okr_ref[...] = dkr

        @pl.when(pl.program_id(1) > 0)
        def _():
            okr_ref[...] += dkr

    head = pl.BlockSpec((None, bs, DP), lambda i, h: (h, i, 0))
    tab = pl.BlockSpec((bs, LANES), lambda i, h: (i, 0))
    return _pcall(
        body, name=name, grid=(S // bs, H),
        in_specs=[head, head, pl.BlockSpec((None, bs, LANES), lambda i, h: (h, i, 0)), tab, tab],
        out_specs=[head, head, tab],
        out_shape=[_sds((H, S, DP), BF16), _sds((H, S, DP), BF16), _sds((S, LANES), F32)],
        compiler_params=_params(("parallel", "arbitrary")),
    )(dqh, dkh, dvh, c128, s128)


def _mm_call(a, b, *, grid, a_spec, b_spec, o_spec, dims, nk, out_shape, name):
    def body(*refs):
        a_ref, b_ref, o_ref = refs[:3]
        part = lax.dot_general(a_ref[...].astype(BF16), b_ref[...].astype(BF16), dims, preferred_element_type=F32)
        if nk == 1:
            o_ref[...] = part.astype(o_ref.dtype)
        else:
            acc_ref = refs[3]
            k = pl.program_id(2)

            @pl.when(k == 0)
            def _():
                acc_ref[...] = part

            @pl.when(k > 0)
            def _():
                acc_ref[...] += part

            @pl.when(k == nk - 1)
            def _():
                o_ref[...] = acc_ref[...].astype(o_ref.dtype)

    return _pcall(body, name=name, grid=grid, in_specs=[a_spec, b_spec], out_specs=o_spec, out_shape=out_shape,
                  scratch_shapes=[pltpu.VMEM(o_spec.block_shape, F32)] if nk > 1 else [],
                  compiler_params=_params(("parallel", "parallel", "arbitrary")))(a, b)


def _mm_heads_out(a, w3, *, name):
    S, K = a.shape
    H, n, _ = w3.shape
    bm = _pick(S, 4096, 16)
    out = _mm_call(
        a, w3.reshape(H * n, K), grid=(S // bm, H, 1), nk=1, dims=(((1,), (1,)), ((), ())),
        a_spec=pl.BlockSpec((bm, K), lambda i, j, k: (i, 0)), b_spec=pl.BlockSpec((n, K), lambda i, j, k: (j, 0)),
        o_spec=pl.BlockSpec((bm, n), lambda i, j, k: (j * (S // bm) + i, 0)), out_shape=_sds((H * S, n), F32), name=name)
    return out.reshape(H, S, n)


def _mm_heads_sum(a3, w3, *, name):
    H, S, n = a3.shape
    N = w3.shape[2]
    bm, bn = _pick(S, 2048, 16), _pick(N, 512, 128)
    return _mm_call(
        a3.reshape(H * S, n), w3.reshape(H * n, N), grid=(S // bm, N // bn, H), nk=H, dims=(((1,), (0,)), ((), ())),
        a_spec=pl.BlockSpec((bm, n), lambda i, j, k: (k * (S // bm) + i, 0)),
        b_spec=pl.BlockSpec((n, bn), lambda i, j, k: (k, j)),
        o_spec=pl.BlockSpec((bm, bn), lambda i, j, k: (i, j)), out_shape=_sds((S, N), F32), name=name)


def _mm_heads_tn(a3, b, *, name):
    H, S, n = a3.shape
    N = b.shape[1]
    bk, bn = _pick(S, 4096, 128), _pick(N, 512, 128)
    nk = S // bk
    return _mm_call(
        a3.reshape(H * S, n), b, grid=(H, N // bn, nk), nk=nk, dims=(((0,), (0,)), ((), ())),
        a_spec=pl.BlockSpec((bk, n), lambda i, j, k: (i * nk + k, 0)),
        b_spec=pl.BlockSpec((bk, bn), lambda i, j, k: (k, j)),
        o_spec=pl.BlockSpec((n, bn), lambda i, j, k: (i, j)), out_shape=_sds((H * n, N), BF16), name=name)


def _chunk_mask(row0, col0, B, G):
    rows = lax.shift_right_logical(row0 + lax.broadcasted_iota(jnp.int32, (B, G), 0), CHUNK_SHIFT)
    cols = lax.shift_right_logical(col0 + lax.broadcasted_iota(jnp.int32, (B, G), 1), CHUNK_SHIFT)
    return rows >= cols


def _mask_tail(x, qi, B, G, fill):
    L = x.shape[1]
    tail = jnp.where(_chunk_mask(qi * B, L - G, B, G), x[:, L - G:], fill)
    return tail if L == G else jnp.concatenate([x[:, :L - G], tail], axis=1)


def _for_key_prefix(qi, B, G, S, fn):
    per = G // B
    for b in range(S // G):
        pl.when(qi // per == b)(lambda b=b: fn((b + 1) * G))


def _nt(a, b):
    return lax.dot_general(a, b, (((1,), (1,)), ((), ())), preferred_element_type=F32)


def _nn(a, b):
    return lax.dot_general(a, b, (((1,), (0,)), ((), ())), preferred_element_type=F32)


def _tn(a, b):
    return lax.dot_general(a, b, (((0,), (0,)), ((), ())), preferred_element_type=F32)


def _attn_fwd(qh, kh, vh, *, job=None, name):
    H, S, DQ = qh.shape
    DV = vh.shape[-1]
    B = _pick(S, ATTN_BLOCK_FWD, CHUNK)
    G = _pick(S, ATTN_KEY_STEP, B)
    scale = float(MLA_QK) ** -0.5
    neg = float(jnp.finfo(jnp.float32).min)

    def body(q_ref, k_ref, v_ref, o_ref, lse_ref):
        qi = pl.program_id(1)

        def run(L):
            raw = _mask_tail(_nt(q_ref[...], k_ref[0:L, :]), qi, B, G, neg)
            m = jnp.max(raw, axis=-1, keepdims=True)
            e = jnp.exp2((raw - m) * (scale * LOG2_E))
            l = jnp.sum(e, axis=-1, keepdims=True)
            o_ref[...] = _nn((e * (1.0 / l)).astype(BF16), v_ref[0:L, :])
            lse_ref[...] = m * scale + jnp.log(l)

        _for_key_prefix(qi, B, G, S, run)

    outs, jouts = _hosted_call(
        body, job, name=name, grid=(H, S // B),
        in_specs=[pl.BlockSpec((None, B, DQ), lambda h, i: (h, i, 0)),
                  pl.BlockSpec((None, S, DQ), lambda h, i: (h, 0, 0)),
                  pl.BlockSpec((None, S, DV), lambda h, i: (h, 0, 0))],
        out_specs=[pl.BlockSpec((B, DV), lambda h, i: (i, h)),
                   pl.BlockSpec((None, B, 1), lambda h, i: (h, i, 0))],
        out_shape=[_sds((S, H * DV), F32), _sds((H, S, 1), F32)], scratch_shapes=[],
        semantics=("parallel", "parallel"), args=(qh, kh, vh))
    return (outs[0], outs[1]) if job is None else (outs[0], outs[1], jouts)


def _attn_bwd(qh, kh, vh, o, do, lse, *, do_col0, job=None, name):
    H, S, DQ = qh.shape
    DV = vh.shape[-1]
    B = _pick(S, ATTN_BLOCK, CHUNK)
    G = _pick(S, ATTN_KEY_STEP, B)
    scale = float(MLA_QK) ** -0.5

    def body(q_ref, k_ref, v_ref, o_ref, do_ref, lse_ref, dq_ref, dk_ref, dv_ref):
        qi = pl.program_id(1)

        @pl.when(qi == 0)
        def _():
            dk_ref[...] = jnp.zeros_like(dk_ref)
            dv_ref[...] = jnp.zeros_like(dv_ref)

        def run(L):
            q = q_ref[...]
            k = k_ref[0:L, :]
            dov = do_ref[...]
            dob = dov.astype(BF16)
            p = jnp.exp2(_nt(q, k) * (scale * LOG2_E) - lse_ref[...] * LOG2_E)
            p = _mask_tail(p, qi, B, G, 0.0)
            dsum = jnp.sum(dov * o_ref[...], axis=-1, keepdims=True)
            ds = (p * (_nt(dob, v_ref[0:L, :]) - dsum)).astype(BF16)
            dv_ref[0:L, :] += _tn(p.astype(BF16), dob)
            dk_ref[0:L, :] += _tn(ds, q)
            dq_ref[...] = _nn(ds, k) * scale

        _for_key_prefix(qi, B, G, S, run)

        @pl.when(qi == S // B - 1)
        def _():
            dk_ref[...] = dk_ref[...] * scale

    outs, jouts = _hosted_call(
        body, job, name=name, grid=(H, S // B),
        in_specs=[pl.BlockSpec((None, B, DQ), lambda h, i: (h, i, 0)),
                  pl.BlockSpec((None, S, DQ), lambda h, i: (h, 0, 0)),
                  pl.BlockSpec((None, S, DV), lambda h, i: (h, 0, 0)),
                  pl.BlockSpec((B, DV), lambda h, i: (i, h)),
                  pl.BlockSpec((B, DV), lambda h, i: (i, do_col0 + h)),
                  pl.BlockSpec((None, B, 1), lambda h, i: (h, i, 0))],
        out_specs=[pl.BlockSpec((None, B, DQ), lambda h, i: (h, i, 0)),
                   pl.BlockSpec((None, S, DQ), lambda h, i: (h, 0, 0)),
                   pl.BlockSpec((None, S, DV), lambda h, i: (h, 0, 0))],
        out_shape=[_sds((H, S, DQ), F32), _sds((H, S, DQ), F32), _sds((H, S, DV), F32)], scratch_shapes=[],
        semantics=("parallel", "arbitrary"), args=(qh, kh, vh, o, do, lse))
    return tuple(outs) if job is None else (*outs, jouts)


def _ret_tables(T):
    H = RET_HEADS
    log_gamma = jnp.log1p(-jnp.exp2(-5.0 - jnp.arange(H, dtype=F32)))
    idx = jnp.arange(T, dtype=F32)
    chunk = jnp.arange(T) // CHUNK
    visible = chunk[:, None] >= chunk[None, :]
    intra = jnp.where(visible[None], jnp.exp(log_gamma[:, None, None] * jnp.abs(idx[:, None] - idx[None, :])), 0.0)
    qd = jnp.exp(log_gamma[:, None] * (idx + 1.0))[:, :, None]
    kd = jnp.exp(log_gamma[:, None] * (T - 1.0 - idx))[:, :, None]
    cd = jnp.exp(log_gamma * T)[:, None, None]
    return intra, qd, kd, cd


def _rot(t, c, s):
    half = t.shape[-1] // 2
    t1, t2 = t[:, :half], t[:, half:]
    return jnp.concatenate([t1 * c - t2 * s, t2 * c + t1 * s], axis=-1)


def _rot_t(t, c, s):
    half = t.shape[-1] // 2
    t1, t2 = t[:, :half], t[:, half:]
    return jnp.concatenate([t1 * c + t2 * s, t2 * c - t1 * s], axis=-1)


def _dot(a, b, ca, cb):
    return lax.dot_general(a.astype(BF16), b.astype(BF16), (((ca,), (cb,)), ((), ())), preferred_element_type=F32)


def _ret_fwd(hR, cos, sin, tables, *, name):
    S = hR.shape[0]
    H, L, DK, DV = RET_HEADS, tables[0].shape[-1], RET_QK, RET_V
    NC = S // L
    qscale = float(DK) ** -0.5
    intra, qd, kd, cd = tables

    def body(q_ref, k_ref, v_ref, c_ref, s_ref, in_ref, qd_ref, kd_ref, cd_ref, o_ref, st_ref, state):
        @pl.when(pl.program_id(1) == 0)
        def _():
            state[...] = jnp.zeros_like(state)

        c, s = c_ref[...], s_ref[...]
        q = _rot(q_ref[...], c, s) * qscale
        k = _rot(k_ref[...], c, s)
        v = v_ref[...]
        st = state[...]
        st_ref[...] = st.astype(BF16)
        scores = _dot(q, k, 1, 1) * in_ref[...]
        o_ref[...] = _dot(scores, v, 1, 0) + _dot(q * qd_ref[...], st, 1, 0)
        state[...] = st * cd_ref[...] + _dot(k * kd_ref[...], v, 0, 0)

    blk = lambda off: pl.BlockSpec((L, DK), lambda h, c: (c, off + h))
    rope = pl.BlockSpec((L, DK // 2), lambda h, c: (c, 0))
    return _pcall(
        body, name=name, grid=(H, NC),
        in_specs=[blk(0), blk(H), blk(2 * H), rope, rope,
                  pl.BlockSpec((None, L, L), lambda h, c: (h, 0, 0)),
                  pl.BlockSpec((None, L, 1), lambda h, c: (h, 0, 0)),
                  pl.BlockSpec((None, L, 1), lambda h, c: (h, 0, 0)),
                  pl.BlockSpec((None, 1, 1), lambda h, c: (h, 0, 0))],
        out_specs=[pl.BlockSpec((L, DV), lambda h, c: (c, h)),
                   pl.BlockSpec((None, None, DK, DV), lambda h, c: (h, c, 0, 0))],
        out_shape=[_sds((S, H * DV), F32), _sds((H, NC, DK, DV), BF16)],
        scratch_shapes=[pltpu.VMEM((DK, DV), F32)],
        compiler_params=_params(("arbitrary", "arbitrary")),
    )(hR, hR, hR, cos, sin, intra, qd, kd, cd)


def _ret_bwd(do, hR, states, cos, sin, tables, *, name):
    S = hR.shape[0]
    H, L, DK, DV = RET_HEADS, tables[0].shape[-1], RET_QK, RET_V
    NC = S // L
    qscale = float(DK) ** -0.5
    intra, qd, kd, cd = tables

    def body(do_ref, q_ref, k_ref, v_ref, st_ref, c_ref, s_ref, in_ref, qd_ref, kd_ref, cd_ref,
             dq_ref, dk_ref, dv_ref, dstate):
        @pl.when(pl.program_id(1) == 0)
        def _():
            dstate[...] = jnp.zeros_like(dstate)

        c, s = c_ref[...], s_ref[...]
        q = _rot(q_ref[...], c, s) * qscale
        k = _rot(k_ref[...], c, s)
        v = v_ref[...]
        dov = do_ref[...]
        dst = dstate[...]
        dec = in_ref[...]
        qdv, kdv = qd_ref[...], kd_ref[...]
        scores = _dot(q, k, 1, 1) * dec
        da = _dot(dov, v, 1, 1) * dec
        dv_ref[...] = (_dot(scores, dov, 0, 0) + _dot(k * kdv, dst, 1, 0)).astype(BF16)
        dq = _dot(da, k, 1, 0) + _dot(dov, st_ref[...], 1, 1) * qdv
        dk = _dot(da, q, 0, 0) + _dot(v, dst, 1, 1) * kdv
        dq_ref[...] = _rot_t(dq * qscale, c, s).astype(BF16)
        dk_ref[...] = _rot_t(dk, c, s).astype(BF16)
        dstate[...] = dst * cd_ref[...] + _dot(q * qdv, dov, 0, 0)

    rev = lambda c: NC - 1 - c
    blk = lambda off: pl.BlockSpec((L, DK), lambda h, c: (rev(c), off + h))
    rope = pl.BlockSpec((L, DK // 2), lambda h, c: (rev(c), 0))
    out = pl.BlockSpec((L, DK), lambda h, c: (rev(c), h))
    return _pcall(
        body, name=name, grid=(H, NC),
        in_specs=[out, blk(0), blk(H), blk(2 * H),
                  pl.BlockSpec((None, None, DK, DV), lambda h, c: (h, rev(c), 0, 0)), rope, rope,
                  pl.BlockSpec((None, L, L), lambda h, c: (h, 0, 0)),
                  pl.BlockSpec((None, L, 1), lambda h, c: (h, 0, 0)),
                  pl.BlockSpec((None, L, 1), lambda h, c: (h, 0, 0)),
                  pl.BlockSpec((None, 1, 1), lambda h, c: (h, 0, 0))],
        out_specs=[out, out, out],
        out_shape=[_sds((S, H * DK), BF16)] * 3,
        scratch_shapes=[pltpu.VMEM((DK, DV), F32)],
        compiler_params=_params(("arbitrary", "arbitrary")),
    )(do, hR, hR, hR, states, cos, sin, intra, qd, kd, cd)


def _gn_gate_fwd(o, hR, g, b, *, name):
    S = o.shape[0]
    H, DV = RET_HEADS, RET_V
    bs = _pick(S, 512, 16)

    def body(o_ref, rg_ref, g_ref, b_ref, r_ref):
        z = o_ref[...]
        z = z - jnp.mean(z, axis=-1, keepdims=True)
        xh = z * lax.rsqrt(jnp.mean(z * z, axis=-1, keepdims=True) + GN_EPS)
        rg = rg_ref[...]
        r_ref[...] = ((rg * jax.nn.sigmoid(rg)) * (xh * g_ref[...] + b_ref[...])).astype(BF16)

    row = pl.BlockSpec((bs, DV), lambda i, h: (i, h))
    vec = pl.BlockSpec((1, DV), lambda i, h: (0, h))
    return _pcall(
        body, name=name, grid=(S // bs, H),
        in_specs=[row, pl.BlockSpec((bs, DV), lambda i, h: (i, 3 * H + h)), vec, vec],
        out_specs=row, out_shape=_sds((S, H * DV), BF16),
        compiler_params=_params(("parallel", "parallel")),
    )(o, hR, g.reshape(1, H * DV), b.reshape(1, H * DV))


def _gn_gate_bwd(dr, o, hR, g, b, *, dr_col0, name):
    S = o.shape[0]
    H, DV = RET_HEADS, RET_V
    bs = _pick(S, 512, 16)

    def body(dr_ref, o_ref, rg_ref, g_ref, b_ref, do_ref, drg_ref, dg_ref, db_ref):
        @pl.when(pl.program_id(1) == 0)
        def _():
            dg_ref[...] = jnp.zeros_like(dg_ref)
            db_ref[...] = jnp.zeros_like(db_ref)

        z = o_ref[...]
        z = z - jnp.mean(z, axis=-1, keepdims=True)
        rstd = lax.rsqrt(jnp.mean(z * z, axis=-1, keepdims=True) + GN_EPS)
        xh = z * rstd
        gv = g_ref[...]
        y = xh * gv + b_ref[...]
        rg = rg_ref[...]
        sg = jax.nn.sigmoid(rg)
        drv = dr_ref[...]
        dy = drv * (rg * sg)
        drg_ref[...] = (drv * y * (sg * (1.0 + rg * (1.0 - sg)))).astype(BF16)
        dg_ref[...] += jnp.sum(dy * xh, axis=0, keepdims=True)
        db_ref[...] += jnp.sum(dy, axis=0, keepdims=True)
        dxh = dy * gv
        do_ref[...] = rstd * (dxh - jnp.mean(dxh, axis=-1, keepdims=True)
                              - xh * jnp.mean(dxh * xh, axis=-1, keepdims=True))

    row = pl.BlockSpec((bs, DV), lambda h, i: (i, h))
    vec = pl.BlockSpec((1, DV), lambda h, i: (0, h))
    do, drg, dg, db = _pcall(
        body, name=name, grid=(H, S // bs),
        in_specs=[pl.BlockSpec((bs, DV), lambda h, i: (i, dr_col0 + h)), row,
                  pl.BlockSpec((bs, DV), lambda h, i: (i, 3 * H + h)), vec, vec],
        out_specs=[row, row, vec, vec],
        out_shape=[_sds((S, H * DV), F32), _sds((S, H * DV), BF16), _sds((1, H * DV), F32), _sds((1, H * DV), F32)],
        compiler_params=_params(("arbitrary", "arbitrary")),
    )(dr, o, hR, g.reshape(1, H * DV), b.reshape(1, H * DV))
    return do, drg, dg[0], db[0]


def _gate_up(xb, wg, wu, *, job=None, name):
    S, D = xb.shape
    F = wg.shape[0]
    bm, bn = _pick(S, 1024, 16), _pick(F, 512, 128)

    def body(x_ref, g_ref, u_ref, og_ref, ou_ref, oa_ref):
        x = x_ref[...]
        g = _nt(x, g_ref[...])
        u = _nt(x, u_ref[...])
        og_ref[...] = g.astype(BF16)
        ou_ref[...] = u.astype(BF16)
        oa_ref[...] = ((g * jax.nn.sigmoid(g)) * u).astype(BF16)

    blk = pl.BlockSpec((bm, bn), lambda i, j: (i, j))
    wspec = pl.BlockSpec((bn, D), lambda i, j: (j, 0))
    outs, jouts = _hosted_call(
        body, job, name=name, grid=(S // bm, F // bn),
        in_specs=[pl.BlockSpec((bm, D), lambda i, j: (i, 0)), wspec, wspec], out_specs=[blk] * 3,
        out_shape=[_sds((S, F), BF16)] * 3, scratch_shapes=[], semantics=("parallel", "parallel"), args=(xb, wg, wu))
    return tuple(outs) if job is None else (*outs, jouts)


def _swiglu_bwd(da, g, u):
    g, u = g.astype(F32), u.astype(F32)
    sg = jax.nn.sigmoid(g)
    return da * u * (sg * (1.0 + g * (1.0 - sg))), da * (g * sg)


def _loss_head(y, t, *, name):
    S, D = y.shape
    bs = _pick(S, 256, 8)
    inv_d = 1.0 / D

    def body(y_ref, t_ref, dy_ref, l_ref):
        @pl.when(pl.program_id(0) == 0)
        def _():
            l_ref[...] = jnp.zeros_like(l_ref)

        e = y_ref[...] - t_ref[...]
        dy_ref[...] = e * inv_d
        l_ref[...] += 0.5 * jnp.sum(jnp.mean(e * e, axis=-1, keepdims=True), axis=0, keepdims=True)

    row = pl.BlockSpec((bs, D), lambda i: (i, 0))
    dy, l = _pcall(
        body, name=name, grid=(S // bs,), in_specs=[row, row],
        out_specs=[row, pl.BlockSpec((1, 1), lambda i: (0, 0))],
        out_shape=[_sds((S, D), F32), _sds((1, 1), F32)],
        compiler_params=_params(("arbitrary",)),
    )(y, t)
    return dy, l[0, 0]


def _adamw(w, g, m, v, *, name):
    shape = w.shape
    C = shape[-1]
    R = w.size // C
    br = _pick(R, 512, 8)

    def body(w_ref, g_ref, m_ref, v_ref, d_ref, nm_ref, nv_ref):
        gv = g_ref[...]
        mn = ADAM_B1 * m_ref[...] + (1.0 - ADAM_B1) * gv
        vn = ADAM_B2 * v_ref[...] + (1.0 - ADAM_B2) * (gv * gv)
        m_hat = mn / (1.0 - ADAM_B1 ** ADAM_STEP)
        v_hat = vn / (1.0 - ADAM_B2 ** ADAM_STEP)
        d_ref[...] = -ADAM_LR * (m_hat / (jnp.sqrt(v_hat) + ADAM_EPS) + ADAM_WD * w_ref[...])
        nm_ref[...] = mn
        nv_ref[...] = vn

    blk = pl.BlockSpec((br, C), lambda i: (i, 0))
    outs = _pcall(body, name=name, grid=(R // br,), in_specs=[blk] * 4, out_specs=[blk] * 3,
                  out_shape=[_sds((R, C), F32)] * 3,
                  compiler_params=_params(("parallel",)))(*[a.reshape(R, C) for a in (w, g, m, v)])
    return tuple(o.reshape(shape) for o in outs)


def _slot_block(n, k):
    return (_pick(n, 1024, 16) if n % 16 == 0 else n), _pick(k, 1024, 128)


def _sum_slots(x, *, name):
    ns, n, k = x.shape
    br, bc = _slot_block(n, k)

    def body(x_ref, o_ref):
        acc = x_ref[0].astype(F32)
        for s in range(1, ns):
            acc = acc + x_ref[s].astype(F32)
        o_ref[...] = acc

    return _pcall(body, name=name, grid=(n // br, k // bc),
                  in_specs=[pl.BlockSpec((ns, br, bc), lambda i, j: (0, i, j))],
                  out_specs=pl.BlockSpec((br, bc), lambda i, j: (i, j)),
                  out_shape=_sds((n, k), F32), compiler_params=_params(("parallel", "parallel")))(x)


def _pair_sum(g, r, *, name):
    _, n, k = r.shape
    br, bc = _slot_block(n, k)
    core = lax.axis_index("c").astype(jnp.int32).reshape(1)

    def body(c_ref, g_ref, r_ref, o_ref):
        o_ref[...] = (g_ref[...].astype(F32) + r_ref[...].astype(F32)).astype(o_ref.dtype)

    blk = pl.BlockSpec((None, br, bc), lambda s, i, j, c_ref: (s, i, j))
    return _pcall(
        body, name=name, out_shape=_sds(r.shape, g.dtype),
        grid_spec=pltpu.PrefetchScalarGridSpec(
            num_scalar_prefetch=1, grid=(N_CHIP, n // br, k // bc),
            in_specs=[pl.BlockSpec((None, br, bc), lambda s, i, j, c_ref: (2 * s + c_ref[0], i, j)), blk],
            out_specs=blk),
        compiler_params=_params(("parallel", "parallel", "parallel")),
    )(core, g, r)


N_PEER = N_DEV - 1
N_CHIP = N_DEV // 2
HOST_TAIL_FRACTION = 8


def _coords():
    return lax.axis_index("x"), lax.axis_index("y"), lax.axis_index("c")


def _ag_phase(phase, x_refs, out_refs, send_sems, recv_sems, local_sems):
    n = len(x_refs)
    x, y, c = _coords()
    me, sibling = (x, y, c), (x, y, 1 - c)
    chips = [(1 - x, y), (x, 1 - y), (1 - x, 1 - y)]

    def copy(a, k, block, to, src=None):
        px, py, pc = block
        dst = out_refs[a].at[4 * px + 2 * py + pc]
        return pltpu.make_async_remote_copy(
            src_ref=dst if src is None else src, dst_ref=dst,
            send_sem=send_sems.at[a * N_PEER + k], recv_sem=recv_sems.at[a * N_PEER + k],
            device_id=to, device_id_type=pl.DeviceIdType.MESH)

    def local(a):
        return pltpu.make_async_copy(x_refs[a], out_refs[a].at[4 * x + 2 * y + c], local_sems.at[a])

    def first(a):
        return ([copy(a, 0, me, sibling, src=x_refs[a])]
                + [copy(a, 1 + j, me, (*chip, c), src=x_refs[a]) for j, chip in enumerate(chips)])

    def passed(a, j):
        return copy(a, 4 + j, (*chips[j], c), sibling)

    if phase == 0:
        for a in range(n):
            local(a).start()
            for cp in first(a):
                cp.start()
    elif phase == 1:
        for j in range(len(chips)):
            for a in range(n):
                copy(a, 1 + j, (*chips[j], c), me).wait_recv()
                passed(a, j).start()
    else:
        for a in range(n):
            copy(a, 0, sibling, me).wait_recv()
            for j in range(len(chips)):
                copy(a, 4 + j, (*chips[j], 1 - c), me).wait_recv()
        for a in range(n):
            for cp in first(a):
                cp.wait_send()
            for j in range(len(chips)):
                passed(a, j).wait_send()
            local(a).wait()


def _ag1_phase(phase, x_refs, out_refs, send_sems, recv_sems, local_sems):
    n = len(x_refs)
    x, y, c = _coords()
    chips = [(1 - x, y), (x, 1 - y), (1 - x, 1 - y)]

    def local(a):
        return pltpu.make_async_copy(x_refs[a], out_refs[a].at[4 * x + 2 * y + c], local_sems.at[a])

    def pair(a, j):
        px, py = chips[j]
        sem = a * N_PEER + j
        send = pltpu.make_async_remote_copy(
            src_ref=x_refs[a], dst_ref=out_refs[a].at[4 * x + 2 * y + c], send_sem=send_sems.at[sem],
            recv_sem=recv_sems.at[sem], device_id=(px, py, c), device_id_type=pl.DeviceIdType.MESH)
        recv = pltpu.make_async_remote_copy(
            src_ref=x_refs[a], dst_ref=out_refs[a].at[4 * px + 2 * py + c], send_sem=send_sems.at[sem],
            recv_sem=recv_sems.at[sem], device_id=(px, py, c), device_id_type=pl.DeviceIdType.MESH)
        return send, recv

    if phase == 0:
        for a in range(n):
            local(a).start()
            for j in range(len(chips)):
                pair(a, j)[0].start()
    elif phase == 2:
        for a in range(n):
            for j in range(len(chips)):
                pair(a, j)[1].wait_recv()
        for a in range(n):
            for j in range(len(chips)):
                pair(a, j)[0].wait_send()
            local(a).wait()


def _ag2_phase(phase, buf_refs, out_refs, send_sems, recv_sems, local_sems):
    n = len(out_refs)
    x, y, c = _coords()

    def copy(a, i):
        return pltpu.make_async_remote_copy(
            src_ref=out_refs[a].at[2 * i + c], dst_ref=out_refs[a].at[2 * i + c],
            send_sem=send_sems.at[a * N_PEER + i], recv_sem=recv_sems.at[a * N_PEER + i],
            device_id=(x, y, 1 - c), device_id_type=pl.DeviceIdType.MESH)

    def arrival(a, i):
        return pltpu.make_async_remote_copy(
            src_ref=out_refs[a].at[2 * i + (1 - c)], dst_ref=out_refs[a].at[2 * i + (1 - c)],
            send_sem=send_sems.at[a * N_PEER + i], recv_sem=recv_sems.at[a * N_PEER + i],
            device_id=(x, y, 1 - c), device_id_type=pl.DeviceIdType.MESH)

    if phase == 0:
        for a in range(n):
            for i in range(N_CHIP):
                copy(a, i).start()
    elif phase == 2:
        for a in range(n):
            for i in range(N_CHIP):
                arrival(a, i).wait_recv()
        for a in range(n):
            for i in range(N_CHIP):
                copy(a, i).wait_send()


def _pair_phase(phase, g_refs, out_refs, send_sems, recv_sems, local_sems):
    n = len(g_refs)
    x, y, c = _coords()

    def copy(a, i):
        return pltpu.make_async_remote_copy(
            src_ref=g_refs[a].at[2 * i + (1 - c)], dst_ref=out_refs[a].at[i],
            send_sem=send_sems.at[a * N_PEER + i], recv_sem=recv_sems.at[a * N_PEER + i],
            device_id=(x, y, 1 - c), device_id_type=pl.DeviceIdType.MESH)

    if phase == 0:
        for a in range(n):
            for i in range(N_CHIP):
                copy(a, i).start()
    elif phase == 2:
        for a in range(n):
            for i in range(N_CHIP):
                copy(a, i).wait_recv()
        for a in range(n):
            for i in range(N_CHIP):
                copy(a, i).wait_send()


def _cross_phase(phase, p_refs, out_refs, send_sems, recv_sems, local_sems):
    n = len(p_refs)
    x, y, c = _coords()
    mine = 2 * x + y

    def local(a):
        return pltpu.make_async_copy(p_refs[a].at[mine], out_refs[a].at[mine], local_sems.at[a])

    def pair(a, k):
        px = 1 - x if k & 2 else x
        py = 1 - y if k & 1 else y
        peer = 2 * px + py
        sem = a * N_PEER + k - 1
        send = pltpu.make_async_remote_copy(
            src_ref=p_refs[a].at[peer], dst_ref=out_refs[a].at[mine], send_sem=send_sems.at[sem],
            recv_sem=recv_sems.at[sem], device_id=(px, py, c), device_id_type=pl.DeviceIdType.MESH)
        recv = pltpu.make_async_remote_copy(
            src_ref=p_refs[a].at[peer], dst_ref=out_refs[a].at[peer], send_sem=send_sems.at[sem],
            recv_sem=recv_sems.at[sem], device_id=(px, py, c), device_id_type=pl.DeviceIdType.MESH)
        return send, recv

    if phase == 0:
        for a in range(n):
            local(a).start()
        for k in range(1, N_CHIP):
            for a in range(n):
                pair(a, k)[0].start()
    elif phase == 2:
        for k in range(1, N_CHIP):
            for a in range(n):
                pair(a, k)[1].wait_recv()
        for k in range(1, N_CHIP):
            for a in range(n):
                pair(a, k)[0].wait_send()
        for a in range(n):
            local(a).wait()


_PHASES = {"ag": _ag_phase, "ag1": _ag1_phase, "ag2": _ag2_phase, "pair": _pair_phase, "cross": _cross_phase}


def _job(kind, arrays):
    arrays = list(arrays)
    if kind in ("ag", "ag1"):
        shapes = [(N_DEV,) + a.shape for a in arrays]
    elif kind == "pair":
        shapes = [(N_CHIP,) + a.shape[1:] for a in arrays]
    else:
        shapes = [a.shape for a in arrays]
    return dict(parts=[(kind, len(arrays))], ins=arrays, outs=[_sds(s, a.dtype) for s, a in zip(shapes, arrays)],
                in_place=[kind == "ag2"] * len(arrays))


def _join(jobs):
    jobs = [j for j in jobs if j is not None]
    if not jobs:
        return None
    return dict(parts=[p for j in jobs for p in j["parts"]], ins=[a for j in jobs for a in j["ins"]],
                outs=[o for j in jobs for o in j["outs"]], in_place=[f for j in jobs for f in j["in_place"]])


def _hosted_call(body, job, *, name, grid, in_specs, out_specs, out_shape, scratch_shapes, semantics, args):
    if job is None:
        outs = _pcall(body, name=name, grid=grid, in_specs=in_specs, out_specs=out_specs, out_shape=out_shape,
                      scratch_shapes=scratch_shapes, compiler_params=_params(semantics))(*args)
        return outs, []
    n_in, n_out, n_scr, nj = len(in_specs), len(out_specs), len(scratch_shapes), len(job["ins"])
    parts = job["parts"]
    total = 1
    for g in grid:
        total *= g
    late = total - max(total // HOST_TAIL_FRACTION, 1) if total > 1 else 0

    def wrapped(*refs):
        ins, refs = refs[:n_in], refs[n_in:]
        jins, refs = refs[:nj], refs[nj:]
        outs, refs = refs[:n_out], refs[n_out:]
        jouts, refs = refs[:nj], refs[nj:]
        scr, sems = refs[:n_scr], refs[n_scr:]
        step = 0
        for d, g in enumerate(grid):
            step = step * g + pl.program_id(d)

        def run(phase):
            off = 0
            for i, (kind, n) in enumerate(parts):
                _PHASES[kind](phase, jins[off:off + n], jouts[off:off + n], *sems[3 * i:3 * i + 3])
                off += n

        pl.when(step == 0)(lambda: run(0))
        pl.when(step == late)(lambda: run(1))
        body(*ins, *outs, *scr)
        pl.when(step == total - 1)(lambda: run(2))

    any_spec = pl.BlockSpec(memory_space=pl.ANY)
    sems = []
    for kind, n in parts:
        sems += [pltpu.SemaphoreType.DMA((n * N_PEER,)), pltpu.SemaphoreType.DMA((n * N_PEER,)),
                 pltpu.SemaphoreType.DMA((n,))]
    outs = _pcall(
        wrapped, name=name, grid=grid, in_specs=list(in_specs) + [any_spec] * nj,
        out_specs=list(out_specs) + [any_spec] * nj, out_shape=list(out_shape) + job["outs"],
        scratch_shapes=list(scratch_shapes) + sems,
        input_output_aliases={n_in + i: n_out + i for i in range(nj) if job["in_place"][i]},
        compiler_params=_params(("arbitrary",) * len(grid)),
    )(*args, *job["ins"])
    return outs[:n_out], outs[n_out:]


def _exchange(job, *, name):
    outs, jouts = _hosted_call(lambda: None, job, name=name, grid=(1,), in_specs=[], out_specs=[], out_shape=[],
                               scratch_shapes=[], semantics=("arbitrary",), args=())
    return jouts


_TRANSPOSED = {"w_in": True, "w_out": False, "w_gate": True, "w_up": True, "w_down": False, "w_uq": True, "w_ukv": True}
_BIG = tuple(_TRANSPOSED)
_SMALL = ("ln_in_g", "ln_in_b", "q_norm_g", "kv_norm_g", "ret_gn_g", "ret_gn_b", "ln1_g", "ln1_b", "ln2_g", "ln2_b")
SMALL_COLS = 128


def _rope_tables(pos, dim):
    inv_freq = ROPE_THETA ** (-jnp.arange(0, dim, 2, dtype=F32) / dim)
    ang = pos.astype(F32)[:, None] * inv_freq
    return jnp.cos(ang), jnp.sin(ang)


def _split_in(wt_in):
    c = MLA_Q_LORA + MLA_KV_LORA + MLA_ROPE
    return jnp.pad(wt_in[:c], ((0, LANES - MLA_ROPE), (0, 0))), wt_in[c:]


def _pad_heads(wt_uq):
    H = MLA_HEADS
    w3 = wt_uq.reshape(H, MLA_QK, wt_uq.shape[1])
    return jnp.pad(w3, ((0, 0), (0, MLA_PAD - MLA_QK), (0, 0)))


class _Plan:
    FIRST = [("ag", ((0, "w_in"),))]
    LAST = [("cross", ((0, "w_in"),))]
    _NEXT_IN = ((1, "w_in"), (1, "w_uq"), (1, "w_ukv"))
    HOSTS = {
        "l0_hA": [("ag", ((0, "w_uq"), (0, "w_ukv")))],
        "l0_hR": [("ag1", ((0, "w_out"),))],
        "l0_attn": [("ag2", ((0, "w_out"),)), ("ag1", ((0, "w_gate"), (0, "w_up")))],
        "l0_out": [("ag2", ((0, "w_gate"), (0, "w_up")))],
        "l0_gateup": [("ag", ((0, "w_down"),)), ("ag1", _NEXT_IN)],
        "l0_down": [("ag2", _NEXT_IN), ("ag1", ((1, "w_gate"),))],
        "l1_hR": [("ag2", ((1, "w_gate"),)), ("ag1", ((1, "w_out"),))],
        "l1_attn": [("ag2", ((1, "w_out"),)), ("ag1", ((1, "w_up"),))],
        "l1_out": [("ag2", ((1, "w_up"),))],
        "l1_gateup": [("ag", ((1, "w_down"),))],
        "l0_b_dact": [("cross", ((1, "w_in"),))]}
    for _l in (0, 1):
        HOSTS.update({
            "l%d_b_wgate" % _l: [("pair", ((_l, "w_down"),))],
            "l%d_b_wup" % _l: [("pair", ((_l, "w_gate"),)), ("cross", ((_l, "w_down"),))],
            "l%d_b_dx1a" % _l: [("pair", ((_l, "w_up"),)), ("cross", ((_l, "w_gate"),))],
            "l%d_b_dx1b" % _l: [("cross", ((_l, "w_up"),))],
            "l%d_b_attn" % _l: [("pair", ((_l, "w_out"),))],
            "l%d_b_win" % _l: [("pair", ((_l, "w_uq"), (_l, "w_ukv"))), ("cross", ((_l, "w_out"),))],
            "l%d_b_dx" % _l: [("pair", ((_l, "w_in"),)), ("cross", ((_l, "w_uq"), (_l, "w_ukv")))]})

    def __init__(self, local):
        self.local = local
        self.half = {}
        self.full = {}
        self.grads = {}
        self.paired = {}
        self.recv = {}

    def _by_device(self, k):
        return self.grads[k].reshape((N_DEV,) + self.local[k].shape)

    def _make(self, kind, keys):
        if kind in ("ag", "ag1"):
            return _job(kind, [self.local[k] for k in keys])
        if kind == "ag2":
            return _job(kind, [self.half[k] for k in keys])
        if kind == "pair":
            return _job(kind, [self._by_device(k) for k in keys])
        return _job(kind, [_pair_sum(self._by_device(k), self.paired[k], name="pairsum_l%d_%s" % k) for k in keys])

    def _done(self, kind, keys, outs):
        for k, o in zip(keys, outs):
            if kind in ("ag", "ag2"):
                self.full[k] = o.reshape(N_DEV * o.shape[1], o.shape[2])
            elif kind == "ag1":
                self.half[k] = o
            elif kind == "pair":
                self.paired[k] = o
            else:
                self.recv[k] = o

    def _run(self, todo, call):
        outs = call(_join([self._make(kind, keys) for kind, keys in todo]))
        for kind, keys in todo:
            self._done(kind, keys, outs[:len(keys)])
            outs = outs[len(keys):]

    def gather_first(self):
        self._run(self.FIRST, lambda job: _exchange(job, name="ag_first"))

    def send_last(self):
        self._run(self.LAST, lambda job: _exchange(job, name="rs_last"))

    def call(self, fn, name, *args, **kw):
        if name not in self.HOSTS:
            return fn(*args, name=name, **kw)
        res = []

        def run(job):
            *outs, jouts = fn(*args, name=name, job=job, **kw)
            res.extend(outs)
            return jouts

        self._run(self.HOSTS[name], run)
        return res[0] if len(res) == 1 else tuple(res)


def _layer_fwd(x, xb, plan, p, tabs, l):
    S, D = x.shape
    H = MLA_HEADS
    nm = lambda s: "l%d_%s" % (l, s)
    w = lambda n: plan.full[(l, n)]
    mm = lambda name, *a, **kw: plan.call(_mm, nm(name), *a, **kw)
    wt_a, wt_r = _split_in(w("w_in"))
    hA = mm("hA", xb, wt_a, tb=True)
    hR = mm("hR", xb, wt_r, tb=True)
    _, qn, qn_hat, q_rstd = _norm_fwd(hA, p["q_norm_g"], None, center=False, eps=RMS_EPS, want_f32=False,
                                      col=0, name=nm("qnorm"))
    _, kvn, kvn_hat, kv_rstd = _norm_fwd(hA, p["kv_norm_g"], None, center=False, eps=RMS_EPS, want_f32=False,
                                         col=MLA_Q_LORA // MLA_KV_LORA, name=nm("kvnorm"))
    w_uq3 = _pad_heads(w("w_uq"))
    w_ukv3 = w("w_ukv").reshape(H, MLA_NOPE + MLA_V, -1)
    q3 = _mm_heads_out(qn, w_uq3, name=nm("uq"))
    kv3 = _mm_heads_out(kvn, w_ukv3, name=nm("ukv"))
    qh, kh, vh = _mla_prep(q3, kv3, hA, tabs["c128"], tabs["s128"],
                           kr_col=(MLA_Q_LORA + MLA_KV_LORA) // LANES, name=nm("rope"))
    a, lse = plan.call(_attn_fwd, nm("attn"), qh, kh, vh)
    o_ret, states = _ret_fwd(hR, tabs["cos_r"], tabs["sin_r"], tabs["ret"], name=nm("ret"))
    r = _gn_gate_fwd(o_ret, hR, p["ret_gn_g"], p["ret_gn_b"], name=nm("gn"))
    mix_in = jnp.concatenate([a.astype(BF16), r], axis=1)
    mix = mm("out", mix_in, w("w_out"))
    x1, x1b, x1_hat, rstd1 = _norm_fwd(x, p["ln1_g"], p["ln1_b"], res=mix, alpha=p["alpha"], eps=LN_EPS, name=nm("ln1"))
    gb, ub, act = plan.call(_gate_up, nm("gateup"), x1b, w("w_gate"), w("w_up"))
    f = mm("down", act, w("w_down"))
    x2, x2b, x2_hat, rstd2 = _norm_fwd(x1, p["ln2_g"], p["ln2_b"], res=f, alpha=p["alpha"], eps=LN_EPS, name=nm("ln2"))
    saved = dict(xb=xb, qn=qn, qn_hat=qn_hat, q_rstd=q_rstd, kvn=kvn, kvn_hat=kvn_hat, kv_rstd=kv_rstd,
                 qh=qh, kh=kh, vh=vh, a=a, lse=lse, hR=hR, o_ret=o_ret, states=states, mix_in=mix_in,
                 x1b=x1b, x1_hat=x1_hat, rstd1=rstd1, gb=gb, ub=ub, act=act, x2_hat=x2_hat, rstd2=rstd2)
    return x2, x2b, saved


def _layer_bwd(dx2, sv, plan, p, tabs, l):
    S, D = dx2.shape
    H = MLA_HEADS
    nm = lambda s: "l%d_b_%s" % (l, s)
    w = lambda n: plan.full[(l, n)]
    mm = lambda name, *a, **kw: plan.call(_mm, nm(name), *a, **kw)
    alpha = p["alpha"]
    gw, gp = plan.grads, {}
    dz2, dz2b, gp["ln2_g"], gp["ln2_b"] = _norm_bwd(dx2, sv["x2_hat"], sv["rstd2"], p["ln2_g"], name=nm("ln2"))
    dg, du = mm("dact", dz2b, w("w_down"), tb=True, extras=(sv["gb"], sv["ub"]), epilogue=_swiglu_bwd,
                out_dtypes=(BF16, BF16))
    gw[(l, "w_down")] = mm("wdown", sv["act"], dz2b, ta=True, out_dtype=BF16)
    gw[(l, "w_gate")] = mm("wgate", dg, sv["x1b"], ta=True, out_dtype=BF16)
    gw[(l, "w_up")] = mm("wup", du, sv["x1b"], ta=True, out_dtype=BF16)
    t = mm("dx1a", dg, w("w_gate"), add=dz2, add_scale=alpha)
    dx1 = mm("dx1b", du, w("w_up"), add=t)
    dz1, dz1b, gp["ln1_g"], gp["ln1_b"] = _norm_bwd(dx1, sv["x1_hat"], sv["rstd1"], p["ln1_g"], name=nm("ln1"))
    dmix = mm("dmix", dz1b, w("w_out"), tb=True)
    gw[(l, "w_out")] = mm("wout", sv["mix_in"], dz1b, ta=True, out_dtype=BF16)
    ret_col0 = (H * MLA_V) // RET_V
    do_ret, drg, gp["ret_gn_g"], gp["ret_gn_b"] = _gn_gate_bwd(
        dmix, sv["o_ret"], sv["hR"], p["ret_gn_g"], p["ret_gn_b"], dr_col0=ret_col0, name=nm("gn"))
    drq, drk, drv = _ret_bwd(do_ret, sv["hR"], sv["states"], tabs["cos_r"], tabs["sin_r"], tabs["ret"], name=nm("ret"))
    dqh, dkh, dvh = plan.call(_attn_bwd, nm("attn"), sv["qh"], sv["kh"], sv["vh"], sv["a"], dmix, sv["lse"], do_col0=0)
    dq3, dkv3, dkrp = _mla_unprep(dqh, dkh, dvh, tabs["c128"], tabs["s128"], name=nm("rope"))
    w_uq3 = _pad_heads(w("w_uq"))
    w_ukv3 = w("w_ukv").reshape(H, MLA_NOPE + MLA_V, -1)
    g_uq = _mm_heads_tn(dq3, sv["qn"], name=nm("wuq")).reshape(H, MLA_PAD, -1)
    gw[(l, "w_uq")] = g_uq[:, :MLA_QK].reshape(H * MLA_QK, -1)
    dqn = _mm_heads_sum(dq3, w_uq3, name=nm("dqn"))
    gw[(l, "w_ukv")] = _mm_heads_tn(dkv3, sv["kvn"], name=nm("wukv"))
    dkvn = _mm_heads_sum(dkv3, w_ukv3, name=nm("dkvn"))
    _, dcq, gp["q_norm_g"], _ = _norm_bwd(dqn, sv["qn_hat"], sv["q_rstd"], p["q_norm_g"], center=False,
                                          want_f32=False, name=nm("qnorm"))
    _, dckv, gp["kv_norm_g"], _ = _norm_bwd(dkvn, sv["kvn_hat"], sv["kv_rstd"], p["kv_norm_g"], center=False,
                                            want_f32=False, name=nm("kvnorm"))
    dh = jnp.concatenate([dcq, dckv, dkrp[:, :MLA_ROPE].astype(BF16), drq, drk, drv, drg], axis=1)
    gw[(l, "w_in")] = mm("win", dh, sv["xb"], ta=True, out_dtype=BF16)
    dx = mm("dx", dh, w("w_in"), add=dz1, add_scale=alpha)
    return dx, gp


def _local_step(x, target, pos, small, plan, depth):
    alpha = (2 * depth) ** 0.25
    cos_m, sin_m = _rope_tables(pos, MLA_ROPE)
    cos_r, sin_r = _rope_tables(pos, RET_QK)
    zeros = jnp.zeros((x.shape[0], LANES - MLA_ROPE), F32)
    tabs = dict(c128=jnp.concatenate([cos_m, cos_m, zeros], axis=1), s128=jnp.concatenate([-sin_m, sin_m, zeros], axis=1),
                cos_r=cos_r, sin_r=sin_r, ret=_ret_tables(_pick(x.shape[0], RET_BLOCK, CHUNK)))
    h, hb, h_hat, h_rstd = _norm_fwd(x, small["ln_in_g"], small["ln_in_b"], eps=LN_EPS, name="ln_in")
    saved, ps = [], []
    for l in range(depth):
        p = {k: small[k][l] for k in _SMALL[2:]}
        p["alpha"] = alpha
        h, hb, sv = _layer_fwd(h, hb, plan, p, tabs, l)
        saved.append(sv)
        ps.append(p)
    dy, loss = _loss_head(h, target, name="loss")
    gps = [None] * depth
    for l in reversed(range(depth)):
        dy, gps[l] = _layer_bwd(dy, saved[l], plan, ps[l], tabs, l)
    grad_x, _, g_in_g, g_in_b = _norm_bwd(dy, h_hat, h_rstd, small["ln_in_g"], want_bf16=False, name="b_ln_in")
    gsmall = {"ln_in_g": g_in_g, "ln_in_b": g_in_b}
    for k in _SMALL[2:]:
        gsmall[k] = jnp.stack([gps[l][k] for l in range(depth)])
    return loss, grad_x, gsmall


def kernel(x, positions, ln_in_g, ln_in_b, w_in, q_norm_g, kv_norm_g, w_uq, w_ukv, ret_gn_g, ret_gn_b, w_out, ln1_g, ln1_b, w_gate, w_up, w_down, ln2_g, ln2_b, loss_target, m_ln_in_g, m_ln_in_b, m_w_in, m_q_norm_g, m_kv_norm_g, m_w_uq, m_w_ukv, m_ret_gn_g, m_ret_gn_b, m_w_out, m_ln1_g, m_ln1_b, m_w_gate, m_w_up, m_w_down, m_ln2_g, m_ln2_b, v_ln_in_g, v_ln_in_b, v_w_in, v_q_norm_g, v_kv_norm_g, v_w_uq, v_w_ukv, v_ret_gn_g, v_ret_gn_b, v_w_out, v_ln1_g, v_ln1_b, v_w_gate, v_w_up, v_w_down, v_ln2_g, v_ln2_b):
    names = ["ln_in_g", "ln_in_b", "w_in", "q_norm_g", "kv_norm_g", "w_uq", "w_ukv", "ret_gn_g", "ret_gn_b", "w_out",
             "ln1_g", "ln1_b", "w_gate", "w_up", "w_down", "ln2_g", "ln2_b"]
    wv = dict(zip(names, (ln_in_g, ln_in_b, w_in, q_norm_g, kv_norm_g, w_uq, w_ukv, ret_gn_g, ret_gn_b, w_out,
                          ln1_g, ln1_b, w_gate, w_up, w_down, ln2_g, ln2_b)))
    mv = dict(zip(names, (m_ln_in_g, m_ln_in_b, m_w_in, m_q_norm_g, m_kv_norm_g, m_w_uq, m_w_ukv, m_ret_gn_g,
                          m_ret_gn_b, m_w_out, m_ln1_g, m_ln1_b, m_w_gate, m_w_up, m_w_down, m_ln2_g, m_ln2_b)))
    vv = dict(zip(names, (v_ln_in_g, v_ln_in_b, v_w_in, v_q_norm_g, v_kv_norm_g, v_w_uq, v_w_ukv, v_ret_gn_g,
                          v_ret_gn_b, v_w_out, v_ln1_g, v_ln1_b, v_w_gate, v_w_up, v_w_down, v_ln2_g, v_ln2_b)))
    depth = w_in.shape[0]
    assert depth == 2, "the exchange plan is written for two layers"

    keys = [(l, n) for l in range(depth) for n in _BIG]
    plan = _Plan({(l, n): (wv[n][l].T if _TRANSPOSED[n] else wv[n][l]).astype(BF16) for l, n in keys})
    plan.gather_first()

    small = {n: wv[n] for n in _SMALL}
    loss, grad_x, gsmall = _local_step(x[0], loss_target[0], positions[0], small, plan, depth)
    loss = lax.psum(loss, MESH_AXES)
    plan.send_last()

    gshard = {n: [None] * depth for n in _BIG}
    for l, n in keys:
        tot = _sum_slots(plan.recv[(l, n)], name="sum_l%d_%s" % (l, n))
        gshard[n][l] = tot.T if _TRANSPOSED[n] else tot
    grads = {n: jnp.stack(v) for n, v in gshard.items()}

    flat = jnp.concatenate([gsmall[n].reshape(-1) for n in _SMALL])
    n_small = flat.shape[0]
    rows = -(-n_small // (SMALL_COLS * 8)) * 8
    flat = jnp.pad(flat, (0, rows * SMALL_COLS - n_small)).reshape(rows, SMALL_COLS)
    tot = _sum_slots(_exchange(_job("ag", [flat]), name="ag_small")[0], name="sum_small").reshape(-1)
    off = 0
    for n in _SMALL:
        grads[n] = tot[off:off + wv[n].size].reshape(wv[n].shape)
        off += wv[n].size

    delta, new_m, new_v = {}, {}, {}
    for n in names:
        w2 = wv[n] if wv[n].ndim > 1 else wv[n].reshape(1, -1)
        d, nm_, nv_ = _adamw(w2, grads[n].reshape(w2.shape), mv[n].reshape(w2.shape), vv[n].reshape(w2.shape),
                             name="adamw_" + n)
        delta[n], new_m[n], new_v[n] = d.reshape(wv[n].shape), nm_.reshape(wv[n].shape), nv_.reshape(wv[n].shape)

    return (loss, grad_x[None], *[grads[n] for n in names], *[delta[n] for n in names],
            *[new_m[n] for n in names], *[new_v[n] for n in names])
```

```python
import jax
import jax.numpy as jnp
from jax import lax
from jax.experimental import pallas as pl
from jax.experimental.pallas import tpu as pltpu

F32 = jnp.float32
BF16 = jnp.bfloat16

CHUNK = 64
CHUNK_SHIFT = 6
MLA_HEADS = 8
MLA_Q_LORA = 512
MLA_KV_LORA = 256
MLA_NOPE = 128
MLA_ROPE = 64
MLA_V = 128
MLA_QK = MLA_NOPE + MLA_ROPE
LANES = 128
MLA_PAD = 2 * LANES
RET_HEADS = 4
RET_QK = 256
RET_V = 256
ROPE_THETA = 10000.0
LN_EPS = 1e-5
RMS_EPS = 1e-6
GN_EPS = 1e-5
ADAM_LR = 0.001
ADAM_B1 = 0.9
ADAM_B2 = 0.999
ADAM_EPS = 1e-08
ADAM_WD = 0.01
ADAM_STEP = 10

LOG2_E = 1.4426950408889634

N_DEV = 8
MESH_AXES = ("x", "y", "c")
VMEM_LIMIT_BYTES = 56 * 1024 * 1024
MM_VMEM_BUDGET = 40 * 1024 * 1024
MM_ACC_PENALTY = 0.85
ATTN_BLOCK = 256
ATTN_BLOCK_FWD = 256
ATTN_KEY_STEP = 512
RET_BLOCK = 256


def _pick(n, pref, mult):
    best = None
    d = mult
    while d <= min(n, pref):
        if n % d == 0:
            best = d
        d += mult
    return n if best is None else best


def _divisors(n, mult, cap):
    ds = [d for d in range(mult, min(n, cap) + 1, mult) if n % d == 0]
    return ds or [n]


def _pcall(body, **kw):
    return pl.pallas_call(body, **kw)


def _params(sem):
    return pltpu.CompilerParams(dimension_semantics=sem, vmem_limit_bytes=VMEM_LIMIT_BYTES)


def _sds(shape, dtype):
    return jax.ShapeDtypeStruct(shape, dtype)


def _mm_tiles(M, N, K, ta, sa, sb, tile_bytes):
    best = None
    for bk in _divisors(K, 128, 8192):
        nk = K // bk
        for bm in _divisors(M, 128 if ta else 16, 1024):
            for bn in _divisors(N, 128, 1024):
                vmem = 2 * (bm * bk * sa + bk * bn * sb) + 2 * bm * bn * tile_bytes + (bm * bn * 4 if nk > 1 else 0)
                if vmem > MM_VMEM_BUDGET:
                    continue
                flops_per_byte = 1.0 / (1.0 / bm + (1.0 / max(N, bn) if nk == 1 else 1.0 / bn))
                score = (flops_per_byte * (1.0 if nk == 1 else MM_ACC_PENALTY), bk, bn, bm)
                if best is None or score > best[0]:
                    best = (score, bm, bn, bk)
    assert best is not None, (M, N, K)
    return best[1:]


def _mm(a, b, *, ta=False, tb=False, add=None, add_scale=1.0, out_dtype=F32, extras=(), epilogue=None,
        out_dtypes=None, job=None, name):
    if ta:
        K, M = a.shape
    else:
        M, K = a.shape
    if tb:
        N, K2 = b.shape
    else:
        K2, N = b.shape
    assert K == K2, (a.shape, b.shape, ta, tb)
    if epilogue is None:
        assert not extras and out_dtypes is None
        out_dtypes = (out_dtype,)
        if add is not None:
            extras, epilogue = (add,), lambda r, c: (r + add_scale * c,)
        else:
            epilogue = lambda r: (r,)
    n_ex, n_out = len(extras), len(out_dtypes)
    tile_bytes = sum(e.dtype.itemsize for e in extras) + sum(jnp.dtype(d).itemsize for d in out_dtypes)
    bm, bn, bk = _mm_tiles(M, N, K, ta, a.dtype.itemsize, b.dtype.itemsize, tile_bytes)
    nk = K // bk
    a_spec = (pl.BlockSpec((bk, bm), lambda i, j, k: (k, i)) if ta
              else pl.BlockSpec((bm, bk), lambda i, j, k: (i, k)))
    b_spec = (pl.BlockSpec((bn, bk), lambda i, j, k: (j, k)) if tb
              else pl.BlockSpec((bk, bn), lambda i, j, k: (k, j)))
    o_spec = pl.BlockSpec((bm, bn), lambda i, j, k: (i, j))
    dims = (((0 if ta else 1,), (1 if tb else 0,)), ((), ()))

    def body(*refs):
        a_ref, b_ref = refs[0], refs[1]
        ex_refs, o_refs = refs[2:2 + n_ex], refs[2 + n_ex:2 + n_ex + n_out]
        part = lax.dot_general(a_ref[...].astype(BF16), b_ref[...].astype(BF16), dims, preferred_element_type=F32)

        def finish(r):
            for o_ref, v in zip(o_refs, epilogue(r, *[e[...] for e in ex_refs])):
                o_ref[...] = v.astype(o_ref.dtype)

        if nk == 1:
            finish(part)
        else:
            acc_ref = refs[-1]
            k = pl.program_id(2)

            @pl.when(k == 0)
            def _():
                acc_ref[...] = part

            @pl.when(k > 0)
            def _():
                acc_ref[...] += part

            @pl.when(k == nk - 1)
            def _():
                finish(acc_ref[...])

    outs, jouts = _hosted_call(
        body, job, name=name, grid=(M // bm, N // bn, nk), in_specs=[a_spec, b_spec] + [o_spec] * n_ex,
        out_specs=[o_spec] * n_out, out_shape=[_sds((M, N), d) for d in out_dtypes],
        scratch_shapes=[pltpu.VMEM((bm, bn), F32)] if nk > 1 else [],
        semantics=("parallel", "parallel", "arbitrary"), args=(a, b, *extras))
    res = list(outs) + ([jouts] if job is not None else [])
    return res[0] if len(res) == 1 else tuple(res)


def _norm_fwd(x, g, b, *, res=None, alpha=1.0, center=True, eps, want_f32=True, want_bf16=True, col=0, job=None,
              name):
    S, W = x.shape[0], g.shape[-1]
    bs = _pick(S, 256, 16)
    has_res, has_b = res is not None, b is not None

    def body(*refs):
        it = iter(refs)
        x_ref = next(it)
        res_ref = next(it) if has_res else None
        g_ref = next(it)
        b_ref = next(it) if has_b else None
        y_ref = next(it) if want_f32 else None
        yb_ref = next(it) if want_bf16 else None
        xh_ref, r_ref = next(it), next(it)
        z = x_ref[...]
        if has_res:
            z = alpha * z + res_ref[...]
        if center:
            z = z - jnp.mean(z, axis=-1, keepdims=True)
        rstd = lax.rsqrt(jnp.mean(z * z, axis=-1, keepdims=True) + eps)
        xh = z * rstd
        y = xh * g_ref[...]
        if has_b:
            y = y + b_ref[...]
        if want_f32:
            y_ref[...] = y
        if want_bf16:
            yb_ref[...] = y.astype(BF16)
        xh_ref[...] = xh
        r_ref[...] = rstd

    row = pl.BlockSpec((bs, W), lambda i: (i, 0))
    vec = pl.BlockSpec((1, W), lambda i: (0, 0))
    in_specs = ([pl.BlockSpec((bs, W), lambda i: (i, col))] + ([row] if has_res else []) + [vec]
                + ([vec] if has_b else []))
    args = (x,) + ((res,) if has_res else ()) + (g.reshape(1, W),) + ((b.reshape(1, W),) if has_b else ())
    out_specs = ([row] if want_f32 else []) + ([row] if want_bf16 else []) + [row, pl.BlockSpec((bs, 1), lambda i: (i, 0))]
    out_shape = (([_sds((S, W), F32)] if want_f32 else []) + ([_sds((S, W), BF16)] if want_bf16 else [])
                 + [_sds((S, W), F32), _sds((S, 1), F32)])
    outs, jouts = _hosted_call(body, job, name=name, grid=(S // bs,), in_specs=in_specs, out_specs=out_specs,
                               out_shape=out_shape, scratch_shapes=[], semantics=("parallel",), args=args)
    outs = list(outs)
    y = outs.pop(0) if want_f32 else None
    yb = outs.pop(0) if want_bf16 else None
    return (y, yb, outs[0], outs[1]) + (() if job is None else (jouts,))


def _norm_bwd(dy, xh, rstd, g, *, center=True, want_f32=True, want_bf16=True, job=None, name):
    S, W = dy.shape
    bs = _pick(S, 256, 16)

    def body(*refs):
        dy_ref, xh_ref, r_ref, g_ref = refs[:4]
        it = iter(refs[4:])
        dz_ref = next(it) if want_f32 else None
        dzb_ref = next(it) if want_bf16 else None
        dg_ref, db_ref = next(it), next(it)

        @pl.when(pl.program_id(0) == 0)
        def _():
            dg_ref[...] = jnp.zeros_like(dg_ref)
            db_ref[...] = jnp.zeros_like(db_ref)

        dyv = dy_ref[...]
        xhv = xh_ref[...]
        dyg = dyv * g_ref[...]
        m2 = jnp.mean(dyg * xhv, axis=-1, keepdims=True)
        t = dyg - xhv * m2
        if center:
            t = t - jnp.mean(dyg, axis=-1, keepdims=True)
        dz = r_ref[...] * t
        if want_f32:
            dz_ref[...] = dz
        if want_bf16:
            dzb_ref[...] = dz.astype(BF16)
        dg_ref[...] += jnp.sum(dyv * xhv, axis=0, keepdims=True)
        db_ref[...] += jnp.sum(dyv, axis=0, keepdims=True)

    row = pl.BlockSpec((bs, W), lambda i: (i, 0))
    vec = pl.BlockSpec((1, W), lambda i: (0, 0))
    out_specs = ([row] if want_f32 else []) + ([row] if want_bf16 else []) + [vec, vec]
    out_shape = (([_sds((S, W), F32)] if want_f32 else []) + ([_sds((S, W), BF16)] if want_bf16 else [])
                 + [_sds((1, W), F32), _sds((1, W), F32)])
    outs, jouts = _hosted_call(body, job, name=name, grid=(S // bs,),
                               in_specs=[row, row, pl.BlockSpec((bs, 1), lambda i: (i, 0)), vec],
                               out_specs=out_specs, out_shape=out_shape, scratch_shapes=[], semantics=("arbitrary",),
                               args=(dy, xh, rstd, g.reshape(1, W)))
    outs = list(outs)
    dz = outs.pop(0) if want_f32 else None
    dzb = outs.pop(0) if want_bf16 else None
    return (dz, dzb, outs[0][0], outs[1][0]) + (() if job is None else (jouts,))


def _rot_group(t, c, s):
    half = MLA_ROPE // 2
    lane = lax.broadcasted_iota(jnp.int32, t.shape, 1)
    swapped = jnp.where(lane < half, pltpu.roll(t, LANES - half, 1), pltpu.roll(t, half, 1))
    return t * c + swapped * s


def _mla_prep(q3, kv3, krp, c128, s128, *, kr_col=0, name):
    H, S, DP = q3.shape
    bs = _pick(S, 2048, 16)

    def body(q_ref, kv_ref, kr_ref, c_ref, s_ref, qh_ref, kh_ref, vh_ref):
        c, s = c_ref[...], s_ref[...]
        q, kv = q_ref[...], kv_ref[...]
        qh_ref[...] = jnp.concatenate([q[:, :LANES], _rot_group(q[:, LANES:], c, s)], axis=1).astype(BF16)
        kh_ref[...] = jnp.concatenate([kv[:, :LANES], _rot_group(kr_ref[...], c, s)], axis=1).astype(BF16)
        vh_ref[...] = kv[:, LANES:].astype(BF16)

    head = pl.BlockSpec((None, bs, DP), lambda h, i: (h, i, 0))
    tab = pl.BlockSpec((bs, LANES), lambda h, i: (i, 0))
    return _pcall(
        body, name=name, grid=(H, S // bs),
        in_specs=[head, head, pl.BlockSpec((bs, LANES), lambda h, i: (i, kr_col)), tab, tab],
        out_specs=[head, head, pl.BlockSpec((None, bs, LANES), lambda h, i: (h, i, 0))],
        out_shape=[_sds((H, S, DP), BF16), _sds((H, S, DP), BF16), _sds((H, S, LANES), BF16)],
        compiler_params=_params(("parallel", "parallel")),
    )(q3, kv3, krp, c128, s128)


def _mla_unprep(dqh, dkh, dvh, c128, s128, *, name):
    H, S, DP = dqh.shape
    bs = _pick(S, 2048, 16)

    def body(dq_ref, dk_ref, dv_ref, c_ref, s_ref, oq_ref, okv_ref, okr_ref):
        c, ns = c_ref[...], -s_ref[...]
        dq, dk = dq_ref[...], dk_ref[...]
        oq_ref[...] = jnp.concatenate([dq[:, :LANES], _rot_group(dq[:, LANES:], c, ns)], axis=1).astype(BF16)
        okv_ref[...] = jnp.concatenate([dk[:, :LANES], dv_ref[...]], axis=1).astype(BF16)
        dkr = _rot_group(dk[:, LANES:], c, ns)

        @pl.when(pl.program_id(1) == 0)
        def _():
            okr_ref[...] = dkr

        @pl.when(pl.program_id(1) > 0)
        def _():
            okr_ref[...] += dkr

    head = pl.BlockSpec((None, bs, DP), lambda i, h: (h, i, 0))
    tab = pl.BlockSpec((bs, LANES), lambda i, h: (i, 0))
    return _pcall(
        body, name=name, grid=(S // bs, H),
        in_specs=[head, head, pl.BlockSpec((None, bs, LANES), lambda i, h: (h, i, 0)), tab, tab],
        out_specs=[head, head, tab],
        out_shape=[_sds((H, S, DP), BF16), _sds((H, S, DP), BF16), _sds((S, LANES), F32)],
        compiler_params=_params(("parallel", "arbitrary")),
    )(dqh, dkh, dvh, c128, s128)


def _mm_call(a, b, *, grid, a_spec, b_spec, o_spec, dims, nk, out_shape, name):
    def body(*refs):
        a_ref, b_ref, o_ref = refs[:3]
        part = lax.dot_general(a_ref[...].astype(BF16), b_ref[...].astype(BF16), dims, preferred_element_type=F32)
        if nk == 1:
            o_ref[...] = part.astype(o_ref.dtype)
        else:
            acc_ref = refs[3]
            k = pl.program_id(2)

            @pl.when(k == 0)
            def _():
                acc_ref[...] = part

            @pl.when(k > 0)
            def _():
                acc_ref[...] += part

            @pl.when(k == nk - 1)
            def _():
                o_ref[...] = acc_ref[...].astype(o_ref.dtype)

    return _pcall(body, name=name, grid=grid, in_specs=[a_spec, b_spec], out_specs=o_spec, out_shape=out_shape,
                  scratch_shapes=[pltpu.VMEM(o_spec.block_shape, F32)] if nk > 1 else [],
                  compiler_params=_params(("parallel", "parallel", "arbitrary")))(a, b)


def _mm_heads_out(a, w3, *, name):
    S, K = a.shape
    H, n, _ = w3.shape
    bm = _pick(S, 4096, 16)
    out = _mm_call(
        a, w3.reshape(H * n, K), grid=(S // bm, H, 1), nk=1, dims=(((1,), (1,)), ((), ())),
        a_spec=pl.BlockSpec((bm, K), lambda i, j, k: (i, 0)), b_spec=pl.BlockSpec((n, K), lambda i, j, k: (j, 0)),
        o_spec=pl.BlockSpec((bm, n), lambda i, j, k: (j * (S // bm) + i, 0)), out_shape=_sds((H * S, n), F32), name=name)
    return out.reshape(H, S, n)


def _mm_heads_sum(a3, w3, *, name):
    H, S, n = a3.shape
    N = w3.shape[2]
    bm, bn = _pick(S, 2048, 16), _pick(N, 512, 128)
    return _mm_call(
        a3.reshape(H * S, n), w3.reshape(H * n, N), grid=(S // bm, N // bn, H), nk=H, dims=(((1,), (0,)), ((), ())),
        a_spec=pl.BlockSpec((bm, n), lambda i, j, k: (k * (S // bm) + i, 0)),
        b_spec=pl.BlockSpec((n, bn), lambda i, j, k: (k, j)),
        o_spec=pl.BlockSpec((bm, bn), lambda i, j, k: (i, j)), out_shape=_sds((S, N), F32), name=name)


def _mm_heads_tn(a3, b, *, name):
    H, S, n = a3.shape
    N = b.shape[1]
    bk, bn = _pick(S, 4096, 128), _pick(N, 512, 128)
    nk = S // bk
    return _mm_call(
        a3.reshape(H * S, n), b, grid=(H, N // bn, nk), nk=nk, dims=(((0,), (0,)), ((), ())),
        a_spec=pl.BlockSpec((bk, n), lambda i, j, k: (i * nk + k, 0)),
        b_spec=pl.BlockSpec((bk, bn), lambda i, j, k: (k, j)),
        o_spec=pl.BlockSpec((n, bn), lambda i, j, k: (i, j)), out_shape=_sds((H * n, N), BF16), name=name)


def _chunk_mask(row0, col0, B, G):
    rows = lax.shift_right_logical(row0 + lax.broadcasted_iota(jnp.int32, (B, G), 0), CHUNK_SHIFT)
    cols = lax.shift_right_logical(col0 + lax.broadcasted_iota(jnp.int32, (B, G), 1), CHUNK_SHIFT)
    return rows >= cols


def _mask_tail(x, qi, B, G, fill):
    L = x.shape[1]
    tail = jnp.where(_chunk_mask(qi * B, L - G, B, G), x[:, L - G:], fill)
    return tail if L == G else jnp.concatenate([x[:, :L - G], tail], axis=1)


def _for_key_prefix(qi, B, G, S, fn):
    per = G // B
    for b in range(S // G):
        pl.when(qi // per == b)(lambda b=b: fn((b + 1) * G))


def _nt(a, b):
    return lax.dot_general(a, b, (((1,), (1,)), ((), ())), preferred_element_type=F32)


def _nn(a, b):
    return lax.dot_general(a, b, (((1,), (0,)), ((), ())), preferred_element_type=F32)


def _tn(a, b):
    return lax.dot_general(a, b, (((0,), (0,)), ((), ())), preferred_element_type=F32)


def _attn_fwd(qh, kh, vh, *, job=None, name):
    H, S, DQ = qh.shape
    DV = vh.shape[-1]
    B = _pick(S, ATTN_BLOCK_FWD, CHUNK)
    G = _pick(S, ATTN_KEY_STEP, B)
    scale = float(MLA_QK) ** -0.5
    neg = float(jnp.finfo(jnp.float32).min)

    def body(q_ref, k_ref, v_ref, o_ref, lse_ref):
        qi = pl.program_id(1)

        def run(L):
            raw = _mask_tail(_nt(q_ref[...], k_ref[0:L, :]), qi, B, G, neg)
            m = jnp.max(raw, axis=-1, keepdims=True)
            e = jnp.exp2((raw - m) * (scale * LOG2_E))
            l = jnp.sum(e, axis=-1, keepdims=True)
            o_ref[...] = _nn((e * (1.0 / l)).astype(BF16), v_ref[0:L, :])
            lse_ref[...] = m * scale + jnp.log(l)

        _for_key_prefix(qi, B, G, S, run)

    outs, jouts = _hosted_call(
        body, job, name=name, grid=(H, S // B),
        in_specs=[pl.BlockSpec((None, B, DQ), lambda h, i: (h, i, 0)),
                  pl.BlockSpec((None, S, DQ), lambda h, i: (h, 0, 0)),
                  pl.BlockSpec((None, S, DV), lambda h, i: (h, 0, 0))],
        out_specs=[pl.BlockSpec((B, DV), lambda h, i: (i, h)),
                   pl.BlockSpec((None, B, 1), lambda h, i: (h, i, 0))],
        out_shape=[_sds((S, H * DV), F32), _sds((H, S, 1), F32)], scratch_shapes=[],
        semantics=("parallel", "parallel"), args=(qh, kh, vh))
    return (outs[0], outs[1]) if job is None else (outs[0], outs[1], jouts)


def _attn_bwd(qh, kh, vh, o, do, lse, *, do_col0, job=None, name):
    H, S, DQ = qh.shape
    DV = vh.shape[-1]
    B = _pick(S, ATTN_BLOCK, CHUNK)
    G = _pick(S, ATTN_KEY_STEP, B)
    scale = float(MLA_QK) ** -0.5

    def body(q_ref, k_ref, v_ref, o_ref, do_ref, lse_ref, dq_ref, dk_ref, dv_ref):
        qi = pl.program_id(1)

        @pl.when(qi == 0)
        def _():
            dk_ref[...] = jnp.zeros_like(dk_ref)
            dv_ref[...] = jnp.zeros_like(dv_ref)

        def run(L):
            q = q_ref[...]
            k = k_ref[0:L, :]
            dov = do_ref[...]
            dob = dov.astype(BF16)
            p = jnp.exp2(_nt(q, k) * (scale * LOG2_E) - lse_ref[...] * LOG2_E)
            p = _mask_tail(p, qi, B, G, 0.0)
            dsum = jnp.sum(dov * o_ref[...], axis=-1, keepdims=True)
            ds = (p * (_nt(dob, v_ref[0:L, :]) - dsum)).astype(BF16)
            dv_ref[0:L, :] += _tn(p.astype(BF16), dob)
            dk_ref[0:L, :] += _tn(ds, q)
            dq_ref[...] = _nn(ds, k) * scale

        _for_key_prefix(qi, B, G, S, run)

        @pl.when(qi == S // B - 1)
        def _():
            dk_ref[...] = dk_ref[...] * scale

    outs, jouts = _hosted_call(
        body, job, name=name, grid=(H, S // B),
        in_specs=[pl.BlockSpec((None, B, DQ), lambda h, i: (h, i, 0)),
                  pl.BlockSpec((None, S, DQ), lambda h, i: (h, 0, 0)),
                  pl.BlockSpec((None, S, DV), lambda h, i: (h, 0, 0)),
                  pl.BlockSpec((B, DV), lambda h, i: (i, h)),
                  pl.BlockSpec((B, DV), lambda h, i: (i, do_col0 + h)),
                  pl.BlockSpec((None, B, 1), lambda h, i: (h, i, 0))],
        out_specs=[pl.BlockSpec((None, B, DQ), lambda h, i: (h, i, 0)),
                   pl.BlockSpec((None, S, DQ), lambda h, i: (h, 0, 0)),
                   pl.BlockSpec((None, S, DV), lambda h, i: (h, 0, 0))],
        out_shape=[_sds((H, S, DQ), F32), _sds((H, S, DQ), F32), _sds((H, S, DV), F32)], scratch_shapes=[],
        semantics=("parallel", "arbitrary"), args=(qh, kh, vh, o, do, lse))
    return tuple(outs) if job is None else (*outs, jouts)


def _ret_tables(T):
    H = RET_HEADS
    log_gamma = jnp.log1p(-jnp.exp2(-5.0 - jnp.arange(H, dtype=F32)))
    idx = jnp.arange(T, dtype=F32)
    chunk = jnp.arange(T) // CHUNK
    visible = chunk[:, None] >= chunk[None, :]
    intra = jnp.where(visible[None], jnp.exp(log_gamma[:, None, None] * jnp.abs(idx[:, None] - idx[None, :])), 0.0)
    qd = jnp.exp(log_gamma[:, None] * (idx + 1.0))[:, :, None]
    kd = jnp.exp(log_gamma[:, None] * (T - 1.0 - idx))[:, :, None]
    cd = jnp.exp(log_gamma * T)[:, None, None]
    return intra, qd, kd, cd


def _rot(t, c, s):
    half = t.shape[-1] // 2
    t1, t2 = t[:, :half], t[:, half:]
    return jnp.concatenate([t1 * c - t2 * s, t2 * c + t1 * s], axis=-1)


def _rot_t(t, c, s):
    half = t.shape[-1] // 2
    t1, t2 = t[:, :half], t[:, half:]
    return jnp.concatenate([t1 * c + t2 * s, t2 * c - t1 * s], axis=-1)


def _dot(a, b, ca, cb):
    return lax.dot_general(a.astype(BF16), b.astype(BF16), (((ca,), (cb,)), ((), ())), preferred_element_type=F32)


def _ret_fwd(hR, cos, sin, tables, *, name):
    S = hR.shape[0]
    H, L, DK, DV = RET_HEADS, tables[0].shape[-1], RET_QK, RET_V
    NC = S // L
    qscale = float(DK) ** -0.5
    intra, qd, kd, cd = tables

    def body(q_ref, k_ref, v_ref, c_ref, s_ref, in_ref, qd_ref, kd_ref, cd_ref, o_ref, st_ref, state):
        @pl.when(pl.program_id(1) == 0)
        def _():
            state[...] = jnp.zeros_like(state)

        c, s = c_ref[...], s_ref[...]
        q = _rot(q_ref[...], c, s) * qscale
        k = _rot(k_ref[...], c, s)
        v = v_ref[...]
        st = state[...]
        st_ref[...] = st.astype(BF16)
        scores = _dot(q, k, 1, 1) * in_ref[...]
        o_ref[...] = _dot(scores, v, 1, 0) + _dot(q * qd_ref[...], st, 1, 0)
        state[...] = st * cd_ref[...] + _dot(k * kd_ref[...], v, 0, 0)

    blk = lambda off: pl.BlockSpec((L, DK), lambda h, c: (c, off + h))
    rope = pl.BlockSpec((L, DK // 2), lambda h, c: (c, 0))
    return _pcall(
        body, name=name, grid=(H, NC),
        in_specs=[blk(0), blk(H), blk(2 * H), rope, rope,
                  pl.BlockSpec((None, L, L), lambda h, c: (h, 0, 0)),
                  pl.BlockSpec((None, L, 1), lambda h, c: (h, 0, 0)),
                  pl.BlockSpec((None, L, 1), lambda h, c: (h, 0, 0)),
                  pl.BlockSpec((None, 1, 1), lambda h, c: (h, 0, 0))],
        out_specs=[pl.BlockSpec((L, DV), lambda h, c: (c, h)),
                   pl.BlockSpec((None, None, DK, DV), lambda h, c: (h, c, 0, 0))],
        out_shape=[_sds((S, H * DV), F32), _sds((H, NC, DK, DV), BF16)],
        scratch_shapes=[pltpu.VMEM((DK, DV), F32)],
        compiler_params=_params(("arbitrary", "arbitrary")),
    )(hR, hR, hR, cos, sin, intra, qd, kd, cd)


def _ret_bwd(do, hR, states, cos, sin, tables, *, name):
    S = hR.shape[0]
    H, L, DK, DV = RET_HEADS, tables[0].shape[-1], RET_QK, RET_V
    NC = S // L
    qscale = float(DK) ** -0.5
    intra, qd, kd, cd = tables

    def body(do_ref, q_ref, k_ref, v_ref, st_ref, c_ref, s_ref, in_ref, qd_ref, kd_ref, cd_ref,
             dq_ref, dk_ref, dv_ref, dstate):
        @pl.when(pl.program_id(1) == 0)
        def _():
            dstate[...] = jnp.zeros_like(dstate)

        c, s = c_ref[...], s_ref[...]
        q = _rot(q_ref[...], c, s) * qscale
        k = _rot(k_ref[...], c, s)
        v = v_ref[...]
        dov = do_ref[...]
        dst = dstate[...]
        dec = in_ref[...]
        qdv, kdv = qd_ref[...], kd_ref[...]
        scores = _dot(q, k, 1, 1) * dec
        da = _dot(dov, v, 1, 1) * dec
        dv_ref[...] = (_dot(scores, dov, 0, 0) + _dot(k * kdv, dst, 1, 0)).astype(BF16)
        dq = _dot(da, k, 1, 0) + _dot(dov, st_ref[...], 1, 1) * qdv
        dk = _dot(da, q, 0, 0) + _dot(v, dst, 1, 1) * kdv
        dq_ref[...] = _rot_t(dq * qscale, c, s).astype(BF16)
        dk_ref[...] = _rot_t(dk, c, s).astype(BF16)
        dstate[...] = dst * cd_ref[...] + _dot(q * qdv, dov, 0, 0)

    rev = lambda c: NC - 1 - c
    blk = lambda off: pl.BlockSpec((L, DK), lambda h, c: (rev(c), off + h))
    rope = pl.BlockSpec((L, DK // 2), lambda h, c: (rev(c), 0))
    out = pl.BlockSpec((L, DK), lambda h, c: (rev(c), h))
    return _pcall(
        body, name=name, grid=(H, NC),
        in_specs=[out, blk(0), blk(H), blk(2 * H),
                  pl.BlockSpec((None, None, DK, DV), lambda h, c: (h, rev(c), 0, 0)), rope, rope,
                  pl.BlockSpec((None, L, L), lambda h, c: (h, 0, 0)),
                  pl.BlockSpec((None, L, 1), lambda h, c: (h, 0, 0)),
                  pl.BlockSpec((None, L, 1), lambda h, c: (h, 0, 0)),
                  pl.BlockSpec((None, 1, 1), lambda h, c: (h, 0, 0))],
        out_specs=[out, out, out],
        out_shape=[_sds((S, H * DK), BF16)] * 3,
        scratch_shapes=[pltpu.VMEM((DK, DV), F32)],
        compiler_params=_params(("arbitrary", "arbitrary")),
    )(do, hR, hR, hR, states, cos, sin, intra, qd, kd, cd)


def _gn_gate_fwd(o, hR, g, b, *, name):
    S = o.shape[0]
    H, DV = RET_HEADS, RET_V
    bs = _pick(S, 512, 16)

    def body(o_ref, rg_ref, g_ref, b_ref, r_ref):
        z = o_ref[...]
        z = z - jnp.mean(z, axis=-1, keepdims=True)
        xh = z * lax.rsqrt(jnp.mean(z * z, axis=-1, keepdims=True) + GN_EPS)
        rg = rg_ref[...]
        r_ref[...] = ((rg * jax.nn.sigmoid(rg)) * (xh * g_ref[...] + b_ref[...])).astype(BF16)

    row = pl.BlockSpec((bs, DV), lambda i, h: (i, h))
    vec = pl.BlockSpec((1, DV), lambda i, h: (0, h))
    return _pcall(
        body, name=name, grid=(S // bs, H),
        in_specs=[row, pl.BlockSpec((bs, DV), lambda i, h: (i, 3 * H + h)), vec, vec],
        out_specs=row, out_shape=_sds((S, H * DV), BF16),
        compiler_params=_params(("parallel", "parallel")),
    )(o, hR, g.reshape(1, H * DV), b.reshape(1, H * DV))


def _gn_gate_bwd(dr, o, hR, g, b, *, dr_col0, name):
    S = o.shape[0]
    H, DV = RET_HEADS, RET_V
    bs = _pick(S, 512, 16)

    def body(dr_ref, o_ref, rg_ref, g_ref, b_ref, do_ref, drg_ref, dg_ref, db_ref):
        @pl.when(pl.program_id(1) == 0)
        def _():
            dg_ref[...] = jnp.zeros_like(dg_ref)
            db_ref[...] = jnp.zeros_like(db_ref)

        z = o_ref[...]
        z = z - jnp.mean(z, axis=-1, keepdims=True)
        rstd = lax.rsqrt(jnp.mean(z * z, axis=-1, keepdims=True) + GN_EPS)
        xh = z * rstd
        gv = g_ref[...]
        y = xh * gv + b_ref[...]
        rg = rg_ref[...]
        sg = jax.nn.sigmoid(rg)
        drv = dr_ref[...]
        dy = drv * (rg * sg)
        drg_ref[...] = (drv * y * (sg * (1.0 + rg * (1.0 - sg)))).astype(BF16)
        dg_ref[...] += jnp.sum(dy * xh, axis=0, keepdims=True)
        db_ref[...] += jnp.sum(dy, axis=0, keepdims=True)
        dxh = dy * gv
        do_ref[...] = rstd * (dxh - jnp.mean(dxh, axis=-1, keepdims=True)
                              - xh * jnp.mean(dxh * xh, axis=-1, keepdims=True))

    row = pl.BlockSpec((bs, DV), lambda h, i: (i, h))
    vec = pl.BlockSpec((1, DV), lambda h, i: (0, h))
    do, drg, dg, db = _pcall(
        body, name=name, grid=(H, S // bs),
        in_specs=[pl.BlockSpec((bs, DV), lambda h, i: (i, dr_col0 + h)), row,
                  pl.BlockSpec((bs, DV), lambda h, i: (i, 3 * H + h)), vec, vec],
        out_specs=[row, row, vec, vec],
        out_shape=[_sds((S, H * DV), F32), _sds((S, H * DV), BF16), _sds((1, H * DV), F32), _sds((1, H * DV), F32)],
        compiler_params=_params(("arbitrary", "arbitrary")),
    )(dr, o, hR, g.reshape(1, H * DV), b.reshape(1, H * DV))
    return do, drg, dg[0], db[0]


def _gate_up(xb, wg, wu, *, job=None, name):
    S, D = xb.shape
    F = wg.shape[0]
    bm, bn = _pick(S, 1024, 16), _pick(F, 512, 128)

    def body(x_ref, g_ref, u_ref, og_ref, ou_ref, oa_ref):
        x = x_ref[...]
        g = _nt(x, g_ref[...])
        u = _nt(x, u_ref[...])
        og_ref[...] = g.astype(BF16)
        ou_ref[...] = u.astype(BF16)
        oa_ref[...] = ((g * jax.nn.sigmoid(g)) * u).astype(BF16)

    blk = pl.BlockSpec((bm, bn), lambda i, j: (i, j))
    wspec = pl.BlockSpec((bn, D), lambda i, j: (j, 0))
    outs, jouts = _hosted_call(
        body, job, name=name, grid=(S // bm, F // bn),
        in_specs=[pl.BlockSpec((bm, D), lambda i, j: (i, 0)), wspec, wspec], out_specs=[blk] * 3,
        out_shape=[_sds((S, F), BF16)] * 3, scratch_shapes=[], semantics=("parallel", "parallel"), args=(xb, wg, wu))
    return tuple(outs) if job is None else (*outs, jouts)


def _swiglu_bwd(da, g, u):
    g, u = g.astype(F32), u.astype(F32)
    sg = jax.nn.sigmoid(g)
    return da * u * (sg * (1.0 + g * (1.0 - sg))), da * (g * sg)


def _loss_head(y, t, *, name):
    S, D = y.shape
    bs = _pick(S, 256, 8)
    inv_d = 1.0 / D

    def body(y_ref, t_ref, dy_ref, l_ref):
        @pl.when(pl.program_id(0) == 0)
        def _():
            l_ref[...] = jnp.zeros_like(l_ref)

        e = y_ref[...] - t_ref[...]
        dy_ref[...] = e * inv_d
        l_ref[...] += 0.5 * jnp.sum(jnp.mean(e * e, axis=-1, keepdims=True), axis=0, keepdims=True)

    row = pl.BlockSpec((bs, D), lambda i: (i, 0))
    dy, l = _pcall(
        body, name=name, grid=(S // bs,), in_specs=[row, row],
        out_specs=[row, pl.BlockSpec((1, 1), lambda i: (0, 0))],
        out_shape=[_sds((S, D), F32), _sds((1, 1), F32)],
        compiler_params=_params(("arbitrary",)),
    )(y, t)
    return dy, l[0, 0]


def _adamw(w, g, m, v, *, name):
    shape = w.shape
    C = shape[-1]
    R = w.size // C
    br = _pick(R, 512, 8)

    def body(w_ref, g_ref, m_ref, v_ref, d_ref, nm_ref, nv_ref):
        gv = g_ref[...]
        mn = ADAM_B1 * m_ref[...] + (1.0 - ADAM_B1) * gv
        vn = ADAM_B2 * v_ref[...] + (1.0 - ADAM_B2) * (gv * gv)
        m_hat = mn / (1.0 - ADAM_B1 ** ADAM_STEP)
        v_hat = vn / (1.0 - ADAM_B2 ** ADAM_STEP)
        d_ref[...] = -ADAM_LR * (m_hat / (jnp.sqrt(v_hat) + ADAM_EPS) + ADAM_WD * w_ref[...])
        nm_ref[...] = mn
        nv_ref[...] = vn

    blk = pl.BlockSpec((br, C), lambda i: (i, 0))
    outs = _pcall(body, name=name, grid=(R // br,), in_specs=[blk] * 4, out_specs=[blk] * 3,
                  out_shape=[_sds((R, C), F32)] * 3,
                  compiler_params=_params(("parallel",)))(*[a.reshape(R, C) for a in (w, g, m, v)])
    return tuple(o.reshape(shape) for o in outs)


def _slot_block(n, k):
    return (_pick(n, 1024, 16) if n % 16 == 0 else n), _pick(k, 1024, 128)


def _sum_slots(x, *, name):
    ns, n, k = x.shape
    br, bc = _slot_block(n, k)

    def body(x_ref, o_ref):
        acc = x_ref[0].astype(F32)
        for s in range(1, ns):
            acc = acc + x_ref[s].astype(F32)
        o_ref[...] = acc

    return _pcall(body, name=name, grid=(n // br, k // bc),
                  in_specs=[pl.BlockSpec((ns, br, bc), lambda i, j: (0, i, j))],
                  out_specs=pl.BlockSpec((br, bc), lambda i, j: (i, j)),
                  out_shape=_sds((n, k), F32), compiler_params=_params(("parallel", "parallel")))(x)


def _pair_sum(g, r, *, name):
    _, n, k = r.shape
    br, bc = _slot_block(n, k)
    core = lax.axis_index("c").astype(jnp.int32).reshape(1)

    def body(c_ref, g_ref, r_ref, o_ref):
        o_ref[...] = (g_ref[...].astype(F32) + r_ref[...].astype(F32)).astype(o_ref.dtype)

    blk = pl.BlockSpec((None, br, bc), lambda s, i, j, c_ref: (s, i, j))
    return _pcall(
        body, name=name, out_shape=_sds(r.shape, g.dtype),
        grid_spec=pltpu.PrefetchScalarGridSpec(
            num_scalar_prefetch=1, grid=(N_CHIP, n // br, k // bc),
            in_specs=[pl.BlockSpec((None, br, bc), lambda s, i, j, c_ref: (2 * s + c_ref[0], i, j)), blk],
            out_specs=blk),
        compiler_params=_params(("parallel", "parallel", "parallel")),
    )(core, g, r)


N_PEER = N_DEV - 1
N_CHIP = N_DEV // 2
HOST_TAIL_FRACTION = 8


def _coords():
    return lax.axis_index("x"), lax.axis_index("y"), lax.axis_index("c")


def _ag_phase(phase, x_refs, out_refs, send_sems, recv_sems, local_sems):
    n = len(x_refs)
    x, y, c = _coords()
    me, sibling = (x, y, c), (x, y, 1 - c)
    chips = [(1 - x, y), (x, 1 - y), (1 - x, 1 - y)]

    def copy(a, k, block, to, src=None):
        px, py, pc = block
        dst = out_refs[a].at[4 * px + 2 * py + pc]
        return pltpu.make_async_remote_copy(
            src_ref=dst if src is None else src, dst_ref=dst,
            send_sem=send_sems.at[a * N_PEER + k], recv_sem=recv_sems.at[a * N_PEER + k],
            device_id=to, device_id_type=pl.DeviceIdType.MESH)

    def local(a):
        return pltpu.make_async_copy(x_refs[a], out_refs[a].at[4 * x + 2 * y + c], local_sems.at[a])

    def first(a):
        return ([copy(a, 0, me, sibling, src=x_refs[a])]
                + [copy(a, 1 + j, me, (*chip, c), src=x_refs[a]) for j, chip in enumerate(chips)])

    def passed(a, j):
        return copy(a, 4 + j, (*chips[j], c), sibling)

    if phase == 0:
        for a in range(n):
            local(a).start()
            for cp in first(a):
                cp.start()
    elif phase == 1:
        for j in range(len(chips)):
            for a in range(n):
                copy(a, 1 + j, (*chips[j], c), me).wait_recv()
                passed(a, j).start()
    else:
        for a in range(n):
            copy(a, 0, sibling, me).wait_recv()
            for j in range(len(chips)):
                copy(a, 4 + j, (*chips[j], 1 - c), me).wait_recv()
        for a in range(n):
            for cp in first(a):
                cp.wait_send()
            for j in range(len(chips)):
                passed(a, j).wait_send()
            local(a).wait()


def _ag1_phase(phase, x_refs, out_refs, send_sems, recv_sems, local_sems):
    n = len(x_refs)
    x, y, c = _coords()
    chips = [(1 - x, y), (x, 1 - y), (1 - x, 1 - y)]

    def local(a):
        return pltpu.make_async_copy(x_refs[a], out_refs[a].at[4 * x + 2 * y + c], local_sems.at[a])

    def pair(a, j):
        px, py = chips[j]
        sem = a * N_PEER + j
        send = pltpu.make_async_remote_copy(
            src_ref=x_refs[a], dst_ref=out_refs[a].at[4 * x + 2 * y + c], send_sem=send_sems.at[sem],
            recv_sem=recv_sems.at[sem], device_id=(px, py, c), device_id_type=pl.DeviceIdType.MESH)
        recv = pltpu.make_async_remote_copy(
            src_ref=x_refs[a], dst_ref=out_refs[a].at[4 * px + 2 * py + c], send_sem=send_sems.at[sem],
            recv_sem=recv_sems.at[sem], device_id=(px, py, c), device_id_type=pl.DeviceIdType.MESH)
        return send, recv

    if phase == 0:
        for a in range(n):
            local(a).start()
            for j in range(len(chips)):
                pair(a, j)[0].start()
    elif phase == 2:
        for a in range(n):
            for j in range(len(chips)):
                pair(a, j)[1].wait_recv()
        for a in range(n):
            for j in range(len(chips)):
                pair(a, j)[0].wait_send()
            local(a).wait()


def _ag2_phase(phase, buf_refs, out_refs, send_sems, recv_sems, local_sems):
    n = len(out_refs)
    x, y, c = _coords()

    def copy(a, i):
        return pltpu.make_async_remote_copy(
            src_ref=out_refs[a].at[2 * i + c], dst_ref=out_refs[a].at[2 * i + c],
            send_sem=send_sems.at[a * N_PEER + i], recv_sem=recv_sems.at[a * N_PEER + i],
            device_id=(x, y, 1 - c), device_id_type=pl.DeviceIdType.MESH)

    def arrival(a, i):
        return pltpu.make_async_remote_copy(
            src_ref=out_refs[a].at[2 * i + (1 - c)], dst_ref=out_refs[a].at[2 * i + (1 - c)],
            send_sem=send_sems.at[a * N_PEER + i], recv_sem=recv_sems.at[a * N_PEER + i],
            device_id=(x, y, 1 - c), device_id_type=pl.DeviceIdType.MESH)

    if phase == 0:
        for a in range(n):
            for i in range(N_CHIP):
                copy(a, i).start()
    elif phase == 2:
        for a in range(n):
            for i in range(N_CHIP):
                arrival(a, i).wait_recv()
        for a in range(n):
            for i in range(N_CHIP):
                copy(a, i).wait_send()


def _pair_phase(phase, g_refs, out_refs, send_sems, recv_sems, local_sems):
    n = len(g_refs)
    x, y, c = _coords()

    def copy(a, i):
        return pltpu.make_async_remote_copy(
            src_ref=g_refs[a].at[2 * i + (1 - c)], dst_ref=out_refs[a].at[i],
            send_sem=send_sems.at[a * N_PEER + i], recv_sem=recv_sems.at[a * N_PEER + i],
            device_id=(x, y, 1 - c), device_id_type=pl.DeviceIdType.MESH)

    if phase == 0:
        for a in range(n):
            for i in range(N_CHIP):
                copy(a, i).start()
    elif phase == 2:
        for a in range(n):
            for i in range(N_CHIP):
                copy(a, i).wait_recv()
        for a in range(n):
            for i in range(N_CHIP):
                copy(a, i).wait_send()


def _cross_phase(phase, p_refs, out_refs, send_sems, recv_sems, local_sems):
    n = len(p_refs)
    x, y, c = _coords()
    mine = 2 * x + y

    def local(a):
        return pltpu.make_async_copy(p_refs[a].at[mine], out_refs[a].at[mine], local_sems.at[a])

    def pair(a, k):
        px = 1 - x if k & 2 else x
        py = 1 - y if k & 1 else y
        peer = 2 * px + py
        sem = a * N_PEER + k - 1
        send = pltpu.make_async_remote_copy(
            src_ref=p_refs[a].at[peer], dst_ref=out_refs[a].at[mine], send_sem=send_sems.at[sem],
            recv_sem=recv_sems.at[sem], device_id=(px, py, c), device_id_type=pl.DeviceIdType.MESH)
        recv = pltpu.make_async_remote_copy(
            src_ref=p_refs[a].at[peer], dst_ref=out_refs[a].at[peer], send_sem=send_sems.at[sem],
            recv_sem=recv_sems.at[sem], device_id=(px, py, c), device_id_type=pl.DeviceIdType.MESH)
        return send, recv

    if phase == 0:
        for a in range(n):
            local(a).start()
        for k in range(1, N_CHIP):
            for a in range(n):
                pair(a, k)[0].start()
    elif phase == 2:
        for k in range(1, N_CHIP):
            for a in range(n):
                pair(a, k)[1].wait_recv()
        for k in range(1, N_CHIP):
            for a in range(n):
                pair(a, k)[0].wait_send()
        for a in range(n):
            local(a).wait()


_PHASES = {"ag": _ag_phase, "ag1": _ag1_phase, "ag2": _ag2_phase, "pair": _pair_phase, "cross": _cross_phase}


def _job(kind, arrays):
    arrays = list(arrays)
    if kind in ("ag", "ag1"):
        shapes = [(N_DEV,) + a.shape for a in arrays]
    elif kind == "pair":
        shapes = [(N_CHIP,) + a.shape[1:] for a in arrays]
    else:
        shapes = [a.shape for a in arrays]
    return dict(parts=[(kind, len(arrays))], ins=arrays, outs=[_sds(s, a.dtype) for s, a in zip(shapes, arrays)],
                in_place=[kind == "ag2"] * len(arrays))


def _join(jobs):
    jobs = [j for j in jobs if j is not None]
    if not jobs:
        return None
    return dict(parts=[p for j in jobs for p in j["parts"]], ins=[a for j in jobs for a in j["ins"]],
                outs=[o for j in jobs for o in j["outs"]], in_place=[f for j in jobs for f in j["in_place"]])


def _hosted_call(body, job, *, name, grid, in_specs, out_specs, out_shape, scratch_shapes, semantics, args):
    if job is None:
        outs = _pcall(body, name=name, grid=grid, in_specs=in_specs, out_specs=out_specs, out_shape=out_shape,
                      scratch_shapes=scratch_shapes, compiler_params=_params(semantics))(*args)
        return outs, []
    n_in, n_out, n_scr, nj = len(in_specs), len(out_specs), len(scratch_shapes), len(job["ins"])
    parts = job["parts"]
    total = 1
    for g in grid:
        total *= g
    late = total - max(total // HOST_TAIL_FRACTION, 1) if total > 1 else 0

    def wrapped(*refs):
        ins, refs = refs[:n_in], refs[n_in:]
        jins, refs = refs[:nj], refs[nj:]
        outs, refs = refs[:n_out], refs[n_out:]
        jouts, refs = refs[:nj], refs[nj:]
        scr, sems = refs[:n_scr], refs[n_scr:]
        step = 0
        for d, g in enumerate(grid):
            step = step * g + pl.program_id(d)

        def run(phase):
            off = 0
            for i, (kind, n) in enumerate(parts):
                _PHASES[kind](phase, jins[off:off + n], jouts[off:off + n], *sems[3 * i:3 * i + 3])
                off += n

        pl.when(step == 0)(lambda: run(0))
        pl.when(step == late)(lambda: run(1))
        body(*ins, *outs, *scr)
        pl.when(step == total - 1)(lambda: run(2))

    any_spec = pl.BlockSpec(memory_space=pl.ANY)
    sems = []
    for kind, n in parts:
        sems += [pltpu.SemaphoreType.DMA((n * N_PEER,)), pltpu.SemaphoreType.DMA((n * N_PEER,)),
                 pltpu.SemaphoreType.DMA((n,))]
    outs = _pcall(
        wrapped, name=name, grid=grid, in_specs=list(in_specs) + [any_spec] * nj,
        out_specs=list(out_specs) + [any_spec] * nj, out_shape=list(out_shape) + job["outs"],
        scratch_shapes=list(scratch_shapes) + sems,
        input_output_aliases={n_in + i: n_out + i for i in range(nj) if job["in_place"][i]},
        compiler_params=_params(("arbitrary",) * len(grid)),
    )(*args, *job["ins"])
    return outs[:n_out], outs[n_out:]


def _exchange(job, *, name):
    outs, jouts = _hosted_call(lambda: None, job, name=name, grid=(1,), in_specs=[], out_specs=[], out_shape=[],
                               scratch_shapes=[], semantics=("arbitrary",), args=())
    return jouts


_TRANSPOSED = {"w_in": True, "w_out": False, "w_gate": True, "w_up": True, "w_down": False, "w_uq": True, "w_ukv": True}
_BIG = tuple(_TRANSPOSED)
_SMALL = ("ln_in_g", "ln_in_b", "q_norm_g", "kv_norm_g", "ret_gn_g", "ret_gn_b", "ln1_g", "ln1_b", "ln2_g", "ln2_b")
SMALL_COLS = 128


def _rope_tables(pos, dim):
    inv_freq = ROPE_THETA ** (-jnp.arange(0, dim, 2, dtype=F32) / dim)
    ang = pos.astype(F32)[:, None] * inv_freq
    return jnp.cos(ang), jnp.sin(ang)


def _split_in(wt_in):
    c = MLA_Q_LORA + MLA_KV_LORA + MLA_ROPE
    return jnp.pad(wt_in[:c], ((0, LANES - MLA_ROPE), (0, 0))), wt_in[c:]


def _pad_heads(wt_uq):
    H = MLA_HEADS
    w3 = wt_uq.reshape(H, MLA_QK, wt_uq.shape[1])
    return jnp.pad(w3, ((0, 0), (0, MLA_PAD - MLA_QK), (0, 0)))


class _Plan:
    _NEXT_IN = ((1, "w_in"), (1, "w_uq"), (1, "w_ukv"))
    HOSTS = {
        "ln_in": [("ag", ((0, "w_in"),))],
        "b_ln_in": [("cross", ((0, "w_in"),))],
        "l0_hA": [("ag", ((0, "w_uq"), (0, "w_ukv")))],
        "l0_hR": [("ag1", ((0, "w_out"),))],
        "l0_attn": [("ag2", ((0, "w_out"),)), ("ag1", ((0, "w_gate"), (0, "w_up")))],
        "l0_out": [("ag2", ((0, "w_gate"), (0, "w_up")))],
        "l0_gateup": [("ag", ((0, "w_down"),)), ("ag1", _NEXT_IN)],
        "l0_down": [("ag2", _NEXT_IN), ("ag1", ((1, "w_gate"),))],
        "l1_hR": [("ag2", ((1, "w_gate"),)), ("ag1", ((1, "w_out"),))],
        "l1_attn": [("ag2", ((1, "w_out"),)), ("ag1", ((1, "w_up"),))],
        "l1_out": [("ag2", ((1, "w_up"),))],
        "l1_gateup": [("ag", ((1, "w_down"),))],
        "l0_b_dact": [("cross", ((1, "w_in"),))]}
    for _l in (0, 1):
        HOSTS.update({
            "l%d_b_wgate" % _l: [("pair", ((_l, "w_down"),))],
            "l%d_b_wup" % _l: [("pair", ((_l, "w_gate"),)), ("cross", ((_l, "w_down"),))],
            "l%d_b_dx1a" % _l: [("pair", ((_l, "w_up"),)), ("cross", ((_l, "w_gate"),))],
            "l%d_b_dx1b" % _l: [("cross", ((_l, "w_up"),))],
            "l%d_b_attn" % _l: [("pair", ((_l, "w_out"),))],
            "l%d_b_win" % _l: [("pair", ((_l, "w_uq"), (_l, "w_ukv"))), ("cross", ((_l, "w_out"),))],
            "l%d_b_dx" % _l: [("pair", ((_l, "w_in"),)), ("cross", ((_l, "w_uq"), (_l, "w_ukv")))]})

    def __init__(self, local):
        self.local = local
        self.half = {}
        self.full = {}
        self.grads = {}
        self.paired = {}
        self.recv = {}

    def _by_device(self, k):
        return self.grads[k].reshape((N_DEV,) + self.local[k].shape)

    def _make(self, kind, keys):
        if kind in ("ag", "ag1"):
            return _job(kind, [self.local[k] for k in keys])
        if kind == "ag2":
            return _job(kind, [self.half[k] for k in keys])
        if kind == "pair":
            return _job(kind, [self._by_device(k) for k in keys])
        return _job(kind, [_pair_sum(self._by_device(k), self.paired[k], name="pairsum_l%d_%s" % k) for k in keys])

    def _done(self, kind, keys, outs):
        for k, o in zip(keys, outs):
            if kind in ("ag", "ag2"):
                self.full[k] = o.reshape(N_DEV * o.shape[1], o.shape[2])
            elif kind == "ag1":
                self.half[k] = o
            elif kind == "pair":
                self.paired[k] = o
            else:
                self.recv[k] = o

    def _run(self, todo, call):
        outs = call(_join([self._make(kind, keys) for kind, keys in todo]))
        for kind, keys in todo:
            self._done(kind, keys, outs[:len(keys)])
            outs = outs[len(keys):]

    def call(self, fn, name, *args, **kw):
        if name not in self.HOSTS:
            return fn(*args, name=name, **kw)
        res = []

        def run(job):
            *outs, jouts = fn(*args, name=name, job=job, **kw)
            res.extend(outs)
            return jouts

        self._run(self.HOSTS[name], run)
        return res[0] if len(res) == 1 else tuple(res)


def _layer_fwd(x, xb, plan, p, tabs, l):
    S, D = x.shape
    H = MLA_HEADS
    nm = lambda s: "l%d_%s" % (l, s)
    w = lambda n: plan.full[(l, n)]
    mm = lambda name, *a, **kw: plan.call(_mm, nm(name), *a, **kw)
    wt_a, wt_r = _split_in(w("w_in"))
    hA = mm("hA", xb, wt_a, tb=True)
    hR = mm("hR", xb, wt_r, tb=True)
    _, qn, qn_hat, q_rstd = _norm_fwd(hA, p["q_norm_g"], None, center=False, eps=RMS_EPS, want_f32=False,
                                      col=0, name=nm("qnorm"))
    _, kvn, kvn_hat, kv_rstd = _norm_fwd(hA, p["kv_norm_g"], None, center=False, eps=RMS_EPS, want_f32=False,
                                         col=MLA_Q_LORA // MLA_KV_LORA, name=nm("kvnorm"))
    w_uq3 = _pad_heads(w("w_uq"))
    w_ukv3 = w("w_ukv").reshape(H, MLA_NOPE + MLA_V, -1)
    q3 = _mm_heads_out(qn, w_uq3, name=nm("uq"))
    kv3 = _mm_heads_out(kvn, w_ukv3, name=nm("ukv"))
    qh, kh, vh = _mla_prep(q3, kv3, hA, tabs["c128"], tabs["s128"],
                           kr_col=(MLA_Q_LORA + MLA_KV_LORA) // LANES, name=nm("rope"))
    a, lse = plan.call(_attn_fwd, nm("attn"), qh, kh, vh)
    o_ret, states = _ret_fwd(hR, tabs["cos_r"], tabs["sin_r"], tabs["ret"], name=nm("ret"))
    r = _gn_gate_fwd(o_ret, hR, p["ret_gn_g"], p["ret_gn_b"], name=nm("gn"))
    mix_in = jnp.concatenate([a.astype(BF16), r], axis=1)
    mix = mm("out", mix_in, w("w_out"))
    x1, x1b, x1_hat, rstd1 = _norm_fwd(x, p["ln1_g"], p["ln1_b"], res=mix, alpha=p["alpha"], eps=LN_EPS, name=nm("ln1"))
    gb, ub, act = plan.call(_gate_up, nm("gateup"), x1b, w("w_gate"), w("w_up"))
    f = mm("down", act, w("w_down"))
    x2, x2b, x2_hat, rstd2 = _norm_fwd(x1, p["ln2_g"], p["ln2_b"], res=f, alpha=p["alpha"], eps=LN_EPS, name=nm("ln2"))
    saved = dict(xb=xb, qn=qn, qn_hat=qn_hat, q_rstd=q_rstd, kvn=kvn, kvn_hat=kvn_hat, kv_rstd=kv_rstd,
                 qh=qh, kh=kh, vh=vh, a=a, lse=lse, hR=hR, o_ret=o_ret, states=states, mix_in=mix_in,
                 x1b=x1b, x1_hat=x1_hat, rstd1=rstd1, gb=gb, ub=ub, act=act, x2_hat=x2_hat, rstd2=rstd2)
    return x2, x2b, saved


def _layer_bwd(dx2, sv, plan, p, tabs, l):
    S, D = dx2.shape
    H = MLA_HEADS
    nm = lambda s: "l%d_b_%s" % (l, s)
    w = lambda n: plan.full[(l, n)]
    mm = lambda name, *a, **kw: plan.call(_mm, nm(name), *a, **kw)
    alpha = p["alpha"]
    gw, gp = plan.grads, {}
    dz2, dz2b, gp["ln2_g"], gp["ln2_b"] = _norm_bwd(dx2, sv["x2_hat"], sv["rstd2"], p["ln2_g"], name=nm("ln2"))
    dg, du = mm("dact", dz2b, w("w_down"), tb=True, extras=(sv["gb"], sv["ub"]), epilogue=_swiglu_bwd,
                out_dtypes=(BF16, BF16))
    gw[(l, "w_down")] = mm("wdown", sv["act"], dz2b, ta=True, out_dtype=BF16)
    gw[(l, "w_gate")] = mm("wgate", dg, sv["x1b"], ta=True, out_dtype=BF16)
    gw[(l, "w_up")] = mm("wup", du, sv["x1b"], ta=True, out_dtype=BF16)
    t = mm("dx1a", dg, w("w_gate"), add=dz2, add_scale=alpha)
    dx1 = mm("dx1b", du, w("w_up"), add=t)
    dz1, dz1b, gp["ln1_g"], gp["ln1_b"] = _norm_bwd(dx1, sv["x1_hat"], sv["rstd1"], p["ln1_g"], name=nm("ln1"))
    dmix = mm("dmix", dz1b, w("w_out"), tb=True)
    gw[(l, "w_out")] = mm("wout", sv["mix_in"], dz1b, ta=True, out_dtype=BF16)
    ret_col0 = (H * MLA_V) // RET_V
    do_ret, drg, gp["ret_gn_g"], gp["ret_gn_b"] = _gn_gate_bwd(
        dmix, sv["o_ret"], sv["hR"], p["ret_gn_g"], p["ret_gn_b"], dr_col0=ret_col0, name=nm("gn"))
    drq, drk, drv = _ret_bwd(do_ret, sv["hR"], sv["states"], tabs["cos_r"], tabs["sin_r"], tabs["ret"], name=nm("ret"))
    dqh, dkh, dvh = plan.call(_attn_bwd, nm("attn"), sv["qh"], sv["kh"], sv["vh"], sv["a"], dmix, sv["lse"], do_col0=0)
    dq3, dkv3, dkrp = _mla_unprep(dqh, dkh, dvh, tabs["c128"], tabs["s128"], name=nm("rope"))
    w_uq3 = _pad_heads(w("w_uq"))
    w_ukv3 = w("w_ukv").reshape(H, MLA_NOPE + MLA_V, -1)
    g_uq = _mm_heads_tn(dq3, sv["qn"], name=nm("wuq")).reshape(H, MLA_PAD, -1)
    gw[(l, "w_uq")] = g_uq[:, :MLA_QK].reshape(H * MLA_QK, -1)
    dqn = _mm_heads_sum(dq3, w_uq3, name=nm("dqn"))
    gw[(l, "w_ukv")] = _mm_heads_tn(dkv3, sv["kvn"], name=nm("wukv"))
    dkvn = _mm_heads_sum(dkv3, w_ukv3, name=nm("dkvn"))
    _, dcq, gp["q_norm_g"], _ = _norm_bwd(dqn, sv["qn_hat"], sv["q_rstd"], p["q_norm_g"], center=False,
                                          want_f32=False, name=nm("qnorm"))
    _, dckv, gp["kv_norm_g"], _ = _norm_bwd(dkvn, sv["kvn_hat"], sv["kv_rstd"], p["kv_norm_g"], center=False,
                                            want_f32=False, name=nm("kvnorm"))
    dh = jnp.concatenate([dcq, dckv, dkrp[:, :MLA_ROPE].astype(BF16), drq, drk, drv, drg], axis=1)
    gw[(l, "w_in")] = mm("win", dh, sv["xb"], ta=True, out_dtype=BF16)
    dx = mm("dx", dh, w("w_in"), add=dz1, add_scale=alpha)
    return dx, gp


def _local_step(x, target, pos, small, plan, depth):
    alpha = (2 * depth) ** 0.25
    cos_m, sin_m = _rope_tables(pos, MLA_ROPE)
    cos_r, sin_r = _rope_tables(pos, RET_QK)
    zeros = jnp.zeros((x.shape[0], LANES - MLA_ROPE), F32)
    tabs = dict(c128=jnp.concatenate([cos_m, cos_m, zeros], axis=1), s128=jnp.concatenate([-sin_m, sin_m, zeros], axis=1),
                cos_r=cos_r, sin_r=sin_r, ret=_ret_tables(_pick(x.shape[0], RET_BLOCK, CHUNK)))
    h, hb, h_hat, h_rstd = plan.call(_norm_fwd, "ln_in", x, small["ln_in_g"], small["ln_in_b"], eps=LN_EPS)
    saved, ps = [], []
    for l in range(depth):
        p = {k: small[k][l] for k in _SMALL[2:]}
        p["alpha"] = alpha
        h, hb, sv = _layer_fwd(h, hb, plan, p, tabs, l)
        saved.append(sv)
        ps.append(p)
    dy, loss = _loss_head(h, target, name="loss")
    gps = [None] * depth
    for l in reversed(range(depth)):
        dy, gps[l] = _layer_bwd(dy, saved[l], plan, ps[l], tabs, l)
    grad_x, _, g_in_g, g_in_b = plan.call(_norm_bwd, "b_ln_in", dy, h_hat, h_rstd, small["ln_in_g"], want_bf16=False)
    gsmall = {"ln_in_g": g_in_g, "ln_in_b": g_in_b}
    for k in _SMALL[2:]:
        gsmall[k] = jnp.stack([gps[l][k] for l in range(depth)])
    return loss, grad_x, gsmall


def kernel(x, positions, ln_in_g, ln_in_b, w_in, q_norm_g, kv_norm_g, w_uq, w_ukv, ret_gn_g, ret_gn_b, w_out, ln1_g, ln1_b, w_gate, w_up, w_down, ln2_g, ln2_b, loss_target, m_ln_in_g, m_ln_in_b, m_w_in, m_q_norm_g, m_kv_norm_g, m_w_uq, m_w_ukv, m_ret_gn_g, m_ret_gn_b, m_w_out, m_ln1_g, m_ln1_b, m_w_gate, m_w_up, m_w_down, m_ln2_g, m_ln2_b, v_ln_in_g, v_ln_in_b, v_w_in, v_q_norm_g, v_kv_norm_g, v_w_uq, v_w_ukv, v_ret_gn_g, v_ret_gn_b, v_w_out, v_ln1_g, v_ln1_b, v_w_gate, v_w_up, v_w_down, v_ln2_g, v_ln2_b):
    names = ["ln_in_g", "ln_in_b", "w_in", "q_norm_g", "kv_norm_g", "w_uq", "w_ukv", "ret_gn_g", "ret_gn_b", "w_out",
             "ln1_g", "ln1_b", "w_gate", "w_up", "w_down", "ln2_g", "ln2_b"]
    wv = dict(zip(names, (ln_in_g, ln_in_b, w_in, q_norm_g, kv_norm_g, w_uq, w_ukv, ret_gn_g, ret_gn_b, w_out,
                          ln1_g, ln1_b, w_gate, w_up, w_down, ln2_g, ln2_b)))
    mv = dict(zip(names, (m_ln_in_g, m_ln_in_b, m_w_in, m_q_norm_g, m_kv_norm_g, m_w_uq, m_w_ukv, m_ret_gn_g,
                          m_ret_gn_b, m_w_out, m_ln1_g, m_ln1_b, m_w_gate, m_w_up, m_w_down, m_ln2_g, m_ln2_b)))
    vv = dict(zip(names, (v_ln_in_g, v_ln_in_b, v_w_in, v_q_norm_g, v_kv_norm_g, v_w_uq, v_w_ukv, v_ret_gn_g,
                          v_ret_gn_b, v_w_out, v_ln1_g, v_ln1_b, v_w_gate, v_w_up, v_w_down, v_ln2_g, v_ln2_b)))
    depth = w_in.shape[0]
    assert depth == 2, "the exchange plan is written for two layers"

    keys = [(l, n) for l in range(depth) for n in _BIG]
    plan = _Plan({(l, n): (wv[n][l].T if _TRANSPOSED[n] else wv[n][l]).astype(BF16) for l, n in keys})

    small = {n: wv[n] for n in _SMALL}
    loss, grad_x, gsmall = _local_step(x[0], loss_target[0], positions[0], small, plan, depth)
    loss = lax.psum(loss, MESH_AXES)

    gshard = {n: [None] * depth for n in _BIG}
    for l, n in keys:
        tot = _sum_slots(plan.recv[(l, n)], name="sum_l%d_%s" % (l, n))
        gshard[n][l] = tot.T if _TRANSPOSED[n] else tot
    grads = {n: jnp.stack(v) for n, v in gshard.items()}

    flat = jnp.concatenate([gsmall[n].reshape(-1) for n in _SMALL])
    n_small = flat.shape[0]
    rows = -(-n_small // (SMALL_COLS * 8)) * 8
    flat = jnp.pad(flat, (0, rows * SMALL_COLS - n_small)).reshape(rows, SMALL_COLS)
    tot = _sum_slots(_exchange(_job("ag", [flat]), name="ag_small")[0], name="sum_small").reshape(-1)
    off = 0
    for n in _SMALL:
        grads[n] = tot[off:off + wv[n].size].reshape(wv[n].shape)
        off += wv[n].size

    delta, new_m, new_v = {}, {}, {}
    for n in names:
        w2 = wv[n] if wv[n].ndim > 1 else wv[n].reshape(1, -1)
        d, nm_, nv_ = _adamw(w2, grads[n].reshape(w2.shape), mv[n].reshape(w2.shape), vv[n].reshape(w2.shape),
                             name="adamw_" + n)
        delta[n], new_m[n], new_v[n] = d.reshape(wv[n].shape), nm_.reshape(wv[n].shape), nv_.reshape(wv[n].shape)

    return (loss, grad_x[None], *[grads[n] for n in names], *[delta[n] for n in names],
            *[new_m[n] for n in names], *[new_v[n] for n in names])
```

```python
import jax
import jax.numpy as jnp
from jax import lax
from jax.experimental import pallas as pl
from jax.experimental.pallas import tpu as pltpu

F32 = jnp.float32
BF16 = jnp.bfloat16

CHUNK = 64
CHUNK_SHIFT = 6
MLA_HEADS = 8
MLA_Q_LORA = 512
MLA_KV_LORA = 256
MLA_NOPE = 128
MLA_ROPE = 64
MLA_V = 128
MLA_QK = MLA_NOPE + MLA_ROPE
LANES = 128
MLA_PAD = 2 * LANES
RET_HEADS = 4
RET_QK = 256
RET_V = 256
ROPE_THETA = 10000.0
LN_EPS = 1e-5
RMS_EPS = 1e-6
GN_EPS = 1e-5
ADAM_LR = 0.001
ADAM_B1 = 0.9
ADAM_B2 = 0.999
ADAM_EPS = 1e-08
ADAM_WD = 0.01
ADAM_STEP = 10

LOG2_E = 1.4426950408889634

N_DEV = 8
MESH_AXES = ("x", "y", "c")
VMEM_LIMIT_BYTES = 56 * 1024 * 1024
MM_VMEM_BUDGET = 40 * 1024 * 1024
MM_ACC_PENALTY = 0.85
ATTN_BLOCK = 256
ATTN_BLOCK_FWD = 256
ATTN_KEY_STEP = 256
RET_BLOCK = 256


def _pick(n, pref, mult):
    best = None
    d = mult
    while d <= min(n, pref):
        if n % d == 0:
            best = d
        d += mult
    return n if best is None else best


def _divisors(n, mult, cap):
    ds = [d for d in range(mult, min(n, cap) + 1, mult) if n % d == 0]
    return ds or [n]


def _pcall(body, **kw):
    return pl.pallas_call(body, **kw)


def _params(sem):
    return pltpu.CompilerParams(dimension_semantics=sem, vmem_limit_bytes=VMEM_LIMIT_BYTES)


def _sds(shape, dtype):
    return jax.ShapeDtypeStruct(shape, dtype)


def _mm_tiles(M, N, K, ta, sa, sb, tile_bytes):
    best = None
    for bk in _divisors(K, 128, 8192):
        nk = K // bk
        for bm in _divisors(M, 128 if ta else 16, 1024):
            for bn in _divisors(N, 128, 1024):
                vmem = 2 * (bm * bk * sa + bk * bn * sb) + 2 * bm * bn * tile_bytes + (bm * bn * 4 if nk > 1 else 0)
                if vmem > MM_VMEM_BUDGET:
                    continue
                flops_per_byte = 1.0 / (1.0 / bm + (1.0 / max(N, bn) if nk == 1 else 1.0 / bn))
                score = (flops_per_byte * (1.0 if nk == 1 else MM_ACC_PENALTY), bk, bn, bm)
                if best is None or score > best[0]:
                    best = (score, bm, bn, bk)
    assert best is not None, (M, N, K)
    return best[1:]


def _mm(a, b, *, ta=False, tb=False, add=None, add_scale=1.0, out_dtype=F32, extras=(), epilogue=None,
        out_dtypes=None, job=None, name):
    if ta:
        K, M = a.shape
    else:
        M, K = a.shape
    if tb:
        N, K2 = b.shape
    else:
        K2, N = b.shape
    assert K == K2, (a.shape, b.shape, ta, tb)
    if epilogue is None:
        assert not extras and out_dtypes is None
        out_dtypes = (out_dtype,)
        if add is not None:
            extras, epilogue = (add,), lambda r, c: (r + add_scale * c,)
        else:
            epilogue = lambda r: (r,)
    n_ex, n_out = len(extras), len(out_dtypes)
    tile_bytes = sum(e.dtype.itemsize for e in extras) + sum(jnp.dtype(d).itemsize for d in out_dtypes)
    bm, bn, bk = _mm_tiles(M, N, K, ta, a.dtype.itemsize, b.dtype.itemsize, tile_bytes)
    nk = K // bk
    a_spec = (pl.BlockSpec((bk, bm), lambda i, j, k: (k, i)) if ta
              else pl.BlockSpec((bm, bk), lambda i, j, k: (i, k)))
    b_spec = (pl.BlockSpec((bn, bk), lambda i, j, k: (j, k)) if tb
              else pl.BlockSpec((bk, bn), lambda i, j, k: (k, j)))
    o_spec = pl.BlockSpec((bm, bn), lambda i, j, k: (i, j))
    dims = (((0 if ta else 1,), (1 if tb else 0,)), ((), ()))

    def body(*refs):
        a_ref, b_ref = refs[0], refs[1]
        ex_refs, o_refs = refs[2:2 + n_ex], refs[2 + n_ex:2 + n_ex + n_out]
        part = lax.dot_general(a_ref[...].astype(BF16), b_ref[...].astype(BF16), dims, preferred_element_type=F32)

        def finish(r):
            for o_ref, v in zip(o_refs, epilogue(r, *[e[...] for e in ex_refs])):
                o_ref[...] = v.astype(o_ref.dtype)

        if nk == 1:
            finish(part)
        else:
            acc_ref = refs[-1]
            k = pl.program_id(2)

            @pl.when(k == 0)
            def _():
                acc_ref[...] = part

            @pl.when(k > 0)
            def _():
                acc_ref[...] += part

            @pl.when(k == nk - 1)
            def _():
                finish(acc_ref[...])

    outs, jouts = _hosted_call(
        body, job, name=name, grid=(M // bm, N // bn, nk), in_specs=[a_spec, b_spec] + [o_spec] * n_ex,
        out_specs=[o_spec] * n_out, out_shape=[_sds((M, N), d) for d in out_dtypes],
        scratch_shapes=[pltpu.VMEM((bm, bn), F32)] if nk > 1 else [],
        semantics=("parallel", "parallel", "arbitrary"), args=(a, b, *extras))
    res = list(outs) + ([jouts] if job is not None else [])
    return res[0] if len(res) == 1 else tuple(res)


def _norm_fwd(x, g, b, *, res=None, alpha=1.0, center=True, eps, want_f32=True, want_bf16=True, col=0, job=None,
              name):
    S, W = x.shape[0], g.shape[-1]
    bs = _pick(S, 256, 16)
    has_res, has_b = res is not None, b is not None

    def body(*refs):
        it = iter(refs)
        x_ref = next(it)
        res_ref = next(it) if has_res else None
        g_ref = next(it)
        b_ref = next(it) if has_b else None
        y_ref = next(it) if want_f32 else None
        yb_ref = next(it) if want_bf16 else None
        xh_ref, r_ref = next(it), next(it)
        z = x_ref[...]
        if has_res:
            z = alpha * z + res_ref[...]
        if center:
            z = z - jnp.mean(z, axis=-1, keepdims=True)
        rstd = lax.rsqrt(jnp.mean(z * z, axis=-1, keepdims=True) + eps)
        xh = z * rstd
        y = xh * g_ref[...]
        if has_b:
            y = y + b_ref[...]
        if want_f32:
            y_ref[...] = y
        if want_bf16:
            yb_ref[...] = y.astype(BF16)
        xh_ref[...] = xh
        r_ref[...] = rstd

    row = pl.BlockSpec((bs, W), lambda i: (i, 0))
    vec = pl.BlockSpec((1, W), lambda i: (0, 0))
    in_specs = ([pl.BlockSpec((bs, W), lambda i: (i, col))] + ([row] if has_res else []) + [vec]
                + ([vec] if has_b else []))
    args = (x,) + ((res,) if has_res else ()) + (g.reshape(1, W),) + ((b.reshape(1, W),) if has_b else ())
    out_specs = ([row] if want_f32 else []) + ([row] if want_bf16 else []) + [row, pl.BlockSpec((bs, 1), lambda i: (i, 0))]
    out_shape = (([_sds((S, W), F32)] if want_f32 else []) + ([_sds((S, W), BF16)] if want_bf16 else [])
                 + [_sds((S, W), F32), _sds((S, 1), F32)])
    outs, jouts = _hosted_call(body, job, name=name, grid=(S // bs,), in_specs=in_specs, out_specs=out_specs,
                               out_shape=out_shape, scratch_shapes=[], semantics=("parallel",), args=args)
    outs = list(outs)
    y = outs.pop(0) if want_f32 else None
    yb = outs.pop(0) if want_bf16 else None
    return (y, yb, outs[0], outs[1]) + (() if job is None else (jouts,))


def _norm_bwd(dy, xh, rstd, g, *, center=True, want_f32=True, want_bf16=True, job=None, name):
    S, W = dy.shape
    bs = _pick(S, 256, 16)

    def body(*refs):
        dy_ref, xh_ref, r_ref, g_ref = refs[:4]
        it = iter(refs[4:])
        dz_ref = next(it) if want_f32 else None
        dzb_ref = next(it) if want_bf16 else None
        dg_ref, db_ref = next(it), next(it)

        @pl.when(pl.program_id(0) == 0)
        def _():
            dg_ref[...] = jnp.zeros_like(dg_ref)
            db_ref[...] = jnp.zeros_like(db_ref)

        dyv = dy_ref[...]
        xhv = xh_ref[...]
        dyg = dyv * g_ref[...]
        m2 = jnp.mean(dyg * xhv, axis=-1, keepdims=True)
        t = dyg - xhv * m2
        if center:
            t = t - jnp.mean(dyg, axis=-1, keepdims=True)
        dz = r_ref[...] * t
        if want_f32:
            dz_ref[...] = dz
        if want_bf16:
            dzb_ref[...] = dz.astype(BF16)
        dg_ref[...] += jnp.sum(dyv * xhv, axis=0, keepdims=True)
        db_ref[...] += jnp.sum(dyv, axis=0, keepdims=True)

    row = pl.BlockSpec((bs, W), lambda i: (i, 0))
    vec = pl.BlockSpec((1, W), lambda i: (0, 0))
    out_specs = ([row] if want_f32 else []) + ([row] if want_bf16 else []) + [vec, vec]
    out_shape = (([_sds((S, W), F32)] if want_f32 else []) + ([_sds((S, W), BF16)] if want_bf16 else [])
                 + [_sds((1, W), F32), _sds((1, W), F32)])
    outs, jouts = _hosted_call(body, job, name=name, grid=(S // bs,),
                               in_specs=[row, row, pl.BlockSpec((bs, 1), lambda i: (i, 0)), vec],
                               out_specs=out_specs, out_shape=out_shape, scratch_shapes=[], semantics=("arbitrary",),
                               args=(dy, xh, rstd, g.reshape(1, W)))
    outs = list(outs)
    dz = outs.pop(0) if want_f32 else None
    dzb = outs.pop(0) if want_bf16 else None
    return (dz, dzb, outs[0][0], outs[1][0]) + (() if job is None else (jouts,))


def _rot_group(t, c, s):
    half = MLA_ROPE // 2
    lane = lax.broadcasted_iota(jnp.int32, t.shape, 1)
    swapped = jnp.where(lane < half, pltpu.roll(t, LANES - half, 1), pltpu.roll(t, half, 1))
    return t * c + swapped * s


def _mla_prep(q3, kv3, krp, c128, s128, *, kr_col=0, name):
    H, S, DP = q3.shape
    bs = _pick(S, 2048, 16)

    def body(q_ref, kv_ref, kr_ref, c_ref, s_ref, qh_ref, kh_ref, vh_ref):
        c, s = c_ref[...], s_ref[...]
        q, kv = q_ref[...], kv_ref[...]
        qh_ref[...] = jnp.concatenate([q[:, :LANES], _rot_group(q[:, LANES:], c, s)], axis=1).astype(BF16)
        kh_ref[...] = jnp.concatenate([kv[:, :LANES], _rot_group(kr_ref[...], c, s)], axis=1).astype(BF16)
        vh_ref[...] = kv[:, LANES:].astype(BF16)

    head = pl.BlockSpec((None, bs, DP), lambda h, i: (h, i, 0))
    tab = pl.BlockSpec((bs, LANES), lambda h, i: (i, 0))
    return _pcall(
        body, name=name, grid=(H, S // bs),
        in_specs=[head, head, pl.BlockSpec((bs, LANES), lambda h, i: (i, kr_col)), tab, tab],
        out_specs=[head, head, pl.BlockSpec((None, bs, LANES), lambda h, i: (h, i, 0))],
        out_shape=[_sds((H, S, DP), BF16), _sds((H, S, DP), BF16), _sds((H, S, LANES), BF16)],
        compiler_params=_params(("parallel", "parallel")),
    )(q3, kv3, krp, c128, s128)


def _mla_unprep(dqh, dkh, dvh, c128, s128, *, name):
    H, S, DP = dqh.shape
    bs = _pick(S, 2048, 16)

    def body(dq_ref, dk_ref, dv_ref, c_ref, s_ref, oq_ref, okv_ref, okr_ref):
        c, ns = c_ref[...], -s_ref[...]
        dq, dk = dq_ref[...], dk_ref[...]
        oq_ref[...] = jnp.concatenate([dq[:, :LANES], _rot_group(dq[:, LANES:], c, ns)], axis=1).astype(BF16)
        okv_ref[...] = jnp.concatenate([dk[:, :LANES], dv_ref[...]], axis=1).astype(BF16)
        dkr = _rot_group(dk[:, LANES:], c, ns)

        @pl.when(pl.program_id(1) == 0)
        def _():
            okr_ref[...] = dkr

        @pl.when(pl.program_id(1) > 0)
        def _():
            okr_ref[...] += dkr

    head = pl.BlockSpec((None, bs, DP), lambda i, h: (h, i, 0))
    tab = pl.BlockSpec((bs, LANES), lambda i, h: (i, 0))
    return _pcall(
        body, name=name, grid=(S // bs, H),
        in_specs=[head, head, pl.BlockSpec((None, bs, LANES), lambda i, h: (h, i, 0)), tab, tab],
        out_specs=[head, head, tab],
        out_shape=[_sds((H, S, DP), BF16), _sds((H, S, DP), BF16), _sds((S, LANES), F32)],
        compiler_params=_params(("parallel", "arbitrary")),
    )(dqh, dkh, dvh, c128, s128)


def _mm_call(a, b, *, grid, a_spec, b_spec, o_spec, dims, nk, out_shape, name):
    def body(*refs):
        a_ref, b_ref, o_ref = refs[:3]
        part = lax.dot_general(a_ref[...].astype(BF16), b_ref[...].astype(BF16), dims, preferred_element_type=F32)
        if nk == 1:
            o_ref[...] = part.astype(o_ref.dtype)
        else:
            acc_ref = refs[3]
            k = pl.program_id(2)

            @pl.when(k == 0)
            def _():
                acc_ref[...] = part

            @pl.when(k > 0)
            def _():
                acc_ref[...] += part

            @pl.when(k == nk - 1)
            def _():
                o_ref[...] = acc_ref[...].astype(o_ref.dtype)

    return _pcall(body, name=name, grid=grid, in_specs=[a_spec, b_spec], out_specs=o_spec, out_shape=out_shape,
                  scratch_shapes=[pltpu.VMEM(o_spec.block_shape, F32)] if nk > 1 else [],
                  compiler_params=_params(("parallel", "parallel", "arbitrary")))(a, b)


def _mm_heads_out(a, w3, *, name):
    S, K = a.shape
    H, n, _ = w3.shape
    bm = _pick(S, 4096, 16)
    out = _mm_call(
        a, w3.reshape(H * n, K), grid=(S // bm, H, 1), nk=1, dims=(((1,), (1,)), ((), ())),
        a_spec=pl.BlockSpec((bm, K), lambda i, j, k: (i, 0)), b_spec=pl.BlockSpec((n, K), lambda i, j, k: (j, 0)),
        o_spec=pl.BlockSpec((bm, n), lambda i, j, k: (j * (S // bm) + i, 0)), out_shape=_sds((H * S, n), F32), name=name)
    return out.reshape(H, S, n)


def _mm_heads_sum(a3, w3, *, name):
    H, S, n = a3.shape
    N = w3.shape[2]
    bm, bn = _pick(S, 2048, 16), _pick(N, 512, 128)
    return _mm_call(
        a3.reshape(H * S, n), w3.reshape(H * n, N), grid=(S // bm, N // bn, H), nk=H, dims=(((1,), (0,)), ((), ())),
        a_spec=pl.BlockSpec((bm, n), lambda i, j, k: (k * (S // bm) + i, 0)),
        b_spec=pl.BlockSpec((n, bn), lambda i, j, k: (k, j)),
        o_spec=pl.BlockSpec((bm, bn), lambda i, j, k: (i, j)), out_shape=_sds((S, N), F32), name=name)


def _mm_heads_tn(a3, b, *, name):
    H, S, n = a3.shape
    N = b.shape[1]
    bk, bn = _pick(S, 4096, 128), _pick(N, 512, 128)
    nk = S // bk
    return _mm_call(
        a3.reshape(H * S, n), b, grid=(H, N // bn, nk), nk=nk, dims=(((0,), (0,)), ((), ())),
        a_spec=pl.BlockSpec((bk, n), lambda i, j, k: (i * nk + k, 0)),
        b_spec=pl.BlockSpec((bk, bn), lambda i, j, k: (k, j)),
        o_spec=pl.BlockSpec((n, bn), lambda i, j, k: (i, j)), out_shape=_sds((H * n, N), BF16), name=name)


def _chunk_mask(row0, col0, B, G):
    rows = lax.shift_right_logical(row0 + lax.broadcasted_iota(jnp.int32, (B, G), 0), CHUNK_SHIFT)
    cols = lax.shift_right_logical(col0 + lax.broadcasted_iota(jnp.int32, (B, G), 1), CHUNK_SHIFT)
    return rows >= cols


def _mask_tail(x, qi, B, G, fill):
    L = x.shape[1]
    tail = jnp.where(_chunk_mask(qi * B, L - G, B, G), x[:, L - G:], fill)
    return tail if L == G else jnp.concatenate([x[:, :L - G], tail], axis=1)


def _for_key_prefix(qi, B, G, S, fn):
    per = G // B
    for b in range(S // G):
        pl.when(qi // per == b)(lambda b=b: fn((b + 1) * G))


def _nt(a, b):
    return lax.dot_general(a, b, (((1,), (1,)), ((), ())), preferred_element_type=F32)


def _nn(a, b):
    return lax.dot_general(a, b, (((1,), (0,)), ((), ())), preferred_element_type=F32)


def _tn(a, b):
    return lax.dot_general(a, b, (((0,), (0,)), ((), ())), preferred_element_type=F32)


def _attn_fwd(qh, kh, vh, *, job=None, name):
    H, S, DQ = qh.shape
    DV = vh.shape[-1]
    B = _pick(S, ATTN_BLOCK_FWD, CHUNK)
    G = _pick(S, ATTN_KEY_STEP, B)
    scale = float(MLA_QK) ** -0.5
    neg = float(jnp.finfo(jnp.float32).min)

    def body(q_ref, k_ref, v_ref, o_ref, lse_ref):
        qi = pl.program_id(1)

        def run(L):
            raw = _mask_tail(_nt(q_ref[...], k_ref[0:L, :]), qi, B, G, neg)
            m = jnp.max(raw, axis=-1, keepdims=True)
            e = jnp.exp2((raw - m) * (scale * LOG2_E))
            l = jnp.sum(e, axis=-1, keepdims=True)
            o_ref[...] = _nn((e * (1.0 / l)).astype(BF16), v_ref[0:L, :])
            lse_ref[...] = m * scale + jnp.log(l)

        _for_key_prefix(qi, B, G, S, run)

    outs, jouts = _hosted_call(
        body, job, name=name, grid=(H, S // B),
        in_specs=[pl.BlockSpec((None, B, DQ), lambda h, i: (h, i, 0)),
                  pl.BlockSpec((None, S, DQ), lambda h, i: (h, 0, 0)),
                  pl.BlockSpec((None, S, DV), lambda h, i: (h, 0, 0))],
        out_specs=[pl.BlockSpec((B, DV), lambda h, i: (i, h)),
                   pl.BlockSpec((None, B, 1), lambda h, i: (h, i, 0))],
        out_shape=[_sds((S, H * DV), F32), _sds((H, S, 1), F32)], scratch_shapes=[],
        semantics=("parallel", "parallel"), args=(qh, kh, vh))
    return (outs[0], outs[1]) if job is None else (outs[0], outs[1], jouts)


def _attn_bwd(qh, kh, vh, o, do, lse, *, do_col0, job=None, name):
    H, S, DQ = qh.shape
    DV = vh.shape[-1]
    B = _pick(S, ATTN_BLOCK, CHUNK)
    G = _pick(S, ATTN_KEY_STEP, B)
    scale = float(MLA_QK) ** -0.5

    def body(q_ref, k_ref, v_ref, o_ref, do_ref, lse_ref, dq_ref, dk_ref, dv_ref):
        qi = pl.program_id(1)

        @pl.when(qi == 0)
        def _():
            dk_ref[...] = jnp.zeros_like(dk_ref)
            dv_ref[...] = jnp.zeros_like(dv_ref)

        def run(L):
            q = q_ref[...]
            k = k_ref[0:L, :]
            dov = do_ref[...]
            dob = dov.astype(BF16)
            p = jnp.exp2(_nt(q, k) * (scale * LOG2_E) - lse_ref[...] * LOG2_E)
            p = _mask_tail(p, qi, B, G, 0.0)
            dsum = jnp.sum(dov * o_ref[...], axis=-1, keepdims=True)
            ds = (p * (_nt(dob, v_ref[0:L, :]) - dsum)).astype(BF16)
            dv_ref[0:L, :] += _tn(p.astype(BF16), dob)
            dk_ref[0:L, :] += _tn(ds, q)
            dq_ref[...] = _nn(ds, k) * scale

        _for_key_prefix(qi, B, G, S, run)

        @pl.when(qi == S // B - 1)
        def _():
            dk_ref[...] = dk_ref[...] * scale

    outs, jouts = _hosted_call(
        body, job, name=name, grid=(H, S // B),
        in_specs=[pl.BlockSpec((None, B, DQ), lambda h, i: (h, i, 0)),
                  pl.BlockSpec((None, S, DQ), lambda h, i: (h, 0, 0)),
                  pl.BlockSpec((None, S, DV), lambda h, i: (h, 0, 0)),
                  pl.BlockSpec((B, DV), lambda h, i: (i, h)),
                  pl.BlockSpec((B, DV), lambda h, i: (i, do_col0 + h)),
                  pl.BlockSpec((None, B, 1), lambda h, i: (h, i, 0))],
        out_specs=[pl.BlockSpec((None, B, DQ), lambda h, i: (h, i, 0)),
                   pl.BlockSpec((None, S, DQ), lambda h, i: (h, 0, 0)),
                   pl.BlockSpec((None, S, DV), lambda h, i: (h, 0, 0))],
        out_shape=[_sds((H, S, DQ), F32), _sds((H, S, DQ), F32), _sds((H, S, DV), F32)], scratch_shapes=[],
        semantics=("parallel", "arbitrary"), args=(qh, kh, vh, o, do, lse))
    return tuple(outs) if job is None else (*outs, jouts)


def _ret_tables(T):
    H = RET_HEADS
    log_gamma = jnp.log1p(-jnp.exp2(-5.0 - jnp.arange(H, dtype=F32)))
    idx = jnp.arange(T, dtype=F32)
    chunk = jnp.arange(T) // CHUNK
    visible = chunk[:, None] >= chunk[None, :]
    intra = jnp.where(visible[None], jnp.exp(log_gamma[:, None, None] * jnp.abs(idx[:, None] - idx[None, :])), 0.0)
    qd = jnp.exp(log_gamma[:, None] * (idx + 1.0))[:, :, None]
    kd = jnp.exp(log_gamma[:, None] * (T - 1.0 - idx))[:, :, None]
    cd = jnp.exp(log_gamma * T)[:, None, None]
    return intra, qd, kd, cd


def _rot(t, c, s):
    half = t.shape[-1] // 2
    t1, t2 = t[:, :half], t[:, half:]
    return jnp.concatenate([t1 * c - t2 * s, t2 * c + t1 * s], axis=-1)


def _rot_t(t, c, s):
    half = t.shape[-1] // 2
    t1, t2 = t[:, :half], t[:, half:]
    return jnp.concatenate([t1 * c + t2 * s, t2 * c - t1 * s], axis=-1)


def _dot(a, b, ca, cb):
    return lax.dot_general(a.astype(BF16), b.astype(BF16), (((ca,), (cb,)), ((), ())), preferred_element_type=F32)


def _ret_fwd(hR, cos, sin, tables, *, name):
    S = hR.shape[0]
    H, L, DK, DV = RET_HEADS, tables[0].shape[-1], RET_QK, RET_V
    NC = S // L
    qscale = float(DK) ** -0.5
    intra, qd, kd, cd = tables

    def body(q_ref, k_ref, v_ref, c_ref, s_ref, in_ref, qd_ref, kd_ref, cd_ref, o_ref, st_ref, state):
        @pl.when(pl.program_id(1) == 0)
        def _():
            state[...] = jnp.zeros_like(state)

        c, s = c_ref[...], s_ref[...]
        q = _rot(q_ref[...], c, s) * qscale
        k = _rot(k_ref[...], c, s)
        v = v_ref[...]
        st = state[...]
        st_ref[...] = st.astype(BF16)
        scores = _dot(q, k, 1, 1) * in_ref[...]
        o_ref[...] = _dot(scores, v, 1, 0) + _dot(q * qd_ref[...], st, 1, 0)
        state[...] = st * cd_ref[...] + _dot(k * kd_ref[...], v, 0, 0)

    blk = lambda off: pl.BlockSpec((L, DK), lambda h, c: (c, off + h))
    rope = pl.BlockSpec((L, DK // 2), lambda h, c: (c, 0))
    return _pcall(
        body, name=name, grid=(H, NC),
        in_specs=[blk(0), blk(H), blk(2 * H), rope, rope,
                  pl.BlockSpec((None, L, L), lambda h, c: (h, 0, 0)),
                  pl.BlockSpec((None, L, 1), lambda h, c: (h, 0, 0)),
                  pl.BlockSpec((None, L, 1), lambda h, c: (h, 0, 0)),
                  pl.BlockSpec((None, 1, 1), lambda h, c: (h, 0, 0))],
        out_specs=[pl.BlockSpec((L, DV), lambda h, c: (c, h)),
                   pl.BlockSpec((None, None, DK, DV), lambda h, c: (h, c, 0, 0))],
        out_shape=[_sds((S, H * DV), F32), _sds((H, NC, DK, DV), BF16)],
        scratch_shapes=[pltpu.VMEM((DK, DV), F32)],
        compiler_params=_params(("arbitrary", "arbitrary")),
    )(hR, hR, hR, cos, sin, intra, qd, kd, cd)


def _ret_bwd(do, hR, states, cos, sin, tables, *, name):
    S = hR.shape[0]
    H, L, DK, DV = RET_HEADS, tables[0].shape[-1], RET_QK, RET_V
    NC = S // L
    qscale = float(DK) ** -0.5
    intra, qd, kd, cd = tables

    def body(do_ref, q_ref, k_ref, v_ref, st_ref, c_ref, s_ref, in_ref, qd_ref, kd_ref, cd_ref,
             dq_ref, dk_ref, dv_ref, dstate):
        @pl.when(pl.program_id(1) == 0)
        def _():
            dstate[...] = jnp.zeros_like(dstate)

        c, s = c_ref[...], s_ref[...]
        q = _rot(q_ref[...], c, s) * qscale
        k = _rot(k_ref[...], c, s)
        v = v_ref[...]
        dov = do_ref[...]
        dst = dstate[...]
        dec = in_ref[...]
        qdv, kdv = qd_ref[...], kd_ref[...]
        scores = _dot(q, k, 1, 1) * dec
        da = _dot(dov, v, 1, 1) * dec
        dv_ref[...] = (_dot(scores, dov, 0, 0) + _dot(k * kdv, dst, 1, 0)).astype(BF16)
        dq = _dot(da, k, 1, 0) + _dot(dov, st_ref[...], 1, 1) * qdv
        dk = _dot(da, q, 0, 0) + _dot(v, dst, 1, 1) * kdv
        dq_ref[...] = _rot_t(dq * qscale, c, s).astype(BF16)
        dk_ref[...] = _rot_t(dk, c, s).astype(BF16)
        dstate[...] = dst * cd_ref[...] + _dot(q * qdv, dov, 0, 0)

    rev = lambda c: NC - 1 - c
    blk = lambda off: pl.BlockSpec((L, DK), lambda h, c: (rev(c), off + h))
    rope = pl.BlockSpec((L, DK // 2), lambda h, c: (rev(c), 0))
    out = pl.BlockSpec((L, DK), lambda h, c: (rev(c), h))
    return _pcall(
        body, name=name, grid=(H, NC),
        in_specs=[out, blk(0), blk(H), blk(2 * H),
                  pl.BlockSpec((None, None, DK, DV), lambda h, c: (h, rev(c), 0, 0)), rope, rope,
                  pl.BlockSpec((None, L, L), lambda h, c: (h, 0, 0)),
                  pl.BlockSpec((None, L, 1), lambda h, c: (h, 0, 0)),
                  pl.BlockSpec((None, L, 1), lambda h, c: (h, 0, 0)),
                  pl.BlockSpec((None, 1, 1), lambda h, c: (h, 0, 0))],
        out_specs=[out, out, out],
        out_shape=[_sds((S, H * DK), BF16)] * 3,
        scratch_shapes=[pltpu.VMEM((DK, DV), F32)],
        compiler_params=_params(("arbitrary", "arbitrary")),
    )(do, hR, hR, hR, states, cos, sin, intra, qd, kd, cd)


def _gn_gate_fwd(o, hR, g, b, *, name):
    S = o.shape[0]
    H, DV = RET_HEADS, RET_V
    bs = _pick(S, 512, 16)

    def body(o_ref, rg_ref, g_ref, b_ref, r_ref):
        z = o_ref[...]
        z = z - jnp.mean(z, axis=-1, keepdims=True)
        xh = z * lax.rsqrt(jnp.mean(z * z, axis=-1, keepdims=True) + GN_EPS)
        rg = rg_ref[...]
        r_ref[...] = ((rg * jax.nn.sigmoid(rg)) * (xh * g_ref[...] + b_ref[...])).astype(BF16)

    row = pl.BlockSpec((bs, DV), lambda i, h: (i, h))
    vec = pl.BlockSpec((1, DV), lambda i, h: (0, h))
    return _pcall(
        body, name=name, grid=(S // bs, H),
        in_specs=[row, pl.BlockSpec((bs, DV), lambda i, h: (i, 3 * H + h)), vec, vec],
        out_specs=row, out_shape=_sds((S, H * DV), BF16),
        compiler_params=_params(("parallel", "parallel")),
    )(o, hR, g.reshape(1, H * DV), b.reshape(1, H * DV))


def _gn_gate_bwd(dr, o, hR, g, b, *, dr_col0, name):
    S = o.shape[0]
    H, DV = RET_HEADS, RET_V
    bs = _pick(S, 512, 16)

    def body(dr_ref, o_ref, rg_ref, g_ref, b_ref, do_ref, drg_ref, dg_ref, db_ref):
        @pl.when(pl.program_id(1) == 0)
        def _():
            dg_ref[...] = jnp.zeros_like(dg_ref)
            db_ref[...] = jnp.zeros_like(db_ref)

        z = o_ref[...]
        z = z - jnp.mean(z, axis=-1, keepdims=True)
        rstd = lax.rsqrt(jnp.mean(z * z, axis=-1, keepdims=True) + GN_EPS)
        xh = z * rstd
        gv = g_ref[...]
        y = xh * gv + b_ref[...]
        rg = rg_ref[...]
        sg = jax.nn.sigmoid(rg)
        drv = dr_ref[...]
        dy = drv * (rg * sg)
        drg_ref[...] = (drv * y * (sg * (1.0 + rg * (1.0 - sg)))).astype(BF16)
        dg_ref[...] += jnp.sum(dy * xh, axis=0, keepdims=True)
        db_ref[...] += jnp.sum(dy, axis=0, keepdims=True)
        dxh = dy * gv
        do_ref[...] = rstd * (dxh - jnp.mean(dxh, axis=-1, keepdims=True)
                              - xh * jnp.mean(dxh * xh, axis=-1, keepdims=True))

    row = pl.BlockSpec((bs, DV), lambda h, i: (i, h))
    vec = pl.BlockSpec((1, DV), lambda h, i: (0, h))
    do, drg, dg, db = _pcall(
        body, name=name, grid=(H, S // bs),
        in_specs=[pl.BlockSpec((bs, DV), lambda h, i: (i, dr_col0 + h)), row,
                  pl.BlockSpec((bs, DV), lambda h, i: (i, 3 * H + h)), vec, vec],
        out_specs=[row, row, vec, vec],
        out_shape=[_sds((S, H * DV), F32), _sds((S, H * DV), BF16), _sds((1, H * DV), F32), _sds((1, H * DV), F32)],
        compiler_params=_params(("arbitrary", "arbitrary")),
    )(dr, o, hR, g.reshape(1, H * DV), b.reshape(1, H * DV))
    return do, drg, dg[0], db[0]


def _gate_up(xb, wg, wu, *, job=None, name):
    S, D = xb.shape
    F = wg.shape[0]
    bm, bn = _pick(S, 1024, 16), _pick(F, 512, 128)

    def body(x_ref, g_ref, u_ref, og_ref, ou_ref, oa_ref):
        x = x_ref[...]
        g = _nt(x, g_ref[...])
        u = _nt(x, u_ref[...])
        og_ref[...] = g.astype(BF16)
        ou_ref[...] = u.astype(BF16)
        oa_ref[...] = ((g * jax.nn.sigmoid(g)) * u).astype(BF16)

    blk = pl.BlockSpec((bm, bn), lambda i, j: (i, j))
    wspec = pl.BlockSpec((bn, D), lambda i, j: (j, 0))
    outs, jouts = _hosted_call(
        body, job, name=name, grid=(S // bm, F // bn),
        in_specs=[pl.BlockSpec((bm, D), lambda i, j: (i, 0)), wspec, wspec], out_specs=[blk] * 3,
        out_shape=[_sds((S, F), BF16)] * 3, scratch_shapes=[], semantics=("parallel", "parallel"), args=(xb, wg, wu))
    return tuple(outs) if job is None else (*outs, jouts)


def _swiglu_bwd(da, g, u):
    g, u = g.astype(F32), u.astype(F32)
    sg = jax.nn.sigmoid(g)
    return da * u * (sg * (1.0 + g * (1.0 - sg))), da * (g * sg)


def _loss_head(y, t, *, name):
    S, D = y.shape
    bs = _pick(S, 256, 8)
    inv_d = 1.0 / D

    def body(y_ref, t_ref, dy_ref, l_ref):
        @pl.when(pl.program_id(0) == 0)
        def _():
            l_ref[...] = jnp.zeros_like(l_ref)

        e = y_ref[...] - t_ref[...]
        dy_ref[...] = e * inv_d
        l_ref[...] += 0.5 * jnp.sum(jnp.mean(e * e, axis=-1, keepdims=True), axis=0, keepdims=True)

    row = pl.BlockSpec((bs, D), lambda i: (i, 0))
    dy, l = _pcall(
        body, name=name, grid=(S // bs,), in_specs=[row, row],
        out_specs=[row, pl.BlockSpec((1, 1), lambda i: (0, 0))],
        out_shape=[_sds((S, D), F32), _sds((1, 1), F32)],
        compiler_params=_params(("arbitrary",)),
    )(y, t)
    return dy, l[0, 0]


def _adamw(w, g, m, v, *, name):
    shape = w.shape
    C = shape[-1]
    R = w.size // C
    br = _pick(R, 512, 8)

    def body(w_ref, g_ref, m_ref, v_ref, d_ref, nm_ref, nv_ref):
        gv = g_ref[...]
        mn = ADAM_B1 * m_ref[...] + (1.0 - ADAM_B1) * gv
        vn = ADAM_B2 * v_ref[...] + (1.0 - ADAM_B2) * (gv * gv)
        m_hat = mn / (1.0 - ADAM_B1 ** ADAM_STEP)
        v_hat = vn / (1.0 - ADAM_B2 ** ADAM_STEP)
        d_ref[...] = -ADAM_LR * (m_hat / (jnp.sqrt(v_hat) + ADAM_EPS) + ADAM_WD * w_ref[...])
        nm_ref[...] = mn
        nv_ref[...] = vn

    blk = pl.BlockSpec((br, C), lambda i: (i, 0))
    outs = _pcall(body, name=name, grid=(R // br,), in_specs=[blk] * 4, out_specs=[blk] * 3,
                  out_shape=[_sds((R, C), F32)] * 3,
                  compiler_params=_params(("parallel",)))(*[a.reshape(R, C) for a in (w, g, m, v)])
    return tuple(o.reshape(shape) for o in outs)


def _slot_block(n, k):
    return (_pick(n, 1024, 16) if n % 16 == 0 else n), _pick(k, 1024, 128)


def _sum_slots(x, *, name):
    ns, n, k = x.shape
    br, bc = _slot_block(n, k)

    def body(x_ref, o_ref):
        acc = x_ref[0].astype(F32)
        for s in range(1, ns):
            acc = acc + x_ref[s].astype(F32)
        o_ref[...] = acc

    return _pcall(body, name=name, grid=(n // br, k // bc),
                  in_specs=[pl.BlockSpec((ns, br, bc), lambda i, j: (0, i, j))],
                  out_specs=pl.BlockSpec((br, bc), lambda i, j: (i, j)),
                  out_shape=_sds((n, k), F32), compiler_params=_params(("parallel", "parallel")))(x)


def _pair_sum(g, r, *, name):
    _, n, k = r.shape
    br, bc = _slot_block(n, k)
    core = lax.axis_index("c").astype(jnp.int32).reshape(1)

    def body(c_ref, g_ref, r_ref, o_ref):
        o_ref[...] = (g_ref[...].astype(F32) + r_ref[...].astype(F32)).astype(o_ref.dtype)

    blk = pl.BlockSpec((None, br, bc), lambda s, i, j, c_ref: (s, i, j))
    return _pcall(
        body, name=name, out_shape=_sds(r.shape, g.dtype),
        grid_spec=pltpu.PrefetchScalarGridSpec(
            num_scalar_prefetch=1, grid=(N_CHIP, n // br, k // bc),
            in_specs=[pl.BlockSpec((None, br, bc), lambda s, i, j, c_ref: (2 * s + c_ref[0], i, j)), blk],
            out_specs=blk),
        compiler_params=_params(("parallel", "parallel", "parallel")),
    )(core, g, r)


N_PEER = N_DEV - 1
N_CHIP = N_DEV // 2
HOST_TAIL_FRACTION = 8


def _coords():
    return lax.axis_index("x"), lax.axis_index("y"), lax.axis_index("c")


def _ag_phase(phase, x_refs, out_refs, send_sems, recv_sems, local_sems):
    n = len(x_refs)
    x, y, c = _coords()
    me, sibling = (x, y, c), (x, y, 1 - c)
    chips = [(1 - x, y), (x, 1 - y), (1 - x, 1 - y)]

    def copy(a, k, block, to, src=None):
        px, py, pc = block
        dst = out_refs[a].at[4 * px + 2 * py + pc]
        return pltpu.make_async_remote_copy(
            src_ref=dst if src is None else src, dst_ref=dst,
            send_sem=send_sems.at[a * N_PEER + k], recv_sem=recv_sems.at[a * N_PEER + k],
            device_id=to, device_id_type=pl.DeviceIdType.MESH)

    def local(a):
        return pltpu.make_async_copy(x_refs[a], out_refs[a].at[4 * x + 2 * y + c], local_sems.at[a])

    def first(a):
        return ([copy(a, 0, me, sibling, src=x_refs[a])]
                + [copy(a, 1 + j, me, (*chip, c), src=x_refs[a]) for j, chip in enumerate(chips)])

    def passed(a, j):
        return copy(a, 4 + j, (*chips[j], c), sibling)

    if phase == 0:
        for a in range(n):
            local(a).start()
            for cp in first(a):
                cp.start()
    elif phase == 1:
        for j in range(len(chips)):
            for a in range(n):
                copy(a, 1 + j, (*chips[j], c), me).wait_recv()
                passed(a, j).start()
    else:
        for a in range(n):
            copy(a, 0, sibling, me).wait_recv()
            for j in range(len(chips)):
                copy(a, 4 + j, (*chips[j], 1 - c), me).wait_recv()
        for a in range(n):
            for cp in first(a):
                cp.wait_send()
            for j in range(len(chips)):
                passed(a, j).wait_send()
            local(a).wait()


def _ag1_phase(phase, x_refs, out_refs, send_sems, recv_sems, local_sems):
    n = len(x_refs)
    x, y, c = _coords()
    chips = [(1 - x, y), (x, 1 - y), (1 - x, 1 - y)]

    def local(a):
        return pltpu.make_async_copy(x_refs[a], out_refs[a].at[4 * x + 2 * y + c], local_sems.at[a])

    def pair(a, j):
        px, py = chips[j]
        sem = a * N_PEER + j
        send = pltpu.make_async_remote_copy(
            src_ref=x_refs[a], dst_ref=out_refs[a].at[4 * x + 2 * y + c], send_sem=send_sems.at[sem],
            recv_sem=recv_sems.at[sem], device_id=(px, py, c), device_id_type=pl.DeviceIdType.MESH)
        recv = pltpu.make_async_remote_copy(
            src_ref=x_refs[a], dst_ref=out_refs[a].at[4 * px + 2 * py + c], send_sem=send_sems.at[sem],
            recv_sem=recv_sems.at[sem], device_id=(px, py, c), device_id_type=pl.DeviceIdType.MESH)
        return send, recv

    if phase == 0:
        for a in range(n):
            local(a).start()
            for j in range(len(chips)):
                pair(a, j)[0].start()
    elif phase == 2:
        for a in range(n):
            for j in range(len(chips)):
                pair(a, j)[1].wait_recv()
        for a in range(n):
            for j in range(len(chips)):
                pair(a, j)[0].wait_send()
            local(a).wait()


def _ag2_phase(phase, buf_refs, out_refs, send_sems, recv_sems, local_sems):
    n = len(out_refs)
    x, y, c = _coords()

    def copy(a, i):
        return pltpu.make_async_remote_copy(
            src_ref=out_refs[a].at[2 * i + c], dst_ref=out_refs[a].at[2 * i + c],
            send_sem=send_sems.at[a * N_PEER + i], recv_sem=recv_sems.at[a * N_PEER + i],
            device_id=(x, y, 1 - c), device_id_type=pl.DeviceIdType.MESH)

    def arrival(a, i):
        return pltpu.make_async_remote_copy(
            src_ref=out_refs[a].at[2 * i + (1 - c)], dst_ref=out_refs[a].at[2 * i + (1 - c)],
            send_sem=send_sems.at[a * N_PEER + i], recv_sem=recv_sems.at[a * N_PEER + i],
            device_id=(x, y, 1 - c), device_id_type=pl.DeviceIdType.MESH)

    if phase == 0:
        for a in range(n):
            for i in range(N_CHIP):
                copy(a, i).start()
    elif phase == 2:
        for a in range(n):
            for i in range(N_CHIP):
                arrival(a, i).wait_recv()
        for a in range(n):
            for i in range(N_CHIP):
                copy(a, i).wait_send()


def _pair_phase(phase, g_refs, out_refs, send_sems, recv_sems, local_sems):
    n = len(g_refs)
    x, y, c = _coords()

    def copy(a, i):
        return pltpu.make_async_remote_copy(
            src_ref=g_refs[a].at[2 * i + (1 - c)], dst_ref=out_refs[a].at[i],
            send_sem=send_sems.at[a * N_PEER + i], recv_sem=recv_sems.at[a * N_PEER + i],
            device_id=(x, y, 1 - c), device_id_type=pl.DeviceIdType.MESH)

    if phase == 0:
        for a in range(n):
            for i in range(N_CHIP):
                copy(a, i).start()
    elif phase == 2:
        for a in range(n):
            for i in range(N_CHIP):
                copy(a, i).wait_recv()
        for a in range(n):
            for i in range(N_CHIP):
                copy(a, i).wait_send()


def _cross_phase(phase, p_refs, out_refs, send_sems, recv_sems, local_sems):
    n = len(p_refs)
    x, y, c = _coords()
    mine = 2 * x + y

    def local(a):
        return pltpu.make_async_copy(p_refs[a].at[mine], out_refs[a].at[mine], local_sems.at[a])

    def pair(a, k):
        px = 1 - x if k & 2 else x
        py = 1 - y if k & 1 else y
        peer = 2 * px + py
        sem = a * N_PEER + k - 1
        send = pltpu.make_async_remote_copy(
            src_ref=p_refs[a].at[peer], dst_ref=out_refs[a].at[mine], send_sem=send_sems.at[sem],
            recv_sem=recv_sems.at[sem], device_id=(px, py, c), device_id_type=pl.DeviceIdType.MESH)
        recv = pltpu.make_async_remote_copy(
            src_ref=p_refs[a].at[peer], dst_ref=out_refs[a].at[peer], send_sem=send_sems.at[sem],
            recv_sem=recv_sems.at[sem], device_id=(px, py, c), device_id_type=pl.DeviceIdType.MESH)
        return send, recv

    if phase == 0:
        for a in range(n):
            local(a).start()
        for k in range(1, N_CHIP):
            for a in range(n):
                pair(a, k)[0].start()
    elif phase == 2:
        for k in range(1, N_CHIP):
            for a in range(n):
                pair(a, k)[1].wait_recv()
        for k in range(1, N_CHIP):
            for a in range(n):
                pair(a, k)[0].wait_send()
        for a in range(n):
            local(a).wait()


_PHASES = {"ag": _ag_phase, "ag1": _ag1_phase, "ag2": _ag2_phase, "pair": _pair_phase, "cross": _cross_phase}


def _job(kind, arrays):
    arrays = list(arrays)
    if kind in ("ag", "ag1"):
        shapes = [(N_DEV,) + a.shape for a in arrays]
    elif kind == "pair":
        shapes = [(N_CHIP,) + a.shape[1:] for a in arrays]
    else:
        shapes = [a.shape for a in arrays]
    return dict(parts=[(kind, len(arrays))], ins=arrays, outs=[_sds(s, a.dtype) for s, a in zip(shapes, arrays)],
                in_place=[kind == "ag2"] * len(arrays))


def _join(jobs):
    jobs = [j for j in jobs if j is not None]
    if not jobs:
        return None
    return dict(parts=[p for j in jobs for p in j["parts"]], ins=[a for j in jobs for a in j["ins"]],
                outs=[o for j in jobs for o in j["outs"]], in_place=[f for j in jobs for f in j["in_place"]])


def _hosted_call(body, job, *, name, grid, in_specs, out_specs, out_shape, scratch_shapes, semantics, args):
    if job is None:
        outs = _pcall(body, name=name, grid=grid, in_specs=in_specs, out_specs=out_specs, out_shape=out_shape,
                      scratch_shapes=scratch_shapes, compiler_params=_params(semantics))(*args)
        return outs, []
    n_in, n_out, n_scr, nj = len(in_specs), len(out_specs), len(scratch_shapes), len(job["ins"])
    parts = job["parts"]
    total = 1
    for g in grid:
        total *= g
    late = total - max(total // HOST_TAIL_FRACTION, 1) if total > 1 else 0

    def wrapped(*refs):
        ins, refs = refs[:n_in], refs[n_in:]
        jins, refs = refs[:nj], refs[nj:]
        outs, refs = refs[:n_out], refs[n_out:]
        jouts, refs = refs[:nj], refs[nj:]
        scr, sems = refs[:n_scr], refs[n_scr:]
        step = 0
        for d, g in enumerate(grid):
            step = step * g + pl.program_id(d)

        def run(phase):
            off = 0
            for i, (kind, n) in enumerate(parts):
                _PHASES[kind](phase, jins[off:off + n], jouts[off:off + n], *sems[3 * i:3 * i + 3])
                off += n

        pl.when(step == 0)(lambda: run(0))
        pl.when(step == late)(lambda: run(1))
        body(*ins, *outs, *scr)
        pl.when(step == total - 1)(lambda: run(2))

    any_spec = pl.BlockSpec(memory_space=pl.ANY)
    sems = []
    for kind, n in parts:
        sems += [pltpu.SemaphoreType.DMA((n * N_PEER,)), pltpu.SemaphoreType.DMA((n * N_PEER,)),
                 pltpu.SemaphoreType.DMA((n,))]
    outs = _pcall(
        wrapped, name=name, grid=grid, in_specs=list(in_specs) + [any_spec] * nj,
        out_specs=list(out_specs) + [any_spec] * nj, out_shape=list(out_shape) + job["outs"],
        scratch_shapes=list(scratch_shapes) + sems,
        input_output_aliases={n_in + i: n_out + i for i in range(nj) if job["in_place"][i]},
        compiler_params=_params(("arbitrary",) * len(grid)),
    )(*args, *job["ins"])
    return outs[:n_out], outs[n_out:]


def _exchange(job, *, name):
    outs, jouts = _hosted_call(lambda: None, job, name=name, grid=(1,), in_specs=[], out_specs=[], out_shape=[],
                               scratch_shapes=[], semantics=("arbitrary",), args=())
    return jouts


_TRANSPOSED = {"w_in": True, "w_out": False, "w_gate": True, "w_up": True, "w_down": False, "w_uq": True, "w_ukv": True}
_BIG = tuple(_TRANSPOSED)
_SMALL = ("ln_in_g", "ln_in_b", "q_norm_g", "kv_norm_g", "ret_gn_g", "ret_gn_b", "ln1_g", "ln1_b", "ln2_g", "ln2_b")
SMALL_COLS = 128


def _rope_tables(pos, dim):
    inv_freq = ROPE_THETA ** (-jnp.arange(0, dim, 2, dtype=F32) / dim)
    ang = pos.astype(F32)[:, None] * inv_freq
    return jnp.cos(ang), jnp.sin(ang)


def _split_in(wt_in):
    c = MLA_Q_LORA + MLA_KV_LORA + MLA_ROPE
    return jnp.pad(wt_in[:c], ((0, LANES - MLA_ROPE), (0, 0))), wt_in[c:]


def _pad_heads(wt_uq):
    H = MLA_HEADS
    w3 = wt_uq.reshape(H, MLA_QK, wt_uq.shape[1])
    return jnp.pad(w3, ((0, 0), (0, MLA_PAD - MLA_QK), (0, 0)))


class _Plan:
    _NEXT_IN = ((1, "w_in"), (1, "w_uq"), (1, "w_ukv"))
    HOSTS = {
        "ln_in": [("ag", ((0, "w_in"),))],
        "b_ln_in": [("cross", ((0, "w_in"),))],
        "l0_hA": [("ag", ((0, "w_uq"), (0, "w_ukv")))],
        "l0_hR": [("ag1", ((0, "w_out"),))],
        "l0_attn": [("ag2", ((0, "w_out"),)), ("ag1", ((0, "w_gate"), (0, "w_up")))],
        "l0_out": [("ag2", ((0, "w_gate"), (0, "w_up")))],
        "l0_gateup": [("ag", ((0, "w_down"),)), ("ag1", _NEXT_IN)],
        "l0_down": [("ag2", _NEXT_IN), ("ag1", ((1, "w_gate"),))],
        "l1_hR": [("ag2", ((1, "w_gate"),)), ("ag1", ((1, "w_out"),))],
        "l1_attn": [("ag2", ((1, "w_out"),)), ("ag1", ((1, "w_up"),))],
        "l1_out": [("ag2", ((1, "w_up"),))],
        "l1_gateup": [("ag", ((1, "w_down"),))],
        "l0_b_dact": [("cross", ((1, "w_in"),))]}
    for _l in (0, 1):
        HOSTS.update({
            "l%d_b_wgate" % _l: [("pair", ((_l, "w_down"),))],
            "l%d_b_wup" % _l: [("pair", ((_l, "w_gate"),)), ("cross", ((_l, "w_down"),))],
            "l%d_b_dx1a" % _l: [("pair", ((_l, "w_up"),)), ("cross", ((_l, "w_gate"),))],
            "l%d_b_dx1b" % _l: [("cross", ((_l, "w_up"),))],
            "l%d_b_attn" % _l: [("pair", ((_l, "w_out"),))],
            "l%d_b_win" % _l: [("pair", ((_l, "w_uq"), (_l, "w_ukv"))), ("cross", ((_l, "w_out"),))],
            "l%d_b_dx" % _l: [("pair", ((_l, "w_in"),)), ("cross", ((_l, "w_uq"), (_l, "w_ukv")))]})

    def __init__(self, local):
        self.local = local
        self.half = {}
        self.full = {}
        self.grads = {}
        self.paired = {}
        self.recv = {}

    def _by_device(self, k):
        return self.grads[k].reshape((N_DEV,) + self.local[k].shape)

    def _make(self, kind, keys):
        if kind in ("ag", "ag1"):
            return _job(kind, [self.local[k] for k in keys])
        if kind == "ag2":
            return _job(kind, [self.half[k] for k in keys])
        if kind == "pair":
            return _job(kind, [self._by_device(k) for k in keys])
        return _job(kind, [_pair_sum(self._by_device(k), self.paired[k], name="pairsum_l%d_%s" % k) for k in keys])

    def _done(self, kind, keys, outs):
        for k, o in zip(keys, outs):
            if kind in ("ag", "ag2"):
                self.full[k] = o.reshape(N_DEV * o.shape[1], o.shape[2])
            elif kind == "ag1":
                self.half[k] = o
            elif kind == "pair":
                self.paired[k] = o
            else:
                self.recv[k] = o

    def _run(self, todo, call):
        outs = call(_join([self._make(kind, keys) for kind, keys in todo]))
        for kind, keys in todo:
            self._done(kind, keys, outs[:len(keys)])
            outs = outs[len(keys):]

    def call(self, fn, name, *args, **kw):
        if name not in self.HOSTS:
            return fn(*args, name=name, **kw)
        res = []

        def run(job):
            *outs, jouts = fn(*args, name=name, job=job, **kw)
            res.extend(outs)
            return jouts

        self._run(self.HOSTS[name], run)
        return res[0] if len(res) == 1 else tuple(res)


def _layer_fwd(x, xb, plan, p, tabs, l):
    S, D = x.shape
    H = MLA_HEADS
    nm = lambda s: "l%d_%s" % (l, s)
    w = lambda n: plan.full[(l, n)]
    mm = lambda name, *a, **kw: plan.call(_mm, nm(name), *a, **kw)
    wt_a, wt_r = _split_in(w("w_in"))
    hA = mm("hA", xb, wt_a, tb=True)
    hR = mm("hR", xb, wt_r, tb=True)
    _, qn, qn_hat, q_rstd = _norm_fwd(hA, p["q_norm_g"], None, center=False, eps=RMS_EPS, want_f32=False,
                                      col=0, name=nm("qnorm"))
    _, kvn, kvn_hat, kv_rstd = _norm_fwd(hA, p["kv_norm_g"], None, center=False, eps=RMS_EPS, want_f32=False,
                                         col=MLA_Q_LORA // MLA_KV_LORA, name=nm("kvnorm"))
    w_uq3 = _pad_heads(w("w_uq"))
    w_ukv3 = w("w_ukv").reshape(H, MLA_NOPE + MLA_V, -1)
    q3 = _mm_heads_out(qn, w_uq3, name=nm("uq"))
    kv3 = _mm_heads_out(kvn, w_ukv3, name=nm("ukv"))
    qh, kh, vh = _mla_prep(q3, kv3, hA, tabs["c128"], tabs["s128"],
                           kr_col=(MLA_Q_LORA + MLA_KV_LORA) // LANES, name=nm("rope"))
    a, lse = plan.call(_attn_fwd, nm("attn"), qh, kh, vh)
    o_ret, states = _ret_fwd(hR, tabs["cos_r"], tabs["sin_r"], tabs["ret"], name=nm("ret"))
    r = _gn_gate_fwd(o_ret, hR, p["ret_gn_g"], p["ret_gn_b"], name=nm("gn"))
    mix_in = jnp.concatenate([a.astype(BF16), r], axis=1)
    mix = mm("out", mix_in, w("w_out"))
    x1, x1b, x1_hat, rstd1 = _norm_fwd(x, p["ln1_g"], p["ln1_b"], res=mix, alpha=p["alpha"], eps=LN_EPS, name=nm("ln1"))
    gb, ub, act = plan.call(_gate_up, nm("gateup"), x1b, w("w_gate"), w("w_up"))
    f = mm("down", act, w("w_down"))
    x2, x2b, x2_hat, rstd2 = _norm_fwd(x1, p["ln2_g"], p["ln2_b"], res=f, alpha=p["alpha"], eps=LN_EPS, name=nm("ln2"))
    saved = dict(xb=xb, qn=qn, qn_hat=qn_hat, q_rstd=q_rstd, kvn=kvn, kvn_hat=kvn_hat, kv_rstd=kv_rstd,
                 qh=qh, kh=kh, vh=vh, a=a, lse=lse, hR=hR, o_ret=o_ret, states=states, mix_in=mix_in,
                 x1b=x1b, x1_hat=x1_hat, rstd1=rstd1, gb=gb, ub=ub, act=act, x2_hat=x2_hat, rstd2=rstd2)
    return x2, x2b, saved


def _layer_bwd(dx2, sv, plan, p, tabs, l):
    S, D = dx2.shape
    H = MLA_HEADS
    nm = lambda s: "l%d_b_%s" % (l, s)
    w = lambda n: plan.full[(l, n)]
    mm = lambda name, *a, **kw: plan.call(_mm, nm(name), *a, **kw)
    alpha = p["alpha"]
    gw, gp = plan.grads, {}
    dz2, dz2b, gp["ln2_g"], gp["ln2_b"] = _norm_bwd(dx2, sv["x2_hat"], sv["rstd2"], p["ln2_g"], name=nm("ln2"))
    dg, du = mm("dact", dz2b, w("w_down"), tb=True, extras=(sv["gb"], sv["ub"]), epilogue=_swiglu_bwd,
                out_dtypes=(BF16, BF16))
    gw[(l, "w_down")] = mm("wdown", sv["act"], dz2b, ta=True, out_dtype=BF16)
    gw[(l, "w_gate")] = mm("wgate", dg, sv["x1b"], ta=True, out_dtype=BF16)
    gw[(l, "w_up")] = mm("wup", du, sv["x1b"], ta=True, out_dtype=BF16)
    t = mm("dx1a", dg, w("w_gate"), add=dz2, add_scale=alpha)
    dx1 = mm("dx1b", du, w("w_up"), add=t)
    dz1, dz1b, gp["ln1_g"], gp["ln1_b"] = _norm_bwd(dx1, sv["x1_hat"], sv["rstd1"], p["ln1_g"], name=nm("ln1"))
    dmix = mm("dmix", dz1b, w("w_out"), tb=True)
    gw[(l, "w_out")] = mm("wout", sv["mix_in"], dz1b, ta=True, out_dtype=BF16)
    ret_col0 = (H * MLA_V) // RET_V
    do_ret, drg, gp["ret_gn_g"], gp["ret_gn_b"] = _gn_gate_bwd(
        dmix, sv["o_ret"], sv["hR"], p["ret_gn_g"], p["ret_gn_b"], dr_col0=ret_col0, name=nm("gn"))
    drq, drk, drv = _ret_bwd(do_ret, sv["hR"], sv["states"], tabs["cos_r"], tabs["sin_r"], tabs["ret"], name=nm("ret"))
    dqh, dkh, dvh = plan.call(_attn_bwd, nm("attn"), sv["qh"], sv["kh"], sv["vh"], sv["a"], dmix, sv["lse"], do_col0=0)
    dq3, dkv3, dkrp = _mla_unprep(dqh, dkh, dvh, tabs["c128"], tabs["s128"], name=nm("rope"))
    w_uq3 = _pad_heads(w("w_uq"))
    w_ukv3 = w("w_ukv").reshape(H, MLA_NOPE + MLA_V, -1)
    g_uq = _mm_heads_tn(dq3, sv["qn"], name=nm("wuq")).reshape(H, MLA_PAD, -1)
    gw[(l, "w_uq")] = g_uq[:, :MLA_QK].reshape(H * MLA_QK, -1)
    dqn = _mm_heads_sum(dq3, w_uq3, name=nm("dqn"))
    gw[(l, "w_ukv")] = _mm_heads_tn(dkv3, sv["kvn"], name=nm("wukv"))
    dkvn = _mm_heads_sum(dkv3, w_ukv3, name=nm("dkvn"))
    _, dcq, gp["q_norm_g"], _ = _norm_bwd(dqn, sv["qn_hat"], sv["q_rstd"], p["q_norm_g"], center=False,
                                          want_f32=False, name=nm("qnorm"))
    _, dckv, gp["kv_norm_g"], _ = _norm_bwd(dkvn, sv["kvn_hat"], sv["kv_rstd"], p["kv_norm_g"], center=False,
                                            want_f32=False, name=nm("kvnorm"))
    dh = jnp.concatenate([dcq, dckv, dkrp[:, :MLA_ROPE].astype(BF16), drq, drk, drv, drg], axis=1)
    gw[(l, "w_in")] = mm("win", dh, sv["xb"], ta=True, out_dtype=BF16)
    dx = mm("dx", dh, w("w_in"), add=dz1, add_scale=alpha)
    return dx, gp


def _local_step(x, target, pos, small, plan, depth):
    alpha = (2 * depth) ** 0.25
    cos_m, sin_m = _rope_tables(pos, MLA_ROPE)
    cos_r, sin_r = _rope_tables(pos, RET_QK)
    zeros = jnp.zeros((x.shape[0], LANES - MLA_ROPE), F32)
    tabs = dict(c128=jnp.concatenate([cos_m, cos_m, zeros], axis=1), s128=jnp.concatenate([-sin_m, sin_m, zeros], axis=1),
                cos_r=cos_r, sin_r=sin_r, ret=_ret_tables(_pick(x.shape[0], RET_BLOCK, CHUNK)))
    h, hb, h_hat, h_rstd = plan.call(_norm_fwd, "ln_in", x, small["ln_in_g"], small["ln_in_b"], eps=LN_EPS)
    saved, ps = [], []
    for l in range(depth):
        p = {k: small[k][l] for k in _SMALL[2:]}
        p["alpha"] = alpha
        h, hb, sv = _layer_fwd(h, hb, plan, p, tabs, l)
        saved.append(sv)
        ps.append(p)
    dy, loss = _loss_head(h, target, name="loss")
    gps = [None] * depth
    for l in reversed(range(depth)):
        dy, gps[l] = _layer_bwd(dy, saved[l], plan, ps[l], tabs, l)
    grad_x, _, g_in_g, g_in_b = plan.call(_norm_bwd, "b_ln_in", dy, h_hat, h_rstd, small["ln_in_g"], want_bf16=False)
    gsmall = {"ln_in_g": g_in_g, "ln_in_b": g_in_b}
    for k in _SMALL[2:]:
        gsmall[k] = jnp.stack([gps[l][k] for l in range(depth)])
    return loss, grad_x, gsmall


def kernel(x, positions, ln_in_g, ln_in_b, w_in, q_norm_g, kv_norm_g, w_uq, w_ukv, ret_gn_g, ret_gn_b, w_out, ln1_g, ln1_b, w_gate, w_up, w_down, ln2_g, ln2_b, loss_target, m_ln_in_g, m_ln_in_b, m_w_in, m_q_norm_g, m_kv_norm_g, m_w_uq, m_w_ukv, m_ret_gn_g, m_ret_gn_b, m_w_out, m_ln1_g, m_ln1_b, m_w_gate, m_w_up, m_w_down, m_ln2_g, m_ln2_b, v_ln_in_g, v_ln_in_b, v_w_in, v_q_norm_g, v_kv_norm_g, v_w_uq, v_w_ukv, v_ret_gn_g, v_ret_gn_b, v_w_out, v_ln1_g, v_ln1_b, v_w_gate, v_w_up, v_w_down, v_ln2_g, v_ln2_b):
    names = ["ln_in_g", "ln_in_b", "w_in", "q_norm_g", "kv_norm_g", "w_uq", "w_ukv", "ret_gn_g", "ret_gn_b", "w_out",
             "ln1_g", "ln1_b", "w_gate", "w_up", "w_down", "ln2_g", "ln2_b"]
    wv = dict(zip(names, (ln_in_g, ln_in_b, w_in, q_norm_g, kv_norm_g, w_uq, w_ukv, ret_gn_g, ret_gn_b, w_out,
                          ln1_g, ln1_b, w_gate, w_up, w_down, ln2_g, ln2_b)))
    mv = dict(zip(names, (m_ln_in_g, m_ln_in_b, m_w_in, m_q_norm_g, m_kv_norm_g, m_w_uq, m_w_ukv, m_ret_gn_g,
                          m_ret_gn_b, m_w_out, m_ln1_g, m_ln1_b, m_w_gate, m_w_up, m_w_down, m_ln2_g, m_ln2_b)))
    vv = dict(zip(names, (v_ln_in_g, v_ln_in_b, v_w_in, v_q_norm_g, v_kv_norm_g, v_w_uq, v_w_ukv, v_ret_gn_g,
                          v_ret_gn_b, v_w_out, v_ln1_g, v_ln1_b, v_w_gate, v_w_up, v_w_down, v_ln2_g, v_ln2_b)))
    depth = w_in.shape[0]
    assert depth == 2, "the exchange plan is written for two layers"

    keys = [(l, n) for l in range(depth) for n in _BIG]
    plan = _Plan({(l, n): (wv[n][l].T if _TRANSPOSED[n] else wv[n][l]).astype(BF16) for l, n in keys})

    small = {n: wv[n] for n in _SMALL}
    loss, grad_x, gsmall = _local_step(x[0], loss_target[0], positions[0], small, plan, depth)
    loss = lax.psum(loss, MESH_AXES)

    gshard = {n: [None] * depth for n in _BIG}
    for l, n in keys:
        tot = _sum_slots(plan.recv[(l, n)], name="sum_l%d_%s" % (l, n))
        gshard[n][l] = tot.T if _TRANSPOSED[n] else tot
    grads = {n: jnp.stack(v) for n, v in gshard.items()}

    flat = jnp.concatenate([gsmall[n].reshape(-1) for n in _SMALL])
    n_small = flat.shape[0]
    rows = -(-n_small // (SMALL_COLS * 8)) * 8
    flat = jnp.pad(flat, (0, rows * SMALL_COLS - n_small)).reshape(rows, SMALL_COLS)
    tot = _sum_slots(_exchange(_job("ag", [flat]), name="ag_small")[0], name="sum_small").reshape(-1)
    off = 0
    for n in _SMALL:
        grads[n] = tot[off:off + wv[n].size].reshape(wv[n].shape)
        off += wv[n].size

    delta, new_m, new_v = {}, {}, {}
    for n in names:
        w2 = wv[n] if wv[n].ndim > 1 else wv[n].reshape(1, -1)
        d, nm_, nv_ = _adamw(w2, grads[n].reshape(w2.shape), mv[n].reshape(w2.shape), vv[n].reshape(w2.shape),
                             name="adamw_" + n)
        delta[n], new_m[n], new_v[n] = d.reshape(wv[n].shape), nm_.reshape(wv[n].shape), nv_.reshape(wv[n].shape)

    return (loss, grad_x[None], *[grads[n] for n in names], *[delta[n] for n in names],
            *[new_m[n] for n in names], *[new_v[n] for n in names])
```

```python
import jax
import jax.numpy as jnp
from jax import lax
from jax.experimental import pallas as pl
from jax.experimental.pallas import tpu as pltpu

F32 = jnp.float32
BF16 = jnp.bfloat16

CHUNK = 64
CHUNK_SHIFT = 6
MLA_HEADS = 8
MLA_Q_LORA = 512
MLA_KV_LORA = 256
MLA_NOPE = 128
MLA_ROPE = 64
MLA_V = 128
MLA_QK = MLA_NOPE + MLA_ROPE
LANES = 128
MLA_PAD = 2 * LANES
RET_HEADS = 4
RET_QK = 256
RET_V = 256
ROPE_THETA = 10000.0
LN_EPS = 1e-5
RMS_EPS = 1e-6
GN_EPS = 1e-5
ADAM_LR = 0.001
ADAM_B1 = 0.9
ADAM_B2 = 0.999
ADAM_EPS = 1e-08
ADAM_WD = 0.01
ADAM_STEP = 10

LOG2_E = 1.4426950408889634

N_DEV = 8
MESH_AXES = ("x", "y", "c")
VMEM_LIMIT_BYTES = 56 * 1024 * 1024
MM_VMEM_BUDGET = 40 * 1024 * 1024
MM_ACC_PENALTY = 0.85
ATTN_BLOCK = 256
ATTN_BLOCK_FWD = 256
ATTN_KEY_STEP = 512
RET_BLOCK = 256


def _pick(n, pref, mult):
    best = None
    d = mult
    while d <= min(n, pref):
        if n % d == 0:
            best = d
        d += mult
    return n if best is None else best


def _divisors(n, mult, cap):
    ds = [d for d in range(mult, min(n, cap) + 1, mult) if n % d == 0]
    return ds or [n]


def _pcall(body, **kw):
    return pl.pallas_call(body, **kw)


def _params(sem):
    return pltpu.CompilerParams(dimension_semantics=sem, vmem_limit_bytes=VMEM_LIMIT_BYTES)


def _sds(shape, dtype):
    return jax.ShapeDtypeStruct(shape, dtype)


def _mm_tiles(M, N, K, ta, sa, sb, tile_bytes):
    best = None
    for bk in _divisors(K, 128, 8192):
        nk = K // bk
        for bm in _divisors(M, 128 if ta else 16, 1024):
            for bn in _divisors(N, 128, 1024):
                vmem = 2 * (bm * bk * sa + bk * bn * sb) + 2 * bm * bn * tile_bytes + (bm * bn * 4 if nk > 1 else 0)
                if vmem > MM_VMEM_BUDGET:
                    continue
                flops_per_byte = 1.0 / (1.0 / bm + (1.0 / max(N, bn) if nk == 1 else 1.0 / bn))
                score = (flops_per_byte * (1.0 if nk == 1 else MM_ACC_PENALTY), bk, bn, bm)
                if best is None or score > best[0]:
                    best = (score, bm, bn, bk)
    assert best is not None, (M, N, K)
    return best[1:]


def _mm(a, b, *, ta=False, tb=False, add=None, add_scale=1.0, out_dtype=F32, extras=(), epilogue=None,
        out_dtypes=None, job=None, name):
    if ta:
        K, M = a.shape
    else:
        M, K = a.shape
    if tb:
        N, K2 = b.shape
    else:
        K2, N = b.shape
    assert K == K2, (a.shape, b.shape, ta, tb)
    if epilogue is None:
        assert not extras and out_dtypes is None
        out_dtypes = (out_dtype,)
        if add is not None:
            extras, epilogue = (add,), lambda r, c: (r + add_scale * c,)
        else:
            epilogue = lambda r: (r,)
    n_ex, n_out = len(extras), len(out_dtypes)
    tile_bytes = sum(e.dtype.itemsize for e in extras) + sum(jnp.dtype(d).itemsize for d in out_dtypes)
    bm, bn, bk = _mm_tiles(M, N, K, ta, a.dtype.itemsize, b.dtype.itemsize, tile_bytes)
    nk = K // bk
    a_spec = (pl.BlockSpec((bk, bm), lambda i, j, k: (k, i)) if ta
              else pl.BlockSpec((bm, bk), lambda i, j, k: (i, k)))
    b_spec = (pl.BlockSpec((bn, bk), lambda i, j, k: (j, k)) if tb
              else pl.BlockSpec((bk, bn), lambda i, j, k: (k, j)))
    o_spec = pl.BlockSpec((bm, bn), lambda i, j, k: (i, j))
    dims = (((0 if ta else 1,), (1 if tb else 0,)), ((), ()))

    def body(*refs):
        a_ref, b_ref = refs[0], refs[1]
        ex_refs, o_refs = refs[2:2 + n_ex], refs[2 + n_ex:2 + n_ex + n_out]
        part = lax.dot_general(a_ref[...].astype(BF16), b_ref[...].astype(BF16), dims, preferred_element_type=F32)

        def finish(r):
            for o_ref, v in zip(o_refs, epilogue(r, *[e[...] for e in ex_refs])):
                o_ref[...] = v.astype(o_ref.dtype)

        if nk == 1:
            finish(part)
        else:
            acc_ref = refs[-1]
            k = pl.program_id(2)

            @pl.when(k == 0)
            def _():
                acc_ref[...] = part

            @pl.when(k > 0)
            def _():
                acc_ref[...] += part

            @pl.when(k == nk - 1)
            def _():
                finish(acc_ref[...])

    outs, jouts = _hosted_call(
        body, job, name=name, grid=(M // bm, N // bn, nk), in_specs=[a_spec, b_spec] + [o_spec] * n_ex,
        out_specs=[o_spec] * n_out, out_shape=[_sds((M, N), d) for d in out_dtypes],
        scratch_shapes=[pltpu.VMEM((bm, bn), F32)] if nk > 1 else [],
        semantics=("parallel", "parallel", "arbitrary"), args=(a, b, *extras))
    res = list(outs) + ([jouts] if job is not None else [])
    return res[0] if len(res) == 1 else tuple(res)


def _norm_fwd(x, g, b, *, res=None, alpha=1.0, center=True, eps, want_f32=True, want_bf16=True, col=0, job=None,
              name):
    S, W = x.shape[0], g.shape[-1]
    bs = _pick(S, 256, 16)
    has_res, has_b = res is not None, b is not None

    def body(*refs):
        it = iter(refs)
        x_ref = next(it)
        res_ref = next(it) if has_res else None
        g_ref = next(it)
        b_ref = next(it) if has_b else None
        y_ref = next(it) if want_f32 else None
        yb_ref = next(it) if want_bf16 else None
        xh_ref, r_ref = next(it), next(it)
        z = x_ref[...]
        if has_res:
            z = alpha * z + res_ref[...]
        if center:
            z = z - jnp.mean(z, axis=-1, keepdims=True)
        rstd = lax.rsqrt(jnp.mean(z * z, axis=-1, keepdims=True) + eps)
        xh = z * rstd
        y = xh * g_ref[...]
        if has_b:
            y = y + b_ref[...]
        if want_f32:
            y_ref[...] = y
        if want_bf16:
            yb_ref[...] = y.astype(BF16)
        xh_ref[...] = xh
        r_ref[...] = rstd

    row = pl.BlockSpec((bs, W), lambda i: (i, 0))
    vec = pl.BlockSpec((1, W), lambda i: (0, 0))
    in_specs = ([pl.BlockSpec((bs, W), lambda i: (i, col))] + ([row] if has_res else []) + [vec]
                + ([vec] if has_b else []))
    args = (x,) + ((res,) if has_res else ()) + (g.reshape(1, W),) + ((b.reshape(1, W),) if has_b else ())
    out_specs = ([row] if want_f32 else []) + ([row] if want_bf16 else []) + [row, pl.BlockSpec((bs, 1), lambda i: (i, 0))]
    out_shape = (([_sds((S, W), F32)] if want_f32 else []) + ([_sds((S, W), BF16)] if want_bf16 else [])
                 + [_sds((S, W), F32), _sds((S, 1), F32)])
    outs, jouts = _hosted_call(body, job, name=name, grid=(S // bs,), in_specs=in_specs, out_specs=out_specs,
                               out_shape=out_shape, scratch_shapes=[], semantics=("parallel",), args=args)
    outs = list(outs)
    y = outs.pop(0) if want_f32 else None
    yb = outs.pop(0) if want_bf16 else None
    return (y, yb, outs[0], outs[1]) + (() if job is None else (jouts,))


def _norm_bwd(dy, xh, rstd, g, *, center=True, want_f32=True, want_bf16=True, job=None, name):
    S, W = dy.shape
    bs = _pick(S, 256, 16)

    def body(*refs):
        dy_ref, xh_ref, r_ref, g_ref = refs[:4]
        it = iter(refs[4:])
        dz_ref = next(it) if want_f32 else None
        dzb_ref = next(it) if want_bf16 else None
        dg_ref, db_ref = next(it), next(it)

        @pl.when(pl.program_id(0) == 0)
        def _():
            dg_ref[...] = jnp.zeros_like(dg_ref)
            db_ref[...] = jnp.zeros_like(db_ref)

        dyv = dy_ref[...]
        xhv = xh_ref[...]
        dyg = dyv * g_ref[...]
        m2 = jnp.mean(dyg * xhv, axis=-1, keepdims=True)
        t = dyg - xhv * m2
        if center:
            t = t - jnp.mean(dyg, axis=-1, keepdims=True)
        dz = r_ref[...] * t
        if want_f32:
            dz_ref[...] = dz
        if want_bf16:
            dzb_ref[...] = dz.astype(BF16)
        dg_ref[...] += jnp.sum(dyv * xhv, axis=0, keepdims=True)
        db_ref[...] += jnp.sum(dyv, axis=0, keepdims=True)

    row = pl.BlockSpec((bs, W), lambda i: (i, 0))
    vec = pl.BlockSpec((1, W), lambda i: (0, 0))
    out_specs = ([row] if want_f32 else []) + ([row] if want_bf16 else []) + [vec, vec]
    out_shape = (([_sds((S, W), F32)] if want_f32 else []) + ([_sds((S, W), BF16)] if want_bf16 else [])
                 + [_sds((1, W), F32), _sds((1, W), F32)])
    outs, jouts = _hosted_call(body, job, name=name, grid=(S // bs,),
                               in_specs=[row, row, pl.BlockSpec((bs, 1), lambda i: (i, 0)), vec],
                               out_specs=out_specs, out_shape=out_shape, scratch_shapes=[], semantics=("arbitrary",),
                               args=(dy, xh, rstd, g.reshape(1, W)))
    outs = list(outs)
    dz = outs.pop(0) if want_f32 else None
    dzb = outs.pop(0) if want_bf16 else None
    return (dz, dzb, outs[0][0], outs[1][0]) + (() if job is None else (jouts,))


def _rot_group(t, c, s):
    half = MLA_ROPE // 2
    lane = lax.broadcasted_iota(jnp.int32, t.shape, 1)
    swapped = jnp.where(lane < half, pltpu.roll(t, LANES - half, 1), pltpu.roll(t, half, 1))
    return t * c + swapped * s


def _mla_prep(q3, kv3, krp, c128, s128, *, kr_col=0, name):
    H, S, DP = q3.shape
    bs = _pick(S, 2048, 16)

    def body(q_ref, kv_ref, kr_ref, c_ref, s_ref, qh_ref, kh_ref, vh_ref):
        c, s = c_ref[...], s_ref[...]
        q, kv = q_ref[...], kv_ref[...]
        qh_ref[...] = jnp.concatenate([q[:, :LANES], _rot_group(q[:, LANES:], c, s)], axis=1).astype(BF16)
        kh_ref[...] = jnp.concatenate([kv[:, :LANES], _rot_group(kr_ref[...], c, s)], axis=1).astype(BF16)
        vh_ref[...] = kv[:, LANES:].astype(BF16)

    head = pl.BlockSpec((None, bs, DP), lambda h, i: (h, i, 0))
    tab = pl.BlockSpec((bs, LANES), lambda h, i: (i, 0))
    return _pcall(
        body, name=name, grid=(H, S // bs),
        in_specs=[head, head, pl.BlockSpec((bs, LANES), lambda h, i: (i, kr_col)), tab, tab],
        out_specs=[head, head, pl.BlockSpec((None, bs, LANES), lambda h, i: (h, i, 0))],
        out_shape=[_sds((H, S, DP), BF16), _sds((H, S, DP), BF16), _sds((H, S, LANES), BF16)],
        compiler_params=_params(("parallel", "parallel")),
    )(q3, kv3, krp, c128, s128)


def _mla_unprep(dqh, dkh, dvh, c128, s128, *, name):
    H, S, DP = dqh.shape
    bs = _pick(S, 2048, 16)

    def body(dq_ref, dk_ref, dv_ref, c_ref, s_ref, oq_ref, okv_ref, okr_ref):
        c, ns = c_ref[...], -s_ref[...]
        dq, dk = dq_ref[...], dk_ref[...]
        oq_ref[...] = jnp.concatenate([dq[:, :LANES], _rot_group(dq[:, LANES:], c, ns)], axis=1).astype(BF16)
        okv_ref[...] = jnp.concatenate([dk[:, :LANES], dv_ref[...]], axis=1).astype(BF16)
        dkr = _rot_group(dk[:, LANES:], c, ns)

        @pl.when(pl.program_id(1) == 0)
        def _():
            okr_ref[...] = dkr

        @pl.when(pl.program_id(1) > 0)
        def _():
            okr_ref[...] += dkr

    head = pl.BlockSpec((None, bs, DP), lambda i, h: (h, i, 0))
    tab = pl.BlockSpec((bs, LANES), lambda i, h: (i, 0))
    return _pcall(
        body, name=name, grid=(S // bs, H),
        in_specs=[head, head, pl.BlockSpec((None, bs, LANES), lambda i, h: (h, i, 0)), tab, tab],
        out_specs=[head, head, tab],
        out_shape=[_sds((H, S, DP), BF16), _sds((H, S, DP), BF16), _sds((S, LANES), F32)],
        compiler_params=_params(("parallel", "arbitrary")),
    )(dqh, dkh, dvh, c128, s128)


def _mm_call(a, b, *, grid, a_spec, b_spec, o_spec, dims, nk, out_shape, name):
    def body(*refs):
        a_ref, b_ref, o_ref = refs[:3]
        part = lax.dot_general(a_ref[...].astype(BF16), b_ref[...].astype(BF16), dims, preferred_element_type=F32)
        if nk == 1:
            o_ref[...] = part.astype(o_ref.dtype)
        else:
            acc_ref = refs[3]
            k = pl.program_id(2)

            @pl.when(k == 0)
            def _():
                acc_ref[...] = part

            @pl.when(k > 0)
            def _():
                acc_ref[...] += part

            @pl.when(k == nk - 1)
            def _():
                o_ref[...] = acc_ref[...].astype(o_ref.dtype)

    return _pcall(body, name=name, grid=grid, in_specs=[a_spec, b_spec], out_specs=o_spec, out_shape=out_shape,
                  scratch_shapes=[pltpu.VMEM(o_spec.block_shape, F32)] if nk > 1 else [],
                  compiler_params=_params(("parallel", "parallel", "arbitrary")))(a, b)


def _mm_heads_out(a, w3, *, name):
    S, K = a.shape
    H, n, _ = w3.shape
    bm = _pick(S, 4096, 16)
    out = _mm_call(
        a, w3.reshape(H * n, K), grid=(S // bm, H, 1), nk=1, dims=(((1,), (1,)), ((), ())),
        a_spec=pl.BlockSpec((bm, K), lambda i, j, k: (i, 0)), b_spec=pl.BlockSpec((n, K), lambda i, j, k: (j, 0)),
        o_spec=pl.BlockSpec((bm, n), lambda i, j, k: (j * (S // bm) + i, 0)), out_shape=_sds((H * S, n), F32), name=name)
    return out.reshape(H, S, n)


def _mm_heads_sum(a3, w3, *, name):
    H, S, n = a3.shape
    N = w3.shape[2]
    bm, bn = _pick(S, 2048, 16), _pick(N, 512, 128)
    return _mm_call(
        a3.reshape(H * S, n), w3.reshape(H * n, N), grid=(S // bm, N // bn, H), nk=H, dims=(((1,), (0,)), ((), ())),
        a_spec=pl.BlockSpec((bm, n), lambda i, j, k: (k * (S // bm) + i, 0)),
        b_spec=pl.BlockSpec((n, bn), lambda i, j, k: (k, j)),
        o_spec=pl.BlockSpec((bm, bn), lambda i, j, k: (i, j)), out_shape=_sds((S, N), F32), name=name)


def _mm_heads_tn(a3, b, *, name):
    H, S, n = a3.shape
    N = b.shape[1]
    bk, bn = _pick(S, 4096, 128), _pick(N, 512, 128)
    nk = S // bk
    return _mm_call(
        a3.reshape(H * S, n), b, grid=(H, N // bn, nk), nk=nk, dims=(((0,), (0,)), ((), ())),
        a_spec=pl.BlockSpec((bk, n), lambda i, j, k: (i * nk + k, 0)),
        b_spec=pl.BlockSpec((bk, bn), lambda i, j, k: (k, j)),
        o_spec=pl.BlockSpec((n, bn), lambda i, j, k: (i, j)), out_shape=_sds((H * n, N), BF16), name=name)


def _chunk_mask(row0, col0, B, G):
    rows = lax.shift_right_logical(row0 + lax.broadcasted_iota(jnp.int32, (B, G), 0), CHUNK_SHIFT)
    cols = lax.shift_right_logical(col0 + lax.broadcasted_iota(jnp.int32, (B, G), 1), CHUNK_SHIFT)
    return rows >= cols


def _mask_tail(x, qi, B, G, fill):
    L = x.shape[1]
    tail = jnp.where(_chunk_mask(qi * B, L - G, B, G), x[:, L - G:], fill)
    return tail if L == G else jnp.concatenate([x[:, :L - G], tail], axis=1)


def _for_key_prefix(qi, B, G, S, fn):
    per = G // B
    for b in range(S // G):
        pl.when(qi // per == b)(lambda b=b: fn((b + 1) * G))


def _nt(a, b):
    return lax.dot_general(a, b, (((1,), (1,)), ((), ())), preferred_element_type=F32)


def _nn(a, b):
    return lax.dot_general(a, b, (((1,), (0,)), ((), ())), preferred_element_type=F32)


def _tn(a, b):
    return lax.dot_general(a, b, (((0,), (0,)), ((), ())), preferred_element_type=F32)


def _attn_fwd(qh, kh, vh, *, job=None, name):
    H, S, DQ = qh.shape
    DV = vh.shape[-1]
    B = _pick(S, ATTN_BLOCK_FWD, CHUNK)
    G = _pick(S, ATTN_KEY_STEP, B)
    scale = float(MLA_QK) ** -0.5
    neg = float(jnp.finfo(jnp.float32).min)

    def body(q_ref, k_ref, v_ref, o_ref, lse_ref):
        qi = pl.program_id(1)

        def run(L):
            raw = _mask_tail(_nt(q_ref[...], k_ref[0:L, :]), qi, B, G, neg)
            m = jnp.max(raw, axis=-1, keepdims=True)
            e = jnp.exp2((raw - m) * (scale * LOG2_E))
            l = jnp.sum(e, axis=-1, keepdims=True)
            o_ref[...] = _nn((e * (1.0 / l)).astype(BF16), v_ref[0:L, :])
            lse_ref[...] = m * scale + jnp.log(l)

        _for_key_prefix(qi, B, G, S, run)

    outs, jouts = _hosted_call(
        body, job, name=name, grid=(H, S // B),
        in_specs=[pl.BlockSpec((None, B, DQ), lambda h, i: (h, i, 0)),
                  pl.BlockSpec((None, S, DQ), lambda h, i: (h, 0, 0)),
                  pl.BlockSpec((None, S, DV), lambda h, i: (h, 0, 0))],
        out_specs=[pl.BlockSpec((B, DV), lambda h, i: (i, h)),
                   pl.BlockSpec((None, B, 1), lambda h, i: (h, i, 0))],
        out_shape=[_sds((S, H * DV), F32), _sds((H, S, 1), F32)], scratch_shapes=[],
        semantics=("parallel", "parallel"), args=(qh, kh, vh))
    return (outs[0], outs[1]) if job is None else (outs[0], outs[1], jouts)


def _attn_bwd(qh, kh, vh, o, do, lse, *, do_col0, job=None, name):
    H, S, DQ = qh.shape
    DV = vh.shape[-1]
    B = _pick(S, ATTN_BLOCK, CHUNK)
    G = _pick(S, ATTN_KEY_STEP, B)
    scale = float(MLA_QK) ** -0.5

    def body(q_ref, k_ref, v_ref, o_ref, do_ref, lse_ref, dq_ref, dk_ref, dv_ref):
        qi = pl.program_id(1)

        @pl.when(qi == 0)
        def _():
            dk_ref[...] = jnp.zeros_like(dk_ref)
            dv_ref[...] = jnp.zeros_like(dv_ref)

        def run(L):
            q = q_ref[...]
            k = k_ref[0:L, :]
            dov = do_ref[...]
            dob = dov.astype(BF16)
            p = jnp.exp2(_nt(q, k) * (scale * LOG2_E) - lse_ref[...] * LOG2_E)
            p = _mask_tail(p, qi, B, G, 0.0)
            dsum = jnp.sum(dov * o_ref[...], axis=-1, keepdims=True)
            ds = (p * (_nt(dob, v_ref[0:L, :]) - dsum)).astype(BF16)
            dv_ref[0:L, :] += _tn(p.astype(BF16), dob)
            dk_ref[0:L, :] += _tn(ds, q)
            dq_ref[...] = _nn(ds, k) * scale

        _for_key_prefix(qi, B, G, S, run)

        @pl.when(qi == S // B - 1)
        def _():
            dk_ref[...] = dk_ref[...] * scale

    outs, jouts = _hosted_call(
        body, job, name=name, grid=(H, S // B),
        in_specs=[pl.BlockSpec((None, B, DQ), lambda h, i: (h, i, 0)),
                  pl.BlockSpec((None, S, DQ), lambda h, i: (h, 0, 0)),
                  pl.BlockSpec((None, S, DV), lambda h, i: (h, 0, 0)),
                  pl.BlockSpec((B, DV), lambda h, i: (i, h)),
                  pl.BlockSpec((B, DV), lambda h, i: (i, do_col0 + h)),
                  pl.BlockSpec((None, B, 1), lambda h, i: (h, i, 0))],
        out_specs=[pl.BlockSpec((None, B, DQ), lambda h, i: (h, i, 0)),
                   pl.BlockSpec((None, S, DQ), lambda h, i: (h, 0, 0)),
                   pl.BlockSpec((None, S, DV), lambda h, i: (h, 0, 0))],
        out_shape=[_sds((H, S, DQ), F32), _sds((H, S, DQ), F32), _sds((H, S, DV), F32)], scratch_shapes=[],
        semantics=("parallel", "arbitrary"), args=(qh, kh, vh, o, do, lse))
    return tuple(outs) if job is None else (*outs, jouts)


def _ret_tables(T):
    H = RET_HEADS
    log_gamma = jnp.log1p(-jnp.exp2(-5.0 - jnp.arange(H, dtype=F32)))
    idx = jnp.arange(T, dtype=F32)
    chunk = jnp.arange(T) // CHUNK
    visible = chunk[:, None] >= chunk[None, :]
    intra = jnp.where(visible[None], jnp.exp(log_gamma[:, None, None] * jnp.abs(idx[:, None] - idx[None, :])), 0.0)
    qd = jnp.exp(log_gamma[:, None] * (idx + 1.0))[:, :, None]
    kd = jnp.exp(log_gamma[:, None] * (T - 1.0 - idx))[:, :, None]
    cd = jnp.exp(log_gamma * T)[:, None, None]
    return intra, qd, kd, cd


def _rot(t, c, s):
    half = t.shape[-1] // 2
    t1, t2 = t[:, :half], t[:, half:]
    return jnp.concatenate([t1 * c - t2 * s, t2 * c + t1 * s], axis=-1)


def _rot_t(t, c, s):
    half = t.shape[-1] // 2
    t1, t2 = t[:, :half], t[:, half:]
    return jnp.concatenate([t1 * c + t2 * s, t2 * c - t1 * s], axis=-1)


def _dot(a, b, ca, cb):
    return lax.dot_general(a.astype(BF16), b.astype(BF16), (((ca,), (cb,)), ((), ())), preferred_element_type=F32)


def _ret_fwd(hR, cos, sin, tables, *, name):
    S = hR.shape[0]
    H, L, DK, DV = RET_HEADS, tables[0].shape[-1], RET_QK, RET_V
    NC = S // L
    qscale = float(DK) ** -0.5
    intra, qd, kd, cd = tables

    def body(q_ref, k_ref, v_ref, c_ref, s_ref, in_ref, qd_ref, kd_ref, cd_ref, o_ref, st_ref, state):
        @pl.when(pl.program_id(1) == 0)
        def _():
            state[...] = jnp.zeros_like(state)

        c, s = c_ref[...], s_ref[...]
        q = _rot(q_ref[...], c, s) * qscale
        k = _rot(k_ref[...], c, s)
        v = v_ref[...]
        st = state[...]
        st_ref[...] = st.astype(BF16)
        scores = _dot(q, k, 1, 1) * in_ref[...]
        o_ref[...] = _dot(scores, v, 1, 0) + _dot(q * qd_ref[...], st, 1, 0)
        state[...] = st * cd_ref[...] + _dot(k * kd_ref[...], v, 0, 0)

    blk = lambda off: pl.BlockSpec((L, DK), lambda h, c: (c, off + h))
    rope = pl.BlockSpec((L, DK // 2), lambda h, c: (c, 0))
    return _pcall(
        body, name=name, grid=(H, NC),
        in_specs=[blk(0), blk(H), blk(2 * H), rope, rope,
                  pl.BlockSpec((None, L, L), lambda h, c: (h, 0, 0)),
                  pl.BlockSpec((None, L, 1), lambda h, c: (h, 0, 0)),
                  pl.BlockSpec((None, L, 1), lambda h, c: (h, 0, 0)),
                  pl.BlockSpec((None, 1, 1), lambda h, c: (h, 0, 0))],
        out_specs=[pl.BlockSpec((L, DV), lambda h, c: (c, h)),
                   pl.BlockSpec((None, None, DK, DV), lambda h, c: (h, c, 0, 0))],
        out_shape=[_sds((S, H * DV), F32), _sds((H, NC, DK, DV), BF16)],
        scratch_shapes=[pltpu.VMEM((DK, DV), F32)],
        compiler_params=_params(("arbitrary", "arbitrary")),
    )(hR, hR, hR, cos, sin, intra, qd, kd, cd)


def _ret_bwd(do, hR, states, cos, sin, tables, *, name):
    S = hR.shape[0]
    H, L, DK, DV = RET_HEADS, tables[0].shape[-1], RET_QK, RET_V
    NC = S // L
    qscale = float(DK) ** -0.5
    intra, qd, kd, cd = tables

    def body(do_ref, q_ref, k_ref, v_ref, st_ref, c_ref, s_ref, in_ref, qd_ref, kd_ref, cd_ref,
             dq_ref, dk_ref, dv_ref, dstate):
        @pl.when(pl.program_id(1) == 0)
        def _():
            dstate[...] = jnp.zeros_like(dstate)

        c, s = c_ref[...], s_ref[...]
        q = _rot(q_ref[...], c, s) * qscale
        k = _rot(k_ref[...], c, s)
        v = v_ref[...]
        dov = do_ref[...]
        dst = dstate[...]
        dec = in_ref[...]
        qdv, kdv = qd_ref[...], kd_ref[...]
        scores = _dot(q, k, 1, 1) * dec
        da = _dot(dov, v, 1, 1) * dec
        dv_ref[...] = (_dot(scores, dov, 0, 0) + _dot(k * kdv, dst, 1, 0)).astype(BF16)
        dq = _dot(da, k, 1, 0) + _dot(dov, st_ref[...], 1, 1) * qdv
        dk = _dot(da, q, 0, 0) + _dot(v, dst, 1, 1) * kdv
        dq_ref[...] = _rot_t(dq * qscale, c, s).astype(BF16)
        dk_ref[...] = _rot_t(dk, c, s).astype(BF16)
        dstate[...] = dst * cd_ref[...] + _dot(q * qdv, dov, 0, 0)

    rev = lambda c: NC - 1 - c
    blk = lambda off: pl.BlockSpec((L, DK), lambda h, c: (rev(c), off + h))
    rope = pl.BlockSpec((L, DK // 2), lambda h, c: (rev(c), 0))
    out = pl.BlockSpec((L, DK), lambda h, c: (rev(c), h))
    return _pcall(
        body, name=name, grid=(H, NC),
        in_specs=[out, blk(0), blk(H), blk(2 * H),
                  pl.BlockSpec((None, None, DK, DV), lambda h, c: (h, rev(c), 0, 0)), rope, rope,
                  pl.BlockSpec((None, L, L), lambda h, c: (h, 0, 0)),
                  pl.BlockSpec((None, L, 1), lambda h, c: (h, 0, 0)),
                  pl.BlockSpec((None, L, 1), lambda h, c: (h, 0, 0)),
                  pl.BlockSpec((None, 1, 1), lambda h, c: (h, 0, 0))],
        out_specs=[out, out, out],
        out_shape=[_sds((S, H * DK), BF16)] * 3,
        scratch_shapes=[pltpu.VMEM((DK, DV), F32)],
        compiler_params=_params(("arbitrary", "arbitrary")),
    )(do, hR, hR, hR, states, cos, sin, intra, qd, kd, cd)


def _gn_gate_fwd(o, hR, g, b, *, name):
    S = o.shape[0]
    H, DV = RET_HEADS, RET_V
    bs = _pick(S, 512, 16)

    def body(o_ref, rg_ref, g_ref, b_ref, r_ref):
        z = o_ref[...]
        z = z - jnp.mean(z, axis=-1, keepdims=True)
        xh = z * lax.rsqrt(jnp.mean(z * z, axis=-1, keepdims=True) + GN_EPS)
        rg = rg_ref[...]
        r_ref[...] = ((rg * jax.nn.sigmoid(rg)) * (xh * g_ref[...] + b_ref[...])).astype(BF16)

    row = pl.BlockSpec((bs, DV), lambda i, h: (i, h))
    vec = pl.BlockSpec((1, DV), lambda i, h: (0, h))
    return _pcall(
        body, name=name, grid=(S // bs, H),
        in_specs=[row, pl.BlockSpec((bs, DV), lambda i, h: (i, 3 * H + h)), vec, vec],
        out_specs=row, out_shape=_sds((S, H * DV), BF16),
        compiler_params=_params(("parallel", "parallel")),
    )(o, hR, g.reshape(1, H * DV), b.reshape(1, H * DV))


def _gn_gate_bwd(dr, o, hR, g, b, *, dr_col0, name):
    S = o.shape[0]
    H, DV = RET_HEADS, RET_V
    bs = _pick(S, 512, 16)

    def body(dr_ref, o_ref, rg_ref, g_ref, b_ref, do_ref, drg_ref, dg_ref, db_ref):
        @pl.when(pl.program_id(1) == 0)
        def _():
            dg_ref[...] = jnp.zeros_like(dg_ref)
            db_ref[...] = jnp.zeros_like(db_ref)

        z = o_ref[...]
        z = z - jnp.mean(z, axis=-1, keepdims=True)
        rstd = lax.rsqrt(jnp.mean(z * z, axis=-1, keepdims=True) + GN_EPS)
        xh = z * rstd
        gv = g_ref[...]
        y = xh * gv + b_ref[...]
        rg = rg_ref[...]
        sg = jax.nn.sigmoid(rg)
        drv = dr_ref[...]
        dy = drv * (rg * sg)
        drg_ref[...] = (drv * y * (sg * (1.0 + rg * (1.0 - sg)))).astype(BF16)
        dg_ref[...] += jnp.sum(dy * xh, axis=0, keepdims=True)
        db_ref[...] += jnp.sum(dy, axis=0, keepdims=True)
        dxh = dy * gv
        do_ref[...] = rstd * (dxh - jnp.mean(dxh, axis=-1, keepdims=True)
                              - xh * jnp.mean(dxh * xh, axis=-1, keepdims=True))

    row = pl.BlockSpec((bs, DV), lambda h, i: (i, h))
    vec = pl.BlockSpec((1, DV), lambda h, i: (0, h))
    do, drg, dg, db = _pcall(
        body, name=name, grid=(H, S // bs),
        in_specs=[pl.BlockSpec((bs, DV), lambda h, i: (i, dr_col0 + h)), row,
                  pl.BlockSpec((bs, DV), lambda h, i: (i, 3 * H + h)), vec, vec],
        out_specs=[row, row, vec, vec],
        out_shape=[_sds((S, H * DV), F32), _sds((S, H * DV), BF16), _sds((1, H * DV), F32), _sds((1, H * DV), F32)],
        compiler_params=_params(("arbitrary", "arbitrary")),
    )(dr, o, hR, g.reshape(1, H * DV), b.reshape(1, H * DV))
    return do, drg, dg[0], db[0]


def _gate_up(xb, wg, wu, *, job=None, name):
    S, D = xb.shape
    F = wg.shape[0]
    bm, bn = _pick(S, 1024, 16), _pick(F, 512, 128)

    def body(x_ref, g_ref, u_ref, og_ref, ou_ref, oa_ref):
        x = x_ref[...]
        g = _nt(x, g_ref[...])
        u = _nt(x, u_ref[...])
        og_ref[...] = g.astype(BF16)
        ou_ref[...] = u.astype(BF16)
        oa_ref[...] = ((g * jax.nn.sigmoid(g)) * u).astype(BF16)

    blk = pl.BlockSpec((bm, bn), lambda i, j: (i, j))
    wspec = pl.BlockSpec((bn, D), lambda i, j: (j, 0))
    outs, jouts = _hosted_call(
        body, job, name=name, grid=(S // bm, F // bn),
        in_specs=[pl.BlockSpec((bm, D), lambda i, j: (i, 0)), wspec, wspec], out_specs=[blk] * 3,
        out_shape=[_sds((S, F), BF16)] * 3, scratch_shapes=[], semantics=("parallel", "parallel"), args=(xb, wg, wu))
    return tuple(outs) if job is None else (*outs, jouts)


def _swiglu_bwd(da, g, u):
    g, u = g.astype(F32), u.astype(F32)
    sg = jax.nn.sigmoid(g)
    return da * u * (sg * (1.0 + g * (1.0 - sg))), da * (g * sg)


def _loss_head(y, t, *, name):
    S, D = y.shape
    bs = _pick(S, 256, 8)
    inv_d = 1.0 / D

    def body(y_ref, t_ref, dy_ref, l_ref):
        @pl.when(pl.program_id(0) == 0)
        def _():
            l_ref[...] = jnp.zeros_like(l_ref)

        e = y_ref[...] - t_ref[...]
        dy_ref[...] = e * inv_d
        l_ref[...] += 0.5 * jnp.sum(jnp.mean(e * e, axis=-1, keepdims=True), axis=0, keepdims=True)

    row = pl.BlockSpec((bs, D), lambda i: (i, 0))
    dy, l = _pcall(
        body, name=name, grid=(S // bs,), in_specs=[row, row],
        out_specs=[row, pl.BlockSpec((1, 1), lambda i: (0, 0))],
        out_shape=[_sds((S, D), F32), _sds((1, 1), F32)],
        compiler_params=_params(("arbitrary",)),
    )(y, t)
    return dy, l[0, 0]


def _adamw(w, g, m, v, *, name):
    shape = w.shape
    C = shape[-1]
    R = w.size // C
    br = _pick(R, 512, 8)

    def body(w_ref, g_ref, m_ref, v_ref, d_ref, nm_ref, nv_ref):
        gv = g_ref[...]
        mn = ADAM_B1 * m_ref[...] + (1.0 - ADAM_B1) * gv
        vn = ADAM_B2 * v_ref[...] + (1.0 - ADAM_B2) * (gv * gv)
        m_hat = mn / (1.0 - ADAM_B1 ** ADAM_STEP)
        v_hat = vn / (1.0 - ADAM_B2 ** ADAM_STEP)
        d_ref[...] = -ADAM_LR * (m_hat / (jnp.sqrt(v_hat) + ADAM_EPS) + ADAM_WD * w_ref[...])
        nm_ref[...] = mn
        nv_ref[...] = vn

    blk = pl.BlockSpec((br, C), lambda i: (i, 0))
    outs = _pcall(body, name=name, grid=(R // br,), in_specs=[blk] * 4, out_specs=[blk] * 3,
                  out_shape=[_sds((R, C), F32)] * 3,
                  compiler_params=_params(("parallel",)))(*[a.reshape(R, C) for a in (w, g, m, v)])
    return tuple(o.reshape(shape) for o in outs)


def _slot_block(n, k):
    return (_pick(n, 1024, 16) if n % 16 == 0 else n), _pick(k, 1024, 128)


def _sum_slots(x, *, name):
    ns, n, k = x.shape
    br, bc = _slot_block(n, k)

    def body(x_ref, o_ref):
        acc = x_ref[0].astype(F32)
        for s in range(1, ns):
            acc = acc + x_ref[s].astype(F32)
        o_ref[...] = acc

    return _pcall(body, name=name, grid=(n // br, k // bc),
                  in_specs=[pl.BlockSpec((ns, br, bc), lambda i, j: (0, i, j))],
                  out_specs=pl.BlockSpec((br, bc), lambda i, j: (i, j)),
                  out_shape=_sds((n, k), F32), compiler_params=_params(("parallel", "parallel")))(x)


def _pair_sum(g, r, *, name):
    _, n, k = r.shape
    br, bc = _slot_block(n, k)
    core = lax.axis_index("c").astype(jnp.int32).reshape(1)

    def body(c_ref, g_ref, r_ref, o_ref):
        o_ref[...] = (g_ref[...].astype(F32) + r_ref[...].astype(F32)).astype(o_ref.dtype)

    blk = pl.BlockSpec((None, br, bc), lambda s, i, j, c_ref: (s, i, j))
    return _pcall(
        body, name=name, out_shape=_sds(r.shape, g.dtype),
        grid_spec=pltpu.PrefetchScalarGridSpec(
            num_scalar_prefetch=1, grid=(N_CHIP, n // br, k // bc),
            in_specs=[pl.BlockSpec((None, br, bc), lambda s, i, j, c_ref: (2 * s + c_ref[0], i, j)), blk],
            out_specs=blk),
        compiler_params=_params(("parallel", "parallel", "parallel")),
    )(core, g, r)


N_PEER = N_DEV - 1
N_CHIP = N_DEV // 2
HOST_TAIL_FRACTION = 8


def _coords():
    return lax.axis_index("x"), lax.axis_index("y"), lax.axis_index("c")


def _ag_phase(phase, x_refs, out_refs, send_sems, recv_sems, local_sems):
    n = len(x_refs)
    x, y, c = _coords()
    me, sibling = (x, y, c), (x, y, 1 - c)
    chips = [(1 - x, y), (x, 1 - y), (1 - x, 1 - y)]

    def copy(a, k, block, to, src=None):
        px, py, pc = block
        dst = out_refs[a].at[4 * px + 2 * py + pc]
        return pltpu.make_async_remote_copy(
            src_ref=dst if src is None else src, dst_ref=dst,
            send_sem=send_sems.at[a * N_PEER + k], recv_sem=recv_sems.at[a * N_PEER + k],
            device_id=to, device_id_type=pl.DeviceIdType.MESH)

    def local(a):
        return pltpu.make_async_copy(x_refs[a], out_refs[a].at[4 * x + 2 * y + c], local_sems.at[a])

    def first(a):
        return ([copy(a, 0, me, sibling, src=x_refs[a])]
                + [copy(a, 1 + j, me, (*chip, c), src=x_refs[a]) for j, chip in enumerate(chips)])

    def passed(a, j):
        return copy(a, 4 + j, (*chips[j], c), sibling)

    if phase == 0:
        for a in range(n):
            local(a).start()
            for cp in first(a):
                cp.start()
    elif phase == 1:
        for j in range(len(chips)):
            for a in range(n):
                copy(a, 1 + j, (*chips[j], c), me).wait_recv()
                passed(a, j).start()
    else:
        for a in range(n):
            copy(a, 0, sibling, me).wait_recv()
            for j in range(len(chips)):
                copy(a, 4 + j, (*chips[j], 1 - c), me).wait_recv()
        for a in range(n):
            for cp in first(a):
                cp.wait_send()
            for j in range(len(chips)):
                passed(a, j).wait_send()
            local(a).wait()


def _ag1_phase(phase, x_refs, out_refs, send_sems, recv_sems, local_sems):
    n = len(x_refs)
    x, y, c = _coords()
    chips = [(1 - x, y), (x, 1 - y), (1 - x, 1 - y)]

    def local(a):
        return pltpu.make_async_copy(x_refs[a], out_refs[a].at[4 * x + 2 * y + c], local_sems.at[a])

    def pair(a, j):
        px, py = chips[j]
        sem = a * N_PEER + j
        send = pltpu.make_async_remote_copy(
            src_ref=x_refs[a], dst_ref=out_refs[a].at[4 * x + 2 * y + c], send_sem=send_sems.at[sem],
            recv_sem=recv_sems.at[sem], device_id=(px, py, c), device_id_type=pl.DeviceIdType.MESH)
        recv = pltpu.make_async_remote_copy(
            src_ref=x_refs[a], dst_ref=out_refs[a].at[4 * px + 2 * py + c], send_sem=send_sems.at[sem],
            recv_sem=recv_sems.at[sem], device_id=(px, py, c), device_id_type=pl.DeviceIdType.MESH)
        return send, recv

    if phase == 0:
        for a in range(n):
            local(a).start()
            for j in range(len(chips)):
                pair(a, j)[0].start()
    elif phase == 2:
        for a in range(n):
            for j in range(len(chips)):
                pair(a, j)[1].wait_recv()
        for a in range(n):
            for j in range(len(chips)):
                pair(a, j)[0].wait_send()
            local(a).wait()


def _ag2_phase(phase, buf_refs, out_refs, send_sems, recv_sems, local_sems):
    n = len(out_refs)
    x, y, c = _coords()

    def copy(a, i):
        return pltpu.make_async_remote_copy(
            src_ref=out_refs[a].at[2 * i + c], dst_ref=out_refs[a].at[2 * i + c],
            send_sem=send_sems.at[a * N_PEER + i], recv_sem=recv_sems.at[a * N_PEER + i],
            device_id=(x, y, 1 - c), device_id_type=pl.DeviceIdType.MESH)

    def arrival(a, i):
        return pltpu.make_async_remote_copy(
            src_ref=out_refs[a].at[2 * i + (1 - c)], dst_ref=out_refs[a].at[2 * i + (1 - c)],
            send_sem=send_sems.at[a * N_PEER + i], recv_sem=recv_sems.at[a * N_PEER + i],
            device_id=(x, y, 1 - c), device_id_type=pl.DeviceIdType.MESH)

    if phase == 0:
        for a in range(n):
            for i in range(N_CHIP):
                copy(a, i).start()
    elif phase == 2:
        for a in range(n):
            for i in range(N_CHIP):
                arrival(a, i).wait_recv()
        for a in range(n):
            for i in range(N_CHIP):
                copy(a, i).wait_send()


def _pair_phase(phase, g_refs, out_refs, send_sems, recv_sems, local_sems):
    n = len(g_refs)
    x, y, c = _coords()

    def copy(a, i):
        return pltpu.make_async_remote_copy(
            src_ref=g_refs[a].at[2 * i + (1 - c)], dst_ref=out_refs[a].at[i],
            send_sem=send_sems.at[a * N_PEER + i], recv_sem=recv_sems.at[a * N_PEER + i],
            device_id=(x, y, 1 - c), device_id_type=pl.DeviceIdType.MESH)

    if phase == 0:
        for a in range(n):
            for i in range(N_CHIP):
                copy(a, i).start()
    elif phase == 2:
        for a in range(n):
            for i in range(N_CHIP):
                copy(a, i).wait_recv()
        for a in range(n):
            for i in range(N_CHIP):
                copy(a, i).wait_send()


def _cross_phase(phase, p_refs, out_refs, send_sems, recv_sems, local_sems):
    n = len(p_refs)
    x, y, c = _coords()
    mine = 2 * x + y

    def local(a):
        return pltpu.make_async_copy(p_refs[a].at[mine], out_refs[a].at[mine], local_sems.at[a])

    def pair(a, k):
        px = 1 - x if k & 2 else x
        py = 1 - y if k & 1 else y
        peer = 2 * px + py
        sem = a * N_PEER + k - 1
        send = pltpu.make_async_remote_copy(
            src_ref=p_refs[a].at[peer], dst_ref=out_refs[a].at[mine], send_sem=send_sems.at[sem],
            recv_sem=recv_sems.at[sem], device_id=(px, py, c), device_id_type=pl.DeviceIdType.MESH)
        recv = pltpu.make_async_remote_copy(
            src_ref=p_refs[a].at[peer], dst_ref=out_refs[a].at[peer], send_sem=send_sems.at[sem],
            recv_sem=recv_sems.at[sem], device_id=(px, py, c), device_id_type=pl.DeviceIdType.MESH)
        return send, recv

    if phase == 0:
        for a in range(n):
            local(a).start()
        for k in range(1, N_CHIP):
            for a in range(n):
                pair(a, k)[0].start()
    elif phase == 2:
        for k in range(1, N_CHIP):
            for a in range(n):
                pair(a, k)[1].wait_recv()
        for k in range(1, N_CHIP):
            for a in range(n):
                pair(a, k)[0].wait_send()
        for a in range(n):
            local(a).wait()


_PHASES = {"ag": _ag_phase, "ag1": _ag1_phase, "ag2": _ag2_phase, "pair": _pair_phase, "cross": _cross_phase}


def _job(kind, arrays):
    arrays = list(arrays)
    if kind in ("ag", "ag1"):
        shapes = [(N_DEV,) + a.shape for a in arrays]
    elif kind == "pair":
        shapes = [(N_CHIP,) + a.shape[1:] for a in arrays]
    else:
        shapes = [a.shape for a in arrays]
    return dict(parts=[(kind, len(arrays))], ins=arrays, outs=[_sds(s, a.dtype) for s, a in zip(shapes, arrays)],
                in_place=[kind == "ag2"] * len(arrays))


def _join(jobs):
    jobs = [j for j in jobs if j is not None]
    if not jobs:
        return None
    return dict(parts=[p for j in jobs for p in j["parts"]], ins=[a for j in jobs for a in j["ins"]],
                outs=[o for j in jobs for o in j["outs"]], in_place=[f for j in jobs for f in j["in_place"]])


def _hosted_call(body, job, *, name, grid, in_specs, out_specs, out_shape, scratch_shapes, semantics, args):
    if job is None:
        outs = _pcall(body, name=name, grid=grid, in_specs=in_specs, out_specs=out_specs, out_shape=out_shape,
                      scratch_shapes=scratch_shapes, compiler_params=_params(semantics))(*args)
        return outs, []
    n_in, n_out, n_scr, nj = len(in_specs), len(out_specs), len(scratch_shapes), len(job["ins"])
    parts = job["parts"]
    total = 1
    for g in grid:
        total *= g
    late = total - max(total // HOST_TAIL_FRACTION, 1) if total > 1 else 0

    def wrapped(*refs):
        ins, refs = refs[:n_in], refs[n_in:]
        jins, refs = refs[:nj], refs[nj:]
        outs, refs = refs[:n_out], refs[n_out:]
        jouts, refs = refs[:nj], refs[nj:]
        scr, sems = refs[:n_scr], refs[n_scr:]
        step = 0
        for d, g in enumerate(grid):
            step = step * g + pl.program_id(d)

        def run(phase):
            off = 0
            for i, (kind, n) in enumerate(parts):
                _PHASES[kind](phase, jins[off:off + n], jouts[off:off + n], *sems[3 * i:3 * i + 3])
                off += n

        pl.when(step == 0)(lambda: run(0))
        pl.when(step == late)(lambda: run(1))
        body(*ins, *outs, *scr)
        pl.when(step == total - 1)(lambda: run(2))

    any_spec = pl.BlockSpec(memory_space=pl.ANY)
    sems = []
    for kind, n in parts:
        sems += [pltpu.SemaphoreType.DMA((n * N_PEER,)), pltpu.SemaphoreType.DMA((n * N_PEER,)),
                 pltpu.SemaphoreType.DMA((n,))]
    outs = _pcall(
        wrapped, name=name, grid=grid, in_specs=list(in_specs) + [any_spec] * nj,
        out_specs=list(out_specs) + [any_spec] * nj, out_shape=list(out_shape) + job["outs"],
        scratch_shapes=list(scratch_shapes) + sems,
        input_output_aliases={n_in + i: n_out + i for i in range(nj) if job["in_place"][i]},
        compiler_params=_params(("arbitrary",) * len(grid)),
    )(*args, *job["ins"])
    return outs[:n_out], outs[n_out:]


def _exchange(job, *, name):
    outs, jouts = _hosted_call(lambda: None, job, name=name, grid=(1,), in_specs=[], out_specs=[], out_shape=[],
                               scratch_shapes=[], semantics=("arbitrary",), args=())
    return jouts


_TRANSPOSED = {"w_in": True, "w_out": False, "w_gate": True, "w_up": True, "w_down": False, "w_uq": True, "w_ukv": True}
_BIG = tuple(_TRANSPOSED)
_SMALL = ("ln_in_g", "ln_in_b", "q_norm_g", "kv_norm_g", "ret_gn_g", "ret_gn_b", "ln1_g", "ln1_b", "ln2_g", "ln2_b")
SMALL_COLS = 128


def _rope_tables(pos, dim):
    inv_freq = ROPE_THETA ** (-jnp.arange(0, dim, 2, dtype=F32) / dim)
    ang = pos.astype(F32)[:, None] * inv_freq
    return jnp.cos(ang), jnp.sin(ang)


def _split_in(wt_in):
    c = MLA_Q_LORA + MLA_KV_LORA + MLA_ROPE
    return jnp.pad(wt_in[:c], ((0, LANES - MLA_ROPE), (0, 0))), wt_in[c:]


def _pad_heads(wt_uq):
    H = MLA_HEADS
    w3 = wt_uq.reshape(H, MLA_QK, wt_uq.shape[1])
    return jnp.pad(w3, ((0, 0), (0, MLA_PAD - MLA_QK), (0, 0)))


class _Plan:
    _NEXT_IN = ((1, "w_in"), (1, "w_uq"), (1, "w_ukv"))
    HOSTS = {
        "ln_in": [("ag", ((0, "w_in"),))],
        "b_ln_in": [("cross", ((0, "w_in"),))],
        "l0_hA": [("ag", ((0, "w_uq"), (0, "w_ukv")))],
        "l0_hR": [("ag1", ((0, "w_out"),))],
        "l0_attn": [("ag2", ((0, "w_out"),)), ("ag1", ((0, "w_gate"), (0, "w_up")))],
        "l0_out": [("ag1", ((1, "w_uq"), (1, "w_ukv")))],
        "l0_ln1": [("ag2", ((0, "w_gate"), (0, "w_up")))],
        "l0_gateup": [("ag", ((0, "w_down"),)), ("ag1", ((1, "w_in"),))],
        "l0_down": [("ag2", _NEXT_IN), ("ag1", ((1, "w_gate"),))],
        "l1_hR": [("ag2", ((1, "w_gate"),)), ("ag1", ((1, "w_out"),))],
        "l1_attn": [("ag2", ((1, "w_out"),)), ("ag1", ((1, "w_up"),))],
        "l1_out": [("ag2", ((1, "w_up"),))],
        "l1_gateup": [("ag", ((1, "w_down"),))],
        "l0_b_dact": [("cross", ((1, "w_in"),))]}
    for _l in (0, 1):
        HOSTS.update({
            "l%d_b_wgate" % _l: [("pair", ((_l, "w_down"),))],
            "l%d_b_wup" % _l: [("pair", ((_l, "w_gate"),)), ("cross", ((_l, "w_down"),))],
            "l%d_b_dx1a" % _l: [("pair", ((_l, "w_up"),)), ("cross", ((_l, "w_gate"),))],
            "l%d_b_dx1b" % _l: [("cross", ((_l, "w_up"),))],
            "l%d_b_attn" % _l: [("pair", ((_l, "w_out"),))],
            "l%d_b_win" % _l: [("pair", ((_l, "w_uq"), (_l, "w_ukv"))), ("cross", ((_l, "w_out"),))],
            "l%d_b_dx" % _l: [("pair", ((_l, "w_in"),)), ("cross", ((_l, "w_uq"), (_l, "w_ukv")))]})

    def __init__(self, local):
        self.local = local
        self.half = {}
        self.full = {}
        self.grads = {}
        self.paired = {}
        self.recv = {}

    def _by_device(self, k):
        return self.grads[k].reshape((N_DEV,) + self.local[k].shape)

    def _make(self, kind, keys):
        if kind in ("ag", "ag1"):
            return _job(kind, [self.local[k] for k in keys])
        if kind == "ag2":
            return _job(kind, [self.half[k] for k in keys])
        if kind == "pair":
            return _job(kind, [self._by_device(k) for k in keys])
        return _job(kind, [_pair_sum(self._by_device(k), self.paired[k], name="pairsum_l%d_%s" % k) for k in keys])

    def _done(self, kind, keys, outs):
        for k, o in zip(keys, outs):
            if kind in ("ag", "ag2"):
                self.full[k] = o.reshape(N_DEV * o.shape[1], o.shape[2])
            elif kind == "ag1":
                self.half[k] = o
            elif kind == "pair":
                self.paired[k] = o
            else:
                self.recv[k] = o

    def _run(self, todo, call):
        outs = call(_join([self._make(kind, keys) for kind, keys in todo]))
        for kind, keys in todo:
            self._done(kind, keys, outs[:len(keys)])
            outs = outs[len(keys):]

    def call(self, fn, name, *args, **kw):
        if name not in self.HOSTS:
            return fn(*args, name=name, **kw)
        res = []

        def run(job):
            *outs, jouts = fn(*args, name=name, job=job, **kw)
            res.extend(outs)
            return jouts

        self._run(self.HOSTS[name], run)
        return res[0] if len(res) == 1 else tuple(res)


def _layer_fwd(x, xb, plan, p, tabs, l):
    S, D = x.shape
    H = MLA_HEADS
    nm = lambda s: "l%d_%s" % (l, s)
    w = lambda n: plan.full[(l, n)]
    mm = lambda name, *a, **kw: plan.call(_mm, nm(name), *a, **kw)
    wt_a, wt_r = _split_in(w("w_in"))
    hA = mm("hA", xb, wt_a, tb=True)
    hR = mm("hR", xb, wt_r, tb=True)
    _, qn, qn_hat, q_rstd = _norm_fwd(hA, p["q_norm_g"], None, center=False, eps=RMS_EPS, want_f32=False,
                                      col=0, name=nm("qnorm"))
    _, kvn, kvn_hat, kv_rstd = _norm_fwd(hA, p["kv_norm_g"], None, center=False, eps=RMS_EPS, want_f32=False,
                                         col=MLA_Q_LORA // MLA_KV_LORA, name=nm("kvnorm"))
    w_uq3 = _pad_heads(w("w_uq"))
    w_ukv3 = w("w_ukv").reshape(H, MLA_NOPE + MLA_V, -1)
    q3 = _mm_heads_out(qn, w_uq3, name=nm("uq"))
    kv3 = _mm_heads_out(kvn, w_ukv3, name=nm("ukv"))
    qh, kh, vh = _mla_prep(q3, kv3, hA, tabs["c128"], tabs["s128"],
                           kr_col=(MLA_Q_LORA + MLA_KV_LORA) // LANES, name=nm("rope"))
    a, lse = plan.call(_attn_fwd, nm("attn"), qh, kh, vh)
    o_ret, states = _ret_fwd(hR, tabs["cos_r"], tabs["sin_r"], tabs["ret"], name=nm("ret"))
    r = _gn_gate_fwd(o_ret, hR, p["ret_gn_g"], p["ret_gn_b"], name=nm("gn"))
    mix_in = jnp.concatenate([a.astype(BF16), r], axis=1)
    mix = mm("out", mix_in, w("w_out"))
    x1, x1b, x1_hat, rstd1 = plan.call(_norm_fwd, nm("ln1"), x, p["ln1_g"], p["ln1_b"], res=mix, alpha=p["alpha"],
                                       eps=LN_EPS)
    gb, ub, act = plan.call(_gate_up, nm("gateup"), x1b, w("w_gate"), w("w_up"))
    f = mm("down", act, w("w_down"))
    x2, x2b, x2_hat, rstd2 = _norm_fwd(x1, p["ln2_g"], p["ln2_b"], res=f, alpha=p["alpha"], eps=LN_EPS, name=nm("ln2"))
    saved = dict(xb=xb, qn=qn, qn_hat=qn_hat, q_rstd=q_rstd, kvn=kvn, kvn_hat=kvn_hat, kv_rstd=kv_rstd,
                 qh=qh, kh=kh, vh=vh, a=a, lse=lse, hR=hR, o_ret=o_ret, states=states, mix_in=mix_in,
                 x1b=x1b, x1_hat=x1_hat, rstd1=rstd1, gb=gb, ub=ub, act=act, x2_hat=x2_hat, rstd2=rstd2)
    return x2, x2b, saved


def _layer_bwd(dx2, sv, plan, p, tabs, l):
    S, D = dx2.shape
    H = MLA_HEADS
    nm = lambda s: "l%d_b_%s" % (l, s)
    w = lambda n: plan.full[(l, n)]
    mm = lambda name, *a, **kw: plan.call(_mm, nm(name), *a, **kw)
    alpha = p["alpha"]
    gw, gp = plan.grads, {}
    dz2, dz2b, gp["ln2_g"], gp["ln2_b"] = _norm_bwd(dx2, sv["x2_hat"], sv["rstd2"], p["ln2_g"], name=nm("ln2"))
    dg, du = mm("dact", dz2b, w("w_down"), tb=True, extras=(sv["gb"], sv["ub"]), epilogue=_swiglu_bwd,
                out_dtypes=(BF16, BF16))
    gw[(l, "w_down")] = mm("wdown", sv["act"], dz2b, ta=True, out_dtype=BF16)
    gw[(l, "w_gate")] = mm("wgate", dg, sv["x1b"], ta=True, out_dtype=BF16)
    gw[(l, "w_up")] = mm("wup", du, sv["x1b"], ta=True, out_dtype=BF16)
    t = mm("dx1a", dg, w("w_gate"), add=dz2, add_scale=alpha)
    dx1 = mm("dx1b", du, w("w_up"), add=t)
    dz1, dz1b, gp["ln1_g"], gp["ln1_b"] = _norm_bwd(dx1, sv["x1_hat"], sv["rstd1"], p["ln1_g"], name=nm("ln1"))
    dmix = mm("dmix", dz1b, w("w_out"), tb=True)
    gw[(l, "w_out")] = mm("wout", sv["mix_in"], dz1b, ta=True, out_dtype=BF16)
    ret_col0 = (H * MLA_V) // RET_V
    do_ret, drg, gp["ret_gn_g"], gp["ret_gn_b"] = _gn_gate_bwd(
        dmix, sv["o_ret"], sv["hR"], p["ret_gn_g"], p["ret_gn_b"], dr_col0=ret_col0, name=nm("gn"))
    drq, drk, drv = _ret_bwd(do_ret, sv["hR"], sv["states"], tabs["cos_r"], tabs["sin_r"], tabs["ret"], name=nm("ret"))
    dqh, dkh, dvh = plan.call(_attn_bwd, nm("attn"), sv["qh"], sv["kh"], sv["vh"], sv["a"], dmix, sv["lse"], do_col0=0)
    dq3, dkv3, dkrp = _mla_unprep(dqh, dkh, dvh, tabs["c128"], tabs["s128"], name=nm("rope"))
    w_uq3 = _pad_heads(w("w_uq"))
    w_ukv3 = w("w_ukv").reshape(H, MLA_NOPE + MLA_V, -1)
    g_uq = _mm_heads_tn(dq3, sv["qn"], name=nm("wuq")).reshape(H, MLA_PAD, -1)
    gw[(l, "w_uq")] = g_uq[:, :MLA_QK].reshape(H * MLA_QK, -1)
    dqn = _mm_heads_sum(dq3, w_uq3, name=nm("dqn"))
    gw[(l, "w_ukv")] = _mm_heads_tn(dkv3, sv["kvn"], name=nm("wukv"))
    dkvn = _mm_heads_sum(dkv3, w_ukv3, name=nm("dkvn"))
    _, dcq, gp["q_norm_g"], _ = _norm_bwd(dqn, sv["qn_hat"], sv["q_rstd"], p["q_norm_g"], center=False,
                                          want_f32=False, name=nm("qnorm"))
    _, dckv, gp["kv_norm_g"], _ = _norm_bwd(dkvn, sv["kvn_hat"], sv["kv_rstd"], p["kv_norm_g"], center=False,
                                            want_f32=False, name=nm("kvnorm"))
    dh = jnp.concatenate([dcq, dckv, dkrp[:, :MLA_ROPE].astype(BF16), drq, drk, drv, drg], axis=1)
    gw[(l, "w_in")] = mm("win", dh, sv["xb"], ta=True, out_dtype=BF16)
    dx = mm("dx", dh, w("w_in"), add=dz1, add_scale=alpha)
    return dx, gp


def _local_step(x, target, pos, small, plan, depth):
    alpha = (2 * depth) ** 0.25
    cos_m, sin_m = _rope_tables(pos, MLA_ROPE)
    cos_r, sin_r = _rope_tables(pos, RET_QK)
    zeros = jnp.zeros((x.shape[0], LANES - MLA_ROPE), F32)
    tabs = dict(c128=jnp.concatenate([cos_m, cos_m, zeros], axis=1), s128=jnp.concatenate([-sin_m, sin_m, zeros], axis=1),
                cos_r=cos_r, sin_r=sin_r, ret=_ret_tables(_pick(x.shape[0], RET_BLOCK, CHUNK)))
    h, hb, h_hat, h_rstd = plan.call(_norm_fwd, "ln_in", x, small["ln_in_g"], small["ln_in_b"], eps=LN_EPS)
    saved, ps = [], []
    for l in range(depth):
        p = {k: small[k][l] for k in _SMALL[2:]}
        p["alpha"] = alpha
        h, hb, sv = _layer_fwd(h, hb, plan, p, tabs, l)
        saved.append(sv)
        ps.append(p)
    dy, loss = _loss_head(h, target, name="loss")
    gps = [None] * depth
    for l in reversed(range(depth)):
        dy, gps[l] = _layer_bwd(dy, saved[l], plan, ps[l], tabs, l)
    grad_x, _, g_in_g, g_in_b = plan.call(_norm_bwd, "b_ln_in", dy, h_hat, h_rstd, small["ln_in_g"], want_bf16=False)
    gsmall = {"ln_in_g": g_in_g, "ln_in_b": g_in_b}
    for k in _SMALL[2:]:
        gsmall[k] = jnp.stack([gps[l][k] for l in range(depth)])
    return loss, grad_x, gsmall


def kernel(x, positions, ln_in_g, ln_in_b, w_in, q_norm_g, kv_norm_g, w_uq, w_ukv, ret_gn_g, ret_gn_b, w_out, ln1_g, ln1_b, w_gate, w_up, w_down, ln2_g, ln2_b, loss_target, m_ln_in_g, m_ln_in_b, m_w_in, m_q_norm_g, m_kv_norm_g, m_w_uq, m_w_ukv, m_ret_gn_g, m_ret_gn_b, m_w_out, m_ln1_g, m_ln1_b, m_w_gate, m_w_up, m_w_down, m_ln2_g, m_ln2_b, v_ln_in_g, v_ln_in_b, v_w_in, v_q_norm_g, v_kv_norm_g, v_w_uq, v_w_ukv, v_ret_gn_g, v_ret_gn_b, v_w_out, v_ln1_g, v_ln1_b, v_w_gate, v_w_up, v_w_down, v_ln2_g, v_ln2_b):
    names = ["ln_in_g", "ln_in_b", "w_in", "q_norm_g", "kv_norm_g", "w_uq", "w_ukv", "ret_gn_g", "ret_gn_b", "w_out",
             "ln1_g", "ln1_b", "w_gate", "w_up", "w_down", "ln2_g", "ln2_b"]
    wv = dict(zip(names, (ln_in_g, ln_in_b, w_in, q_norm_g, kv_norm_g, w_uq, w_ukv, ret_gn_g, ret_gn_b, w_out,
                          ln1_g, ln1_b, w_gate, w_up, w_down, ln2_g, ln2_b)))
    mv = dict(zip(names, (m_ln_in_g, m_ln_in_b, m_w_in, m_q_norm_g, m_kv_norm_g, m_w_uq, m_w_ukv, m_ret_gn_g,
                          m_ret_gn_b, m_w_out, m_ln1_g, m_ln1_b, m_w_gate, m_w_up, m_w_down, m_ln2_g, m_ln2_b)))
    vv = dict(zip(names, (v_ln_in_g, v_ln_in_b, v_w_in, v_q_norm_g, v_kv_norm_g, v_w_uq, v_w_ukv, v_ret_gn_g,
                          v_ret_gn_b, v_w_out, v_ln1_g, v_ln1_b, v_w_gate, v_w_up, v_w_down, v_ln2_g, v_ln2_b)))
    depth = w_in.shape[0]
    assert depth == 2, "the exchange plan is written for two layers"

    keys = [(l, n) for l in range(depth) for n in _BIG]
    plan = _Plan({(l, n): (wv[n][l].T if _TRANSPOSED[n] else wv[n][l]).astype(BF16) for l, n in keys})

    small = {n: wv[n] for n in _SMALL}
    loss, grad_x, gsmall = _local_step(x[0], loss_target[0], positions[0], small, plan, depth)
    loss = lax.psum(loss, MESH_AXES)

    gshard = {n: [None] * depth for n in _BIG}
    for l, n in keys:
        tot = _sum_slots(plan.recv[(l, n)], name="sum_l%d_%s" % (l, n))
        gshard[n][l] = tot.T if _TRANSPOSED[n] else tot
    grads = {n: jnp.stack(v) for n, v in gshard.items()}

    flat = jnp.concatenate([gsmall[n].reshape(-1) for n in _SMALL])
    n_small = flat.shape[0]
    rows = -(-n_small // (SMALL_COLS * 8)) * 8
    flat = jnp.pad(flat, (0, rows * SMALL_COLS - n_small)).reshape(rows, SMALL_COLS)
    tot = _sum_slots(_exchange(_job("ag", [flat]), name="ag_small")[0], name="sum_small").reshape(-1)
    off = 0
    for n in _SMALL:
        grads[n] = tot[off:off + wv[n].size].reshape(wv[n].shape)
        off += wv[n].size

    delta, new_m, new_v = {}, {}, {}
    for n in names:
        w2 = wv[n] if wv[n].ndim > 1 else wv[n].reshape(1, -1)
        d, nm_, nv_ = _adamw(w2, grads[n].reshape(w2.shape), mv[n].reshape(w2.shape), vv[n].reshape(w2.shape),
                             name="adamw_" + n)
        delta[n], new_m[n], new_v[n] = d.reshape(wv[n].shape), nm_.reshape(wv[n].shape), nv_.reshape(wv[n].shape)

    return (loss, grad_x[None], *[grads[n] for n in names], *[delta[n] for n in names],
            *[new_m[n] for n in names], *[new_v[n] for n in names])
```
